```python
import jax, jax.numpy as jnp
from jax import lax
import numpy as np

D_MODEL = 2048
BATCH = 4
SEQ = 2048
DEPTH = 2

HEAD_DIM = 128
A_HEADS = 6
A_KV_HEADS = 2
A_ROPE_THETA = 10000.0
B_HEADS = 4
B_PATTERNS = ((128, 1), (512, 4), (2048, 16))
B_N_GROUPS = 3
C_HEADS = 6
C_Q_RANK = 512
C_KV_RANK = 512
C_NOPE_DIM = 128
C_ROPE_DIM = 64
C_V_DIM = 128
C_ROPE_THETA = 10000.0
PARTIAL_ROPE_DIM = HEAD_DIM // 4
PARTIAL_ROPE_THETA = 500000.0
GRID_W = 64
Q_BLOCK = 128
D_FF = 5632
CONV_W = 3
EPS = 1e-6

A_WIDTH = A_HEADS * HEAD_DIM
B_WIDTH = B_HEADS * HEAD_DIM
C_WIDTH = C_HEADS * C_V_DIM
MIX_WIDTH = A_WIDTH + B_WIDTH + C_WIDTH

A_Q_COLS = A_HEADS * HEAD_DIM
A_KV_COLS = A_KV_HEADS * HEAD_DIM
B_Q_COLS = B_N_GROUPS * B_HEADS * HEAD_DIM
B_KV_COLS = B_HEADS * HEAD_DIM
IN_SPLITS = (A_Q_COLS, A_KV_COLS, A_KV_COLS, B_Q_COLS, B_KV_COLS, B_KV_COLS, C_Q_RANK, C_KV_RANK, C_ROPE_DIM)
IN_WIDTH = A_Q_COLS + 2 * A_KV_COLS + B_Q_COLS + 2 * B_KV_COLS + C_Q_RANK + C_KV_RANK + C_ROPE_DIM

kernel_name = "hybrid_parallel_gqa_dilated_mla_convffn"


def rms_norm(x, g):
    xf = x.astype(jnp.float32)
    y = xf * lax.rsqrt(jnp.mean(xf * xf, axis=-1, keepdims=True) + EPS)
    return (y * g.astype(jnp.float32)).astype(x.dtype)


def rope_table(pos, dim, theta):
    inv = theta ** (-jnp.arange(0, dim, 2, dtype=jnp.float32) / dim)
    ang = pos.astype(jnp.float32)[:, None] * inv[None, :]
    return jnp.cos(ang), jnp.sin(ang)


def apply_rope(x, cs):
    cos, sin = cs
    cos = cos[:, None, :]
    sin = sin[:, None, :]
    xf = x.astype(jnp.float32)
    x1, x2 = jnp.split(xf, 2, axis=-1)
    return jnp.concatenate([x1 * cos - x2 * sin, x1 * sin + x2 * cos], axis=-1).astype(x.dtype)


def axial_rope(x, cs_row, cs_col):
    half = x.shape[-1] // 2
    return jnp.concatenate([apply_rope(x[..., :half], cs_row), apply_rope(x[..., half:], cs_col)], axis=-1)


def partial_rope(x, cs):
    return jnp.concatenate([apply_rope(x[..., :PARTIAL_ROPE_DIM], cs), x[..., PARTIAL_ROPE_DIM:]], axis=-1)


def dense_block_attention(q, k, v):
    b, s, hq, dk = q.shape
    hkv = k.shape[2]
    g = hq // hkv
    nb = s // Q_BLOCK
    scale = dk ** -0.5
    qb = q.reshape(b, nb, Q_BLOCK, hkv, g, dk).transpose(1, 0, 2, 3, 4, 5)

    def one_block(qblk):
        sc = jnp.einsum('bqhgd,bkhd->bhgqk', qblk, k).astype(jnp.float32) * scale
        p = jax.nn.softmax(sc, axis=-1).astype(v.dtype)
        return jnp.einsum('bhgqk,bkhd->bqhgd', p, v)

    o = lax.map(one_block, qb)
    return o.transpose(1, 0, 2, 3, 4, 5).reshape(b, s, hq, v.shape[-1])


def fold_stride(x, d):
    b, s, h, dh = x.shape
    return x.reshape(b, s // d, d, h, dh).transpose(0, 2, 1, 3, 4).reshape(b * d, s // d, h, dh)


def unfold_stride(x, b, d):
    l = x.shape[1]
    rest = x.shape[2:]
    perm = (0, 2, 1) + tuple(range(3, 3 + len(rest)))
    return x.reshape((b, d, l) + rest).transpose(perm).reshape((b, l * d) + rest)


def banded_attention(q, k, v, half):
    n, l, h, dh = q.shape
    blk = half
    nb = -(-l // blk)
    lp = nb * blk
    qp = jnp.pad(q, ((0, 0), (0, lp - l), (0, 0), (0, 0))).reshape(n, nb, blk, h, dh)

    def key_blocks(t):
        tp = jnp.pad(t, ((0, 0), (blk, lp - l + blk), (0, 0), (0, 0))).reshape(n, nb + 2, blk, h, dh)
        return jnp.concatenate([tp[:, :-2], tp[:, 1:-1], tp[:, 2:]], axis=2)

    kb = key_blocks(k)
    vb = key_blocks(v)
    qpos = jnp.arange(nb)[:, None] * blk + jnp.arange(blk)[None, :]
    kpos = jnp.arange(nb)[:, None] * blk - blk + jnp.arange(3 * blk)[None, :]
    rel = kpos[:, None, :] - qpos[:, :, None]
    valid = (jnp.abs(rel) <= half) & (kpos[:, None, :] >= 0) & (kpos[:, None, :] < l)
    sc = jnp.einsum('nbqhd,nbkhd->nbhqk', qp, kb).astype(jnp.float32) * (dh ** -0.5)
    sc = jnp.where(valid[None, :, None], sc, -jnp.inf)
    mx = jnp.max(sc, axis=-1, keepdims=True)
    e = jnp.exp(sc - mx)
    den = jnp.sum(e, axis=-1, keepdims=True)
    o = jnp.einsum('nbhqk,nbkhd->nbqhd', (e / den).astype(v.dtype), vb)
    lse = (mx + jnp.log(den))[..., 0]
    o = o.reshape(n, lp, h, dh)[:, :l]
    lse = lse.transpose(0, 1, 3, 2).reshape(n, lp, h)[:, :l]
    return o, lse


def dilated_mixture(q_groups, k, v):
    b = k.shape[0]
    outs = []
    lses = []
    for g, (window, dil) in enumerate(B_PATTERNS):
        half = window // (2 * dil)
        o, lse = banded_attention(fold_stride(q_groups[:, :, g], dil), fold_stride(k, dil),
                                  fold_stride(v, dil), half)
        outs.append(unfold_stride(o, b, dil))
        lses.append(unfold_stride(lse, b, dil))
    w = jax.nn.softmax(jnp.stack(lses, axis=0), axis=0)
    y = jnp.einsum('gbsh,gbshd->bshd', w, jnp.stack(outs, axis=0).astype(jnp.float32))
    return y.astype(k.dtype)


def conv_ffn(x, w_up, conv_w, conv_b, w_down):
    h = x @ w_up
    s = h.shape[1]
    hp = jnp.pad(h, ((0, 0), (1, 1), (0, 0)))
    h = hp[:, :s] * conv_w[0] + hp[:, 1:s + 1] * conv_w[1] + hp[:, 2:] * conv_w[2] + conv_b
    gate, up = jnp.split(h, 2, axis=-1)
    return (jax.nn.silu(gate) * up) @ w_down


def setup_inputs(seed: int = 0) -> dict:
    key = jax.random.key(seed)
    ks = jax.random.split(key, 20)
    f32 = jnp.float32

    def nrm(k, shape, scale):
        return jax.random.normal(k, shape, f32) * scale

    def gain(k, shape):
        return 1.0 + 0.02 * jax.random.normal(k, shape, f32)

    return {
        "x": jax.random.normal(ks[0], (BATCH, SEQ, D_MODEL), f32),
        "attn_norm": gain(ks[1], (DEPTH, D_MODEL)),
        "w_in": nrm(ks[2], (DEPTH, D_MODEL, IN_WIDTH), D_MODEL ** -0.5),
        "a_q_norm": gain(ks[3], (DEPTH, HEAD_DIM)),
        "a_k_norm": gain(ks[4], (DEPTH, HEAD_DIM)),
        "c_q_norm": gain(ks[5], (DEPTH, C_Q_RANK)),
        "c_kv_norm": gain(ks[6], (DEPTH, C_KV_RANK)),
        "w_uq": nrm(ks[7], (DEPTH, C_Q_RANK, C_HEADS * (C_NOPE_DIM + C_ROPE_DIM)), C_Q_RANK ** -0.5),
        "w_ukv": nrm(ks[8], (DEPTH, C_KV_RANK, C_HEADS * (C_NOPE_DIM + C_V_DIM)), C_KV_RANK ** -0.5),
        "out_norm": gain(ks[9], (DEPTH, MIX_WIDTH)),
        "w_out": nrm(ks[10], (DEPTH, MIX_WIDTH, D_MODEL), MIX_WIDTH ** -0.5),
        "ffn_norm": gain(ks[11], (DEPTH, D_MODEL)),
        "w_up": nrm(ks[12], (DEPTH, D_MODEL, 2 * D_FF), D_MODEL ** -0.5),
        "conv_w": nrm(ks[13], (DEPTH, CONV_W, 2 * D_FF), CONV_W ** -0.5),
        "conv_b": nrm(ks[14], (DEPTH, 2 * D_FF), 0.01),
        "w_down": nrm(ks[15], (DEPTH, D_FF, D_MODEL), D_FF ** -0.5),
        "final_norm": gain(ks[16], (D_MODEL,)),
    }


def reference(x, attn_norm, w_in, a_q_norm, a_k_norm, c_q_norm, c_kv_norm, w_uq, w_ukv, out_norm,
              w_out, ffn_norm, w_up, conv_w, conv_b, w_down, final_norm):
    b, s, _ = x.shape
    rows = s // GRID_W
    row_pos = jnp.repeat(jnp.arange(rows), GRID_W)
    col_pos = jnp.tile(jnp.arange(GRID_W), rows)
    t_pos = jnp.arange(s)
    cs_row = rope_table(row_pos, HEAD_DIM // 2, A_ROPE_THETA)
    cs_col = rope_table(col_pos, HEAD_DIM // 2, A_ROPE_THETA)
    cs_part = rope_table(t_pos, PARTIAL_ROPE_DIM, PARTIAL_ROPE_THETA)
    cs_mla = rope_table(t_pos, C_ROPE_DIM, C_ROPE_THETA)
    split_idx = np.cumsum(np.array(IN_SPLITS))[:-1].tolist()

    for l in range(DEPTH):
        hn = rms_norm(x, attn_norm[l])
        proj = hn @ w_in[l]
        aq, ak, av, bq, bk, bv, cq, ckv, ckr = jnp.split(proj, split_idx, axis=-1)

        aq = axial_rope(rms_norm(aq.reshape(b, s, A_HEADS, HEAD_DIM), a_q_norm[l]), cs_row, cs_col)
        ak = axial_rope(rms_norm(ak.reshape(b, s, A_KV_HEADS, HEAD_DIM), a_k_norm[l]), cs_row, cs_col)
        av = av.reshape(b, s, A_KV_HEADS, HEAD_DIM)
        ya = dense_block_attention(aq, ak, av).reshape(b, s, A_WIDTH)

        bq = partial_rope(bq.reshape(b, s, B_N_GROUPS * B_HEADS, HEAD_DIM), cs_part)
        bq = bq.reshape(b, s, B_N_GROUPS, B_HEADS, HEAD_DIM)
        bk = partial_rope(bk.reshape(b, s, B_HEADS, HEAD_DIM), cs_part)
        bv = bv.reshape(b, s, B_HEADS, HEAD_DIM)
        yb = dilated_mixture(bq, bk, bv).reshape(b, s, B_WIDTH)

        q = (rms_norm(cq, c_q_norm[l]) @ w_uq[l]).reshape(b, s, C_HEADS, C_NOPE_DIM + C_ROPE_DIM)
        q = jnp.concatenate([q[..., :C_NOPE_DIM], apply_rope(q[..., C_NOPE_DIM:], cs_mla)], axis=-1)
        kv = (rms_norm(ckv, c_kv_norm[l]) @ w_ukv[l]).reshape(b, s, C_HEADS, C_NOPE_DIM + C_V_DIM)
        k_nope = kv[..., :C_NOPE_DIM]
        cv = kv[..., C_NOPE_DIM:]
        k_rope = apply_rope(ckr[:, :, None, :], cs_mla)
        k = jnp.concatenate([k_nope, jnp.broadcast_to(k_rope, (b, s, C_HEADS, C_ROPE_DIM))], axis=-1)
        yc = dense_block_attention(q, k, cv).reshape(b, s, C_WIDTH)

        g_a, g_b, g_c = jnp.split(out_norm[l], [A_WIDTH, A_WIDTH + B_WIDTH])
        y = jnp.concatenate([rms_norm(ya, g_a), rms_norm(yb, g_b), rms_norm(yc, g_c)], axis=-1)
        x = x + y @ w_out[l]

        x = x + conv_ffn(rms_norm(x, ffn_norm[l]), w_up[l], conv_w[l], conv_b[l], w_down[l])

    return rms_norm(x, final_norm)
```

```python
import functools
import math

import jax
import jax.numpy as jnp
from jax import lax
from jax.experimental import pallas as pl
from jax.experimental.pallas import tpu as pltpu

D_MODEL = 2048
HEAD_DIM = 128
A_HEADS = 6
A_KV_HEADS = 2
A_GROUP = A_HEADS // A_KV_HEADS
A_ROPE_THETA = 10000.0
B_HEADS = 4
B_PATTERNS = ((128, 1), (512, 4), (2048, 16))
B_N_GROUPS = 3
C_HEADS = 6
C_Q_RANK = 512
C_KV_RANK = 512
C_NOPE_DIM = 128
C_ROPE_DIM = 64
C_V_DIM = 128
C_ROPE_THETA = 10000.0
PARTIAL_ROPE_DIM = HEAD_DIM // 4
PARTIAL_ROPE_THETA = 500000.0
GRID_W = 64
D_FF = 5632
EPS = 1e-6

A_WIDTH = A_HEADS * HEAD_DIM
B_WIDTH = B_HEADS * HEAD_DIM
C_WIDTH = C_HEADS * C_V_DIM
C_QK_PAD = 256

LANES = 128
LOG2E = math.log2(math.e)
F32 = jnp.float32
BF16 = jnp.bfloat16

PROJ_WIDTH = 5120
CB_CQ, CB_CKV = 0, 1
CB_AQ, CB_AK, CB_AV = 8, 14, 16
CB_BQ, CB_BK, CB_BV = 18, 30, 34
CB_CKR = 38


def _params(semantics, vmem_mib):
    return pltpu.CompilerParams(dimension_semantics=semantics, vmem_limit_bytes=vmem_mib * 1024 * 1024)


def _rms(xf, g):
    return xf * lax.rsqrt(jnp.mean(xf * xf, axis=-1, keepdims=True) + EPS) * g


def _rope(xf, cos, sin_lo, sin_hi, shift):
    return xf * cos + pltpu.roll(xf, LANES - shift, 1) * sin_lo + pltpu.roll(xf, shift, 1) * sin_hi


def _dot(a, b):
    return jnp.dot(a, b, preferred_element_type=F32)


def _dot_nt(a, b):
    return lax.dot_general(a, b, (((1,), (1,)), ((), ())), preferred_element_type=F32)


def _norm_kernel(x_ref, g_ref, o_ref):
    o_ref[...] = _rms(x_ref[...], g_ref[...]).astype(o_ref.dtype)


def _rmsnorm_rows(x, g, tm=512):
    m, d = x.shape
    return pl.pallas_call(
        _norm_kernel,
        grid=(m // tm,),
        in_specs=[pl.BlockSpec((tm, d), lambda i: (i, 0)), pl.BlockSpec((1, d), lambda i: (0, 0))],
        out_specs=pl.BlockSpec((tm, d), lambda i: (i, 0)),
        out_shape=jax.ShapeDtypeStruct((m, d), BF16),
        compiler_params=_params(("parallel",), 40),
        name="rmsnorm",
    )(x, g)


def _mm_kernel(a_ref, w_ref, o_ref):
    o_ref[...] = _dot(a_ref[...], w_ref[...]).astype(o_ref.dtype)


def _matmul(a, w, tm=1024, tn=1024):
    m, k = a.shape
    n = w.shape[1]
    return pl.pallas_call(
        _mm_kernel,
        grid=(m // tm, n // tn),
        in_specs=[pl.BlockSpec((tm, k), lambda i, j: (i, 0)), pl.BlockSpec((k, tn), lambda i, j: (0, j))],
        out_specs=pl.BlockSpec((tm, tn), lambda i, j: (i, j)),
        out_shape=jax.ShapeDtypeStruct((m, n), BF16),
        compiler_params=_params(("parallel", "arbitrary"), 48),
        name="in_proj",
    )(a, w)


def _latent_up_kernel(cq_ref, ckv_ref, gq_ref, gkv_ref, wq_ref, wkv_ref, q_ref, kv_ref, *, q_scale):
    cq = _rms(cq_ref[...].astype(F32), gq_ref[...]).astype(BF16)
    q_ref[...] = (_dot(cq, wq_ref[...]) * q_scale).astype(q_ref.dtype)
    ckv = _rms(ckv_ref[...].astype(F32), gkv_ref[...]).astype(BF16)
    kv_ref[...] = _dot(ckv, wkv_ref[...]).astype(kv_ref.dtype)


def _latent_up(proj, gq, gkv, wq, wkv, tm=1024):
    m = proj.shape[0]
    nq, nkv = wq.shape[1], wkv.shape[1]
    q_scale = (C_NOPE_DIM + C_ROPE_DIM) ** -0.5 * LOG2E
    return pl.pallas_call(
        functools.partial(_latent_up_kernel, q_scale=q_scale),
        grid=(m // tm,),
        in_specs=[
            pl.BlockSpec((tm, C_Q_RANK), lambda i: (i, CB_CQ)),
            pl.BlockSpec((tm, C_KV_RANK), lambda i: (i, CB_CKV)),
            pl.BlockSpec((1, C_Q_RANK), lambda i: (0, 0)),
            pl.BlockSpec((1, C_KV_RANK), lambda i: (0, 0)),
            pl.BlockSpec((C_Q_RANK, nq), lambda i: (0, 0)),
            pl.BlockSpec((C_KV_RANK, nkv), lambda i: (0, 0)),
        ],
        out_specs=[pl.BlockSpec((tm, nq), lambda i: (i, 0)), pl.BlockSpec((tm, nkv), lambda i: (i, 0))],
        out_shape=[jax.ShapeDtypeStruct((m, nq), BF16), jax.ShapeDtypeStruct((m, nkv), BF16)],
        compiler_params=_params(("parallel",), 48),
        name="latent_up",
    )(proj, proj, gq, gkv, wq, wkv)


def _softmax_pv(s, v):
    m = jnp.max(s, axis=-1, keepdims=True)
    p = jnp.exp2(s - m)
    l = jnp.sum(p, axis=-1, keepdims=True)
    return _dot(p.astype(BF16), v) * (1.0 / l)


def _attn_a_kernel(q0_ref, q1_ref, q2_ref, k_ref, v_ref, cos_ref, slo_ref, shi_ref, gq_ref, gk_ref,
                   o_ref, kbuf, *, tq):
    qi = pl.program_id(2)

    @pl.when(qi == 0)
    def _():
        k = _rms(k_ref[...].astype(F32), gk_ref[...])
        kbuf[...] = _rope(k, cos_ref[...], slo_ref[...], shi_ref[...], 32).astype(BF16)

    rows = pl.ds(pl.multiple_of(qi * tq, tq), tq)
    cos, slo, shi = cos_ref[rows, :], slo_ref[rows, :], shi_ref[rows, :]
    q_scale = HEAD_DIM ** -0.5 * LOG2E
    qs = []
    for q_ref in (q0_ref, q1_ref, q2_ref):
        q = _rms(q_ref[...].astype(F32), gq_ref[...])
        qs.append((_rope(q, cos, slo, shi, 32) * q_scale).astype(BF16))
    q_all = jnp.concatenate(qs, axis=0)
    o = _softmax_pv(_dot_nt(q_all, kbuf[...]), v_ref[...])
    for g in range(A_GROUP):
        o_ref[:, g * HEAD_DIM:(g + 1) * HEAD_DIM] = o[g * tq:(g + 1) * tq].astype(o_ref.dtype)


def _attn_a(proj, tabs, gq, gk, batch, seq, tq=256):
    nq = seq // tq
    cos, slo, shi = tabs

    def q_spec(g):
        return pl.BlockSpec((tq, HEAD_DIM), lambda b, h, qi: (b * nq + qi, CB_AQ + h * A_GROUP + g))

    tab_spec = pl.BlockSpec((seq, LANES), lambda b, h, qi: (0, 0))
    gain_spec = pl.BlockSpec((1, HEAD_DIM), lambda b, h, qi: (0, 0))
    return pl.pallas_call(
        functools.partial(_attn_a_kernel, tq=tq),
        grid=(batch, A_KV_HEADS, nq),
        in_specs=[
            q_spec(0), q_spec(1), q_spec(2),
            pl.BlockSpec((seq, HEAD_DIM), lambda b, h, qi: (b, CB_AK + h)),
            pl.BlockSpec((seq, HEAD_DIM), lambda b, h, qi: (b, CB_AV + h)),
            tab_spec, tab_spec, tab_spec, gain_spec, gain_spec,
        ],
        out_specs=pl.BlockSpec((tq, A_GROUP * HEAD_DIM), lambda b, h, qi: (b * nq + qi, h)),
        out_shape=jax.ShapeDtypeStruct((batch * seq, A_WIDTH), BF16),
        scratch_shapes=[pltpu.VMEM((seq, HEAD_DIM), BF16)],
        compiler_params=_params(("parallel", "parallel", "arbitrary"), 48),
        name="attn_a",
    )(proj, proj, proj, proj, proj, cos, slo, shi, gq, gk)


def _attn_c_kernel(q_ref, kn_ref, v_ref, kr_ref, cos_ref, slo_ref, shi_ref, o_ref, kbuf, *, tq):
    qi = pl.program_id(2)

    @pl.when(qi == 0)
    def _():
        kbuf[:, :C_NOPE_DIM] = kn_ref[...]
        kr = _rope(kr_ref[...].astype(F32), cos_ref[...], slo_ref[...], shi_ref[...], 32)
        kbuf[:, C_NOPE_DIM:] = kr.astype(BF16)

    rows = pl.ds(pl.multiple_of(qi * tq, tq), tq)
    q_rope = _rope(q_ref[:, C_NOPE_DIM:].astype(F32), cos_ref[rows, :], slo_ref[rows, :], shi_ref[rows, :], 32)
    q_all = jnp.concatenate([q_ref[:, :C_NOPE_DIM], q_rope.astype(BF16)], axis=1)
    o_ref[...] = _softmax_pv(_dot_nt(q_all, kbuf[...]), v_ref[...]).astype(o_ref.dtype)


def _attn_c(qc, kvc, proj, tabs, batch, seq, tq=512):
    nq = seq // tq
    cos, slo, shi = tabs
    tab_spec = pl.BlockSpec((seq, LANES), lambda b, h, qi: (0, 0))
    return pl.pallas_call(
        functools.partial(_attn_c_kernel, tq=tq),
        grid=(batch, C_HEADS, nq),
        in_specs=[
            pl.BlockSpec((tq, C_QK_PAD), lambda b, h, qi: (b * nq + qi, h)),
            pl.BlockSpec((seq, C_NOPE_DIM), lambda b, h, qi: (b, h)),
            pl.BlockSpec((seq, C_V_DIM), lambda b, h, qi: (b, C_HEADS + h)),
            pl.BlockSpec((seq, LANES), lambda b, h, qi: (b, CB_CKR)),
            tab_spec, tab_spec, tab_spec,
        ],
        out_specs=pl.BlockSpec((tq, C_V_DIM), lambda b, h, qi: (b * nq + qi, h)),
        out_shape=jax.ShapeDtypeStruct((batch * seq, C_WIDTH), BF16),
        scratch_shapes=[pltpu.VMEM((seq, C_QK_PAD), BF16)],
        compiler_params=_params(("parallel", "parallel", "arbitrary"), 48),
        name="attn_c",
    )(qc, kvc, kvc, proj, cos, slo, shi)


B_QBLK = 128
B_PREP_ROWS = 256


def _attn_b_kernel(q0_ref, q1_ref, q2_ref, k_ref, v_ref, cos_ref, slo_ref, shi_ref, o_ref,
                   q0f, q1f, q2f, kf, vf, m_s, l_s, acc_s, *, seq):
    q_refs = (q0_ref, q1_ref, q2_ref)
    q_bufs = (q0f, q1f, q2f)
    q_scale = HEAD_DIM ** -0.5 * LOG2E

    def prep(ci, carry):
        rows = pl.ds(pl.multiple_of(ci * B_PREP_ROWS, B_PREP_ROWS), B_PREP_ROWS)
        cos, slo, shi = cos_ref[rows, :], slo_ref[rows, :], shi_ref[rows, :]
        for q_ref, q_buf in zip(q_refs, q_bufs):
            q_buf[rows, :] = _rope(q_ref[rows, :].astype(F32), cos, slo, shi, 16) * q_scale
        kf[rows, :] = _rope(k_ref[rows, :].astype(F32), cos, slo, shi, 16)
        vf[rows, :] = v_ref[rows, :].astype(F32)
        return carry

    lax.fori_loop(0, seq // B_PREP_ROWS, prep, 0)

    for g, (window, dil) in enumerate(B_PATTERNS):
        half = window // (2 * dil)
        length = seq // dil
        nblk = length // B_QBLK
        kwin = min(2 * B_QBLK, length)
        q_buf = q_bufs[g]

        def rows_of(start, size, dil=dil):
            return pl.ds(start, size) if dil == 1 else pl.ds(start, size, stride=dil)

        def block(n, carry, g=g, half=half, length=length, nblk=nblk, kwin=kwin, q_buf=q_buf, dil=dil,
                  rows_of=rows_of):
            r = n // nblk
            i = n % nblk
            k0 = jnp.clip(i * B_QBLK - half, 0, length - kwin)
            q_rows = rows_of(r + dil * B_QBLK * i, B_QBLK)
            k_rows = rows_of(r + dil * k0, kwin)
            q = q_buf[q_rows, :].astype(BF16)
            k = kf[k_rows, :].astype(BF16)
            v = vf[k_rows, :].astype(BF16)
            s = _dot_nt(q, k)
            qpos = i * B_QBLK + lax.broadcasted_iota(jnp.int32, (B_QBLK, kwin), 0)
            kpos = k0 + lax.broadcasted_iota(jnp.int32, (B_QBLK, kwin), 1)
            s = jnp.where(jnp.abs(kpos - qpos) <= half, s, -jnp.inf)
            m_b = jnp.max(s, axis=-1, keepdims=True)
            p = jnp.exp2(s - m_b)
            l_b = jnp.sum(p, axis=-1, keepdims=True)
            a_b = _dot(p.astype(BF16), v)
            full = (B_QBLK, HEAD_DIM)
            if g == 0:
                m_s[q_rows, :] = jnp.broadcast_to(m_b, full)
                l_s[q_rows, :] = jnp.broadcast_to(l_b, full)
                acc_s[q_rows, :] = a_b
            else:
                m_o = m_s[q_rows, :]
                m_n = jnp.maximum(m_o, m_b)
                w_o = jnp.exp2(m_o - m_n)
                w_b = jnp.exp2(m_b - m_n)
                acc_s[q_rows, :] = acc_s[q_rows, :] * w_o + a_b * w_b
                l_s[q_rows, :] = l_s[q_rows, :] * w_o + l_b * w_b
                m_s[q_rows, :] = m_n
            return carry

        lax.fori_loop(0, seq // B_QBLK, block, 0)

    def finish(ci, carry):
        rows = pl.ds(pl.multiple_of(ci * B_PREP_ROWS, B_PREP_ROWS), B_PREP_ROWS)
        o_ref[rows, :] = (acc_s[rows, :] * (1.0 / l_s[rows, :])).astype(o_ref.dtype)
        return carry

    lax.fori_loop(0, seq // B_PREP_ROWS, finish, 0)


def _attn_b(proj, tabs, batch, seq):
    cos, slo, shi = tabs

    def q_spec(g):
        return pl.BlockSpec((seq, HEAD_DIM), lambda b, h: (b, CB_BQ + g * B_HEADS + h))

    tab_spec = pl.BlockSpec((seq, LANES), lambda b, h: (0, 0))
    slab = pltpu.VMEM((seq, HEAD_DIM), F32)
    return pl.pallas_call(
        functools.partial(_attn_b_kernel, seq=seq),
        grid=(batch, B_HEADS),
        in_specs=[
            q_spec(0), q_spec(1), q_spec(2),
            pl.BlockSpec((seq, HEAD_DIM), lambda b, h: (b, CB_BK + h)),
            pl.BlockSpec((seq, HEAD_DIM), lambda b, h: (b, CB_BV + h)),
            tab_spec, tab_spec, tab_spec,
        ],
        out_specs=pl.BlockSpec((seq, HEAD_DIM), lambda b, h: (b, h)),
        out_shape=jax.ShapeDtypeStruct((batch * seq, B_WIDTH), BF16),
        scratch_shapes=[slab] * 8,
        compiler_params=_params(("parallel", "parallel"), 48),
        name="attn_b",
    )(proj, proj, proj, proj, proj, cos, slo, shi)


def _out_kernel(ya_ref, yb_ref, yc_ref, ga_ref, gb_ref, gc_ref, w_ref, x_ref, gf_ref, x1_ref, xn_ref):
    y = jnp.concatenate([
        _rms(ya_ref[...].astype(F32), ga_ref[...]).astype(BF16),
        _rms(yb_ref[...].astype(F32), gb_ref[...]).astype(BF16),
        _rms(yc_ref[...].astype(F32), gc_ref[...]).astype(BF16),
    ], axis=1)
    x1 = x_ref[...] + _dot(y, w_ref[...])
    x1_ref[...] = x1
    xn_ref[...] = _rms(x1, gf_ref[...]).astype(xn_ref.dtype)


def _out_proj(ya, yb, yc, ga, gb, gc, w, x, gf, tm=512):
    m, d = x.shape

    def rows(width):
        return pl.BlockSpec((tm, width), lambda i: (i, 0))

    def const(r, c):
        return pl.BlockSpec((r, c), lambda i: (0, 0))

    return pl.pallas_call(
        _out_kernel,
        grid=(m // tm,),
        in_specs=[rows(A_WIDTH), rows(B_WIDTH), rows(C_WIDTH), const(1, A_WIDTH), const(1, B_WIDTH),
                  const(1, C_WIDTH), const(d, d), rows(d), const(1, d)],
        out_specs=[rows(d), rows(d)],
        out_shape=[jax.ShapeDtypeStruct((m, d), F32), jax.ShapeDtypeStruct((m, d), BF16)],
        compiler_params=_params(("parallel",), 56),
        name="out_proj",
    )(ya, yb, yc, ga, gb, gc, w, x, gf)


def _ffn_up_kernel(xn_ref, wg_ref, wu_ref, cg_ref, cu_ref, bg_ref, bu_ref, o_ref, *, seq):
    xn = xn_ref[...]
    row = lax.broadcasted_iota(jnp.int32, (seq, 1), 0)

    def conv(h, c_ref, b_ref):
        prev = jnp.where(row == 0, 0.0, pltpu.roll(h, 1, 0))
        nxt = jnp.where(row == seq - 1, 0.0, pltpu.roll(h, seq - 1, 0))
        return prev * c_ref[0:1, :] + h * c_ref[1:2, :] + nxt * c_ref[2:3, :] + b_ref[...]

    gate = conv(_dot(xn, wg_ref[...]), cg_ref, bg_ref)
    up = conv(_dot(xn, wu_ref[...]), cu_ref, bu_ref)
    o_ref[...] = (gate * (1.0 / (1.0 + jnp.exp(-gate))) * up).astype(o_ref.dtype)


def _ffn_up(xn, w_up, conv_w, conv_b, batch, seq, tn=256):
    d = xn.shape[1]
    nj = D_FF // tn
    return pl.pallas_call(
        functools.partial(_ffn_up_kernel, seq=seq),
        grid=(batch, nj),
        in_specs=[
            pl.BlockSpec((seq, d), lambda b, j: (b, 0)),
            pl.BlockSpec((d, tn), lambda b, j: (0, j)),
            pl.BlockSpec((d, tn), lambda b, j: (0, nj + j)),
            pl.BlockSpec((3, tn), lambda b, j: (0, j)),
            pl.BlockSpec((3, tn), lambda b, j: (0, nj + j)),
            pl.BlockSpec((1, tn), lambda b, j: (0, j)),
            pl.BlockSpec((1, tn), lambda b, j: (0, nj + j)),
        ],
        out_specs=pl.BlockSpec((seq, tn), lambda b, j: (b, j)),
        out_shape=jax.ShapeDtypeStruct((batch * seq, D_FF), BF16),
        compiler_params=_params(("parallel", "arbitrary"), 56),
        name="ffn_up",
    )(xn, w_up, w_up, conv_w, conv_w, conv_b, conv_b)


def _ffn_down_kernel(a_ref, w_ref, x1_ref, g_ref, x2_ref, xn_ref):
    k = pl.program_id(1)

    @pl.when(k == 0)
    def _():
        x2_ref[...] = x1_ref[...]

    x2_ref[...] += _dot(a_ref[...], w_ref[...])

    @pl.when(k == pl.num_programs(1) - 1)
    def _():
        xn_ref[...] = _rms(x2_ref[...], g_ref[...]).astype(xn_ref.dtype)


def _ffn_down(act, w, x1, g, norm_dtype, tm=512, tk=512):
    m, kdim = act.shape
    d = w.shape[1]
    return pl.pallas_call(
        _ffn_down_kernel,
        grid=(m // tm, kdim // tk),
        in_specs=[
            pl.BlockSpec((tm, tk), lambda i, k: (i, k)),
            pl.BlockSpec((tk, d), lambda i, k: (k, 0)),
            pl.BlockSpec((tm, d), lambda i, k: (i, 0)),
            pl.BlockSpec((1, d), lambda i, k: (0, 0)),
        ],
        out_specs=[pl.BlockSpec((tm, d), lambda i, k: (i, 0)), pl.BlockSpec((tm, d), lambda i, k: (i, 0))],
        out_shape=[jax.ShapeDtypeStruct((m, d), F32), jax.ShapeDtypeStruct((m, d), norm_dtype)],
        compiler_params=_params(("parallel", "arbitrary"), 56),
        name="ffn_down",
    )(act, w, x1, g)


def _rope_table(pos, dim, theta):
    inv = theta ** (-jnp.arange(0, dim, 2, dtype=F32) / dim)
    ang = pos.astype(F32)[:, None] * inv[None, :]
    return jnp.cos(ang), jnp.sin(ang)


def _lane_tables(seq):
    t = jnp.arange(seq)
    z = lambda w: jnp.zeros((seq, w), F32)
    one = lambda w: jnp.ones((seq, w), F32)
    cr, sr = _rope_table(t // GRID_W, HEAD_DIM // 2, A_ROPE_THETA)
    cc, sc = _rope_table(t % GRID_W, HEAD_DIM // 2, A_ROPE_THETA)
    tab_a = (jnp.concatenate([cr, cr, cc, cc], 1),
             jnp.concatenate([-sr, z(32), -sc, z(32)], 1),
             jnp.concatenate([z(32), sr, z(32), sc], 1))
    cp, sp = _rope_table(t, PARTIAL_ROPE_DIM, PARTIAL_ROPE_THETA)
    tab_b = (jnp.concatenate([cp, cp, one(96)], 1),
             jnp.concatenate([-sp, z(112)], 1),
             jnp.concatenate([z(16), sp, z(96)], 1))
    cm, sm = _rope_table(t, C_ROPE_DIM, C_ROPE_THETA)
    tab_c = (jnp.concatenate([cm, cm, one(64)], 1),
             jnp.concatenate([-sm, z(96)], 1),
             jnp.concatenate([z(32), sm, z(64)], 1))
    return tab_a, tab_b, tab_c


def _prep_w_in(w):
    aqw = A_HEADS * HEAD_DIM
    akw = A_KV_HEADS * HEAD_DIM
    bqw = B_N_GROUPS * B_HEADS * HEAD_DIM
    bkw = B_HEADS * HEAD_DIM
    o_c = aqw + 2 * akw + bqw + 2 * bkw
    front = w[:, :o_c]
    cq = w[:, o_c:o_c + C_Q_RANK]
    ckv = w[:, o_c + C_Q_RANK:o_c + C_Q_RANK + C_KV_RANK]
    ckr = w[:, o_c + C_Q_RANK + C_KV_RANK:]
    pad = jnp.zeros((w.shape[0], PROJ_WIDTH - w.shape[1]), w.dtype)
    return jnp.concatenate([cq, ckv, front, ckr, pad], axis=1).astype(BF16)


def _prep_w_uq(w):
    w = w.reshape(C_Q_RANK, C_HEADS, C_NOPE_DIM + C_ROPE_DIM)
    w = jnp.pad(w, ((0, 0), (0, 0), (0, C_QK_PAD - C_NOPE_DIM - C_ROPE_DIM)))
    return w.reshape(C_Q_RANK, C_HEADS * C_QK_PAD).astype(BF16)


def _prep_w_ukv(w):
    w = w.reshape(C_KV_RANK, C_HEADS, C_NOPE_DIM + C_V_DIM)
    kn = w[:, :, :C_NOPE_DIM].reshape(C_KV_RANK, C_HEADS * C_NOPE_DIM)
    v = w[:, :, C_NOPE_DIM:].reshape(C_KV_RANK, C_HEADS * C_V_DIM)
    return jnp.concatenate([kn, v], axis=1).astype(BF16)


def kernel(x, attn_norm, w_in, a_q_norm, a_k_norm, c_q_norm, c_kv_norm, w_uq, w_ukv, out_norm, w_out,
           ffn_norm, w_up, conv_w, conv_b, w_down, final_norm):
    batch, seq, d = x.shape
    depth = w_in.shape[0]
    tab_a, tab_b, tab_c = _lane_tables(seq)
    row = lambda v: v.reshape(1, -1)

    xr = x.reshape(batch * seq, d)
    xn = _rmsnorm_rows(xr, row(attn_norm[0]))
    for l in range(depth):
        proj = _matmul(xn, _prep_w_in(w_in[l]))
        qc, kvc = _latent_up(proj, row(c_q_norm[l]), row(c_kv_norm[l]), _prep_w_uq(w_uq[l]),
                             _prep_w_ukv(w_ukv[l]))
        ya = _attn_a(proj, tab_a, row(a_q_norm[l]), row(a_k_norm[l]), batch, seq)
        yb = _attn_b(proj, tab_b, batch, seq)
        yc = _attn_c(qc, kvc, proj, tab_c, batch, seq)
        g = out_norm[l]
        x1, xn1 = _out_proj(ya, yb, yc, row(g[:A_WIDTH]), row(g[A_WIDTH:A_WIDTH + B_WIDTH]),
                            row(g[A_WIDTH + B_WIDTH:]), w_out[l].astype(BF16), xr, row(ffn_norm[l]))
        act = _ffn_up(xn1, w_up[l].astype(BF16), conv_w[l], row(conv_b[l]), batch, seq)
        last = l == depth - 1
        g_next = final_norm if last else attn_norm[l + 1]
        xr, xn = _ffn_down(act, w_down[l].astype(BF16), x1, row(g_next), F32 if last else BF16)
    return xn.reshape(batch, seq, d)
```

```python
import functools
import math

import jax
import jax.numpy as jnp
from jax import lax
from jax.experimental import pallas as pl
from jax.experimental.pallas import tpu as pltpu

D_MODEL = 2048
HEAD_DIM = 128
A_HEADS = 6
A_KV_HEADS = 2
A_GROUP = A_HEADS // A_KV_HEADS
A_ROPE_THETA = 10000.0
B_HEADS = 4
B_PATTERNS = ((128, 1), (512, 4), (2048, 16))
B_N_GROUPS = 3
C_HEADS = 6
C_Q_RANK = 512
C_KV_RANK = 512
C_NOPE_DIM = 128
C_ROPE_DIM = 64
C_V_DIM = 128
C_ROPE_THETA = 10000.0
PARTIAL_ROPE_DIM = HEAD_DIM // 4
PARTIAL_ROPE_THETA = 500000.0
GRID_W = 64
D_FF = 5632
EPS = 1e-6

A_WIDTH = A_HEADS * HEAD_DIM
B_WIDTH = B_HEADS * HEAD_DIM
C_WIDTH = C_HEADS * C_V_DIM
C_QK_PAD = 256

LANES = 128
LOG2E = math.log2(math.e)
F32 = jnp.float32
BF16 = jnp.bfloat16

PROJ_WIDTH = 5120
CB_CQ, CB_CKV = 0, 1
CB_AQ, CB_AK, CB_AV = 8, 14, 16
CB_BQ, CB_BK, CB_BV = 18, 30, 34
CB_CKR = 38


def _params(semantics, vmem_mib):
    return pltpu.CompilerParams(dimension_semantics=semantics, vmem_limit_bytes=vmem_mib * 1024 * 1024)


def _rms(xf, g):
    return xf * lax.rsqrt(jnp.mean(xf * xf, axis=-1, keepdims=True) + EPS) * g


def _rope(xf, cos, sin_lo, sin_hi, shift):
    return xf * cos + pltpu.roll(xf, LANES - shift, 1) * sin_lo + pltpu.roll(xf, shift, 1) * sin_hi


def _dot(a, b):
    return jnp.dot(a, b, preferred_element_type=F32)


def _dot_nt(a, b):
    return lax.dot_general(a, b, (((1,), (1,)), ((), ())), preferred_element_type=F32)


def _norm_kernel(x_ref, g_ref, o_ref):
    o_ref[...] = _rms(x_ref[...], g_ref[...]).astype(o_ref.dtype)


def _rmsnorm_rows(x, g, tm=512):
    m, d = x.shape
    return pl.pallas_call(
        _norm_kernel,
        grid=(m // tm,),
        in_specs=[pl.BlockSpec((tm, d), lambda i: (i, 0)), pl.BlockSpec((1, d), lambda i: (0, 0))],
        out_specs=pl.BlockSpec((tm, d), lambda i: (i, 0)),
        out_shape=jax.ShapeDtypeStruct((m, d), BF16),
        compiler_params=_params(("parallel",), 40),
        name="rmsnorm",
    )(x, g)


def _mm_kernel(a_ref, w_ref, o_ref):
    o_ref[...] = _dot(a_ref[...], w_ref[...]).astype(o_ref.dtype)


def _matmul(a, w, tm=1024, tn=1024):
    m, k = a.shape
    n = w.shape[1]
    return pl.pallas_call(
        _mm_kernel,
        grid=(m // tm, n // tn),
        in_specs=[pl.BlockSpec((tm, k), lambda i, j: (i, 0)), pl.BlockSpec((k, tn), lambda i, j: (0, j))],
        out_specs=pl.BlockSpec((tm, tn), lambda i, j: (i, j)),
        out_shape=jax.ShapeDtypeStruct((m, n), BF16),
        compiler_params=_params(("parallel", "arbitrary"), 48),
        name="in_proj",
    )(a, w)


def _latent_up_kernel(cq_ref, ckv_ref, gq_ref, gkv_ref, wq_ref, wkv_ref, q_ref, kv_ref, *, q_scale):
    cq = _rms(cq_ref[...].astype(F32), gq_ref[...]).astype(BF16)
    q_ref[...] = (_dot(cq, wq_ref[...]) * q_scale).astype(q_ref.dtype)
    ckv = _rms(ckv_ref[...].astype(F32), gkv_ref[...]).astype(BF16)
    kv_ref[...] = _dot(ckv, wkv_ref[...]).astype(kv_ref.dtype)


def _latent_up(proj, gq, gkv, wq, wkv, tm=1024):
    m = proj.shape[0]
    nq, nkv = wq.shape[1], wkv.shape[1]
    q_scale = (C_NOPE_DIM + C_ROPE_DIM) ** -0.5 * LOG2E
    return pl.pallas_call(
        functools.partial(_latent_up_kernel, q_scale=q_scale),
        grid=(m // tm,),
        in_specs=[
            pl.BlockSpec((tm, C_Q_RANK), lambda i: (i, CB_CQ)),
            pl.BlockSpec((tm, C_KV_RANK), lambda i: (i, CB_CKV)),
            pl.BlockSpec((1, C_Q_RANK), lambda i: (0, 0)),
            pl.BlockSpec((1, C_KV_RANK), lambda i: (0, 0)),
            pl.BlockSpec((C_Q_RANK, nq), lambda i: (0, 0)),
            pl.BlockSpec((C_KV_RANK, nkv), lambda i: (0, 0)),
        ],
        out_specs=[pl.BlockSpec((tm, nq), lambda i: (i, 0)), pl.BlockSpec((tm, nkv), lambda i: (i, 0))],
        out_shape=[jax.ShapeDtypeStruct((m, nq), BF16), jax.ShapeDtypeStruct((m, nkv), BF16)],
        compiler_params=_params(("parallel",), 48),
        name="latent_up",
    )(proj, proj, gq, gkv, wq, wkv)


ATTN_ROW_CHUNK = 128


def _attend(q_all, k_ref, v1_ref, dv):
    outs = []
    for c in range(q_all.shape[0] // ATTN_ROW_CHUNK):
        q = q_all[c * ATTN_ROW_CHUNK:(c + 1) * ATTN_ROW_CHUNK]
        s = _dot_nt(q, k_ref[...])
        p = jnp.exp2(s - jnp.max(s, axis=-1, keepdims=True))
        ol = _dot(p.astype(BF16), v1_ref[...])
        outs.append(ol[:, :dv] * (1.0 / ol[:, dv:]))
    return outs


def _attn_a_kernel(q0_ref, q1_ref, q2_ref, k_ref, v_ref, cos_ref, slo_ref, shi_ref, gq_ref, gk_ref,
                   o_ref, kbuf, v1buf, *, tq):
    qi = pl.program_id(2)

    @pl.when(qi == 0)
    def _():
        k = _rms(k_ref[...].astype(F32), gk_ref[...])
        kbuf[...] = _rope(k, cos_ref[...], slo_ref[...], shi_ref[...], 32).astype(BF16)
        v1buf[:, :HEAD_DIM] = v_ref[...]
        v1buf[:, HEAD_DIM:] = jnp.ones((v1buf.shape[0], HEAD_DIM), BF16)

    rows = pl.ds(pl.multiple_of(qi * tq, tq), tq)
    cos, slo, shi = cos_ref[rows, :], slo_ref[rows, :], shi_ref[rows, :]
    q_scale = HEAD_DIM ** -0.5 * LOG2E
    qs = []
    for q_ref in (q0_ref, q1_ref, q2_ref):
        q = _rms(q_ref[...].astype(F32), gq_ref[...])
        qs.append((_rope(q, cos, slo, shi, 32) * q_scale).astype(BF16))
    q_all = jnp.concatenate(qs, axis=0)
    per_head = tq // ATTN_ROW_CHUNK
    for c, o in enumerate(_attend(q_all, kbuf, v1buf, HEAD_DIM)):
        g, r = divmod(c, per_head)
        o_ref[r * ATTN_ROW_CHUNK:(r + 1) * ATTN_ROW_CHUNK, g * HEAD_DIM:(g + 1) * HEAD_DIM] = o.astype(o_ref.dtype)


def _attn_a(proj, tabs, gq, gk, batch, seq, tq=256):
    nq = seq // tq
    cos, slo, shi = tabs

    def q_spec(g):
        return pl.BlockSpec((tq, HEAD_DIM), lambda b, h, qi: (b * nq + qi, CB_AQ + h * A_GROUP + g))

    tab_spec = pl.BlockSpec((seq, LANES), lambda b, h, qi: (0, 0))
    gain_spec = pl.BlockSpec((1, HEAD_DIM), lambda b, h, qi: (0, 0))
    return pl.pallas_call(
        functools.partial(_attn_a_kernel, tq=tq),
        grid=(batch, A_KV_HEADS, nq),
        in_specs=[
            q_spec(0), q_spec(1), q_spec(2),
            pl.BlockSpec((seq, HEAD_DIM), lambda b, h, qi: (b, CB_AK + h)),
            pl.BlockSpec((seq, HEAD_DIM), lambda b, h, qi: (b, CB_AV + h)),
            tab_spec, tab_spec, tab_spec, gain_spec, gain_spec,
        ],
        out_specs=pl.BlockSpec((tq, A_GROUP * HEAD_DIM), lambda b, h, qi: (b * nq + qi, h)),
        out_shape=jax.ShapeDtypeStruct((batch * seq, A_WIDTH), BF16),
        scratch_shapes=[pltpu.VMEM((seq, HEAD_DIM), BF16), pltpu.VMEM((seq, 2 * HEAD_DIM), BF16)],
        compiler_params=_params(("parallel", "parallel", "arbitrary"), 48),
        name="attn_a",
    )(proj, proj, proj, proj, proj, cos, slo, shi, gq, gk)


def _attn_c_kernel(q_ref, kn_ref, v_ref, kr_ref, cos_ref, slo_ref, shi_ref, o_ref, kbuf, v1buf, *, tq):
    qi = pl.program_id(2)

    @pl.when(qi == 0)
    def _():
        kbuf[:, :C_NOPE_DIM] = kn_ref[...]
        kr = _rope(kr_ref[...].astype(F32), cos_ref[...], slo_ref[...], shi_ref[...], 32)
        kbuf[:, C_NOPE_DIM:] = kr.astype(BF16)
        v1buf[:, :C_V_DIM] = v_ref[...]
        v1buf[:, C_V_DIM:] = jnp.ones((v1buf.shape[0], C_V_DIM), BF16)

    rows = pl.ds(pl.multiple_of(qi * tq, tq), tq)
    q_rope = _rope(q_ref[:, C_NOPE_DIM:].astype(F32), cos_ref[rows, :], slo_ref[rows, :], shi_ref[rows, :], 32)
    q_all = jnp.concatenate([q_ref[:, :C_NOPE_DIM], q_rope.astype(BF16)], axis=1)
    for c, o in enumerate(_attend(q_all, kbuf, v1buf, C_V_DIM)):
        o_ref[c * ATTN_ROW_CHUNK:(c + 1) * ATTN_ROW_CHUNK, :] = o.astype(o_ref.dtype)


def _attn_c(qc, kvc, proj, tabs, batch, seq, tq=512):
    nq = seq // tq
    cos, slo, shi = tabs
    tab_spec = pl.BlockSpec((seq, LANES), lambda b, h, qi: (0, 0))
    return pl.pallas_call(
        functools.partial(_attn_c_kernel, tq=tq),
        grid=(batch, C_HEADS, nq),
        in_specs=[
            pl.BlockSpec((tq, C_QK_PAD), lambda b, h, qi: (b * nq + qi, h)),
            pl.BlockSpec((seq, C_NOPE_DIM), lambda b, h, qi: (b, h)),
            pl.BlockSpec((seq, C_V_DIM), lambda b, h, qi: (b, C_HEADS + h)),
            pl.BlockSpec((seq, LANES), lambda b, h, qi: (b, CB_CKR)),
            tab_spec, tab_spec, tab_spec,
        ],
        out_specs=pl.BlockSpec((tq, C_V_DIM), lambda b, h, qi: (b * nq + qi, h)),
        out_shape=jax.ShapeDtypeStruct((batch * seq, C_WIDTH), BF16),
        scratch_shapes=[pltpu.VMEM((seq, C_QK_PAD), BF16), pltpu.VMEM((seq, 2 * C_V_DIM), BF16)],
        compiler_params=_params(("parallel", "parallel", "arbitrary"), 48),
        name="attn_c",
    )(qc, kvc, kvc, proj, cos, slo, shi)


B_QBLK = 128
B_PREP_ROWS = 256
B_UNROLL = 4


def _attn_b_kernel(q0_ref, q1_ref, q2_ref, k_ref, v_ref, cos_ref, slo_ref, shi_ref, o_ref,
                   q0f, q1f, q2f, kf, vf, m_s, l_s, acc_s, *, seq):
    q_refs = (q0_ref, q1_ref, q2_ref)
    q_bufs = (q0f, q1f, q2f)
    q_scale = HEAD_DIM ** -0.5 * LOG2E

    def prep(ci, carry):
        rows = pl.ds(pl.multiple_of(ci * B_PREP_ROWS, B_PREP_ROWS), B_PREP_ROWS)
        cos, slo, shi = cos_ref[rows, :], slo_ref[rows, :], shi_ref[rows, :]
        for q_ref, q_buf in zip(q_refs, q_bufs):
            q_buf[rows, :] = _rope(q_ref[rows, :].astype(F32), cos, slo, shi, 16) * q_scale
        kf[rows, :] = _rope(k_ref[rows, :].astype(F32), cos, slo, shi, 16)
        vf[rows, :] = v_ref[rows, :].astype(F32)
        return carry

    lax.fori_loop(0, seq // B_PREP_ROWS, prep, 0)

    for g, (window, dil) in enumerate(B_PATTERNS):
        half = window // (2 * dil)
        length = seq // dil
        nblk = length // B_QBLK
        kwin = min(2 * B_QBLK, length)
        q_buf = q_bufs[g]

        def rows_of(start, size, dil=dil):
            return pl.ds(start, size) if dil == 1 else pl.ds(start, size, stride=dil)

        def block(n, carry, g=g, half=half, length=length, nblk=nblk, kwin=kwin, q_buf=q_buf, dil=dil,
                  rows_of=rows_of):
            r = n // nblk
            i = n % nblk
            k0 = jnp.clip(i * B_QBLK - half, 0, length - kwin)
            q_rows = rows_of(r + dil * B_QBLK * i, B_QBLK)
            k_rows = rows_of(r + dil * k0, kwin)
            q = q_buf[q_rows, :].astype(BF16)
            k = kf[k_rows, :].astype(BF16)
            v = vf[k_rows, :].astype(BF16)
            s = _dot_nt(q, k)
            qpos = i * B_QBLK + lax.broadcasted_iota(jnp.int32, (B_QBLK, kwin), 0)
            kpos = k0 + lax.broadcasted_iota(jnp.int32, (B_QBLK, kwin), 1)
            s = jnp.where(jnp.abs(kpos - qpos) <= half, s, -jnp.inf)
            m_b = jnp.max(s, axis=-1, keepdims=True)
            p = jnp.exp2(s - m_b)
            l_b = jnp.sum(p, axis=-1, keepdims=True)
            a_b = _dot(p.astype(BF16), v)
            full = (B_QBLK, HEAD_DIM)
            if g == 0:
                m_s[q_rows, :] = jnp.broadcast_to(m_b, full)
                l_s[q_rows, :] = jnp.broadcast_to(l_b, full)
                acc_s[q_rows, :] = a_b
            else:
                m_o = m_s[q_rows, :]
                m_n = jnp.maximum(m_o, m_b)
                w_o = jnp.exp2(m_o - m_n)
                w_b = jnp.exp2(m_b - m_n)
                acc_s[q_rows, :] = acc_s[q_rows, :] * w_o + a_b * w_b
                l_s[q_rows, :] = l_s[q_rows, :] * w_o + l_b * w_b
                m_s[q_rows, :] = m_n
            return carry

        lax.fori_loop(0, seq // B_QBLK, block, 0, unroll=B_UNROLL)

    def finish(ci, carry):
        rows = pl.ds(pl.multiple_of(ci * B_PREP_ROWS, B_PREP_ROWS), B_PREP_ROWS)
        o_ref[rows, :] = (acc_s[rows, :] * (1.0 / l_s[rows, :])).astype(o_ref.dtype)
        return carry

    lax.fori_loop(0, seq // B_PREP_ROWS, finish, 0)


def _attn_b(proj, tabs, batch, seq):
    cos, slo, shi = tabs

    def q_spec(g):
        return pl.BlockSpec((seq, HEAD_DIM), lambda b, h: (b, CB_BQ + g * B_HEADS + h))

    tab_spec = pl.BlockSpec((seq, LANES), lambda b, h: (0, 0))
    slab = pltpu.VMEM((seq, HEAD_DIM), F32)
    return pl.pallas_call(
        functools.partial(_attn_b_kernel, seq=seq),
        grid=(batch, B_HEADS),
        in_specs=[
            q_spec(0), q_spec(1), q_spec(2),
            pl.BlockSpec((seq, HEAD_DIM), lambda b, h: (b, CB_BK + h)),
            pl.BlockSpec((seq, HEAD_DIM), lambda b, h: (b, CB_BV + h)),
            tab_spec, tab_spec, tab_spec,
        ],
        out_specs=pl.BlockSpec((seq, HEAD_DIM), lambda b, h: (b, h)),
        out_shape=jax.ShapeDtypeStruct((batch * seq, B_WIDTH), BF16),
        scratch_shapes=[slab] * 8,
        compiler_params=_params(("parallel", "parallel"), 48),
        name="attn_b",
    )(proj, proj, proj, proj, proj, cos, slo, shi)


def _out_kernel(ya_ref, yb_ref, yc_ref, ga_ref, gb_ref, gc_ref, w_ref, x_ref, gf_ref, x1_ref, xn_ref):
    y = jnp.concatenate([
        _rms(ya_ref[...].astype(F32), ga_ref[...]).astype(BF16),
        _rms(yb_ref[...].astype(F32), gb_ref[...]).astype(BF16),
        _rms(yc_ref[...].astype(F32), gc_ref[...]).astype(BF16),
    ], axis=1)
    x1 = x_ref[...] + _dot(y, w_ref[...])
    x1_ref[...] = x1
    xn_ref[...] = _rms(x1, gf_ref[...]).astype(xn_ref.dtype)


def _out_proj(ya, yb, yc, ga, gb, gc, w, x, gf, tm=512):
    m, d = x.shape

    def rows(width):
        return pl.BlockSpec((tm, width), lambda i: (i, 0))

    def const(r, c):
        return pl.BlockSpec((r, c), lambda i: (0, 0))

    return pl.pallas_call(
        _out_kernel,
        grid=(m // tm,),
        in_specs=[rows(A_WIDTH), rows(B_WIDTH), rows(C_WIDTH), const(1, A_WIDTH), const(1, B_WIDTH),
                  const(1, C_WIDTH), const(d, d), rows(d), const(1, d)],
        out_specs=[rows(d), rows(d)],
        out_shape=[jax.ShapeDtypeStruct((m, d), F32), jax.ShapeDtypeStruct((m, d), BF16)],
        compiler_params=_params(("parallel",), 56),
        name="out_proj",
    )(ya, yb, yc, ga, gb, gc, w, x, gf)


def _ffn_up_kernel(xn_ref, wg_ref, wu_ref, cg_ref, cu_ref, bg_ref, bu_ref, o_ref, *, seq):
    xn = xn_ref[...]
    row = lax.broadcasted_iota(jnp.int32, (seq, 1), 0)

    def conv(h, c_ref, b_ref):
        prev = jnp.where(row == 0, 0.0, pltpu.roll(h, 1, 0))
        nxt = jnp.where(row == seq - 1, 0.0, pltpu.roll(h, seq - 1, 0))
        return prev * c_ref[0:1, :] + h * c_ref[1:2, :] + nxt * c_ref[2:3, :] + b_ref[...]

    gate = conv(_dot(xn, wg_ref[...]), cg_ref, bg_ref)
    up = conv(_dot(xn, wu_ref[...]), cu_ref, bu_ref)
    o_ref[...] = (gate * (1.0 / (1.0 + jnp.exp(-gate))) * up).astype(o_ref.dtype)


def _ffn_up(xn, w_up, conv_w, conv_b, batch, seq, tn=256):
    d = xn.shape[1]
    nj = D_FF // tn
    return pl.pallas_call(
        functools.partial(_ffn_up_kernel, seq=seq),
        grid=(batch, nj),
        in_specs=[
            pl.BlockSpec((seq, d), lambda b, j: (b, 0)),
            pl.BlockSpec((d, tn), lambda b, j: (0, j)),
            pl.BlockSpec((d, tn), lambda b, j: (0, nj + j)),
            pl.BlockSpec((3, tn), lambda b, j: (0, j)),
            pl.BlockSpec((3, tn), lambda b, j: (0, nj + j)),
            pl.BlockSpec((1, tn), lambda b, j: (0, j)),
            pl.BlockSpec((1, tn), lambda b, j: (0, nj + j)),
        ],
        out_specs=pl.BlockSpec((seq, tn), lambda b, j: (b, j)),
        out_shape=jax.ShapeDtypeStruct((batch * seq, D_FF), BF16),
        compiler_params=_params(("parallel", "arbitrary"), 56),
        name="ffn_up",
    )(xn, w_up, w_up, conv_w, conv_w, conv_b, conv_b)


def _ffn_down_kernel(a_ref, w_ref, x1_ref, g_ref, x2_ref, xn_ref):
    k = pl.program_id(1)

    @pl.when(k == 0)
    def _():
        x2_ref[...] = x1_ref[...]

    x2_ref[...] += _dot(a_ref[...], w_ref[...])

    @pl.when(k == pl.num_programs(1) - 1)
    def _():
        xn_ref[...] = _rms(x2_ref[...], g_ref[...]).astype(xn_ref.dtype)


def _ffn_down(act, w, x1, g, norm_dtype, tm=512, tk=512):
    m, kdim = act.shape
    d = w.shape[1]
    return pl.pallas_call(
        _ffn_down_kernel,
        grid=(m // tm, kdim // tk),
        in_specs=[
            pl.BlockSpec((tm, tk), lambda i, k: (i, k)),
            pl.BlockSpec((tk, d), lambda i, k: (k, 0)),
            pl.BlockSpec((tm, d), lambda i, k: (i, 0)),
            pl.BlockSpec((1, d), lambda i, k: (0, 0)),
        ],
        out_specs=[pl.BlockSpec((tm, d), lambda i, k: (i, 0)), pl.BlockSpec((tm, d), lambda i, k: (i, 0))],
        out_shape=[jax.ShapeDtypeStruct((m, d), F32), jax.ShapeDtypeStruct((m, d), norm_dtype)],
        compiler_params=_params(("parallel", "arbitrary"), 56),
        name="ffn_down",
    )(act, w, x1, g)


def _rope_table(pos, dim, theta):
    inv = theta ** (-jnp.arange(0, dim, 2, dtype=F32) / dim)
    ang = pos.astype(F32)[:, None] * inv[None, :]
    return jnp.cos(ang), jnp.sin(ang)


def _lane_tables(seq):
    t = jnp.arange(seq)
    z = lambda w: jnp.zeros((seq, w), F32)
    one = lambda w: jnp.ones((seq, w), F32)
    cr, sr = _rope_table(t // GRID_W, HEAD_DIM // 2, A_ROPE_THETA)
    cc, sc = _rope_table(t % GRID_W, HEAD_DIM // 2, A_ROPE_THETA)
    tab_a = (jnp.concatenate([cr, cr, cc, cc], 1),
             jnp.concatenate([-sr, z(32), -sc, z(32)], 1),
             jnp.concatenate([z(32), sr, z(32), sc], 1))
    cp, sp = _rope_table(t, PARTIAL_ROPE_DIM, PARTIAL_ROPE_THETA)
    tab_b = (jnp.concatenate([cp, cp, one(96)], 1),
             jnp.concatenate([-sp, z(112)], 1),
             jnp.concatenate([z(16), sp, z(96)], 1))
    cm, sm = _rope_table(t, C_ROPE_DIM, C_ROPE_THETA)
    tab_c = (jnp.concatenate([cm, cm, one(64)], 1),
             jnp.concatenate([-sm, z(96)], 1),
             jnp.concatenate([z(32), sm, z(64)], 1))
    return tab_a, tab_b, tab_c


def _prep_w_in(w):
    aqw = A_HEADS * HEAD_DIM
    akw = A_KV_HEADS * HEAD_DIM
    bqw = B_N_GROUPS * B_HEADS * HEAD_DIM
    bkw = B_HEADS * HEAD_DIM
    o_c = aqw + 2 * akw + bqw + 2 * bkw
    front = w[:, :o_c]
    cq = w[:, o_c:o_c + C_Q_RANK]
    ckv = w[:, o_c + C_Q_RANK:o_c + C_Q_RANK + C_KV_RANK]
    ckr = w[:, o_c + C_Q_RANK + C_KV_RANK:]
    pad = jnp.zeros((w.shape[0], PROJ_WIDTH - w.shape[1]), w.dtype)
    return jnp.concatenate([cq, ckv, front, ckr, pad], axis=1).astype(BF16)


def _prep_w_uq(w):
    w = w.reshape(C_Q_RANK, C_HEADS, C_NOPE_DIM + C_ROPE_DIM)
    w = jnp.pad(w, ((0, 0), (0, 0), (0, C_QK_PAD - C_NOPE_DIM - C_ROPE_DIM)))
    return w.reshape(C_Q_RANK, C_HEADS * C_QK_PAD).astype(BF16)


def _prep_w_ukv(w):
    w = w.reshape(C_KV_RANK, C_HEADS, C_NOPE_DIM + C_V_DIM)
    kn = w[:, :, :C_NOPE_DIM].reshape(C_KV_RANK, C_HEADS * C_NOPE_DIM)
    v = w[:, :, C_NOPE_DIM:].reshape(C_KV_RANK, C_HEADS * C_V_DIM)
    return jnp.concatenate([kn, v], axis=1).astype(BF16)


def kernel(x, attn_norm, w_in, a_q_norm, a_k_norm, c_q_norm, c_kv_norm, w_uq, w_ukv, out_norm, w_out,
           ffn_norm, w_up, conv_w, conv_b, w_down, final_norm):
    batch, seq, d = x.shape
    depth = w_in.shape[0]
    tab_a, tab_b, tab_c = _lane_tables(seq)
    row = lambda v: v.reshape(1, -1)

    xr = x.reshape(batch * seq, d)
    xn = _rmsnorm_rows(xr, row(attn_norm[0]))
    for l in range(depth):
        proj = _matmul(xn, _prep_w_in(w_in[l]))
        qc, kvc = _latent_up(proj, row(c_q_norm[l]), row(c_kv_norm[l]), _prep_w_uq(w_uq[l]),
                             _prep_w_ukv(w_ukv[l]))
        ya = _attn_a(proj, tab_a, row(a_q_norm[l]), row(a_k_norm[l]), batch, seq)
        yb = _attn_b(proj, tab_b, batch, seq)
        yc = _attn_c(qc, kvc, proj, tab_c, batch, seq)
        g = out_norm[l]
        x1, xn1 = _out_proj(ya, yb, yc, row(g[:A_WIDTH]), row(g[A_WIDTH:A_WIDTH + B_WIDTH]),
                            row(g[A_WIDTH + B_WIDTH:]), w_out[l].astype(BF16), xr, row(ffn_norm[l]))
        act = _ffn_up(xn1, w_up[l].astype(BF16), conv_w[l], row(conv_b[l]), batch, seq)
        last = l == depth - 1
        g_next = final_norm if last else attn_norm[l + 1]
        xr, xn = _ffn_down(act, w_down[l].astype(BF16), x1, row(g_next), F32 if last else BF16)
    return xn.reshape(batch, seq, d)
```

```python
import functools
import math

import jax
import jax.numpy as jnp
from jax import lax
from jax.experimental import pallas as pl
from jax.experimental.pallas import tpu as pltpu

D_MODEL = 2048
HEAD_DIM = 128
A_HEADS = 6
A_KV_HEADS = 2
A_GROUP = A_HEADS // A_KV_HEADS
A_ROPE_THETA = 10000.0
B_HEADS = 4
B_PATTERNS = ((128, 1), (512, 4), (2048, 16))
B_N_GROUPS = 3
C_HEADS = 6
C_Q_RANK = 512
C_KV_RANK = 512
C_NOPE_DIM = 128
C_ROPE_DIM = 64
C_V_DIM = 128
C_ROPE_THETA = 10000.0
PARTIAL_ROPE_DIM = HEAD_DIM // 4
PARTIAL_ROPE_THETA = 500000.0
GRID_W = 64
D_FF = 5632
EPS = 1e-6

A_WIDTH = A_HEADS * HEAD_DIM
B_WIDTH = B_HEADS * HEAD_DIM
C_WIDTH = C_HEADS * C_V_DIM
C_QK_PAD = 256

LANES = 128
LOG2E = math.log2(math.e)
F32 = jnp.float32
BF16 = jnp.bfloat16

PROJ_WIDTH = 5120
CB_CQ, CB_CKV = 0, 1
CB_AQ, CB_AK, CB_AV = 8, 14, 16
CB_BQ, CB_BK, CB_BV = 18, 30, 34
CB_CKR = 38


def _params(semantics, vmem_mib):
    return pltpu.CompilerParams(dimension_semantics=semantics, vmem_limit_bytes=vmem_mib * 1024 * 1024)


def _rms(xf, g):
    return xf * lax.rsqrt(jnp.mean(xf * xf, axis=-1, keepdims=True) + EPS) * g


def _rope(xf, cos, sin_lo, sin_hi, shift):
    return xf * cos + pltpu.roll(xf, LANES - shift, 1) * sin_lo + pltpu.roll(xf, shift, 1) * sin_hi


def _dot(a, b):
    return jnp.dot(a, b, preferred_element_type=F32)


def _dot_nt(a, b):
    return lax.dot_general(a, b, (((1,), (1,)), ((), ())), preferred_element_type=F32)


def _norm_kernel(x_ref, g_ref, o_ref):
    o_ref[...] = _rms(x_ref[...], g_ref[...]).astype(o_ref.dtype)


def _rmsnorm_rows(x, g, tm=512):
    m, d = x.shape
    return pl.pallas_call(
        _norm_kernel,
        grid=(m // tm,),
        in_specs=[pl.BlockSpec((tm, d), lambda i: (i, 0)), pl.BlockSpec((1, d), lambda i: (0, 0))],
        out_specs=pl.BlockSpec((tm, d), lambda i: (i, 0)),
        out_shape=jax.ShapeDtypeStruct((m, d), BF16),
        compiler_params=_params(("parallel",), 40),
        name="rmsnorm",
    )(x, g)


def _mm_kernel(a_ref, w_ref, o_ref):
    o_ref[...] = _dot(a_ref[...], w_ref[...]).astype(o_ref.dtype)


def _matmul(a, w, tm=1024, tn=1024):
    m, k = a.shape
    n = w.shape[1]
    return pl.pallas_call(
        _mm_kernel,
        grid=(m // tm, n // tn),
        in_specs=[pl.BlockSpec((tm, k), lambda i, j: (i, 0)), pl.BlockSpec((k, tn), lambda i, j: (0, j))],
        out_specs=pl.BlockSpec((tm, tn), lambda i, j: (i, j)),
        out_shape=jax.ShapeDtypeStruct((m, n), BF16),
        compiler_params=_params(("parallel", "arbitrary"), 48),
        name="in_proj",
    )(a, w)


def _latent_up_kernel(cq_ref, ckv_ref, gq_ref, gkv_ref, wq_ref, wkv_ref, q_ref, kv_ref, *, q_scale):
    cq = _rms(cq_ref[...].astype(F32), gq_ref[...]).astype(BF16)
    q_ref[...] = (_dot(cq, wq_ref[...]) * q_scale).astype(q_ref.dtype)
    ckv = _rms(ckv_ref[...].astype(F32), gkv_ref[...]).astype(BF16)
    kv_ref[...] = _dot(ckv, wkv_ref[...]).astype(kv_ref.dtype)


def _latent_up(proj, gq, gkv, wq, wkv, tm=1024):
    m = proj.shape[0]
    nq, nkv = wq.shape[1], wkv.shape[1]
    q_scale = (C_NOPE_DIM + C_ROPE_DIM) ** -0.5 * LOG2E
    return pl.pallas_call(
        functools.partial(_latent_up_kernel, q_scale=q_scale),
        grid=(m // tm,),
        in_specs=[
            pl.BlockSpec((tm, C_Q_RANK), lambda i: (i, CB_CQ)),
            pl.BlockSpec((tm, C_KV_RANK), lambda i: (i, CB_CKV)),
            pl.BlockSpec((1, C_Q_RANK), lambda i: (0, 0)),
            pl.BlockSpec((1, C_KV_RANK), lambda i: (0, 0)),
            pl.BlockSpec((C_Q_RANK, nq), lambda i: (0, 0)),
            pl.BlockSpec((C_KV_RANK, nkv), lambda i: (0, 0)),
        ],
        out_specs=[pl.BlockSpec((tm, nq), lambda i: (i, 0)), pl.BlockSpec((tm, nkv), lambda i: (i, 0))],
        out_shape=[jax.ShapeDtypeStruct((m, nq), BF16), jax.ShapeDtypeStruct((m, nkv), BF16)],
        compiler_params=_params(("parallel",), 48),
        name="latent_up",
    )(proj, proj, gq, gkv, wq, wkv)


A_ROW_CHUNK = 128
C_ROW_CHUNK = 256


def _attend(q_all, k_ref, v1_ref, dv, chunk):
    outs = []
    for c in range(q_all.shape[0] // chunk):
        q = q_all[c * chunk:(c + 1) * chunk]
        s = _dot_nt(q, k_ref[...])
        p = jnp.exp2(s - jnp.max(s, axis=-1, keepdims=True))
        ol = _dot(p.astype(BF16), v1_ref[...])
        outs.append(ol[:, :dv] * (1.0 / ol[:, dv:]))
    return outs


def _attn_a_kernel(q0_ref, q1_ref, q2_ref, k_ref, v_ref, cos_ref, slo_ref, shi_ref, gq_ref, gk_ref,
                   o_ref, kbuf, v1buf, *, tq):
    qi = pl.program_id(2)

    @pl.when(qi == 0)
    def _():
        k = _rms(k_ref[...].astype(F32), gk_ref[...])
        kbuf[...] = _rope(k, cos_ref[...], slo_ref[...], shi_ref[...], 32).astype(BF16)
        v1buf[:, :HEAD_DIM] = v_ref[...]
        v1buf[:, HEAD_DIM:] = jnp.ones((v1buf.shape[0], HEAD_DIM), BF16)

    rows = pl.ds(pl.multiple_of(qi * tq, tq), tq)
    cos, slo, shi = cos_ref[rows, :], slo_ref[rows, :], shi_ref[rows, :]
    q_scale = HEAD_DIM ** -0.5 * LOG2E
    qs = []
    for q_ref in (q0_ref, q1_ref, q2_ref):
        q = _rms(q_ref[...].astype(F32), gq_ref[...])
        qs.append((_rope(q, cos, slo, shi, 32) * q_scale).astype(BF16))
    q_all = jnp.concatenate(qs, axis=0)
    per_head = tq // A_ROW_CHUNK
    for c, o in enumerate(_attend(q_all, kbuf, v1buf, HEAD_DIM, A_ROW_CHUNK)):
        g, r = divmod(c, per_head)
        o_ref[r * A_ROW_CHUNK:(r + 1) * A_ROW_CHUNK, g * HEAD_DIM:(g + 1) * HEAD_DIM] = o.astype(o_ref.dtype)


def _attn_a(proj, tabs, gq, gk, batch, seq, tq=256):
    nq = seq // tq
    cos, slo, shi = tabs

    def q_spec(g):
        return pl.BlockSpec((tq, HEAD_DIM), lambda b, h, qi: (b * nq + qi, CB_AQ + h * A_GROUP + g))

    tab_spec = pl.BlockSpec((seq, LANES), lambda b, h, qi: (0, 0))
    gain_spec = pl.BlockSpec((1, HEAD_DIM), lambda b, h, qi: (0, 0))
    return pl.pallas_call(
        functools.partial(_attn_a_kernel, tq=tq),
        grid=(batch, A_KV_HEADS, nq),
        in_specs=[
            q_spec(0), q_spec(1), q_spec(2),
            pl.BlockSpec((seq, HEAD_DIM), lambda b, h, qi: (b, CB_AK + h)),
            pl.BlockSpec((seq, HEAD_DIM), lambda b, h, qi: (b, CB_AV + h)),
            tab_spec, tab_spec, tab_spec, gain_spec, gain_spec,
        ],
        out_specs=pl.BlockSpec((tq, A_GROUP * HEAD_DIM), lambda b, h, qi: (b * nq + qi, h)),
        out_shape=jax.ShapeDtypeStruct((batch * seq, A_WIDTH), BF16),
        scratch_shapes=[pltpu.VMEM((seq, HEAD_DIM), BF16), pltpu.VMEM((seq, 2 * HEAD_DIM), BF16)],
        compiler_params=_params(("parallel", "parallel", "arbitrary"), 48),
        name="attn_a",
    )(proj, proj, proj, proj, proj, cos, slo, shi, gq, gk)


def _attn_c_kernel(q_ref, kn_ref, v_ref, kr_ref, cos_ref, slo_ref, shi_ref, o_ref, kbuf, v1buf, *, tq):
    qi = pl.program_id(2)

    @pl.when(qi == 0)
    def _():
        kbuf[:, :C_NOPE_DIM] = kn_ref[...]
        kr = _rope(kr_ref[...].astype(F32), cos_ref[...], slo_ref[...], shi_ref[...], 32)
        kbuf[:, C_NOPE_DIM:] = kr.astype(BF16)
        v1buf[:, :C_V_DIM] = v_ref[...]
        v1buf[:, C_V_DIM:] = jnp.ones((v1buf.shape[0], C_V_DIM), BF16)

    rows = pl.ds(pl.multiple_of(qi * tq, tq), tq)
    q_rope = _rope(q_ref[:, C_NOPE_DIM:].astype(F32), cos_ref[rows, :], slo_ref[rows, :], shi_ref[rows, :], 32)
    q_all = jnp.concatenate([q_ref[:, :C_NOPE_DIM], q_rope.astype(BF16)], axis=1)
    for c, o in enumerate(_attend(q_all, kbuf, v1buf, C_V_DIM, C_ROW_CHUNK)):
        o_ref[c * C_ROW_CHUNK:(c + 1) * C_ROW_CHUNK, :] = o.astype(o_ref.dtype)


def _attn_c(qc, kvc, proj, tabs, batch, seq, tq=512):
    nq = seq // tq
    cos, slo, shi = tabs
    tab_spec = pl.BlockSpec((seq, LANES), lambda b, h, qi: (0, 0))
    return pl.pallas_call(
        functools.partial(_attn_c_kernel, tq=tq),
        grid=(batch, C_HEADS, nq),
        in_specs=[
            pl.BlockSpec((tq, C_QK_PAD), lambda b, h, qi: (b * nq + qi, h)),
            pl.BlockSpec((seq, C_NOPE_DIM), lambda b, h, qi: (b, h)),
            pl.BlockSpec((seq, C_V_DIM), lambda b, h, qi: (b, C_HEADS + h)),
            pl.BlockSpec((seq, LANES), lambda b, h, qi: (b, CB_CKR)),
            tab_spec, tab_spec, tab_spec,
        ],
        out_specs=pl.BlockSpec((tq, C_V_DIM), lambda b, h, qi: (b * nq + qi, h)),
        out_shape=jax.ShapeDtypeStruct((batch * seq, C_WIDTH), BF16),
        scratch_shapes=[pltpu.VMEM((seq, C_QK_PAD), BF16), pltpu.VMEM((seq, 2 * C_V_DIM), BF16)],
        compiler_params=_params(("parallel", "parallel", "arbitrary"), 48),
        name="attn_c",
    )(qc, kvc, kvc, proj, cos, slo, shi)


B_QBLK = 128
B_PREP_ROWS = 256
B_UNROLL = 8


def _attn_b_kernel(q0_ref, q1_ref, q2_ref, k_ref, v_ref, cos_ref, slo_ref, shi_ref, o_ref,
                   q0f, q1f, q2f, kf, vf, m_s, l_s, acc_s, *, seq):
    q_refs = (q0_ref, q1_ref, q2_ref)
    q_bufs = (q0f, q1f, q2f)
    q_scale = HEAD_DIM ** -0.5 * LOG2E

    def prep(ci, carry):
        rows = pl.ds(pl.multiple_of(ci * B_PREP_ROWS, B_PREP_ROWS), B_PREP_ROWS)
        cos, slo, shi = cos_ref[rows, :], slo_ref[rows, :], shi_ref[rows, :]
        for q_ref, q_buf in zip(q_refs, q_bufs):
            q_buf[rows, :] = _rope(q_ref[rows, :].astype(F32), cos, slo, shi, 16) * q_scale
        kf[rows, :] = _rope(k_ref[rows, :].astype(F32), cos, slo, shi, 16)
        vf[rows, :] = v_ref[rows, :].astype(F32)
        return carry

    lax.fori_loop(0, seq // B_PREP_ROWS, prep, 0)

    for g, (window, dil) in reversed(list(enumerate(B_PATTERNS))):
        first = g == len(B_PATTERNS) - 1
        half = window // (2 * dil)
        length = seq // dil
        nblk = length // B_QBLK
        kwin = min(2 * B_QBLK, length)
        q_buf = q_bufs[g]

        def rows_of(start, size, dil=dil):
            return pl.ds(start, size) if dil == 1 else pl.ds(start, size, stride=dil)

        def block(n, carry, first=first, half=half, length=length, nblk=nblk, kwin=kwin, q_buf=q_buf, dil=dil,
                  rows_of=rows_of):
            r = n // nblk
            i = n % nblk
            k0 = jnp.clip(i * B_QBLK - half, 0, length - kwin)
            q_rows = rows_of(r + dil * B_QBLK * i, B_QBLK)
            k_rows = rows_of(r + dil * k0, kwin)
            q = q_buf[q_rows, :].astype(BF16)
            k = kf[k_rows, :].astype(BF16)
            v = vf[k_rows, :].astype(BF16)
            s = _dot_nt(q, k)
            qpos = i * B_QBLK + lax.broadcasted_iota(jnp.int32, (B_QBLK, kwin), 0)
            kpos = k0 + lax.broadcasted_iota(jnp.int32, (B_QBLK, kwin), 1)
            s = jnp.where(jnp.abs(kpos - qpos) <= half, s, -jnp.inf)
            m_b = jnp.max(s, axis=-1, keepdims=True)
            p = jnp.exp2(s - m_b)
            l_b = jnp.sum(p, axis=-1, keepdims=True)
            a_b = _dot(p.astype(BF16), v)
            full = (B_QBLK, HEAD_DIM)
            if first:
                m_s[q_rows, :] = jnp.broadcast_to(m_b, full)
                l_s[q_rows, :] = jnp.broadcast_to(l_b, full)
                acc_s[q_rows, :] = a_b
            else:
                m_o = m_s[q_rows, :]
                m_n = jnp.maximum(m_o, m_b)
                w_o = jnp.exp2(m_o - m_n)
                w_b = jnp.exp2(m_b - m_n)
                acc_s[q_rows, :] = acc_s[q_rows, :] * w_o + a_b * w_b
                l_s[q_rows, :] = l_s[q_rows, :] * w_o + l_b * w_b
                m_s[q_rows, :] = m_n
            return carry

        lax.fori_loop(0, seq // B_QBLK, block, 0, unroll=B_UNROLL)

    def finish(ci, carry):
        rows = pl.ds(pl.multiple_of(ci * B_PREP_ROWS, B_PREP_ROWS), B_PREP_ROWS)
        o_ref[rows, :] = (acc_s[rows, :] * (1.0 / l_s[rows, :])).astype(o_ref.dtype)
        return carry

    lax.fori_loop(0, seq // B_PREP_ROWS, finish, 0)


def _attn_b(proj, tabs, batch, seq):
    cos, slo, shi = tabs

    def q_spec(g):
        return pl.BlockSpec((seq, HEAD_DIM), lambda b, h: (b, CB_BQ + g * B_HEADS + h))

    tab_spec = pl.BlockSpec((seq, LANES), lambda b, h: (0, 0))
    slab = pltpu.VMEM((seq, HEAD_DIM), F32)
    return pl.pallas_call(
        functools.partial(_attn_b_kernel, seq=seq),
        grid=(batch, B_HEADS),
        in_specs=[
            q_spec(0), q_spec(1), q_spec(2),
            pl.BlockSpec((seq, HEAD_DIM), lambda b, h: (b, CB_BK + h)),
            pl.BlockSpec((seq, HEAD_DIM), lambda b, h: (b, CB_BV + h)),
            tab_spec, tab_spec, tab_spec,
        ],
        out_specs=pl.BlockSpec((seq, HEAD_DIM), lambda b, h: (b, h)),
        out_shape=jax.ShapeDtypeStruct((batch * seq, B_WIDTH), BF16),
        scratch_shapes=[slab] * 8,
        compiler_params=_params(("parallel", "parallel"), 48),
        name="attn_b",
    )(proj, proj, proj, proj, proj, cos, slo, shi)


def _out_kernel(ya_ref, yb_ref, yc_ref, ga_ref, gb_ref, gc_ref, w_ref, x_ref, gf_ref, x1_ref, xn_ref):
    y = jnp.concatenate([
        _rms(ya_ref[...].astype(F32), ga_ref[...]).astype(BF16),
        _rms(yb_ref[...].astype(F32), gb_ref[...]).astype(BF16),
        _rms(yc_ref[...].astype(F32), gc_ref[...]).astype(BF16),
    ], axis=1)
    x1 = x_ref[...] + _dot(y, w_ref[...])
    x1_ref[...] = x1
    xn_ref[...] = _rms(x1, gf_ref[...]).astype(xn_ref.dtype)


def _out_proj(ya, yb, yc, ga, gb, gc, w, x, gf, tm=512):
    m, d = x.shape

    def rows(width):
        return pl.BlockSpec((tm, width), lambda i: (i, 0))

    def const(r, c):
        return pl.BlockSpec((r, c), lambda i: (0, 0))

    return pl.pallas_call(
        _out_kernel,
        grid=(m // tm,),
        in_specs=[rows(A_WIDTH), rows(B_WIDTH), rows(C_WIDTH), const(1, A_WIDTH), const(1, B_WIDTH),
                  const(1, C_WIDTH), const(d, d), rows(d), const(1, d)],
        out_specs=[rows(d), rows(d)],
        out_shape=[jax.ShapeDtypeStruct((m, d), F32), jax.ShapeDtypeStruct((m, d), BF16)],
        compiler_params=_params(("parallel",), 56),
        name="out_proj",
    )(ya, yb, yc, ga, gb, gc, w, x, gf)


FFN_SUBTILE = 256
FFN_ROWS = 256
FFN_GATE_ROWS = 256
FFN_GUARD = 8

def _ffn_up_kernel(xn_ref, wg_ref, wu_ref, cg_ref, cu_ref, bg_ref, bu_ref, o_ref, hg, hu, *, seq):
    tn = o_ref.shape[1]
    zeros = jnp.zeros((FFN_GUARD, tn), F32)
    for h in (hg, hu):
        h[0:FFN_GUARD, :] = zeros
        h[FFN_GUARD + seq:2 * FFN_GUARD + seq, :] = zeros

    def matmuls(c, r):
        cols = slice(c * FFN_SUBTILE, (c + 1) * FFN_SUBTILE)
        xr = xn_ref[r * FFN_ROWS:(r + 1) * FFN_ROWS, :]
        dst = slice(FFN_GUARD + r * FFN_ROWS, FFN_GUARD + (r + 1) * FFN_ROWS)
        hg[dst, cols] = _dot(xr, wg_ref[:, cols])
        hu[dst, cols] = _dot(xr, wu_ref[:, cols])

    def gate_stage(c, r):
        cols = slice(c * FFN_SUBTILE, (c + 1) * FFN_SUBTILE)

        def conv(h, c_ref, b_ref, base):
            taps = [h[base + t - 1:base + t - 1 + FFN_GATE_ROWS, cols] * c_ref[t:t + 1, cols] for t in range(3)]
            return taps[0] + taps[1] + taps[2] + b_ref[:, cols]

        for k in range(FFN_ROWS // FFN_GATE_ROWS):
            row0 = r * FFN_ROWS + k * FFN_GATE_ROWS
            gate = conv(hg, cg_ref, bg_ref, FFN_GUARD + row0)
            up = conv(hu, cu_ref, bu_ref, FFN_GUARD + row0)
            o_ref[row0:row0 + FFN_GATE_ROWS, cols] = (
                gate * (1.0 / (1.0 + jnp.exp(-gate))) * up).astype(o_ref.dtype)

    units = [(c, r) for c in range(tn // FFN_SUBTILE) for r in range(seq // FFN_ROWS)]
    for i, unit in enumerate(units):
        matmuls(*unit)
        if i > 0:
            gate_stage(*units[i - 1])
    gate_stage(*units[-1])


def _ffn_up(xn, w_up, conv_w, conv_b, batch, seq, tn=512):
    d = xn.shape[1]
    nj = D_FF // tn
    return pl.pallas_call(
        functools.partial(_ffn_up_kernel, seq=seq),
        grid=(batch, nj),
        in_specs=[
            pl.BlockSpec((seq, d), lambda b, j: (b, 0)),
            pl.BlockSpec((d, tn), lambda b, j: (0, j)),
            pl.BlockSpec((d, tn), lambda b, j: (0, nj + j)),
            pl.BlockSpec((3, tn), lambda b, j: (0, j)),
            pl.BlockSpec((3, tn), lambda b, j: (0, nj + j)),
            pl.BlockSpec((1, tn), lambda b, j: (0, j)),
            pl.BlockSpec((1, tn), lambda b, j: (0, nj + j)),
        ],
        out_specs=pl.BlockSpec((seq, tn), lambda b, j: (b, j)),
        out_shape=jax.ShapeDtypeStruct((batch * seq, D_FF), BF16),
        scratch_shapes=[pltpu.VMEM((seq + 2 * FFN_GUARD, tn), F32)] * 2,
        compiler_params=_params(("parallel", "arbitrary"), 56),
        name="ffn_up",
    )(xn, w_up, w_up, conv_w, conv_w, conv_b, conv_b)


def _ffn_down_kernel(a_ref, w_ref, x1_ref, g_ref, x2_ref, xn_ref):
    k = pl.program_id(1)

    @pl.when(k == 0)
    def _():
        x2_ref[...] = x1_ref[...]

    x2_ref[...] += _dot(a_ref[...], w_ref[...])

    @pl.when(k == pl.num_programs(1) - 1)
    def _():
        xn_ref[...] = _rms(x2_ref[...], g_ref[...]).astype(xn_ref.dtype)


def _ffn_down(act, w, x1, g, norm_dtype, tm=1024, tk=512):
    m, kdim = act.shape
    d = w.shape[1]
    return pl.pallas_call(
        _ffn_down_kernel,
        grid=(m // tm, kdim // tk),
        in_specs=[
            pl.BlockSpec((tm, tk), lambda i, k: (i, k)),
            pl.BlockSpec((tk, d), lambda i, k: (k, 0)),
            pl.BlockSpec((tm, d), lambda i, k: (i, 0)),
            pl.BlockSpec((1, d), lambda i, k: (0, 0)),
        ],
        out_specs=[pl.BlockSpec((tm, d), lambda i, k: (i, 0)), pl.BlockSpec((tm, d), lambda i, k: (i, 0))],
        out_shape=[jax.ShapeDtypeStruct((m, d), F32), jax.ShapeDtypeStruct((m, d), norm_dtype)],
        compiler_params=_params(("parallel", "arbitrary"), 56),
        name="ffn_down",
    )(act, w, x1, g)


def _rope_table(pos, dim, theta):
    inv = theta ** (-jnp.arange(0, dim, 2, dtype=F32) / dim)
    ang = pos.astype(F32)[:, None] * inv[None, :]
    return jnp.cos(ang), jnp.sin(ang)


def _lane_tables(seq):
    t = jnp.arange(seq)
    z = lambda w: jnp.zeros((seq, w), F32)
    one = lambda w: jnp.ones((seq, w), F32)
    cr, sr = _rope_table(t // GRID_W, HEAD_DIM // 2, A_ROPE_THETA)
    cc, sc = _rope_table(t % GRID_W, HEAD_DIM // 2, A_ROPE_THETA)
    tab_a = (jnp.concatenate([cr, cr, cc, cc], 1),
             jnp.concatenate([-sr, z(32), -sc, z(32)], 1),
             jnp.concatenate([z(32), sr, z(32), sc], 1))
    cp, sp = _rope_table(t, PARTIAL_ROPE_DIM, PARTIAL_ROPE_THETA)
    tab_b = (jnp.concatenate([cp, cp, one(96)], 1),
             jnp.concatenate([-sp, z(112)], 1),
             jnp.concatenate([z(16), sp, z(96)], 1))
    cm, sm = _rope_table(t, C_ROPE_DIM, C_ROPE_THETA)
    tab_c = (jnp.concatenate([cm, cm, one(64)], 1),
             jnp.concatenate([-sm, z(96)], 1),
             jnp.concatenate([z(32), sm, z(64)], 1))
    return tab_a, tab_b, tab_c


def _prep_w_in(w):
    aqw = A_HEADS * HEAD_DIM
    akw = A_KV_HEADS * HEAD_DIM
    bqw = B_N_GROUPS * B_HEADS * HEAD_DIM
    bkw = B_HEADS * HEAD_DIM
    o_c = aqw + 2 * akw + bqw + 2 * bkw
    front = w[:, :o_c]
    cq = w[:, o_c:o_c + C_Q_RANK]
    ckv = w[:, o_c + C_Q_RANK:o_c + C_Q_RANK + C_KV_RANK]
    ckr = w[:, o_c + C_Q_RANK + C_KV_RANK:]
    pad = jnp.zeros((w.shape[0], PROJ_WIDTH - w.shape[1]), w.dtype)
    return jnp.concatenate([cq, ckv, front, ckr, pad], axis=1).astype(BF16)


def _prep_w_uq(w):
    w = w.reshape(C_Q_RANK, C_HEADS, C_NOPE_DIM + C_ROPE_DIM)
    w = jnp.pad(w, ((0, 0), (0, 0), (0, C_QK_PAD - C_NOPE_DIM - C_ROPE_DIM)))
    return w.reshape(C_Q_RANK, C_HEADS * C_QK_PAD).astype(BF16)


def _prep_w_ukv(w):
    w = w.reshape(C_KV_RANK, C_HEADS, C_NOPE_DIM + C_V_DIM)
    kn = w[:, :, :C_NOPE_DIM].reshape(C_KV_RANK, C_HEADS * C_NOPE_DIM)
    v = w[:, :, C_NOPE_DIM:].reshape(C_KV_RANK, C_HEADS * C_V_DIM)
    return jnp.concatenate([kn, v], axis=1).astype(BF16)


def kernel(x, attn_norm, w_in, a_q_norm, a_k_norm, c_q_norm, c_kv_norm, w_uq, w_ukv, out_norm, w_out,
           ffn_norm, w_up, conv_w, conv_b, w_down, final_norm):
    batch, seq, d = x.shape
    depth = w_in.shape[0]
    tab_a, tab_b, tab_c = _lane_tables(seq)
    row = lambda v: v.reshape(1, -1)

    xr = x.reshape(batch * seq, d)
    xn = _rmsnorm_rows(xr, row(attn_norm[0]))
    for l in range(depth):
        proj = _matmul(xn, _prep_w_in(w_in[l]))
        qc, kvc = _latent_up(proj, row(c_q_norm[l]), row(c_kv_norm[l]), _prep_w_uq(w_uq[l]),
                             _prep_w_ukv(w_ukv[l]))
        ya = _attn_a(proj, tab_a, row(a_q_norm[l]), row(a_k_norm[l]), batch, seq)
        yb = _attn_b(proj, tab_b, batch, seq)
        yc = _attn_c(qc, kvc, proj, tab_c, batch, seq)
        g = out_norm[l]
        x1, xn1 = _out_proj(ya, yb, yc, row(g[:A_WIDTH]), row(g[A_WIDTH:A_WIDTH + B_WIDTH]),
                            row(g[A_WIDTH + B_WIDTH:]), w_out[l].astype(BF16), xr, row(ffn_norm[l]))
        act = _ffn_up(xn1, w_up[l].astype(BF16), conv_w[l], row(conv_b[l]), batch, seq)
        last = l == depth - 1
        g_next = final_norm if last else attn_norm[l + 1]
        xr, xn = _ffn_down(act, w_down[l].astype(BF16), x1, row(g_next), F32 if last else BF16)
    return xn.reshape(batch, seq, d)
```

```python
import functools
import math

import jax
import jax.numpy as jnp
from jax import lax
from jax.experimental import pallas as pl
from jax.experimental.pallas import tpu as pltpu

D_MODEL = 2048
HEAD_DIM = 128
A_HEADS = 6
A_KV_HEADS = 2
A_GROUP = A_HEADS // A_KV_HEADS
A_ROPE_THETA = 10000.0
B_HEADS = 4
B_PATTERNS = ((128, 1), (512, 4), (2048, 16))
B_N_GROUPS = 3
C_HEADS = 6
C_Q_RANK = 512
C_KV_RANK = 512
C_NOPE_DIM = 128
C_ROPE_DIM = 64
C_V_DIM = 128
C_ROPE_THETA = 10000.0
PARTIAL_ROPE_DIM = HEAD_DIM // 4
PARTIAL_ROPE_THETA = 500000.0
GRID_W = 64
D_FF = 5632
EPS = 1e-6

A_WIDTH = A_HEADS * HEAD_DIM
B_WIDTH = B_HEADS * HEAD_DIM
C_WIDTH = C_HEADS * C_V_DIM
C_QK_PAD = 256

LANES = 128
LOG2E = math.log2(math.e)
F32 = jnp.float32
BF16 = jnp.bfloat16

PROJ_WIDTH = 5120
CB_CQ, CB_CKV = 0, 1
CB_AQ, CB_AK, CB_AV = 8, 14, 16
CB_BQ, CB_BK, CB_BV = 18, 30, 34
CB_CKR = 38


def _params(semantics, vmem_mib):
    return pltpu.CompilerParams(dimension_semantics=semantics, vmem_limit_bytes=vmem_mib * 1024 * 1024)


def _rms(xf, g):
    return xf * lax.rsqrt(jnp.mean(xf * xf, axis=-1, keepdims=True) + EPS) * g


def _rope(xf, cos, sin_lo, sin_hi, shift):
    return xf * cos + pltpu.roll(xf, LANES - shift, 1) * sin_lo + pltpu.roll(xf, shift, 1) * sin_hi


def _dot(a, b):
    return jnp.dot(a, b, preferred_element_type=F32)


def _dot_nt(a, b):
    return lax.dot_general(a, b, (((1,), (1,)), ((), ())), preferred_element_type=F32)


def _norm_kernel(x_ref, g_ref, o_ref):
    o_ref[...] = _rms(x_ref[...], g_ref[...]).astype(o_ref.dtype)


def _rmsnorm_rows(x, g, tm=512):
    m, d = x.shape
    return pl.pallas_call(
        _norm_kernel,
        grid=(m // tm,),
        in_specs=[pl.BlockSpec((tm, d), lambda i: (i, 0)), pl.BlockSpec((1, d), lambda i: (0, 0))],
        out_specs=pl.BlockSpec((tm, d), lambda i: (i, 0)),
        out_shape=jax.ShapeDtypeStruct((m, d), BF16),
        compiler_params=_params(("parallel",), 40),
        name="rmsnorm",
    )(x, g)


IN_WBLK = 256
IN_TN = 1024
IN_SRC_CQ = 15
IN_N_SRC = 19


def _in_proj_kernel(a_ref, *refs):
    w_refs, wck_ref, o_ref, w_bf = refs[:-3], refs[-3], refs[-2], refs[-1]
    j = pl.program_id(0)
    last = pl.num_programs(0) - 1

    @pl.when(pl.program_id(1) == 0)
    def _():
        for q, w_ref in enumerate(w_refs[:-1]):
            w_bf[:, q * IN_WBLK:(q + 1) * IN_WBLK] = w_ref[...].astype(BF16)
        tail = slice((len(w_refs) - 1) * IN_WBLK, len(w_refs) * IN_WBLK)

        @pl.when(j < last)
        def _():
            w_bf[:, tail] = w_refs[-1][...].astype(BF16)

        @pl.when(j == last)
        def _():
            w_bf[:, tail] = wck_ref[...].astype(BF16)

    o_ref[...] = _dot(a_ref[...], w_bf[...]).astype(o_ref.dtype)


def _in_proj(a, w_in, w_ckr, tm=1024):
    m, k = a.shape
    per_step = IN_TN // IN_WBLK

    def w_spec(q):
        def index(j, i):
            n = j * per_step + q
            return 0, jnp.where(n < 4, n + IN_SRC_CQ, n - 4)
        return pl.BlockSpec((k, IN_WBLK), index)

    return pl.pallas_call(
        _in_proj_kernel,
        grid=(PROJ_WIDTH // IN_TN, m // tm),
        in_specs=[pl.BlockSpec((tm, k), lambda j, i: (i, 0))] + [w_spec(q) for q in range(per_step)]
        + [pl.BlockSpec((k, IN_WBLK), lambda j, i: (0, 0))],
        out_specs=pl.BlockSpec((tm, IN_TN), lambda j, i: (i, j)),
        out_shape=jax.ShapeDtypeStruct((m, PROJ_WIDTH), BF16),
        scratch_shapes=[pltpu.VMEM((k, IN_TN), BF16)],
        compiler_params=_params(("parallel", "arbitrary"), 48),
        name="in_proj",
    )(a, *([w_in] * per_step), w_ckr)


def _latent_up_kernel(cq_ref, ckv_ref, gq_ref, gkv_ref, wq_ref, wkv_ref, q_ref, kv_ref, *, q_scale):
    cq = _rms(cq_ref[...].astype(F32), gq_ref[...]).astype(BF16)
    q_ref[...] = (_dot(cq, wq_ref[...]) * q_scale).astype(q_ref.dtype)
    ckv = _rms(ckv_ref[...].astype(F32), gkv_ref[...]).astype(BF16)
    kv_ref[...] = _dot(ckv, wkv_ref[...]).astype(kv_ref.dtype)


def _latent_up(proj, gq, gkv, wq, wkv, tm=1024):
    m = proj.shape[0]
    nq, nkv = wq.shape[1], wkv.shape[1]
    q_scale = (C_NOPE_DIM + C_ROPE_DIM) ** -0.5 * LOG2E
    return pl.pallas_call(
        functools.partial(_latent_up_kernel, q_scale=q_scale),
        grid=(m // tm,),
        in_specs=[
            pl.BlockSpec((tm, C_Q_RANK), lambda i: (i, CB_CQ)),
            pl.BlockSpec((tm, C_KV_RANK), lambda i: (i, CB_CKV)),
            pl.BlockSpec((1, C_Q_RANK), lambda i: (0, 0)),
            pl.BlockSpec((1, C_KV_RANK), lambda i: (0, 0)),
            pl.BlockSpec((C_Q_RANK, nq), lambda i: (0, 0)),
            pl.BlockSpec((C_KV_RANK, nkv), lambda i: (0, 0)),
        ],
        out_specs=[pl.BlockSpec((tm, nq), lambda i: (i, 0)), pl.BlockSpec((tm, nkv), lambda i: (i, 0))],
        out_shape=[jax.ShapeDtypeStruct((m, nq), BF16), jax.ShapeDtypeStruct((m, nkv), BF16)],
        compiler_params=_params(("parallel",), 48),
        name="latent_up",
    )(proj, proj, gq, gkv, wq, wkv)


A_ROW_CHUNK = 128
C_ROW_CHUNK = 256


def _attend(q_all, k_ref, v1_ref, dv, chunk):
    outs = []
    for c in range(q_all.shape[0] // chunk):
        q = q_all[c * chunk:(c + 1) * chunk]
        s = _dot_nt(q, k_ref[...])
        p = jnp.exp2(s - jnp.max(s, axis=-1, keepdims=True))
        ol = _dot(p.astype(BF16), v1_ref[...])
        outs.append(ol[:, :dv] * (1.0 / ol[:, dv:]))
    return outs


def _attn_a_kernel(q0_ref, q1_ref, q2_ref, k_ref, v_ref, cos_ref, slo_ref, shi_ref, gq_ref, gk_ref,
                   o_ref, kbuf, v1buf, *, tq):
    qi = pl.program_id(2)

    @pl.when(qi == 0)
    def _():
        k = _rms(k_ref[...].astype(F32), gk_ref[...])
        kbuf[...] = _rope(k, cos_ref[...], slo_ref[...], shi_ref[...], 32).astype(BF16)
        v1buf[:, :HEAD_DIM] = v_ref[...]
        v1buf[:, HEAD_DIM:] = jnp.ones((v1buf.shape[0], HEAD_DIM), BF16)

    rows = pl.ds(pl.multiple_of(qi * tq, tq), tq)
    cos, slo, shi = cos_ref[rows, :], slo_ref[rows, :], shi_ref[rows, :]
    q_scale = HEAD_DIM ** -0.5 * LOG2E
    qs = []
    for q_ref in (q0_ref, q1_ref, q2_ref):
        q = _rms(q_ref[...].astype(F32), gq_ref[...])
        qs.append((_rope(q, cos, slo, shi, 32) * q_scale).astype(BF16))
    q_all = jnp.concatenate(qs, axis=0)
    per_head = tq // A_ROW_CHUNK
    for c, o in enumerate(_attend(q_all, kbuf, v1buf, HEAD_DIM, A_ROW_CHUNK)):
        g, r = divmod(c, per_head)
        o_ref[r * A_ROW_CHUNK:(r + 1) * A_ROW_CHUNK, g * HEAD_DIM:(g + 1) * HEAD_DIM] = o.astype(o_ref.dtype)


def _attn_a(proj, tabs, gq, gk, batch, seq, tq=256):
    nq = seq // tq
    cos, slo, shi = tabs

    def q_spec(g):
        return pl.BlockSpec((tq, HEAD_DIM), lambda b, h, qi: (b * nq + qi, CB_AQ + h * A_GROUP + g))

    tab_spec = pl.BlockSpec((seq, LANES), lambda b, h, qi: (0, 0))
    gain_spec = pl.BlockSpec((1, HEAD_DIM), lambda b, h, qi: (0, 0))
    return pl.pallas_call(
        functools.partial(_attn_a_kernel, tq=tq),
        grid=(batch, A_KV_HEADS, nq),
        in_specs=[
            q_spec(0), q_spec(1), q_spec(2),
            pl.BlockSpec((seq, HEAD_DIM), lambda b, h, qi: (b, CB_AK + h)),
            pl.BlockSpec((seq, HEAD_DIM), lambda b, h, qi: (b, CB_AV + h)),
            tab_spec, tab_spec, tab_spec, gain_spec, gain_spec,
        ],
        out_specs=pl.BlockSpec((tq, A_GROUP * HEAD_DIM), lambda b, h, qi: (b * nq + qi, h)),
        out_shape=jax.ShapeDtypeStruct((batch * seq, A_WIDTH), BF16),
        scratch_shapes=[pltpu.VMEM((seq, HEAD_DIM), BF16), pltpu.VMEM((seq, 2 * HEAD_DIM), BF16)],
        compiler_params=_params(("parallel", "parallel", "arbitrary"), 48),
        name="attn_a",
    )(proj, proj, proj, proj, proj, cos, slo, shi, gq, gk)


def _attn_c_kernel(q_ref, kn_ref, v_ref, kr_ref, cos_ref, slo_ref, shi_ref, o_ref, kbuf, v1buf, *, tq):
    qi = pl.program_id(2)

    @pl.when(qi == 0)
    def _():
        kbuf[:, :C_NOPE_DIM] = kn_ref[...]
        kr = _rope(kr_ref[...].astype(F32), cos_ref[...], slo_ref[...], shi_ref[...], 32)
        kbuf[:, C_NOPE_DIM:] = kr.astype(BF16)
        v1buf[:, :C_V_DIM] = v_ref[...]
        v1buf[:, C_V_DIM:] = jnp.ones((v1buf.shape[0], C_V_DIM), BF16)

    rows = pl.ds(pl.multiple_of(qi * tq, tq), tq)
    q_rope = _rope(q_ref[:, C_NOPE_DIM:].astype(F32), cos_ref[rows, :], slo_ref[rows, :], shi_ref[rows, :], 32)
    q_all = jnp.concatenate([q_ref[:, :C_NOPE_DIM], q_rope.astype(BF16)], axis=1)
    for c, o in enumerate(_attend(q_all, kbuf, v1buf, C_V_DIM, C_ROW_CHUNK)):
        o_ref[c * C_ROW_CHUNK:(c + 1) * C_ROW_CHUNK, :] = o.astype(o_ref.dtype)


def _attn_c(qc, kvc, proj, tabs, batch, seq, tq=512):
    nq = seq // tq
    cos, slo, shi = tabs
    tab_spec = pl.BlockSpec((seq, LANES), lambda b, h, qi: (0, 0))
    return pl.pallas_call(
        functools.partial(_attn_c_kernel, tq=tq),
        grid=(batch, C_HEADS, nq),
        in_specs=[
            pl.BlockSpec((tq, C_QK_PAD), lambda b, h, qi: (b * nq + qi, h)),
            pl.BlockSpec((seq, C_NOPE_DIM), lambda b, h, qi: (b, h)),
            pl.BlockSpec((seq, C_V_DIM), lambda b, h, qi: (b, C_HEADS + h)),
            pl.BlockSpec((seq, LANES), lambda b, h, qi: (b, CB_CKR)),
            tab_spec, tab_spec, tab_spec,
        ],
        out_specs=pl.BlockSpec((tq, C_V_DIM), lambda b, h, qi: (b * nq + qi, h)),
        out_shape=jax.ShapeDtypeStruct((batch * seq, C_WIDTH), BF16),
        scratch_shapes=[pltpu.VMEM((seq, C_QK_PAD), BF16), pltpu.VMEM((seq, 2 * C_V_DIM), BF16)],
        compiler_params=_params(("parallel", "parallel", "arbitrary"), 48),
        name="attn_c",
    )(qc, kvc, kvc, proj, cos, slo, shi)


B_QBLK = 128
B_PREP_ROWS = 256
B_UNROLL = 8


def _attn_b_kernel(q0_ref, q1_ref, q2_ref, k_ref, v_ref, cos_ref, slo_ref, shi_ref, o_ref,
                   q0f, q1f, q2f, kf, vf, m_s, l_s, acc_s, *, seq):
    q_refs = (q0_ref, q1_ref, q2_ref)
    q_bufs = (q0f, q1f, q2f)
    q_scale = HEAD_DIM ** -0.5 * LOG2E

    def prep(ci, carry):
        rows = pl.ds(pl.multiple_of(ci * B_PREP_ROWS, B_PREP_ROWS), B_PREP_ROWS)
        cos, slo, shi = cos_ref[rows, :], slo_ref[rows, :], shi_ref[rows, :]
        for q_ref, q_buf in zip(q_refs, q_bufs):
            q_buf[rows, :] = _rope(q_ref[rows, :].astype(F32), cos, slo, shi, 16) * q_scale
        kf[rows, :] = _rope(k_ref[rows, :].astype(F32), cos, slo, shi, 16)
        vf[rows, :] = v_ref[rows, :].astype(F32)
        return carry

    lax.fori_loop(0, seq // B_PREP_ROWS, prep, 0)

    for g, (window, dil) in reversed(list(enumerate(B_PATTERNS))):
        first = g == len(B_PATTERNS) - 1
        half = window // (2 * dil)
        length = seq // dil
        nblk = length // B_QBLK
        kwin = min(2 * B_QBLK, length)
        q_buf = q_bufs[g]

        def rows_of(start, size, dil=dil):
            return pl.ds(start, size) if dil == 1 else pl.ds(start, size, stride=dil)

        def block(n, carry, first=first, half=half, length=length, nblk=nblk, kwin=kwin, q_buf=q_buf, dil=dil,
                  rows_of=rows_of):
            r = n // nblk
            i = n % nblk
            k0 = jnp.clip(i * B_QBLK - half, 0, length - kwin)
            q_rows = rows_of(r + dil * B_QBLK * i, B_QBLK)
            k_rows = rows_of(r + dil * k0, kwin)
            q = q_buf[q_rows, :].astype(BF16)
            k = kf[k_rows, :].astype(BF16)
            v = vf[k_rows, :].astype(BF16)
            s = _dot_nt(q, k)
            qpos = i * B_QBLK + lax.broadcasted_iota(jnp.int32, (B_QBLK, kwin), 0)
            kpos = k0 + lax.broadcasted_iota(jnp.int32, (B_QBLK, kwin), 1)
            s = jnp.where(jnp.abs(kpos - qpos) <= half, s, -jnp.inf)
            m_b = jnp.max(s, axis=-1, keepdims=True)
            p = jnp.exp2(s - m_b)
            l_b = jnp.sum(p, axis=-1, keepdims=True)
            a_b = _dot(p.astype(BF16), v)
            full = (B_QBLK, HEAD_DIM)
            if first:
                m_s[q_rows, :] = jnp.broadcast_to(m_b, full)
                l_s[q_rows, :] = jnp.broadcast_to(l_b, full)
                acc_s[q_rows, :] = a_b
            else:
                m_o = m_s[q_rows, :]
                m_n = jnp.maximum(m_o, m_b)
                w_o = jnp.exp2(m_o - m_n)
                w_b = jnp.exp2(m_b - m_n)
                acc_s[q_rows, :] = acc_s[q_rows, :] * w_o + a_b * w_b
                l_s[q_rows, :] = l_s[q_rows, :] * w_o + l_b * w_b
                m_s[q_rows, :] = m_n
            return carry

        lax.fori_loop(0, seq // B_QBLK, block, 0, unroll=B_UNROLL)

    def finish(ci, carry):
        rows = pl.ds(pl.multiple_of(ci * B_PREP_ROWS, B_PREP_ROWS), B_PREP_ROWS)
        o_ref[rows, :] = (acc_s[rows, :] * (1.0 / l_s[rows, :])).astype(o_ref.dtype)
        return carry

    lax.fori_loop(0, seq // B_PREP_ROWS, finish, 0)


def _attn_b(proj, tabs, batch, seq):
    cos, slo, shi = tabs

    def q_spec(g):
        return pl.BlockSpec((seq, HEAD_DIM), lambda b, h: (b, CB_BQ + g * B_HEADS + h))

    tab_spec = pl.BlockSpec((seq, LANES), lambda b, h: (0, 0))
    slab = pltpu.VMEM((seq, HEAD_DIM), F32)
    return pl.pallas_call(
        functools.partial(_attn_b_kernel, seq=seq),
        grid=(batch, B_HEADS),
        in_specs=[
            q_spec(0), q_spec(1), q_spec(2),
            pl.BlockSpec((seq, HEAD_DIM), lambda b, h: (b, CB_BK + h)),
            pl.BlockSpec((seq, HEAD_DIM), lambda b, h: (b, CB_BV + h)),
            tab_spec, tab_spec, tab_spec,
        ],
        out_specs=pl.BlockSpec((seq, HEAD_DIM), lambda b, h: (b, h)),
        out_shape=jax.ShapeDtypeStruct((batch * seq, B_WIDTH), BF16),
        scratch_shapes=[slab] * 8,
        compiler_params=_params(("parallel", "parallel"), 48),
        name="attn_b",
    )(proj, proj, proj, proj, proj, cos, slo, shi)


def _out_kernel(ya_ref, yb_ref, yc_ref, ga_ref, gb_ref, gc_ref, w_ref, x_ref, gf_ref, x1_ref, xn_ref):
    y = jnp.concatenate([
        _rms(ya_ref[...].astype(F32), ga_ref[...]).astype(BF16),
        _rms(yb_ref[...].astype(F32), gb_ref[...]).astype(BF16),
        _rms(yc_ref[...].astype(F32), gc_ref[...]).astype(BF16),
    ], axis=1)
    x1 = x_ref[...] + _dot(y, w_ref[...])
    x1_ref[...] = x1
    xn_ref[...] = _rms(x1, gf_ref[...]).astype(xn_ref.dtype)


def _out_proj(ya, yb, yc, ga, gb, gc, w, x, gf, tm=512):
    m, d = x.shape

    def rows(width):
        return pl.BlockSpec((tm, width), lambda i: (i, 0))

    def const(r, c):
        return pl.BlockSpec((r, c), lambda i: (0, 0))

    return pl.pallas_call(
        _out_kernel,
        grid=(m // tm,),
        in_specs=[rows(A_WIDTH), rows(B_WIDTH), rows(C_WIDTH), const(1, A_WIDTH), const(1, B_WIDTH),
                  const(1, C_WIDTH), const(d, d), rows(d), const(1, d)],
        out_specs=[rows(d), rows(d)],
        out_shape=[jax.ShapeDtypeStruct((m, d), F32), jax.ShapeDtypeStruct((m, d), BF16)],
        compiler_params=_params(("parallel",), 56),
        name="out_proj",
    )(ya, yb, yc, ga, gb, gc, w, x, gf)


FFN_SUBTILE = 256
FFN_ROWS = 256
FFN_GATE_ROWS = 64
FFN_GUARD = 8

def _ffn_up_kernel(xn_ref, wg_ref, wu_ref, cg_ref, cu_ref, bg_ref, bu_ref, o_ref, hg, hu, wg_bf, wu_bf, *, seq):
    tn = o_ref.shape[1]

    @pl.when(pl.program_id(1) == 0)
    def _():
        wg_bf[...] = wg_ref[...].astype(BF16)
        wu_bf[...] = wu_ref[...].astype(BF16)

    zeros = jnp.zeros((FFN_GUARD, tn), F32)
    for h in (hg, hu):
        h[0:FFN_GUARD, :] = zeros
        h[FFN_GUARD + seq:2 * FFN_GUARD + seq, :] = zeros

    def matmuls(c, r):
        cols = slice(c * FFN_SUBTILE, (c + 1) * FFN_SUBTILE)
        xr = xn_ref[r * FFN_ROWS:(r + 1) * FFN_ROWS, :]
        dst = slice(FFN_GUARD + r * FFN_ROWS, FFN_GUARD + (r + 1) * FFN_ROWS)
        hg[dst, cols] = _dot(xr, wg_bf[:, cols])
        hu[dst, cols] = _dot(xr, wu_bf[:, cols])

    def gate_stage(c, r):
        cols = slice(c * FFN_SUBTILE, (c + 1) * FFN_SUBTILE)
        base = FFN_GUARD + r * FFN_ROWS

        def conv(h, c_ref, b_ref):
            taps = [h[base + t - 1:base + t - 1 + FFN_ROWS, cols] * c_ref[t:t + 1, cols] for t in range(3)]
            return taps[0] + taps[1] + taps[2] + b_ref[:, cols]

        gate = conv(hg, cg_ref, bg_ref)
        up = conv(hu, cu_ref, bu_ref)
        o_ref[r * FFN_ROWS:(r + 1) * FFN_ROWS, cols] = (
            gate * (1.0 / (1.0 + jnp.exp(-gate))) * up).astype(o_ref.dtype)

    units = [(c, r) for c in range(tn // FFN_SUBTILE) for r in range(seq // FFN_ROWS)]
    for i, unit in enumerate(units):
        matmuls(*unit)
        if i > 0:
            gate_stage(*units[i - 1])
    gate_stage(*units[-1])


def _ffn_up(xn, w_up, conv_w, conv_b, batch, seq, tn=512):
    d = xn.shape[1]
    nj = D_FF // tn
    return pl.pallas_call(
        functools.partial(_ffn_up_kernel, seq=seq),
        grid=(nj, batch),
        in_specs=[
            pl.BlockSpec((seq, d), lambda j, b: (b, 0)),
            pl.BlockSpec((d, tn), lambda j, b: (0, j)),
            pl.BlockSpec((d, tn), lambda j, b: (0, nj + j)),
            pl.BlockSpec((3, tn), lambda j, b: (0, j)),
            pl.BlockSpec((3, tn), lambda j, b: (0, nj + j)),
            pl.BlockSpec((1, tn), lambda j, b: (0, j)),
            pl.BlockSpec((1, tn), lambda j, b: (0, nj + j)),
        ],
        out_specs=pl.BlockSpec((seq, tn), lambda j, b: (b, j)),
        out_shape=jax.ShapeDtypeStruct((batch * seq, D_FF), BF16),
        scratch_shapes=[pltpu.VMEM((seq + 2 * FFN_GUARD, tn), F32)] * 2 + [pltpu.VMEM((d, tn), BF16)] * 2,
        compiler_params=_params(("parallel", "arbitrary"), 60),
        name="ffn_up",
    )(xn, w_up, w_up, conv_w, conv_w, conv_b, conv_b)


def _ffn_down_kernel(a_ref, w_ref, x1_ref, g_ref, x2_ref, xn_ref):
    k = pl.program_id(1)

    @pl.when(k == 0)
    def _():
        x2_ref[...] = x1_ref[...]

    x2_ref[...] += _dot(a_ref[...], w_ref[...])

    @pl.when(k == pl.num_programs(1) - 1)
    def _():
        xn_ref[...] = _rms(x2_ref[...], g_ref[...]).astype(xn_ref.dtype)


def _ffn_down(act, w, x1, g, norm_dtype, tm=1024, tk=512):
    m, kdim = act.shape
    d = w.shape[1]
    return pl.pallas_call(
        _ffn_down_kernel,
        grid=(m // tm, kdim // tk),
        in_specs=[
            pl.BlockSpec((tm, tk), lambda i, k: (i, k)),
            pl.BlockSpec((tk, d), lambda i, k: (k, 0)),
            pl.BlockSpec((tm, d), lambda i, k: (i, 0)),
            pl.BlockSpec((1, d), lambda i, k: (0, 0)),
        ],
        out_specs=[pl.BlockSpec((tm, d), lambda i, k: (i, 0)), pl.BlockSpec((tm, d), lambda i, k: (i, 0))],
        out_shape=[jax.ShapeDtypeStruct((m, d), F32), jax.ShapeDtypeStruct((m, d), norm_dtype)],
        compiler_params=_params(("parallel", "arbitrary"), 56),
        name="ffn_down",
    )(act, w, x1, g)


def _rope_table(pos, dim, theta):
    inv = theta ** (-jnp.arange(0, dim, 2, dtype=F32) / dim)
    ang = pos.astype(F32)[:, None] * inv[None, :]
    return jnp.cos(ang), jnp.sin(ang)


def _lane_tables(seq):
    t = jnp.arange(seq)
    z = lambda w: jnp.zeros((seq, w), F32)
    one = lambda w: jnp.ones((seq, w), F32)
    cr, sr = _rope_table(t // GRID_W, HEAD_DIM // 2, A_ROPE_THETA)
    cc, sc = _rope_table(t % GRID_W, HEAD_DIM // 2, A_ROPE_THETA)
    tab_a = (jnp.concatenate([cr, cr, cc, cc], 1),
             jnp.concatenate([-sr, z(32), -sc, z(32)], 1),
             jnp.concatenate([z(32), sr, z(32), sc], 1))
    cp, sp = _rope_table(t, PARTIAL_ROPE_DIM, PARTIAL_ROPE_THETA)
    tab_b = (jnp.concatenate([cp, cp, one(96)], 1),
             jnp.concatenate([-sp, z(112)], 1),
             jnp.concatenate([z(16), sp, z(96)], 1))
    cm, sm = _rope_table(t, C_ROPE_DIM, C_ROPE_THETA)
    tab_c = (jnp.concatenate([cm, cm, one(64)], 1),
             jnp.concatenate([-sm, z(96)], 1),
             jnp.concatenate([z(32), sm, z(64)], 1))
    return tab_a, tab_b, tab_c


def _prep_w_ckr(w):
    ckr = w[:, IN_N_SRC * IN_WBLK:]
    return jnp.pad(ckr, ((0, 0), (0, IN_WBLK - ckr.shape[1])))


def _prep_w_uq(w):
    w = w.reshape(C_Q_RANK, C_HEADS, C_NOPE_DIM + C_ROPE_DIM)
    w = jnp.pad(w, ((0, 0), (0, 0), (0, C_QK_PAD - C_NOPE_DIM - C_ROPE_DIM)))
    return w.reshape(C_Q_RANK, C_HEADS * C_QK_PAD).astype(BF16)


def _prep_w_ukv(w):
    w = w.reshape(C_KV_RANK, C_HEADS, C_NOPE_DIM + C_V_DIM)
    kn = w[:, :, :C_NOPE_DIM].reshape(C_KV_RANK, C_HEADS * C_NOPE_DIM)
    v = w[:, :, C_NOPE_DIM:].reshape(C_KV_RANK, C_HEADS * C_V_DIM)
    return jnp.concatenate([kn, v], axis=1).astype(BF16)


def kernel(x, attn_norm, w_in, a_q_norm, a_k_norm, c_q_norm, c_kv_norm, w_uq, w_ukv, out_norm, w_out,
           ffn_norm, w_up, conv_w, conv_b, w_down, final_norm):
    batch, seq, d = x.shape
    depth = w_in.shape[0]
    tab_a, tab_b, tab_c = _lane_tables(seq)
    row = lambda v: v.reshape(1, -1)

    xr = x.reshape(batch * seq, d)
    xn = _rmsnorm_rows(xr, row(attn_norm[0]))
    for l in range(depth):
        proj = _in_proj(xn, w_in[l], _prep_w_ckr(w_in[l]))
        qc, kvc = _latent_up(proj, row(c_q_norm[l]), row(c_kv_norm[l]), _prep_w_uq(w_uq[l]),
                             _prep_w_ukv(w_ukv[l]))
        ya = _attn_a(proj, tab_a, row(a_q_norm[l]), row(a_k_norm[l]), batch, seq)
        yb = _attn_b(proj, tab_b, batch, seq)
        yc = _attn_c(qc, kvc, proj, tab_c, batch, seq)
        g = out_norm[l]
        x1, xn1 = _out_proj(ya, yb, yc, row(g[:A_WIDTH]), row(g[A_WIDTH:A_WIDTH + B_WIDTH]),
                            row(g[A_WIDTH + B_WIDTH:]), w_out[l].astype(BF16), xr, row(ffn_norm[l]))
        act = _ffn_up(xn1, w_up[l], conv_w[l], row(conv_b[l]), batch, seq)
        last = l == depth - 1
        g_next = final_norm if last else attn_norm[l + 1]
        xr, xn = _ffn_down(act, w_down[l].astype(BF16), x1, row(g_next), F32 if last else BF16)
    return xn.reshape(batch, seq, d)
```

```python
import functools
import math

import jax
import jax.numpy as jnp
from jax import lax
from jax.experimental import pallas as pl
from jax.experimental.pallas import tpu as pltpu

D_MODEL = 2048
HEAD_DIM = 128
A_HEADS = 6
A_KV_HEADS = 2
A_GROUP = A_HEADS // A_KV_HEADS
A_ROPE_THETA = 10000.0
B_HEADS = 4
B_PATTERNS = ((128, 1), (512, 4), (2048, 16))
B_N_GROUPS = 3
C_HEADS = 6
C_Q_RANK = 512
C_KV_RANK = 512
C_NOPE_DIM = 128
C_ROPE_DIM = 64
C_V_DIM = 128
C_ROPE_THETA = 10000.0
PARTIAL_ROPE_DIM = HEAD_DIM // 4
PARTIAL_ROPE_THETA = 500000.0
GRID_W = 64
D_FF = 5632
EPS = 1e-6

A_WIDTH = A_HEADS * HEAD_DIM
B_WIDTH = B_HEADS * HEAD_DIM
C_WIDTH = C_HEADS * C_V_DIM
C_QK_PAD = 256

LANES = 128
LOG2E = math.log2(math.e)
F32 = jnp.float32
BF16 = jnp.bfloat16

PROJ_WIDTH = 5120
CB_CQ, CB_CKV = 0, 1
CB_AQ, CB_AK, CB_AV = 8, 14, 16
CB_BQ, CB_BK, CB_BV = 18, 30, 34
CB_CKR = 38


def _params(semantics, vmem_mib):
    return pltpu.CompilerParams(dimension_semantics=semantics, vmem_limit_bytes=vmem_mib * 1024 * 1024)


def _rms(xf, g):
    return xf * lax.rsqrt(jnp.mean(xf * xf, axis=-1, keepdims=True) + EPS) * g


def _rope(xf, cos, sin_lo, sin_hi, shift):
    return xf * cos + pltpu.roll(xf, LANES - shift, 1) * sin_lo + pltpu.roll(xf, shift, 1) * sin_hi


def _dot(a, b):
    return jnp.dot(a, b, preferred_element_type=F32)


def _dot_nt(a, b):
    return lax.dot_general(a, b, (((1,), (1,)), ((), ())), preferred_element_type=F32)


def _norm_kernel(x_ref, g_ref, o_ref):
    o_ref[...] = _rms(x_ref[...], g_ref[...]).astype(o_ref.dtype)


def _rmsnorm_rows(x, g, tm=512):
    m, d = x.shape
    return pl.pallas_call(
        _norm_kernel,
        grid=(m // tm,),
        in_specs=[pl.BlockSpec((tm, d), lambda i: (i, 0)), pl.BlockSpec((1, d), lambda i: (0, 0))],
        out_specs=pl.BlockSpec((tm, d), lambda i: (i, 0)),
        out_shape=jax.ShapeDtypeStruct((m, d), BF16),
        compiler_params=_params(("parallel",), 40),
        name="rmsnorm",
    )(x, g)


IN_WBLK = 256
IN_TN = 1024
IN_SRC_CQ = 15
IN_N_SRC = 19


def _in_proj_kernel(a_ref, *refs):
    w_refs, wck_ref, o_ref, w_bf = refs[:-3], refs[-3], refs[-2], refs[-1]
    j = pl.program_id(0)
    last = pl.num_programs(0) - 1

    @pl.when(pl.program_id(1) == 0)
    def _():
        for q, w_ref in enumerate(w_refs[:-1]):
            w_bf[:, q * IN_WBLK:(q + 1) * IN_WBLK] = w_ref[...].astype(BF16)
        tail = slice((len(w_refs) - 1) * IN_WBLK, len(w_refs) * IN_WBLK)

        @pl.when(j < last)
        def _():
            w_bf[:, tail] = w_refs[-1][...].astype(BF16)

        @pl.when(j == last)
        def _():
            w_bf[:, tail] = wck_ref[...].astype(BF16)

    o_ref[...] = _dot(a_ref[...], w_bf[...]).astype(o_ref.dtype)


def _in_proj(a, w_in, layer, w_ckr, tm=1024):
    m, k = a.shape
    per_step = IN_TN // IN_WBLK

    def w_spec(q):
        def index(j, i):
            n = j * per_step + q
            return layer, 0, jnp.where(n < 4, n + IN_SRC_CQ, n - 4)
        return pl.BlockSpec((None, k, IN_WBLK), index)

    return pl.pallas_call(
        _in_proj_kernel,
        grid=(PROJ_WIDTH // IN_TN, m // tm),
        in_specs=[pl.BlockSpec((tm, k), lambda j, i: (i, 0))] + [w_spec(q) for q in range(per_step)]
        + [pl.BlockSpec((k, IN_WBLK), lambda j, i: (0, 0))],
        out_specs=pl.BlockSpec((tm, IN_TN), lambda j, i: (i, j)),
        out_shape=jax.ShapeDtypeStruct((m, PROJ_WIDTH), BF16),
        scratch_shapes=[pltpu.VMEM((k, IN_TN), BF16)],
        compiler_params=_params(("parallel", "arbitrary"), 48),
        name="in_proj",
    )(a, *([w_in] * per_step), w_ckr)


def _latent_up_kernel(cq_ref, ckv_ref, gq_ref, gkv_ref, wq_ref, wkv_ref, q_ref, kv_ref, *, q_scale):
    cq = _rms(cq_ref[...].astype(F32), gq_ref[...]).astype(BF16)
    q_ref[...] = (_dot(cq, wq_ref[...]) * q_scale).astype(q_ref.dtype)
    ckv = _rms(ckv_ref[...].astype(F32), gkv_ref[...]).astype(BF16)
    kv_ref[...] = _dot(ckv, wkv_ref[...]).astype(kv_ref.dtype)


def _latent_up(proj, gq, gkv, wq, wkv, tm=1024):
    m = proj.shape[0]
    nq, nkv = wq.shape[1], wkv.shape[1]
    q_scale = (C_NOPE_DIM + C_ROPE_DIM) ** -0.5 * LOG2E
    return pl.pallas_call(
        functools.partial(_latent_up_kernel, q_scale=q_scale),
        grid=(m // tm,),
        in_specs=[
            pl.BlockSpec((tm, C_Q_RANK), lambda i: (i, CB_CQ)),
            pl.BlockSpec((tm, C_KV_RANK), lambda i: (i, CB_CKV)),
            pl.BlockSpec((1, C_Q_RANK), lambda i: (0, 0)),
            pl.BlockSpec((1, C_KV_RANK), lambda i: (0, 0)),
            pl.BlockSpec((C_Q_RANK, nq), lambda i: (0, 0)),
            pl.BlockSpec((C_KV_RANK, nkv), lambda i: (0, 0)),
        ],
        out_specs=[pl.BlockSpec((tm, nq), lambda i: (i, 0)), pl.BlockSpec((tm, nkv), lambda i: (i, 0))],
        out_shape=[jax.ShapeDtypeStruct((m, nq), BF16), jax.ShapeDtypeStruct((m, nkv), BF16)],
        compiler_params=_params(("parallel",), 48),
        name="latent_up",
    )(proj, proj, gq, gkv, wq, wkv)


A_ROW_CHUNK = 128
C_ROW_CHUNK = 256


def _attend(q_all, k_ref, v1_ref, dv, chunk):
    outs = []
    for c in range(q_all.shape[0] // chunk):
        q = q_all[c * chunk:(c + 1) * chunk]
        s = _dot_nt(q, k_ref[...])
        p = jnp.exp2(s - jnp.max(s, axis=-1, keepdims=True))
        ol = _dot(p.astype(BF16), v1_ref[...])
        outs.append(ol[:, :dv] * (1.0 / ol[:, dv:]))
    return outs


def _attn_a_kernel(q0_ref, q1_ref, q2_ref, k_ref, v_ref, cos_ref, slo_ref, shi_ref, gq_ref, gk_ref,
                   o_ref, kbuf, v1buf, *, tq):
    qi = pl.program_id(2)

    @pl.when(qi == 0)
    def _():
        k = _rms(k_ref[...].astype(F32), gk_ref[...])
        kbuf[...] = _rope(k, cos_ref[...], slo_ref[...], shi_ref[...], 32).astype(BF16)
        v1buf[:, :HEAD_DIM] = v_ref[...]
        v1buf[:, HEAD_DIM:] = jnp.ones((v1buf.shape[0], HEAD_DIM), BF16)

    rows = pl.ds(pl.multiple_of(qi * tq, tq), tq)
    cos, slo, shi = cos_ref[rows, :], slo_ref[rows, :], shi_ref[rows, :]
    q_scale = HEAD_DIM ** -0.5 * LOG2E
    qs = []
    for q_ref in (q0_ref, q1_ref, q2_ref):
        q = _rms(q_ref[...].astype(F32), gq_ref[...])
        qs.append((_rope(q, cos, slo, shi, 32) * q_scale).astype(BF16))
    q_all = jnp.concatenate(qs, axis=0)
    per_head = tq // A_ROW_CHUNK
    for c, o in enumerate(_attend(q_all, kbuf, v1buf, HEAD_DIM, A_ROW_CHUNK)):
        g, r = divmod(c, per_head)
        o_ref[r * A_ROW_CHUNK:(r + 1) * A_ROW_CHUNK, g * HEAD_DIM:(g + 1) * HEAD_DIM] = o.astype(o_ref.dtype)


def _attn_a(proj, tabs, gq, gk, batch, seq, tq=1024):
    nq = seq // tq
    cos, slo, shi = tabs

    def q_spec(g):
        return pl.BlockSpec((tq, HEAD_DIM), lambda b, h, qi: (b * nq + qi, CB_AQ + h * A_GROUP + g))

    tab_spec = pl.BlockSpec((seq, LANES), lambda b, h, qi: (0, 0))
    gain_spec = pl.BlockSpec((1, HEAD_DIM), lambda b, h, qi: (0, 0))
    return pl.pallas_call(
        functools.partial(_attn_a_kernel, tq=tq),
        grid=(batch, A_KV_HEADS, nq),
        in_specs=[
            q_spec(0), q_spec(1), q_spec(2),
            pl.BlockSpec((seq, HEAD_DIM), lambda b, h, qi: (b, CB_AK + h)),
            pl.BlockSpec((seq, HEAD_DIM), lambda b, h, qi: (b, CB_AV + h)),
            tab_spec, tab_spec, tab_spec, gain_spec, gain_spec,
        ],
        out_specs=pl.BlockSpec((tq, A_GROUP * HEAD_DIM), lambda b, h, qi: (b * nq + qi, h)),
        out_shape=jax.ShapeDtypeStruct((batch * seq, A_WIDTH), BF16),
        scratch_shapes=[pltpu.VMEM((seq, HEAD_DIM), BF16), pltpu.VMEM((seq, 2 * HEAD_DIM), BF16)],
        compiler_params=_params(("parallel", "parallel", "arbitrary"), 48),
        name="attn_a",
    )(proj, proj, proj, proj, proj, cos, slo, shi, gq, gk)


def _attn_c_kernel(q_ref, kn_ref, v_ref, kr_ref, cos_ref, slo_ref, shi_ref, o_ref, kbuf, v1buf, *, tq):
    qi = pl.program_id(2)

    @pl.when(qi == 0)
    def _():
        kbuf[:, :C_NOPE_DIM] = kn_ref[...]
        kr = _rope(kr_ref[...].astype(F32), cos_ref[...], slo_ref[...], shi_ref[...], 32)
        kbuf[:, C_NOPE_DIM:] = kr.astype(BF16)
        v1buf[:, :C_V_DIM] = v_ref[...]
        v1buf[:, C_V_DIM:] = jnp.ones((v1buf.shape[0], C_V_DIM), BF16)

    rows = pl.ds(pl.multiple_of(qi * tq, tq), tq)
    q_rope = _rope(q_ref[:, C_NOPE_DIM:].astype(F32), cos_ref[rows, :], slo_ref[rows, :], shi_ref[rows, :], 32)
    q_all = jnp.concatenate([q_ref[:, :C_NOPE_DIM], q_rope.astype(BF16)], axis=1)
    for c, o in enumerate(_attend(q_all, kbuf, v1buf, C_V_DIM, C_ROW_CHUNK)):
        o_ref[c * C_ROW_CHUNK:(c + 1) * C_ROW_CHUNK, :] = o.astype(o_ref.dtype)


def _attn_c(qc, kvc, proj, tabs, batch, seq, tq=2048):
    nq = seq // tq
    cos, slo, shi = tabs
    tab_spec = pl.BlockSpec((seq, LANES), lambda b, h, qi: (0, 0))
    return pl.pallas_call(
        functools.partial(_attn_c_kernel, tq=tq),
        grid=(batch, C_HEADS, nq),
        in_specs=[
            pl.BlockSpec((tq, C_QK_PAD), lambda b, h, qi: (b * nq + qi, h)),
            pl.BlockSpec((seq, C_NOPE_DIM), lambda b, h, qi: (b, h)),
            pl.BlockSpec((seq, C_V_DIM), lambda b, h, qi: (b, C_HEADS + h)),
            pl.BlockSpec((seq, LANES), lambda b, h, qi: (b, CB_CKR)),
            tab_spec, tab_spec, tab_spec,
        ],
        out_specs=pl.BlockSpec((tq, C_V_DIM), lambda b, h, qi: (b * nq + qi, h)),
        out_shape=jax.ShapeDtypeStruct((batch * seq, C_WIDTH), BF16),
        scratch_shapes=[pltpu.VMEM((seq, C_QK_PAD), BF16), pltpu.VMEM((seq, 2 * C_V_DIM), BF16)],
        compiler_params=_params(("parallel", "parallel", "arbitrary"), 48),
        name="attn_c",
    )(qc, kvc, kvc, proj, cos, slo, shi)


B_QBLK = 128
B_PREP_ROWS = 256
B_UNROLL = 8


def _attn_b_kernel(q0_ref, q1_ref, q2_ref, k_ref, v_ref, cos_ref, slo_ref, shi_ref, o_ref,
                   q0f, q1f, q2f, kf, vf, m_s, l_s, acc_s, *, seq):
    q_refs = (q0_ref, q1_ref, q2_ref)
    q_bufs = (q0f, q1f, q2f)
    q_scale = HEAD_DIM ** -0.5 * LOG2E

    def prep(ci, carry):
        rows = pl.ds(pl.multiple_of(ci * B_PREP_ROWS, B_PREP_ROWS), B_PREP_ROWS)
        cos, slo, shi = cos_ref[rows, :], slo_ref[rows, :], shi_ref[rows, :]
        for q_ref, q_buf in zip(q_refs, q_bufs):
            q_buf[rows, :] = _rope(q_ref[rows, :].astype(F32), cos, slo, shi, 16) * q_scale
        kf[rows, :] = _rope(k_ref[rows, :].astype(F32), cos, slo, shi, 16)
        vf[rows, :] = v_ref[rows, :].astype(F32)
        return carry

    lax.fori_loop(0, seq // B_PREP_ROWS, prep, 0)

    for g, (window, dil) in reversed(list(enumerate(B_PATTERNS))):
        first = g == len(B_PATTERNS) - 1
        half = window // (2 * dil)
        length = seq // dil
        nblk = length // B_QBLK
        kwin = min(2 * B_QBLK, length)
        q_buf = q_bufs[g]

        def rows_of(start, size, dil=dil):
            return pl.ds(start, size) if dil == 1 else pl.ds(start, size, stride=dil)

        def block(n, carry, first=first, half=half, length=length, nblk=nblk, kwin=kwin, q_buf=q_buf, dil=dil,
                  rows_of=rows_of):
            r = n // nblk
            i = n % nblk
            k0 = jnp.clip(i * B_QBLK - half, 0, length - kwin)
            q_rows = rows_of(r + dil * B_QBLK * i, B_QBLK)
            k_rows = rows_of(r + dil * k0, kwin)
            q = q_buf[q_rows, :].astype(BF16)
            k = kf[k_rows, :].astype(BF16)
            v = vf[k_rows, :].astype(BF16)
            s = _dot_nt(q, k)
            qpos = i * B_QBLK + lax.broadcasted_iota(jnp.int32, (B_QBLK, kwin), 0)
            kpos = k0 + lax.broadcasted_iota(jnp.int32, (B_QBLK, kwin), 1)
            s = jnp.where(jnp.abs(kpos - qpos) <= half, s, -jnp.inf)
            m_b = jnp.max(s, axis=-1, keepdims=True)
            p = jnp.exp2(s - m_b)
            l_b = jnp.sum(p, axis=-1, keepdims=True)
            a_b = _dot(p.astype(BF16), v)
            full = (B_QBLK, HEAD_DIM)
            if first:
                m_s[q_rows, :] = jnp.broadcast_to(m_b, full)
                l_s[q_rows, :] = jnp.broadcast_to(l_b, full)
                acc_s[q_rows, :] = a_b
            else:
                m_o = m_s[q_rows, :]
                m_n = jnp.maximum(m_o, m_b)
                w_o = jnp.exp2(m_o - m_n)
                w_b = jnp.exp2(m_b - m_n)
                acc_s[q_rows, :] = acc_s[q_rows, :] * w_o + a_b * w_b
                l_s[q_rows, :] = l_s[q_rows, :] * w_o + l_b * w_b
                m_s[q_rows, :] = m_n
            return carry

        lax.fori_loop(0, seq // B_QBLK, block, 0, unroll=B_UNROLL)

    def finish(ci, carry):
        rows = pl.ds(pl.multiple_of(ci * B_PREP_ROWS, B_PREP_ROWS), B_PREP_ROWS)
        o_ref[rows, :] = (acc_s[rows, :] * (1.0 / l_s[rows, :])).astype(o_ref.dtype)
        return carry

    lax.fori_loop(0, seq // B_PREP_ROWS, finish, 0)


def _attn_b(proj, tabs, batch, seq):
    cos, slo, shi = tabs

    def q_spec(g):
        return pl.BlockSpec((seq, HEAD_DIM), lambda b, h: (b, CB_BQ + g * B_HEADS + h))

    tab_spec = pl.BlockSpec((seq, LANES), lambda b, h: (0, 0))
    slab = pltpu.VMEM((seq, HEAD_DIM), F32)
    return pl.pallas_call(
        functools.partial(_attn_b_kernel, seq=seq),
        grid=(batch, B_HEADS),
        in_specs=[
            q_spec(0), q_spec(1), q_spec(2),
            pl.BlockSpec((seq, HEAD_DIM), lambda b, h: (b, CB_BK + h)),
            pl.BlockSpec((seq, HEAD_DIM), lambda b, h: (b, CB_BV + h)),
            tab_spec, tab_spec, tab_spec,
        ],
        out_specs=pl.BlockSpec((seq, HEAD_DIM), lambda b, h: (b, h)),
        out_shape=jax.ShapeDtypeStruct((batch * seq, B_WIDTH), BF16),
        scratch_shapes=[slab] * 8,
        compiler_params=_params(("parallel", "parallel"), 48),
        name="attn_b",
    )(proj, proj, proj, proj, proj, cos, slo, shi)


def _out_kernel(ya_ref, yb_ref, yc_ref, ga_ref, gb_ref, gc_ref, w_ref, x_ref, gf_ref, x1_ref, xn_ref):
    y = jnp.concatenate([
        _rms(ya_ref[...].astype(F32), ga_ref[...]).astype(BF16),
        _rms(yb_ref[...].astype(F32), gb_ref[...]).astype(BF16),
        _rms(yc_ref[...].astype(F32), gc_ref[...]).astype(BF16),
    ], axis=1)
    x1 = x_ref[...] + _dot(y, w_ref[...])
    x1_ref[...] = x1
    xn_ref[...] = _rms(x1, gf_ref[...]).astype(xn_ref.dtype)


def _out_proj(ya, yb, yc, ga, gb, gc, w, x, gf, tm=512):
    m, d = x.shape

    def rows(width):
        return pl.BlockSpec((tm, width), lambda i: (i, 0))

    def const(r, c):
        return pl.BlockSpec((r, c), lambda i: (0, 0))

    return pl.pallas_call(
        _out_kernel,
        grid=(m // tm,),
        in_specs=[rows(A_WIDTH), rows(B_WIDTH), rows(C_WIDTH), const(1, A_WIDTH), const(1, B_WIDTH),
                  const(1, C_WIDTH), const(d, d), rows(d), const(1, d)],
        out_specs=[rows(d), rows(d)],
        out_shape=[jax.ShapeDtypeStruct((m, d), F32), jax.ShapeDtypeStruct((m, d), BF16)],
        compiler_params=_params(("parallel",), 56),
        name="out_proj",
    )(ya, yb, yc, ga, gb, gc, w, x, gf)


FFN_SUBTILE = 256
FFN_ROWS = 256
FFN_GATE_ROWS = 64
FFN_GUARD = 8

def _ffn_up_kernel(xn_ref, wg_ref, wu_ref, cg_ref, cu_ref, bg_ref, bu_ref, o_ref, hg, hu, wg_bf, wu_bf, *, seq):
    tn = o_ref.shape[1]

    @pl.when(pl.program_id(1) == 0)
    def _():
        wg_bf[...] = wg_ref[...].astype(BF16)
        wu_bf[...] = wu_ref[...].astype(BF16)

    zeros = jnp.zeros((FFN_GUARD, tn), F32)
    for h in (hg, hu):
        h[0:FFN_GUARD, :] = zeros
        h[FFN_GUARD + seq:2 * FFN_GUARD + seq, :] = zeros

    def matmuls(c, r):
        cols = slice(c * FFN_SUBTILE, (c + 1) * FFN_SUBTILE)
        xr = xn_ref[r * FFN_ROWS:(r + 1) * FFN_ROWS, :]
        dst = slice(FFN_GUARD + r * FFN_ROWS, FFN_GUARD + (r + 1) * FFN_ROWS)
        hg[dst, cols] = _dot(xr, wg_bf[:, cols])
        hu[dst, cols] = _dot(xr, wu_bf[:, cols])

    def gate_stage(c, r):
        cols = slice(c * FFN_SUBTILE, (c + 1) * FFN_SUBTILE)
        base = FFN_GUARD + r * FFN_ROWS

        def conv(h, c_ref, b_ref):
            taps = [h[base + t - 1:base + t - 1 + FFN_ROWS, cols] * c_ref[t:t + 1, cols] for t in range(3)]
            return taps[0] + taps[1] + taps[2] + b_ref[:, cols]

        gate = conv(hg, cg_ref, bg_ref)
        up = conv(hu, cu_ref, bu_ref)
        o_ref[r * FFN_ROWS:(r + 1) * FFN_ROWS, cols] = (
            gate * (1.0 / (1.0 + jnp.exp(-gate))) * up).astype(o_ref.dtype)

    units = [(c, r) for c in range(tn // FFN_SUBTILE) for r in range(seq // FFN_ROWS)]
    for i, unit in enumerate(units):
        matmuls(*unit)
        if i > 0:
            gate_stage(*units[i - 1])
    gate_stage(*units[-1])


def _ffn_up(xn, w_up, layer, conv_w, conv_b, batch, seq, tn=512):
    d = xn.shape[1]
    nj = D_FF // tn
    return pl.pallas_call(
        functools.partial(_ffn_up_kernel, seq=seq),
        grid=(nj, batch),
        in_specs=[
            pl.BlockSpec((seq, d), lambda j, b: (b, 0)),
            pl.BlockSpec((None, d, tn), lambda j, b: (layer, 0, j)),
            pl.BlockSpec((None, d, tn), lambda j, b: (layer, 0, nj + j)),
            pl.BlockSpec((3, tn), lambda j, b: (0, j)),
            pl.BlockSpec((3, tn), lambda j, b: (0, nj + j)),
            pl.BlockSpec((1, tn), lambda j, b: (0, j)),
            pl.BlockSpec((1, tn), lambda j, b: (0, nj + j)),
        ],
        out_specs=pl.BlockSpec((seq, tn), lambda j, b: (b, j)),
        out_shape=jax.ShapeDtypeStruct((batch * seq, D_FF), BF16),
        scratch_shapes=[pltpu.VMEM((seq + 2 * FFN_GUARD, tn), F32)] * 2 + [pltpu.VMEM((d, tn), BF16)] * 2,
        compiler_params=_params(("parallel", "arbitrary"), 60),
        name="ffn_up",
    )(xn, w_up, w_up, conv_w, conv_w, conv_b, conv_b)


def _ffn_down_kernel(a_ref, w_ref, x1_ref, g_ref, x2_ref, xn_ref):
    k = pl.program_id(1)

    @pl.when(k == 0)
    def _():
        x2_ref[...] = x1_ref[...]

    x2_ref[...] += _dot(a_ref[...], w_ref[...])

    @pl.when(k == pl.num_programs(1) - 1)
    def _():
        xn_ref[...] = _rms(x2_ref[...], g_ref[...]).astype(xn_ref.dtype)


def _ffn_down_final_kernel(a_ref, w_ref, x1_ref, g_ref, xn_ref, acc_ref):
    _ffn_down_kernel(a_ref, w_ref, x1_ref, g_ref, acc_ref, xn_ref)


def _ffn_down(act, w, x1, g, final, tm=1024, tk=512):
    m, kdim = act.shape
    d = w.shape[1]
    tile = pl.BlockSpec((tm, d), lambda i, k: (i, 0))
    in_specs = [
        pl.BlockSpec((tm, tk), lambda i, k: (i, k)),
        pl.BlockSpec((tk, d), lambda i, k: (k, 0)),
        tile,
        pl.BlockSpec((1, d), lambda i, k: (0, 0)),
    ]
    common = dict(grid=(m // tm, kdim // tk), in_specs=in_specs,
                  compiler_params=_params(("parallel", "arbitrary"), 56))
    if final:
        xn = pl.pallas_call(_ffn_down_final_kernel, out_specs=tile, out_shape=jax.ShapeDtypeStruct((m, d), F32),
                            scratch_shapes=[pltpu.VMEM((tm, d), F32)], name="ffn_down_final", **common)(act, w, x1, g)
        return None, xn
    return pl.pallas_call(_ffn_down_kernel, out_specs=[tile, tile],
                          out_shape=[jax.ShapeDtypeStruct((m, d), F32), jax.ShapeDtypeStruct((m, d), BF16)],
                          name="ffn_down", **common)(act, w, x1, g)


def _rope_table(pos, dim, theta):
    inv = theta ** (-jnp.arange(0, dim, 2, dtype=F32) / dim)
    ang = pos.astype(F32)[:, None] * inv[None, :]
    return jnp.cos(ang), jnp.sin(ang)


def _lane_tables(seq):
    t = jnp.arange(seq)
    z = lambda w: jnp.zeros((seq, w), F32)
    one = lambda w: jnp.ones((seq, w), F32)
    cr, sr = _rope_table(t // GRID_W, HEAD_DIM // 2, A_ROPE_THETA)
    cc, sc = _rope_table(t % GRID_W, HEAD_DIM // 2, A_ROPE_THETA)
    tab_a = (jnp.concatenate([cr, cr, cc, cc], 1),
             jnp.concatenate([-sr, z(32), -sc, z(32)], 1),
             jnp.concatenate([z(32), sr, z(32), sc], 1))
    cp, sp = _rope_table(t, PARTIAL_ROPE_DIM, PARTIAL_ROPE_THETA)
    tab_b = (jnp.concatenate([cp, cp, one(96)], 1),
             jnp.concatenate([-sp, z(112)], 1),
             jnp.concatenate([z(16), sp, z(96)], 1))
    cm, sm = _rope_table(t, C_ROPE_DIM, C_ROPE_THETA)
    tab_c = (jnp.concatenate([cm, cm, one(64)], 1),
             jnp.concatenate([-sm, z(96)], 1),
             jnp.concatenate([z(32), sm, z(64)], 1))
    return tab_a, tab_b, tab_c


def _prep_w_ckr(w):
    ckr = w[:, IN_N_SRC * IN_WBLK:]
    return jnp.pad(ckr, ((0, 0), (0, IN_WBLK - ckr.shape[1])))


def _prep_w_uq(w):
    w = w.reshape(C_Q_RANK, C_HEADS, C_NOPE_DIM + C_ROPE_DIM)
    w = jnp.pad(w, ((0, 0), (0, 0), (0, C_QK_PAD - C_NOPE_DIM - C_ROPE_DIM)))
    return w.reshape(C_Q_RANK, C_HEADS * C_QK_PAD).astype(BF16)


def _prep_w_ukv(w):
    w = w.reshape(C_KV_RANK, C_HEADS, C_NOPE_DIM + C_V_DIM)
    kn = w[:, :, :C_NOPE_DIM].reshape(C_KV_RANK, C_HEADS * C_NOPE_DIM)
    v = w[:, :, C_NOPE_DIM:].reshape(C_KV_RANK, C_HEADS * C_V_DIM)
    return jnp.concatenate([kn, v], axis=1).astype(BF16)


def kernel(x, attn_norm, w_in, a_q_norm, a_k_norm, c_q_norm, c_kv_norm, w_uq, w_ukv, out_norm, w_out,
           ffn_norm, w_up, conv_w, conv_b, w_down, final_norm):
    batch, seq, d = x.shape
    depth = w_in.shape[0]
    tab_a, tab_b, tab_c = _lane_tables(seq)
    row = lambda v: v.reshape(1, -1)

    xr = x.reshape(batch * seq, d)
    xn = _rmsnorm_rows(xr, row(attn_norm[0]))
    for l in range(depth):
        proj = _in_proj(xn, w_in, l, _prep_w_ckr(w_in[l]))
        qc, kvc = _latent_up(proj, row(c_q_norm[l]), row(c_kv_norm[l]), _prep_w_uq(w_uq[l]),
                             _prep_w_ukv(w_ukv[l]))
        ya = _attn_a(proj, tab_a, row(a_q_norm[l]), row(a_k_norm[l]), batch, seq)
        yb = _attn_b(proj, tab_b, batch, seq)
        yc = _attn_c(qc, kvc, proj, tab_c, batch, seq)
        g = out_norm[l]
        x1, xn1 = _out_proj(ya, yb, yc, row(g[:A_WIDTH]), row(g[A_WIDTH:A_WIDTH + B_WIDTH]),
                            row(g[A_WIDTH + B_WIDTH:]), w_out[l].astype(BF16), xr, row(ffn_norm[l]))
        act = _ffn_up(xn1, w_up, l, conv_w[l], row(conv_b[l]), batch, seq)
        last = l == depth - 1
        g_next = final_norm if last else attn_norm[l + 1]
        xr, xn = _ffn_down(act, w_down[l].astype(BF16), x1, row(g_next), last)
    return xn.reshape(batch, seq, d)
```

```python
import functools
import math

import jax
import jax.numpy as jnp
from jax import lax
from jax.experimental import pallas as pl
from jax.experimental.pallas import tpu as pltpu

D_MODEL = 2048
HEAD_DIM = 128
A_HEADS = 6
A_KV_HEADS = 2
A_GROUP = A_HEADS // A_KV_HEADS
A_ROPE_THETA = 10000.0
B_HEADS = 4
B_PATTERNS = ((128, 1), (512, 4), (2048, 16))
B_N_GROUPS = 3
C_HEADS = 6
C_Q_RANK = 512
C_KV_RANK = 512
C_NOPE_DIM = 128
C_ROPE_DIM = 64
C_V_DIM = 128
C_ROPE_THETA = 10000.0
PARTIAL_ROPE_DIM = HEAD_DIM // 4
PARTIAL_ROPE_THETA = 500000.0
GRID_W = 64
D_FF = 5632
EPS = 1e-6

A_WIDTH = A_HEADS * HEAD_DIM
B_WIDTH = B_HEADS * HEAD_DIM
C_WIDTH = C_HEADS * C_V_DIM
C_QK_PAD = 256

LANES = 128
LOG2E = math.log2(math.e)
F32 = jnp.float32
BF16 = jnp.bfloat16

PROJ_WIDTH = 5120
CB_CQ, CB_CKV = 0, 1
CB_AQ, CB_AK, CB_AV = 8, 14, 16
CB_BQ, CB_BK, CB_BV = 18, 30, 34
CB_CKR = 38


def _params(semantics, vmem_mib):
    return pltpu.CompilerParams(dimension_semantics=semantics, vmem_limit_bytes=vmem_mib * 1024 * 1024)


def _rms(xf, g):
    return xf * lax.rsqrt(jnp.mean(xf * xf, axis=-1, keepdims=True) + EPS) * g


def _rope(xf, cos, sin_lo, sin_hi, shift):
    return xf * cos + pltpu.roll(xf, LANES - shift, 1) * sin_lo + pltpu.roll(xf, shift, 1) * sin_hi


def _dot(a, b):
    return jnp.dot(a, b, preferred_element_type=F32)


def _dot_nt(a, b):
    return lax.dot_general(a, b, (((1,), (1,)), ((), ())), preferred_element_type=F32)


def _norm_kernel(x_ref, g_ref, o_ref):
    o_ref[...] = _rms(x_ref[...], g_ref[...]).astype(o_ref.dtype)


def _rmsnorm_rows(x, g, tm=512):
    m, d = x.shape
    return pl.pallas_call(
        _norm_kernel,
        grid=(m // tm,),
        in_specs=[pl.BlockSpec((tm, d), lambda i: (i, 0)), pl.BlockSpec((1, d), lambda i: (0, 0))],
        out_specs=pl.BlockSpec((tm, d), lambda i: (i, 0)),
        out_shape=jax.ShapeDtypeStruct((m, d), BF16),
        compiler_params=_params(("parallel",), 40),
        name="rmsnorm",
    )(x, g)


IN_WBLK = 256
IN_TN = 1024
IN_SRC_CQ = 15
IN_N_SRC = 19


def _in_proj_kernel(a_ref, *refs):
    w_refs, wck_ref, o_ref, w_bf = refs[:-3], refs[-3], refs[-2], refs[-1]
    j = pl.program_id(0)
    last = pl.num_programs(0) - 1

    @pl.when(pl.program_id(1) == 0)
    def _():
        for q, w_ref in enumerate(w_refs[:-1]):
            w_bf[q * IN_WBLK:(q + 1) * IN_WBLK, :] = w_ref[...].astype(BF16)
        tail = slice((len(w_refs) - 1) * IN_WBLK, len(w_refs) * IN_WBLK)

        @pl.when(j < last)
        def _():
            w_bf[tail, :] = w_refs[-1][...].astype(BF16)

        @pl.when(j == last)
        def _():
            w_bf[tail, :] = wck_ref[...].astype(BF16)

    o_ref[...] = _dot_nt(a_ref[...], w_bf[...]).astype(o_ref.dtype)


def _in_proj(a, w_in_t, layer, w_ckr_t, tm=1024):
    m, k = a.shape
    per_step = IN_TN // IN_WBLK

    def w_spec(q):
        def index(j, i):
            n = j * per_step + q
            return layer, jnp.where(n < 4, n + IN_SRC_CQ, n - 4), 0
        return pl.BlockSpec((None, IN_WBLK, k), index)

    return pl.pallas_call(
        _in_proj_kernel,
        grid=(PROJ_WIDTH // IN_TN, m // tm),
        in_specs=[pl.BlockSpec((tm, k), lambda j, i: (i, 0))] + [w_spec(q) for q in range(per_step)]
        + [pl.BlockSpec((IN_WBLK, k), lambda j, i: (0, 0))],
        out_specs=pl.BlockSpec((tm, IN_TN), lambda j, i: (i, j)),
        out_shape=jax.ShapeDtypeStruct((m, PROJ_WIDTH), BF16),
        scratch_shapes=[pltpu.VMEM((IN_TN, k), BF16)],
        compiler_params=_params(("parallel", "arbitrary"), 48),
        name="in_proj",
    )(a, *([w_in_t] * per_step), w_ckr_t)


def _latent_up_kernel(cq_ref, ckv_ref, gq_ref, gkv_ref, wq_ref, wkv_ref, q_ref, kv_ref, *, q_scale):
    cq = _rms(cq_ref[...].astype(F32), gq_ref[...]).astype(BF16)
    q_ref[...] = (_dot(cq, wq_ref[...]) * q_scale).astype(q_ref.dtype)
    ckv = _rms(ckv_ref[...].astype(F32), gkv_ref[...]).astype(BF16)
    kv_ref[...] = _dot(ckv, wkv_ref[...]).astype(kv_ref.dtype)


def _latent_up(proj, gq, gkv, wq, wkv, tm=1024):
    m = proj.shape[0]
    nq, nkv = wq.shape[1], wkv.shape[1]
    q_scale = (C_NOPE_DIM + C_ROPE_DIM) ** -0.5 * LOG2E
    return pl.pallas_call(
        functools.partial(_latent_up_kernel, q_scale=q_scale),
        grid=(m // tm,),
        in_specs=[
            pl.BlockSpec((tm, C_Q_RANK), lambda i: (i, CB_CQ)),
            pl.BlockSpec((tm, C_KV_RANK), lambda i: (i, CB_CKV)),
            pl.BlockSpec((1, C_Q_RANK), lambda i: (0, 0)),
            pl.BlockSpec((1, C_KV_RANK), lambda i: (0, 0)),
            pl.BlockSpec((C_Q_RANK, nq), lambda i: (0, 0)),
            pl.BlockSpec((C_KV_RANK, nkv), lambda i: (0, 0)),
        ],
        out_specs=[pl.BlockSpec((tm, nq), lambda i: (i, 0)), pl.BlockSpec((tm, nkv), lambda i: (i, 0))],
        out_shape=[jax.ShapeDtypeStruct((m, nq), BF16), jax.ShapeDtypeStruct((m, nkv), BF16)],
        compiler_params=_params(("parallel",), 48),
        name="latent_up",
    )(proj, proj, gq, gkv, wq, wkv)


A_ROW_CHUNK = 128
C_ROW_CHUNK = 256


def _attend(q_all, k_ref, v1_ref, dv, chunk):
    outs = []
    for c in range(q_all.shape[0] // chunk):
        q = q_all[c * chunk:(c + 1) * chunk]
        s = _dot_nt(q, k_ref[...])
        p = jnp.exp2(s - jnp.max(s, axis=-1, keepdims=True))
        ol = _dot(p.astype(BF16), v1_ref[...])
        outs.append(ol[:, :dv] * (1.0 / ol[:, dv:]))
    return outs


def _attn_a_kernel(q0_ref, q1_ref, q2_ref, k_ref, v_ref, cos_ref, slo_ref, shi_ref, gq_ref, gk_ref,
                   o_ref, kbuf, v1buf, *, tq):
    qi = pl.program_id(2)

    @pl.when(qi == 0)
    def _():
        k = _rms(k_ref[...].astype(F32), gk_ref[...])
        kbuf[...] = _rope(k, cos_ref[...], slo_ref[...], shi_ref[...], 32).astype(BF16)
        v1buf[:, :HEAD_DIM] = v_ref[...]
        v1buf[:, HEAD_DIM:] = jnp.ones((v1buf.shape[0], HEAD_DIM), BF16)

    rows = pl.ds(pl.multiple_of(qi * tq, tq), tq)
    cos, slo, shi = cos_ref[rows, :], slo_ref[rows, :], shi_ref[rows, :]
    q_scale = HEAD_DIM ** -0.5 * LOG2E
    qs = []
    for q_ref in (q0_ref, q1_ref, q2_ref):
        q = _rms(q_ref[...].astype(F32), gq_ref[...])
        qs.append((_rope(q, cos, slo, shi, 32) * q_scale).astype(BF16))
    q_all = jnp.concatenate(qs, axis=0)
    per_head = tq // A_ROW_CHUNK
    for c, o in enumerate(_attend(q_all, kbuf, v1buf, HEAD_DIM, A_ROW_CHUNK)):
        g, r = divmod(c, per_head)
        o_ref[r * A_ROW_CHUNK:(r + 1) * A_ROW_CHUNK, g * HEAD_DIM:(g + 1) * HEAD_DIM] = o.astype(o_ref.dtype)


def _attn_a(proj, tabs, gq, gk, batch, seq, tq=1024):
    nq = seq // tq
    cos, slo, shi = tabs

    def q_spec(g):
        return pl.BlockSpec((tq, HEAD_DIM), lambda b, h, qi: (b * nq + qi, CB_AQ + h * A_GROUP + g))

    tab_spec = pl.BlockSpec((seq, LANES), lambda b, h, qi: (0, 0))
    gain_spec = pl.BlockSpec((1, HEAD_DIM), lambda b, h, qi: (0, 0))
    return pl.pallas_call(
        functools.partial(_attn_a_kernel, tq=tq),
        grid=(batch, A_KV_HEADS, nq),
        in_specs=[
            q_spec(0), q_spec(1), q_spec(2),
            pl.BlockSpec((seq, HEAD_DIM), lambda b, h, qi: (b, CB_AK + h)),
            pl.BlockSpec((seq, HEAD_DIM), lambda b, h, qi: (b, CB_AV + h)),
            tab_spec, tab_spec, tab_spec, gain_spec, gain_spec,
        ],
        out_specs=pl.BlockSpec((tq, A_GROUP * HEAD_DIM), lambda b, h, qi: (b * nq + qi, h)),
        out_shape=jax.ShapeDtypeStruct((batch * seq, A_WIDTH), BF16),
        scratch_shapes=[pltpu.VMEM((seq, HEAD_DIM), BF16), pltpu.VMEM((seq, 2 * HEAD_DIM), BF16)],
        compiler_params=_params(("parallel", "parallel", "arbitrary"), 48),
        name="attn_a",
    )(proj, proj, proj, proj, proj, cos, slo, shi, gq, gk)


def _attn_c_kernel(q_ref, kn_ref, v_ref, kr_ref, cos_ref, slo_ref, shi_ref, o_ref, kbuf, v1buf, *, tq):
    qi = pl.program_id(2)

    @pl.when(qi == 0)
    def _():
        kbuf[:, :C_NOPE_DIM] = kn_ref[...]
        kr = _rope(kr_ref[...].astype(F32), cos_ref[...], slo_ref[...], shi_ref[...], 32)
        kbuf[:, C_NOPE_DIM:] = kr.astype(BF16)
        v1buf[:, :C_V_DIM] = v_ref[...]
        v1buf[:, C_V_DIM:] = jnp.ones((v1buf.shape[0], C_V_DIM), BF16)

    rows = pl.ds(pl.multiple_of(qi * tq, tq), tq)
    q_rope = _rope(q_ref[:, C_NOPE_DIM:].astype(F32), cos_ref[rows, :], slo_ref[rows, :], shi_ref[rows, :], 32)
    q_all = jnp.concatenate([q_ref[:, :C_NOPE_DIM], q_rope.astype(BF16)], axis=1)
    for c, o in enumerate(_attend(q_all, kbuf, v1buf, C_V_DIM, C_ROW_CHUNK)):
        o_ref[c * C_ROW_CHUNK:(c + 1) * C_ROW_CHUNK, :] = o.astype(o_ref.dtype)


def _attn_c(qc, kvc, proj, tabs, batch, seq, tq=2048):
    nq = seq // tq
    cos, slo, shi = tabs
    tab_spec = pl.BlockSpec((seq, LANES), lambda b, h, qi: (0, 0))
    return pl.pallas_call(
        functools.partial(_attn_c_kernel, tq=tq),
        grid=(batch, C_HEADS, nq),
        in_specs=[
            pl.BlockSpec((tq, C_QK_PAD), lambda b, h, qi: (b * nq + qi, h)),
            pl.BlockSpec((seq, C_NOPE_DIM), lambda b, h, qi: (b, h)),
            pl.BlockSpec((seq, C_V_DIM), lambda b, h, qi: (b, C_HEADS + h)),
            pl.BlockSpec((seq, LANES), lambda b, h, qi: (b, CB_CKR)),
            tab_spec, tab_spec, tab_spec,
        ],
        out_specs=pl.BlockSpec((tq, C_V_DIM), lambda b, h, qi: (b * nq + qi, h)),
        out_shape=jax.ShapeDtypeStruct((batch * seq, C_WIDTH), BF16),
        scratch_shapes=[pltpu.VMEM((seq, C_QK_PAD), BF16), pltpu.VMEM((seq, 2 * C_V_DIM), BF16)],
        compiler_params=_params(("parallel", "parallel", "arbitrary"), 48),
        name="attn_c",
    )(qc, kvc, kvc, proj, cos, slo, shi)


B_QBLK = 128
B_PREP_ROWS = 256
B_UNROLL = 8


def _attn_b_kernel(q0_ref, q1_ref, q2_ref, k_ref, v_ref, cos_ref, slo_ref, shi_ref, o_ref,
                   q0f, q1f, q2f, kf, vf, m_s, l_s, acc_s, *, seq):
    q_refs = (q0_ref, q1_ref, q2_ref)
    q_bufs = (q0f, q1f, q2f)
    q_scale = HEAD_DIM ** -0.5 * LOG2E

    def prep(ci, carry):
        rows = pl.ds(pl.multiple_of(ci * B_PREP_ROWS, B_PREP_ROWS), B_PREP_ROWS)
        cos, slo, shi = cos_ref[rows, :], slo_ref[rows, :], shi_ref[rows, :]
        for q_ref, q_buf in zip(q_refs, q_bufs):
            q_buf[rows, :] = _rope(q_ref[rows, :].astype(F32), cos, slo, shi, 16) * q_scale
        kf[rows, :] = _rope(k_ref[rows, :].astype(F32), cos, slo, shi, 16)
        vf[rows, :] = v_ref[rows, :].astype(F32)
        return carry

    lax.fori_loop(0, seq // B_PREP_ROWS, prep, 0)

    for g, (window, dil) in reversed(list(enumerate(B_PATTERNS))):
        first = g == len(B_PATTERNS) - 1
        half = window // (2 * dil)
        length = seq // dil
        nblk = length // B_QBLK
        kwin = min(2 * B_QBLK, length)
        q_buf = q_bufs[g]

        def rows_of(start, size, dil=dil):
            return pl.ds(start, size) if dil == 1 else pl.ds(start, size, stride=dil)

        def block(n, carry, first=first, half=half, length=length, nblk=nblk, kwin=kwin, q_buf=q_buf, dil=dil,
                  rows_of=rows_of):
            r = n // nblk
            i = n % nblk
            k0 = jnp.clip(i * B_QBLK - half, 0, length - kwin)
            q_rows = rows_of(r + dil * B_QBLK * i, B_QBLK)
            k_rows = rows_of(r + dil * k0, kwin)
            q = q_buf[q_rows, :].astype(BF16)
            k = kf[k_rows, :].astype(BF16)
            v = vf[k_rows, :].astype(BF16)
            s = _dot_nt(q, k)
            qpos = i * B_QBLK + lax.broadcasted_iota(jnp.int32, (B_QBLK, kwin), 0)
            kpos = k0 + lax.broadcasted_iota(jnp.int32, (B_QBLK, kwin), 1)
            s = jnp.where(jnp.abs(kpos - qpos) <= half, s, -jnp.inf)
            m_b = jnp.max(s, axis=-1, keepdims=True)
            p = jnp.exp2(s - m_b)
            l_b = jnp.sum(p, axis=-1, keepdims=True)
            a_b = _dot(p.astype(BF16), v)
            full = (B_QBLK, HEAD_DIM)
            if first:
                m_s[q_rows, :] = jnp.broadcast_to(m_b, full)
                l_s[q_rows, :] = jnp.broadcast_to(l_b, full)
                acc_s[q_rows, :] = a_b
            else:
                m_o = m_s[q_rows, :]
                m_n = jnp.maximum(m_o, m_b)
                w_o = jnp.exp2(m_o - m_n)
                w_b = jnp.exp2(m_b - m_n)
                acc_s[q_rows, :] = acc_s[q_rows, :] * w_o + a_b * w_b
                l_s[q_rows, :] = l_s[q_rows, :] * w_o + l_b * w_b
                m_s[q_rows, :] = m_n
            return carry

        lax.fori_loop(0, seq // B_QBLK, block, 0, unroll=B_UNROLL)

    def finish(ci, carry):
        rows = pl.ds(pl.multiple_of(ci * B_PREP_ROWS, B_PREP_ROWS), B_PREP_ROWS)
        o_ref[rows, :] = (acc_s[rows, :] * (1.0 / l_s[rows, :])).astype(o_ref.dtype)
        return carry

    lax.fori_loop(0, seq // B_PREP_ROWS, finish, 0)


def _attn_b(proj, tabs, batch, seq):
    cos, slo, shi = tabs

    def q_spec(g):
        return pl.BlockSpec((seq, HEAD_DIM), lambda b, h: (b, CB_BQ + g * B_HEADS + h))

    tab_spec = pl.BlockSpec((seq, LANES), lambda b, h: (0, 0))
    slab = pltpu.VMEM((seq, HEAD_DIM), F32)
    return pl.pallas_call(
        functools.partial(_attn_b_kernel, seq=seq),
        grid=(batch, B_HEADS),
        in_specs=[
            q_spec(0), q_spec(1), q_spec(2),
            pl.BlockSpec((seq, HEAD_DIM), lambda b, h: (b, CB_BK + h)),
            pl.BlockSpec((seq, HEAD_DIM), lambda b, h: (b, CB_BV + h)),
            tab_spec, tab_spec, tab_spec,
        ],
        out_specs=pl.BlockSpec((seq, HEAD_DIM), lambda b, h: (b, h)),
        out_shape=jax.ShapeDtypeStruct((batch * seq, B_WIDTH), BF16),
        scratch_shapes=[slab] * 8,
        compiler_params=_params(("parallel", "parallel"), 48),
        name="attn_b",
    )(proj, proj, proj, proj, proj, cos, slo, shi)


def _out_kernel(ya_ref, yb_ref, yc_ref, ga_ref, gb_ref, gc_ref, w_ref, x_ref, gf_ref, x1_ref, xn_ref):
    y = jnp.concatenate([
        _rms(ya_ref[...].astype(F32), ga_ref[...]).astype(BF16),
        _rms(yb_ref[...].astype(F32), gb_ref[...]).astype(BF16),
        _rms(yc_ref[...].astype(F32), gc_ref[...]).astype(BF16),
    ], axis=1)
    x1 = x_ref[...] + _dot(y, w_ref[...])
    x1_ref[...] = x1
    xn_ref[...] = _rms(x1, gf_ref[...]).astype(xn_ref.dtype)


def _out_proj(ya, yb, yc, ga, gb, gc, w, layer, x, gf, tm=512):
    m, d = x.shape

    def rows(width):
        return pl.BlockSpec((tm, width), lambda i: (i, 0))

    def const(r, c):
        return pl.BlockSpec((r, c), lambda i: (0, 0))

    return pl.pallas_call(
        _out_kernel,
        grid=(m // tm,),
        in_specs=[rows(A_WIDTH), rows(B_WIDTH), rows(C_WIDTH), const(1, A_WIDTH), const(1, B_WIDTH),
                  const(1, C_WIDTH), pl.BlockSpec((None, d, d), lambda i: (layer, 0, 0)), rows(d), const(1, d)],
        out_specs=[rows(d), rows(d)],
        out_shape=[jax.ShapeDtypeStruct((m, d), F32), jax.ShapeDtypeStruct((m, d), BF16)],
        compiler_params=_params(("parallel",), 56),
        name="out_proj",
    )(ya, yb, yc, ga, gb, gc, w, x, gf)


FFN_SUBTILE = 256
FFN_ROWS = 256
FFN_GATE_ROWS = 64
FFN_GUARD = 8

def _ffn_up_kernel(xn_ref, wg_ref, wu_ref, cg_ref, cu_ref, bg_ref, bu_ref, o_ref, hg, hu, wg_bf, wu_bf, *, seq):
    tn = o_ref.shape[1]

    @pl.when(pl.program_id(1) == 0)
    def _():
        wg_bf[...] = wg_ref[...].astype(BF16)
        wu_bf[...] = wu_ref[...].astype(BF16)

    zeros = jnp.zeros((FFN_GUARD, tn), F32)
    for h in (hg, hu):
        h[0:FFN_GUARD, :] = zeros
        h[FFN_GUARD + seq:2 * FFN_GUARD + seq, :] = zeros

    def matmuls(c, r):
        cols = slice(c * FFN_SUBTILE, (c + 1) * FFN_SUBTILE)
        xr = xn_ref[r * FFN_ROWS:(r + 1) * FFN_ROWS, :]
        dst = slice(FFN_GUARD + r * FFN_ROWS, FFN_GUARD + (r + 1) * FFN_ROWS)
        hg[dst, cols] = _dot(xr, wg_bf[:, cols])
        hu[dst, cols] = _dot(xr, wu_bf[:, cols])

    def gate_stage(c, r):
        cols = slice(c * FFN_SUBTILE, (c + 1) * FFN_SUBTILE)
        base = FFN_GUARD + r * FFN_ROWS

        def conv(h, c_ref, b_ref):
            taps = [h[base + t - 1:base + t - 1 + FFN_ROWS, cols] * c_ref[t:t + 1, cols] for t in range(3)]
            return taps[0] + taps[1] + taps[2] + b_ref[:, cols]

        gate = conv(hg, cg_ref, bg_ref)
        up = conv(hu, cu_ref, bu_ref)
        o_ref[r * FFN_ROWS:(r + 1) * FFN_ROWS, cols] = (
            gate * (1.0 / (1.0 + jnp.exp(-gate))) * up).astype(o_ref.dtype)

    units = [(c, r) for c in range(tn // FFN_SUBTILE) for r in range(seq // FFN_ROWS)]
    for i, unit in enumerate(units):
        matmuls(*unit)
        if i > 0:
            gate_stage(*units[i - 1])
    gate_stage(*units[-1])


def _ffn_up(xn, w_up, layer, conv_w, conv_b, batch, seq, tn=512):
    d = xn.shape[1]
    nj = D_FF // tn
    return pl.pallas_call(
        functools.partial(_ffn_up_kernel, seq=seq),
        grid=(nj, batch),
        in_specs=[
            pl.BlockSpec((seq, d), lambda j, b: (b, 0)),
            pl.BlockSpec((None, d, tn), lambda j, b: (layer, 0, j)),
            pl.BlockSpec((None, d, tn), lambda j, b: (layer, 0, nj + j)),
            pl.BlockSpec((3, tn), lambda j, b: (0, j)),
            pl.BlockSpec((3, tn), lambda j, b: (0, nj + j)),
            pl.BlockSpec((1, tn), lambda j, b: (0, j)),
            pl.BlockSpec((1, tn), lambda j, b: (0, nj + j)),
        ],
        out_specs=pl.BlockSpec((seq, tn), lambda j, b: (b, j)),
        out_shape=jax.ShapeDtypeStruct((batch * seq, D_FF), BF16),
        scratch_shapes=[pltpu.VMEM((seq + 2 * FFN_GUARD, tn), F32)] * 2 + [pltpu.VMEM((d, tn), BF16)] * 2,
        compiler_params=_params(("parallel", "arbitrary"), 60),
        name="ffn_up",
    )(xn, w_up, w_up, conv_w, conv_w, conv_b, conv_b)


def _ffn_down_kernel(a_ref, w_ref, x1_ref, g_ref, x2_ref, xn_ref):
    k = pl.program_id(1)

    @pl.when(k == 0)
    def _():
        x2_ref[...] = x1_ref[...]

    x2_ref[...] += _dot(a_ref[...], w_ref[...])

    @pl.when(k == pl.num_programs(1) - 1)
    def _():
        xn_ref[...] = _rms(x2_ref[...], g_ref[...]).astype(xn_ref.dtype)


def _ffn_down_final_kernel(a_ref, w_ref, x1_ref, g_ref, xn_ref, acc_ref):
    _ffn_down_kernel(a_ref, w_ref, x1_ref, g_ref, acc_ref, xn_ref)


def _ffn_down(act, w, layer, x1, g, final, tm=1024, tk=512):
    m, kdim = act.shape
    d = w.shape[2]
    tile = pl.BlockSpec((tm, d), lambda i, k: (i, 0))
    in_specs = [
        pl.BlockSpec((tm, tk), lambda i, k: (i, k)),
        pl.BlockSpec((None, tk, d), lambda i, k: (layer, k, 0)),
        tile,
        pl.BlockSpec((1, d), lambda i, k: (0, 0)),
    ]
    common = dict(grid=(m // tm, kdim // tk), in_specs=in_specs,
                  compiler_params=_params(("parallel", "arbitrary"), 56))
    if final:
        xn = pl.pallas_call(_ffn_down_final_kernel, out_specs=tile, out_shape=jax.ShapeDtypeStruct((m, d), F32),
                            scratch_shapes=[pltpu.VMEM((tm, d), F32)], name="ffn_down_final", **common)(act, w, x1, g)
        return None, xn
    return pl.pallas_call(_ffn_down_kernel, out_specs=[tile, tile],
                          out_shape=[jax.ShapeDtypeStruct((m, d), F32), jax.ShapeDtypeStruct((m, d), BF16)],
                          name="ffn_down", **common)(act, w, x1, g)


def _rope_table(pos, dim, theta):
    inv = theta ** (-jnp.arange(0, dim, 2, dtype=F32) / dim)
    ang = pos.astype(F32)[:, None] * inv[None, :]
    return jnp.cos(ang), jnp.sin(ang)


def _lane_tables(seq):
    t = jnp.arange(seq)
    z = lambda w: jnp.zeros((seq, w), F32)
    one = lambda w: jnp.ones((seq, w), F32)
    cr, sr = _rope_table(t // GRID_W, HEAD_DIM // 2, A_ROPE_THETA)
    cc, sc = _rope_table(t % GRID_W, HEAD_DIM // 2, A_ROPE_THETA)
    tab_a = (jnp.concatenate([cr, cr, cc, cc], 1),
             jnp.concatenate([-sr, z(32), -sc, z(32)], 1),
             jnp.concatenate([z(32), sr, z(32), sc], 1))
    cp, sp = _rope_table(t, PARTIAL_ROPE_DIM, PARTIAL_ROPE_THETA)
    tab_b = (jnp.concatenate([cp, cp, one(96)], 1),
             jnp.concatenate([-sp, z(112)], 1),
             jnp.concatenate([z(16), sp, z(96)], 1))
    cm, sm = _rope_table(t, C_ROPE_DIM, C_ROPE_THETA)
    tab_c = (jnp.concatenate([cm, cm, one(64)], 1),
             jnp.concatenate([-sm, z(96)], 1),
             jnp.concatenate([z(32), sm, z(64)], 1))
    return tab_a, tab_b, tab_c


def _prep_w_ckr(w_in_t, layer):
    ckr = w_in_t[layer, IN_N_SRC * IN_WBLK:, :]
    return jnp.pad(ckr, ((0, IN_WBLK - ckr.shape[0]), (0, 0)))


def _prep_w_uq(w):
    w = w.reshape(C_Q_RANK, C_HEADS, C_NOPE_DIM + C_ROPE_DIM)
    w = jnp.pad(w, ((0, 0), (0, 0), (0, C_QK_PAD - C_NOPE_DIM - C_ROPE_DIM)))
    return w.reshape(C_Q_RANK, C_HEADS * C_QK_PAD).astype(BF16)


def _prep_w_ukv(w):
    w = w.reshape(C_KV_RANK, C_HEADS, C_NOPE_DIM + C_V_DIM)
    kn = w[:, :, :C_NOPE_DIM].reshape(C_KV_RANK, C_HEADS * C_NOPE_DIM)
    v = w[:, :, C_NOPE_DIM:].reshape(C_KV_RANK, C_HEADS * C_V_DIM)
    return jnp.concatenate([kn, v], axis=1).astype(BF16)


def kernel(x, attn_norm, w_in, a_q_norm, a_k_norm, c_q_norm, c_kv_norm, w_uq, w_ukv, out_norm, w_out,
           ffn_norm, w_up, conv_w, conv_b, w_down, final_norm):
    batch, seq, d = x.shape
    depth = w_in.shape[0]
    tab_a, tab_b, tab_c = _lane_tables(seq)
    row = lambda v: v.reshape(1, -1)

    xr = x.reshape(batch * seq, d)
    xn = _rmsnorm_rows(xr, row(attn_norm[0]))
    w_in_t = jnp.swapaxes(w_in, 1, 2)
    w_out_bf = w_out.astype(BF16)
    w_down_bf = w_down.astype(BF16)
    for l in range(depth):
        proj = _in_proj(xn, w_in_t, l, _prep_w_ckr(w_in_t, l))
        qc, kvc = _latent_up(proj, row(c_q_norm[l]), row(c_kv_norm[l]), _prep_w_uq(w_uq[l]),
                             _prep_w_ukv(w_ukv[l]))
        ya = _attn_a(proj, tab_a, row(a_q_norm[l]), row(a_k_norm[l]), batch, seq)
        yb = _attn_b(proj, tab_b, batch, seq)
        yc = _attn_c(qc, kvc, proj, tab_c, batch, seq)
        g = out_norm[l]
        x1, xn1 = _out_proj(ya, yb, yc, row(g[:A_WIDTH]), row(g[A_WIDTH:A_WIDTH + B_WIDTH]),
                            row(g[A_WIDTH + B_WIDTH:]), w_out_bf, l, xr, row(ffn_norm[l]))
        act = _ffn_up(xn1, w_up, l, conv_w[l], row(conv_b[l]), batch, seq)
        last = l == depth - 1
        g_next = final_norm if last else attn_norm[l + 1]
        xr, xn = _ffn_down(act, w_down_bf, l, x1, row(g_next), last)
    return xn.reshape(batch, seq, d)
```

```python
import functools
import math

import jax
import jax.numpy as jnp
from jax import lax
from jax.experimental import pallas as pl
from jax.experimental.pallas import tpu as pltpu

D_MODEL = 2048
HEAD_DIM = 128
A_HEADS = 6
A_KV_HEADS = 2
A_GROUP = A_HEADS // A_KV_HEADS
A_ROPE_THETA = 10000.0
B_HEADS = 4
B_PATTERNS = ((128, 1), (512, 4), (2048, 16))
B_N_GROUPS = 3
C_HEADS = 6
C_Q_RANK = 512
C_KV_RANK = 512
C_NOPE_DIM = 128
C_ROPE_DIM = 64
C_V_DIM = 128
C_ROPE_THETA = 10000.0
PARTIAL_ROPE_DIM = HEAD_DIM // 4
PARTIAL_ROPE_THETA = 500000.0
GRID_W = 64
D_FF = 5632
EPS = 1e-6

A_WIDTH = A_HEADS * HEAD_DIM
B_WIDTH = B_HEADS * HEAD_DIM
C_WIDTH = C_HEADS * C_V_DIM
C_QK_PAD = 256

LANES = 128
LOG2E = math.log2(math.e)
F32 = jnp.float32
BF16 = jnp.bfloat16

PROJ_WIDTH = 5120
CB_CQ, CB_CKV = 0, 1
CB_AQ, CB_AK, CB_AV = 8, 14, 16
CB_BQ, CB_BK, CB_BV = 18, 30, 34
CB_CKR = 38


def _params(semantics, vmem_mib):
    return pltpu.CompilerParams(dimension_semantics=semantics, vmem_limit_bytes=vmem_mib * 1024 * 1024)


def _rms(xf, g):
    return xf * lax.rsqrt(jnp.mean(xf * xf, axis=-1, keepdims=True) + EPS) * g


def _rope(xf, cos, sin_lo, sin_hi, shift):
    return xf * cos + pltpu.roll(xf, LANES - shift, 1) * sin_lo + pltpu.roll(xf, shift, 1) * sin_hi


def _dot(a, b):
    return jnp.dot(a, b, preferred_element_type=F32)


def _dot_nt(a, b):
    return lax.dot_general(a, b, (((1,), (1,)), ((), ())), preferred_element_type=F32)


def _norm_kernel(x_ref, g_ref, o_ref):
    o_ref[...] = _rms(x_ref[...], g_ref[...]).astype(o_ref.dtype)


def _rmsnorm_rows(x, g, tm=512):
    m, d = x.shape
    return pl.pallas_call(
        _norm_kernel,
        grid=(m // tm,),
        in_specs=[pl.BlockSpec((tm, d), lambda i: (i, 0)), pl.BlockSpec((1, d), lambda i: (0, 0))],
        out_specs=pl.BlockSpec((tm, d), lambda i: (i, 0)),
        out_shape=jax.ShapeDtypeStruct((m, d), BF16),
        compiler_params=_params(("parallel",), 40),
        name="rmsnorm",
    )(x, g)


IN_WBLK = 256
IN_TN = 1024
IN_SRC_CQ = 15
IN_N_SRC = 19


def _in_proj_kernel(a_ref, *refs):
    w_refs, wck_ref, o_ref, w_bf = refs[:-3], refs[-3], refs[-2], refs[-1]
    j = pl.program_id(0)
    last = pl.num_programs(0) - 1

    @pl.when(pl.program_id(1) == 0)
    def _():
        for q, w_ref in enumerate(w_refs[:-1]):
            w_bf[q * IN_WBLK:(q + 1) * IN_WBLK, :] = w_ref[...].astype(BF16)
        tail = slice((len(w_refs) - 1) * IN_WBLK, len(w_refs) * IN_WBLK)

        @pl.when(j < last)
        def _():
            w_bf[tail, :] = w_refs[-1][...].astype(BF16)

        @pl.when(j == last)
        def _():
            w_bf[tail, :] = wck_ref[...].astype(BF16)

    o_ref[...] = _dot_nt(a_ref[...], w_bf[...]).astype(o_ref.dtype)


def _in_proj(a, w_in_t, layer, w_ckr_t, tm=1024):
    m, k = a.shape
    per_step = IN_TN // IN_WBLK

    def w_spec(q):
        def index(j, i):
            n = j * per_step + q
            return layer, jnp.where(n < 4, n + IN_SRC_CQ, n - 4), 0
        return pl.BlockSpec((None, IN_WBLK, k), index)

    return pl.pallas_call(
        _in_proj_kernel,
        grid=(PROJ_WIDTH // IN_TN, m // tm),
        in_specs=[pl.BlockSpec((tm, k), lambda j, i: (i, 0))] + [w_spec(q) for q in range(per_step)]
        + [pl.BlockSpec((IN_WBLK, k), lambda j, i: (0, 0))],
        out_specs=pl.BlockSpec((tm, IN_TN), lambda j, i: (i, j)),
        out_shape=jax.ShapeDtypeStruct((m, PROJ_WIDTH), BF16),
        scratch_shapes=[pltpu.VMEM((IN_TN, k), BF16)],
        compiler_params=_params(("parallel", "arbitrary"), 48),
        name="in_proj",
    )(a, *([w_in_t] * per_step), w_ckr_t)


def _latent_up_kernel(cq_ref, ckv_ref, gq_ref, gkv_ref, wq_ref, wkv_ref, q_ref, kv_ref, *, q_scale):
    cq = _rms(cq_ref[...].astype(F32), gq_ref[...]).astype(BF16)
    q_ref[...] = (_dot(cq, wq_ref[...]) * q_scale).astype(q_ref.dtype)
    ckv = _rms(ckv_ref[...].astype(F32), gkv_ref[...]).astype(BF16)
    kv_ref[...] = _dot(ckv, wkv_ref[...]).astype(kv_ref.dtype)


def _latent_up(proj, gq, gkv, wq, wkv, tm=1024):
    m = proj.shape[0]
    nq, nkv = wq.shape[1], wkv.shape[1]
    q_scale = (C_NOPE_DIM + C_ROPE_DIM) ** -0.5 * LOG2E
    return pl.pallas_call(
        functools.partial(_latent_up_kernel, q_scale=q_scale),
        grid=(m // tm,),
        in_specs=[
            pl.BlockSpec((tm, C_Q_RANK), lambda i: (i, CB_CQ)),
            pl.BlockSpec((tm, C_KV_RANK), lambda i: (i, CB_CKV)),
            pl.BlockSpec((1, C_Q_RANK), lambda i: (0, 0)),
            pl.BlockSpec((1, C_KV_RANK), lambda i: (0, 0)),
            pl.BlockSpec((C_Q_RANK, nq), lambda i: (0, 0)),
            pl.BlockSpec((C_KV_RANK, nkv), lambda i: (0, 0)),
        ],
        out_specs=[pl.BlockSpec((tm, nq), lambda i: (i, 0)), pl.BlockSpec((tm, nkv), lambda i: (i, 0))],
        out_shape=[jax.ShapeDtypeStruct((m, nq), BF16), jax.ShapeDtypeStruct((m, nkv), BF16)],
        compiler_params=_params(("parallel",), 48),
        name="latent_up",
    )(proj, proj, gq, gkv, wq, wkv)


A_ROW_CHUNK = 128
C_ROW_CHUNK = 256


def _attend(q_all, k_ref, v1_ref, dv, chunk):
    outs = []
    for c in range(q_all.shape[0] // chunk):
        q = q_all[c * chunk:(c + 1) * chunk]
        s = _dot_nt(q, k_ref[...])
        p = jnp.exp2(s - jnp.max(s, axis=-1, keepdims=True))
        ol = _dot(p.astype(BF16), v1_ref[...])
        outs.append(ol[:, :dv] * (1.0 / ol[:, dv:]))
    return outs


def _attn_a_kernel(q0_ref, q1_ref, q2_ref, k_ref, v_ref, cos_ref, slo_ref, shi_ref, gq_ref, gk_ref,
                   o_ref, kbuf, v1buf, *, tq):
    qi = pl.program_id(2)

    @pl.when(qi == 0)
    def _():
        k = _rms(k_ref[...].astype(F32), gk_ref[...])
        kbuf[...] = _rope(k, cos_ref[...], slo_ref[...], shi_ref[...], 32).astype(BF16)
        v1buf[:, :HEAD_DIM] = v_ref[...]
        v1buf[:, HEAD_DIM:] = jnp.ones((v1buf.shape[0], HEAD_DIM), BF16)

    rows = pl.ds(pl.multiple_of(qi * tq, tq), tq)
    cos, slo, shi = cos_ref[rows, :], slo_ref[rows, :], shi_ref[rows, :]
    q_scale = HEAD_DIM ** -0.5 * LOG2E
    qs = []
    for q_ref in (q0_ref, q1_ref, q2_ref):
        q = _rms(q_ref[...].astype(F32), gq_ref[...])
        qs.append((_rope(q, cos, slo, shi, 32) * q_scale).astype(BF16))
    q_all = jnp.concatenate(qs, axis=0)
    per_head = tq // A_ROW_CHUNK
    for c, o in enumerate(_attend(q_all, kbuf, v1buf, HEAD_DIM, A_ROW_CHUNK)):
        g, r = divmod(c, per_head)
        o_ref[r * A_ROW_CHUNK:(r + 1) * A_ROW_CHUNK, g * HEAD_DIM:(g + 1) * HEAD_DIM] = o.astype(o_ref.dtype)


def _attn_a(proj, tabs, gq, gk, batch, seq, tq=1024):
    nq = seq // tq
    cos, slo, shi = tabs

    def q_spec(g):
        return pl.BlockSpec((tq, HEAD_DIM), lambda b, h, qi: (b * nq + qi, CB_AQ + h * A_GROUP + g))

    tab_spec = pl.BlockSpec((seq, LANES), lambda b, h, qi: (0, 0))
    gain_spec = pl.BlockSpec((1, HEAD_DIM), lambda b, h, qi: (0, 0))
    return pl.pallas_call(
        functools.partial(_attn_a_kernel, tq=tq),
        grid=(batch, A_KV_HEADS, nq),
        in_specs=[
            q_spec(0), q_spec(1), q_spec(2),
            pl.BlockSpec((seq, HEAD_DIM), lambda b, h, qi: (b, CB_AK + h)),
            pl.BlockSpec((seq, HEAD_DIM), lambda b, h, qi: (b, CB_AV + h)),
            tab_spec, tab_spec, tab_spec, gain_spec, gain_spec,
        ],
        out_specs=pl.BlockSpec((tq, A_GROUP * HEAD_DIM), lambda b, h, qi: (b * nq + qi, h)),
        out_shape=jax.ShapeDtypeStruct((batch * seq, A_WIDTH), BF16),
        scratch_shapes=[pltpu.VMEM((seq, HEAD_DIM), BF16), pltpu.VMEM((seq, 2 * HEAD_DIM), BF16)],
        compiler_params=_params(("parallel", "parallel", "arbitrary"), 48),
        name="attn_a",
    )(proj, proj, proj, proj, proj, cos, slo, shi, gq, gk)


def _attn_c_kernel(q_ref, kn_ref, v_ref, kr_ref, cos_ref, slo_ref, shi_ref, o_ref, kbuf, v1buf, *, tq):
    qi = pl.program_id(2)

    @pl.when(qi == 0)
    def _():
        kbuf[:, :C_NOPE_DIM] = kn_ref[...]
        kr = _rope(kr_ref[...].astype(F32), cos_ref[...], slo_ref[...], shi_ref[...], 32)
        kbuf[:, C_NOPE_DIM:] = kr.astype(BF16)
        v1buf[:, :C_V_DIM] = v_ref[...]
        v1buf[:, C_V_DIM:] = jnp.ones((v1buf.shape[0], C_V_DIM), BF16)

    rows = pl.ds(pl.multiple_of(qi * tq, tq), tq)
    q_rope = _rope(q_ref[:, C_NOPE_DIM:].astype(F32), cos_ref[rows, :], slo_ref[rows, :], shi_ref[rows, :], 32)
    q_all = jnp.concatenate([q_ref[:, :C_NOPE_DIM], q_rope.astype(BF16)], axis=1)
    for c, o in enumerate(_attend(q_all, kbuf, v1buf, C_V_DIM, C_ROW_CHUNK)):
        o_ref[c * C_ROW_CHUNK:(c + 1) * C_ROW_CHUNK, :] = o.astype(o_ref.dtype)


def _attn_c(qc, kvc, proj, tabs, batch, seq, tq=2048):
    nq = seq // tq
    cos, slo, shi = tabs
    tab_spec = pl.BlockSpec((seq, LANES), lambda b, h, qi: (0, 0))
    return pl.pallas_call(
        functools.partial(_attn_c_kernel, tq=tq),
        grid=(batch, C_HEADS, nq),
        in_specs=[
            pl.BlockSpec((tq, C_QK_PAD), lambda b, h, qi: (b * nq + qi, h)),
            pl.BlockSpec((seq, C_NOPE_DIM), lambda b, h, qi: (b, h)),
            pl.BlockSpec((seq, C_V_DIM), lambda b, h, qi: (b, C_HEADS + h)),
            pl.BlockSpec((seq, LANES), lambda b, h, qi: (b, CB_CKR)),
            tab_spec, tab_spec, tab_spec,
        ],
        out_specs=pl.BlockSpec((tq, C_V_DIM), lambda b, h, qi: (b * nq + qi, h)),
        out_shape=jax.ShapeDtypeStruct((batch * seq, C_WIDTH), BF16),
        scratch_shapes=[pltpu.VMEM((seq, C_QK_PAD), BF16), pltpu.VMEM((seq, 2 * C_V_DIM), BF16)],
        compiler_params=_params(("parallel", "parallel", "arbitrary"), 48),
        name="attn_c",
    )(qc, kvc, kvc, proj, cos, slo, shi)


B_QBLK = 128
B_PREP_ROWS = 256
B_UNROLL = 8


def _attn_b_kernel(q0_ref, q1_ref, q2_ref, k_ref, v_ref, cos_ref, slo_ref, shi_ref, o_ref,
                   q0f, q1f, q2f, kf, vf, m_s, l_s, acc_s, *, seq):
    q_refs = (q0_ref, q1_ref, q2_ref)
    q_bufs = (q0f, q1f, q2f)
    q_scale = HEAD_DIM ** -0.5 * LOG2E

    def prep(ci, carry):
        rows = pl.ds(pl.multiple_of(ci * B_PREP_ROWS, B_PREP_ROWS), B_PREP_ROWS)
        cos, slo, shi = cos_ref[rows, :], slo_ref[rows, :], shi_ref[rows, :]
        for q_ref, q_buf in zip(q_refs, q_bufs):
            q_buf[rows, :] = _rope(q_ref[rows, :].astype(F32), cos, slo, shi, 16) * q_scale
        kf[rows, :] = _rope(k_ref[rows, :].astype(F32), cos, slo, shi, 16)
        vf[rows, :] = v_ref[rows, :].astype(F32)
        return carry

    lax.fori_loop(0, seq // B_PREP_ROWS, prep, 0)

    for g, (window, dil) in reversed(list(enumerate(B_PATTERNS))):
        first = g == len(B_PATTERNS) - 1
        half = window // (2 * dil)
        length = seq // dil
        nblk = length // B_QBLK
        kwin = min(2 * B_QBLK, length)
        q_buf = q_bufs[g]

        def rows_of(start, size, dil=dil):
            return pl.ds(start, size) if dil == 1 else pl.ds(start, size, stride=dil)

        def block(n, carry, first=first, half=half, length=length, nblk=nblk, kwin=kwin, q_buf=q_buf, dil=dil,
                  rows_of=rows_of):
            r = n // nblk
            i = n % nblk
            k0 = jnp.clip(i * B_QBLK - half, 0, length - kwin)
            q_rows = rows_of(r + dil * B_QBLK * i, B_QBLK)
            k_rows = rows_of(r + dil * k0, kwin)
            q = q_buf[q_rows, :].astype(BF16)
            k = kf[k_rows, :].astype(BF16)
            v = vf[k_rows, :].astype(BF16)
            s = _dot_nt(q, k)
            qpos = i * B_QBLK + lax.broadcasted_iota(jnp.int32, (B_QBLK, kwin), 0)
            kpos = k0 + lax.broadcasted_iota(jnp.int32, (B_QBLK, kwin), 1)
            s = jnp.where(jnp.abs(kpos - qpos) <= half, s, -jnp.inf)
            m_b = jnp.max(s, axis=-1, keepdims=True)
            p = jnp.exp2(s - m_b)
            l_b = jnp.sum(p, axis=-1, keepdims=True)
            a_b = _dot(p.astype(BF16), v)
            full = (B_QBLK, HEAD_DIM)
            if first:
                m_s[q_rows, :] = jnp.broadcast_to(m_b, full)
                l_s[q_rows, :] = jnp.broadcast_to(l_b, full)
                acc_s[q_rows, :] = a_b
            else:
                m_o = m_s[q_rows, :]
                m_n = jnp.maximum(m_o, m_b)
                w_o = jnp.exp2(m_o - m_n)
                w_b = jnp.exp2(m_b - m_n)
                acc_s[q_rows, :] = acc_s[q_rows, :] * w_o + a_b * w_b
                l_s[q_rows, :] = l_s[q_rows, :] * w_o + l_b * w_b
                m_s[q_rows, :] = m_n
            return carry

        lax.fori_loop(0, seq // B_QBLK, block, 0, unroll=B_UNROLL)

    def finish(ci, carry):
        rows = pl.ds(pl.multiple_of(ci * B_PREP_ROWS, B_PREP_ROWS), B_PREP_ROWS)
        o_ref[rows, :] = (acc_s[rows, :] * (1.0 / l_s[rows, :])).astype(o_ref.dtype)
        return carry

    lax.fori_loop(0, seq // B_PREP_ROWS, finish, 0)


def _attn_b(proj, tabs, batch, seq):
    cos, slo, shi = tabs

    def q_spec(g):
        return pl.BlockSpec((seq, HEAD_DIM), lambda b, h: (b, CB_BQ + g * B_HEADS + h))

    tab_spec = pl.BlockSpec((seq, LANES), lambda b, h: (0, 0))
    slab = pltpu.VMEM((seq, HEAD_DIM), F32)
    return pl.pallas_call(
        functools.partial(_attn_b_kernel, seq=seq),
        grid=(batch, B_HEADS),
        in_specs=[
            q_spec(0), q_spec(1), q_spec(2),
            pl.BlockSpec((seq, HEAD_DIM), lambda b, h: (b, CB_BK + h)),
            pl.BlockSpec((seq, HEAD_DIM), lambda b, h: (b, CB_BV + h)),
            tab_spec, tab_spec, tab_spec,
        ],
        out_specs=pl.BlockSpec((seq, HEAD_DIM), lambda b, h: (b, h)),
        out_shape=jax.ShapeDtypeStruct((batch * seq, B_WIDTH), BF16),
        scratch_shapes=[slab] * 8,
        compiler_params=_params(("parallel", "parallel"), 48),
        name="attn_b",
    )(proj, proj, proj, proj, proj, cos, slo, shi)


def _out_kernel(ya_ref, yb_ref, yc_ref, ga_ref, gb_ref, gc_ref, w_ref, x_ref, gf_ref, x1_ref, xn_ref):
    y = jnp.concatenate([
        _rms(ya_ref[...].astype(F32), ga_ref[...]).astype(BF16),
        _rms(yb_ref[...].astype(F32), gb_ref[...]).astype(BF16),
        _rms(yc_ref[...].astype(F32), gc_ref[...]).astype(BF16),
    ], axis=1)
    x1 = x_ref[...] + _dot(y, w_ref[...])
    x1_ref[...] = x1
    xn = _rms(x1, gf_ref[...]).astype(xn_ref.dtype)
    seg, d = xn_ref.shape[0], x1.shape[1]
    for a in range(x1.shape[0] // seg):
        xn_ref[:, a * d:(a + 1) * d] = xn[a * seg:(a + 1) * seg]


def _out_proj(ya, yb, yc, ga, gb, gc, w, layer, x, gf, seq, tm=512):
    m, d = x.shape
    per_seq = seq // tm
    seg = seq // FFN_INTERLEAVE
    segs = tm // seg

    def rows(width):
        return pl.BlockSpec((tm, width), lambda i: (i, 0))

    def const(r, c):
        return pl.BlockSpec((r, c), lambda i: (0, 0))

    x1, xn_view = pl.pallas_call(
        _out_kernel,
        grid=(m // tm,),
        in_specs=[rows(A_WIDTH), rows(B_WIDTH), rows(C_WIDTH), const(1, A_WIDTH), const(1, B_WIDTH),
                  const(1, C_WIDTH), pl.BlockSpec((None, d, d), lambda i: (layer, 0, 0)), rows(d), const(1, d)],
        out_specs=[rows(d), pl.BlockSpec((seg, segs * d), lambda i: (i // per_seq, i % per_seq))],
        out_shape=[jax.ShapeDtypeStruct((m, d), F32),
                   jax.ShapeDtypeStruct((m // FFN_INTERLEAVE, FFN_INTERLEAVE * d), BF16)],
        compiler_params=_params(("parallel",), 56),
        name="out_proj",
    )(ya, yb, yc, ga, gb, gc, w, x, gf)
    return x1, xn_view.reshape(m, d)


FFN_SUBTILE = 256
FFN_ROWS = 256
FFN_GATE_ROWS = 64
FFN_INTERLEAVE = 8
FFN_GUARD = FFN_INTERLEAVE

def _ffn_up_kernel(xn_ref, wg_ref, wu_ref, cg_ref, cu_ref, bg_ref, bu_ref, o_ref, hg, hu, wg_bf, wu_bf, *, seq):
    tn = o_ref.shape[1]
    il = FFN_INTERLEAVE

    @pl.when(pl.program_id(1) == 0)
    def _():
        wg_bf[...] = wg_ref[...].astype(BF16)
        wu_bf[...] = wu_ref[...].astype(BF16)

    def matmuls(c, r):
        cols = slice(c * FFN_SUBTILE, (c + 1) * FFN_SUBTILE)
        xr = xn_ref[r * FFN_ROWS:(r + 1) * FFN_ROWS, :]
        dst = slice(FFN_GUARD + r * FFN_ROWS, FFN_GUARD + (r + 1) * FFN_ROWS)
        hg[dst, cols] = _dot(xr, wg_bf[:, cols])
        hu[dst, cols] = _dot(xr, wu_bf[:, cols])

    def guards(c):
        cols = slice(c * FFN_SUBTILE, (c + 1) * FFN_SUBTILE)
        sublane = lax.broadcasted_iota(jnp.int32, (il, FFN_SUBTILE), 0)
        for h in (hg, hu):
            last = h[FFN_GUARD + seq - il:FFN_GUARD + seq, cols]
            h[FFN_GUARD - il:FFN_GUARD, cols] = jnp.where(sublane == 0, 0.0, pltpu.roll(last, 1, 0))
            first = h[FFN_GUARD:FFN_GUARD + il, cols]
            h[FFN_GUARD + seq:FFN_GUARD + seq + il, cols] = jnp.where(sublane == il - 1, 0.0,
                                                                      pltpu.roll(first, il - 1, 0))

    def gate_stage(c, r):
        cols = slice(c * FFN_SUBTILE, (c + 1) * FFN_SUBTILE)
        base = FFN_GUARD + r * FFN_ROWS

        def conv(h, c_ref, b_ref):
            taps = [h[base + il * (t - 1):base + il * (t - 1) + FFN_ROWS, cols] * c_ref[t:t + 1, cols]
                    for t in range(3)]
            return taps[0] + taps[1] + taps[2] + b_ref[:, cols]

        gate = conv(hg, cg_ref, bg_ref)
        up = conv(hu, cu_ref, bu_ref)
        o_ref[r * FFN_ROWS:(r + 1) * FFN_ROWS, cols] = (
            gate * (1.0 / (1.0 + jnp.exp(-gate))) * up).astype(o_ref.dtype)

    chunks = seq // FFN_ROWS
    for c in range(tn // FFN_SUBTILE):
        for r in range(chunks):
            matmuls(c, r)
            if r >= 2:
                gate_stage(c, r - 1)
        guards(c)
        gate_stage(c, chunks - 1)
        gate_stage(c, 0)


def _ffn_up(xn, w_up, layer, conv_w, conv_b, batch, seq, tn=512):
    d = xn.shape[1]
    nj = D_FF // tn
    return pl.pallas_call(
        functools.partial(_ffn_up_kernel, seq=seq),
        grid=(nj, batch),
        in_specs=[
            pl.BlockSpec((seq, d), lambda j, b: (b, 0)),
            pl.BlockSpec((None, d, tn), lambda j, b: (layer, 0, j)),
            pl.BlockSpec((None, d, tn), lambda j, b: (layer, 0, nj + j)),
            pl.BlockSpec((3, tn), lambda j, b: (0, j)),
            pl.BlockSpec((3, tn), lambda j, b: (0, nj + j)),
            pl.BlockSpec((1, tn), lambda j, b: (0, j)),
            pl.BlockSpec((1, tn), lambda j, b: (0, nj + j)),
        ],
        out_specs=pl.BlockSpec((seq, tn), lambda j, b: (b, j)),
        out_shape=jax.ShapeDtypeStruct((batch * seq, D_FF), BF16),
        scratch_shapes=[pltpu.VMEM((seq + 2 * FFN_GUARD, tn), F32)] * 2 + [pltpu.VMEM((d, tn), BF16)] * 2,
        compiler_params=_params(("parallel", "arbitrary"), 60),
        name="ffn_up",
    )(xn, w_up, w_up, conv_w, conv_w, conv_b, conv_b)


def _ffn_down_kernel(*refs, n_act):
    a_refs, (w_ref, x1_ref, g_ref, x2_ref, xn_ref) = refs[:n_act], refs[n_act:]
    k = pl.program_id(1)

    @pl.when(k == 0)
    def _():
        x2_ref[...] = x1_ref[...]

    act = jnp.concatenate([a_ref[...] for a_ref in a_refs], axis=0)
    x2_ref[...] += _dot(act, w_ref[...])

    @pl.when(k == pl.num_programs(1) - 1)
    def _():
        xn_ref[...] = _rms(x2_ref[...], g_ref[...]).astype(xn_ref.dtype)


def _ffn_down_final_kernel(*refs, n_act):
    *ins, xn_ref, acc_ref = refs
    _ffn_down_kernel(*ins, acc_ref, xn_ref, n_act=n_act)


def _ffn_down(act, w, layer, x1, g, final, seq, tm=1024, tk=512):
    m, kdim = act.shape
    d = w.shape[2]
    seg = seq // FFN_INTERLEAVE
    n_act = tm // seg
    per_seq = seq // tm
    kblocks = kdim // tk
    act_view = act.reshape(m // FFN_INTERLEAVE, FFN_INTERLEAVE * kdim)

    def act_spec(q):
        return pl.BlockSpec((seg, tk), lambda i, k: (i // per_seq, ((i % per_seq) * n_act + q) * kblocks + k))

    tile = pl.BlockSpec((tm, d), lambda i, k: (i, 0))
    in_specs = [act_spec(q) for q in range(n_act)] + [
        pl.BlockSpec((None, tk, d), lambda i, k: (layer, k, 0)),
        tile,
        pl.BlockSpec((1, d), lambda i, k: (0, 0)),
    ]
    common = dict(grid=(m // tm, kblocks), in_specs=in_specs,
                  compiler_params=_params(("parallel", "arbitrary"), 56))
    operands = [act_view] * n_act + [w, x1, g]
    if final:
        xn = pl.pallas_call(functools.partial(_ffn_down_final_kernel, n_act=n_act), out_specs=tile,
                            out_shape=jax.ShapeDtypeStruct((m, d), F32),
                            scratch_shapes=[pltpu.VMEM((tm, d), F32)], name="ffn_down_final", **common)(*operands)
        return None, xn
    return pl.pallas_call(functools.partial(_ffn_down_kernel, n_act=n_act), out_specs=[tile, tile],
                          out_shape=[jax.ShapeDtypeStruct((m, d), F32), jax.ShapeDtypeStruct((m, d), BF16)],
                          name="ffn_down", **common)(*operands)


def _rope_table(pos, dim, theta):
    inv = theta ** (-jnp.arange(0, dim, 2, dtype=F32) / dim)
    ang = pos.astype(F32)[:, None] * inv[None, :]
    return jnp.cos(ang), jnp.sin(ang)


def _lane_tables(seq):
    t = jnp.arange(seq)
    z = lambda w: jnp.zeros((seq, w), F32)
    one = lambda w: jnp.ones((seq, w), F32)
    cr, sr = _rope_table(t // GRID_W, HEAD_DIM // 2, A_ROPE_THETA)
    cc, sc = _rope_table(t % GRID_W, HEAD_DIM // 2, A_ROPE_THETA)
    tab_a = (jnp.concatenate([cr, cr, cc, cc], 1),
             jnp.concatenate([-sr, z(32), -sc, z(32)], 1),
             jnp.concatenate([z(32), sr, z(32), sc], 1))
    cp, sp = _rope_table(t, PARTIAL_ROPE_DIM, PARTIAL_ROPE_THETA)
    tab_b = (jnp.concatenate([cp, cp, one(96)], 1),
             jnp.concatenate([-sp, z(112)], 1),
             jnp.concatenate([z(16), sp, z(96)], 1))
    cm, sm = _rope_table(t, C_ROPE_DIM, C_ROPE_THETA)
    tab_c = (jnp.concatenate([cm, cm, one(64)], 1),
             jnp.concatenate([-sm, z(96)], 1),
             jnp.concatenate([z(32), sm, z(64)], 1))
    return tab_a, tab_b, tab_c


def _prep_w_ckr(w_in_t, layer):
    ckr = w_in_t[layer, IN_N_SRC * IN_WBLK:, :]
    return jnp.pad(ckr, ((0, IN_WBLK - ckr.shape[0]), (0, 0)))


def _prep_w_uq(w):
    w = w.reshape(C_Q_RANK, C_HEADS, C_NOPE_DIM + C_ROPE_DIM)
    w = jnp.pad(w, ((0, 0), (0, 0), (0, C_QK_PAD - C_NOPE_DIM - C_ROPE_DIM)))
    return w.reshape(C_Q_RANK, C_HEADS * C_QK_PAD).astype(BF16)


def _prep_w_ukv(w):
    w = w.reshape(C_KV_RANK, C_HEADS, C_NOPE_DIM + C_V_DIM)
    kn = w[:, :, :C_NOPE_DIM].reshape(C_KV_RANK, C_HEADS * C_NOPE_DIM)
    v = w[:, :, C_NOPE_DIM:].reshape(C_KV_RANK, C_HEADS * C_V_DIM)
    return jnp.concatenate([kn, v], axis=1).astype(BF16)


def kernel(x, attn_norm, w_in, a_q_norm, a_k_norm, c_q_norm, c_kv_norm, w_uq, w_ukv, out_norm, w_out,
           ffn_norm, w_up, conv_w, conv_b, w_down, final_norm):
    batch, seq, d = x.shape
    depth = w_in.shape[0]
    tab_a, tab_b, tab_c = _lane_tables(seq)
    row = lambda v: v.reshape(1, -1)

    xr = x.reshape(batch * seq, d)
    xn = _rmsnorm_rows(xr, row(attn_norm[0]))
    w_in_t = jnp.swapaxes(w_in, 1, 2)
    w_out_bf = w_out.astype(BF16)
    w_down_bf = w_down.astype(BF16)
    for l in range(depth):
        proj = _in_proj(xn, w_in_t, l, _prep_w_ckr(w_in_t, l))
        qc, kvc = _latent_up(proj, row(c_q_norm[l]), row(c_kv_norm[l]), _prep_w_uq(w_uq[l]),
                             _prep_w_ukv(w_ukv[l]))
        ya = _attn_a(proj, tab_a, row(a_q_norm[l]), row(a_k_norm[l]), batch, seq)
        yb = _attn_b(proj, tab_b, batch, seq)
        yc = _attn_c(qc, kvc, proj, tab_c, batch, seq)
        g = out_norm[l]
        x1, xn1 = _out_proj(ya, yb, yc, row(g[:A_WIDTH]), row(g[A_WIDTH:A_WIDTH + B_WIDTH]),
                            row(g[A_WIDTH + B_WIDTH:]), w_out_bf, l, xr, row(ffn_norm[l]), seq)
        act = _ffn_up(xn1, w_up, l, conv_w[l], row(conv_b[l]), batch, seq)
        last = l == depth - 1
        g_next = final_norm if last else attn_norm[l + 1]
        xr, xn = _ffn_down(act, w_down_bf, l, x1, row(g_next), last, seq)
    return xn.reshape(batch, seq, d)
```

```python
import functools
import math

import jax
import jax.numpy as jnp
from jax import lax
from jax.experimental import pallas as pl
from jax.experimental.pallas import tpu as pltpu

D_MODEL = 2048
HEAD_DIM = 128
A_HEADS = 6
A_KV_HEADS = 2
A_GROUP = A_HEADS // A_KV_HEADS
A_ROPE_THETA = 10000.0
B_HEADS = 4
B_PATTERNS = ((128, 1), (512, 4), (2048, 16))
B_N_GROUPS = 3
C_HEADS = 6
C_Q_RANK = 512
C_KV_RANK = 512
C_NOPE_DIM = 128
C_ROPE_DIM = 64
C_V_DIM = 128
C_ROPE_THETA = 10000.0
PARTIAL_ROPE_DIM = HEAD_DIM // 4
PARTIAL_ROPE_THETA = 500000.0
GRID_W = 64
D_FF = 5632
EPS = 1e-6

A_WIDTH = A_HEADS * HEAD_DIM
B_WIDTH = B_HEADS * HEAD_DIM
C_WIDTH = C_HEADS * C_V_DIM
C_QK_PAD = 256

LANES = 128
LOG2E = math.log2(math.e)
F32 = jnp.float32
BF16 = jnp.bfloat16

PROJ_WIDTH = 5120
CB_CQ, CB_CKV = 0, 1
CB_AQ, CB_AK, CB_AV = 8, 14, 16
CB_BQ, CB_BK, CB_BV = 18, 30, 34
CB_CKR = 38


def _params(semantics, vmem_mib):
    return pltpu.CompilerParams(dimension_semantics=semantics, vmem_limit_bytes=vmem_mib * 1024 * 1024)


def _rms(xf, g):
    return xf * lax.rsqrt(jnp.mean(xf * xf, axis=-1, keepdims=True) + EPS) * g


def _rope(xf, cos, sin_lo, sin_hi, shift):
    return xf * cos + pltpu.roll(xf, LANES - shift, 1) * sin_lo + pltpu.roll(xf, shift, 1) * sin_hi


def _dot(a, b):
    return jnp.dot(a, b, preferred_element_type=F32)


def _dot_nt(a, b):
    return lax.dot_general(a, b, (((1,), (1,)), ((), ())), preferred_element_type=F32)


def _norm_kernel(x_ref, g_ref, o_ref):
    o_ref[...] = _rms(x_ref[...], g_ref[...]).astype(o_ref.dtype)


def _rmsnorm_rows(x, g, tm=512):
    m, d = x.shape
    return pl.pallas_call(
        _norm_kernel,
        grid=(m // tm,),
        in_specs=[pl.BlockSpec((tm, d), lambda i: (i, 0)), pl.BlockSpec((1, d), lambda i: (0, 0))],
        out_specs=pl.BlockSpec((tm, d), lambda i: (i, 0)),
        out_shape=jax.ShapeDtypeStruct((m, d), BF16),
        compiler_params=_params(("parallel",), 40),
        name="rmsnorm",
    )(x, g)


IN_WBLK = 256
IN_TN = 1024
IN_SRC_CQ = 15
IN_N_SRC = 19


def _in_proj_kernel(a_ref, *refs):
    w_refs, wck_ref, o_ref, w_bf = refs[:-3], refs[-3], refs[-2], refs[-1]
    j = pl.program_id(0)
    last = pl.num_programs(0) - 1

    @pl.when(pl.program_id(1) == 0)
    def _():
        for q, w_ref in enumerate(w_refs[:-1]):
            w_bf[q * IN_WBLK:(q + 1) * IN_WBLK, :] = w_ref[...].astype(BF16)
        tail = slice((len(w_refs) - 1) * IN_WBLK, len(w_refs) * IN_WBLK)

        @pl.when(j < last)
        def _():
            w_bf[tail, :] = w_refs[-1][...].astype(BF16)

        @pl.when(j == last)
        def _():
            w_bf[tail, :] = wck_ref[...].astype(BF16)

    o_ref[...] = _dot_nt(a_ref[...], w_bf[...]).astype(o_ref.dtype)


def _in_proj(a, w_in_t, layer, w_ckr_t, tm=1024):
    m, k = a.shape
    per_step = IN_TN // IN_WBLK

    def w_spec(q):
        def index(j, i):
            n = j * per_step + q
            return layer, jnp.where(n < 4, n + IN_SRC_CQ, n - 4), 0
        return pl.BlockSpec((None, IN_WBLK, k), index)

    return pl.pallas_call(
        _in_proj_kernel,
        grid=(PROJ_WIDTH // IN_TN, m // tm),
        in_specs=[pl.BlockSpec((tm, k), lambda j, i: (i, 0))] + [w_spec(q) for q in range(per_step)]
        + [pl.BlockSpec((IN_WBLK, k), lambda j, i: (0, 0))],
        out_specs=pl.BlockSpec((tm, IN_TN), lambda j, i: (i, j)),
        out_shape=jax.ShapeDtypeStruct((m, PROJ_WIDTH), BF16),
        scratch_shapes=[pltpu.VMEM((IN_TN, k), BF16)],
        compiler_params=_params(("parallel", "arbitrary"), 48),
        name="in_proj",
    )(a, *([w_in_t] * per_step), w_ckr_t)


def _latent_up_kernel(cq_ref, ckv_ref, gq_ref, gkv_ref, wq_ref, wkv_ref, q_ref, kv_ref, *, q_scale):
    cq = _rms(cq_ref[...].astype(F32), gq_ref[...]).astype(BF16)
    q_ref[...] = (_dot(cq, wq_ref[...]) * q_scale).astype(q_ref.dtype)
    ckv = _rms(ckv_ref[...].astype(F32), gkv_ref[...]).astype(BF16)
    kv_ref[...] = _dot(ckv, wkv_ref[...]).astype(kv_ref.dtype)


def _latent_up(proj, gq, gkv, wq, wkv, tm=1024):
    m = proj.shape[0]
    nq, nkv = wq.shape[1], wkv.shape[1]
    q_scale = (C_NOPE_DIM + C_ROPE_DIM) ** -0.5 * LOG2E
    return pl.pallas_call(
        functools.partial(_latent_up_kernel, q_scale=q_scale),
        grid=(m // tm,),
        in_specs=[
            pl.BlockSpec((tm, C_Q_RANK), lambda i: (i, CB_CQ)),
            pl.BlockSpec((tm, C_KV_RANK), lambda i: (i, CB_CKV)),
            pl.BlockSpec((1, C_Q_RANK), lambda i: (0, 0)),
            pl.BlockSpec((1, C_KV_RANK), lambda i: (0, 0)),
            pl.BlockSpec((C_Q_RANK, nq), lambda i: (0, 0)),
            pl.BlockSpec((C_KV_RANK, nkv), lambda i: (0, 0)),
        ],
        out_specs=[pl.BlockSpec((tm, nq), lambda i: (i, 0)), pl.BlockSpec((tm, nkv), lambda i: (i, 0))],
        out_shape=[jax.ShapeDtypeStruct((m, nq), BF16), jax.ShapeDtypeStruct((m, nkv), BF16)],
        compiler_params=_params(("parallel",), 48),
        name="latent_up",
    )(proj, proj, gq, gkv, wq, wkv)


A_ROW_CHUNK = 128
C_ROW_CHUNK = 256


def _attend(q_all, k_ref, v1_ref, dv, chunk):
    outs = []
    for c in range(q_all.shape[0] // chunk):
        q = q_all[c * chunk:(c + 1) * chunk]
        s = _dot_nt(q, k_ref[...])
        p = jnp.exp2(s - jnp.max(s, axis=-1, keepdims=True))
        ol = _dot(p.astype(BF16), v1_ref[...])
        outs.append(ol[:, :dv] * (1.0 / ol[:, dv:]))
    return outs


def _attn_a_kernel(q0_ref, q1_ref, q2_ref, k_ref, v_ref, cos_ref, slo_ref, shi_ref, gq_ref, gk_ref,
                   o_ref, kbuf, v1buf, *, tq):
    qi = pl.program_id(2)

    @pl.when(qi == 0)
    def _():
        k = _rms(k_ref[...].astype(F32), gk_ref[...])
        kbuf[...] = _rope(k, cos_ref[...], slo_ref[...], shi_ref[...], 32).astype(BF16)
        v1buf[:, :HEAD_DIM] = v_ref[...]
        v1buf[:, HEAD_DIM:] = jnp.ones((v1buf.shape[0], HEAD_DIM), BF16)

    rows = pl.ds(pl.multiple_of(qi * tq, tq), tq)
    cos, slo, shi = cos_ref[rows, :], slo_ref[rows, :], shi_ref[rows, :]
    q_scale = HEAD_DIM ** -0.5 * LOG2E
    qs = []
    for q_ref in (q0_ref, q1_ref, q2_ref):
        q = _rms(q_ref[...].astype(F32), gq_ref[...])
        qs.append((_rope(q, cos, slo, shi, 32) * q_scale).astype(BF16))
    q_all = jnp.concatenate(qs, axis=0)
    per_head = tq // A_ROW_CHUNK
    for c, o in enumerate(_attend(q_all, kbuf, v1buf, HEAD_DIM, A_ROW_CHUNK)):
        g, r = divmod(c, per_head)
        o_ref[r * A_ROW_CHUNK:(r + 1) * A_ROW_CHUNK, g * HEAD_DIM:(g + 1) * HEAD_DIM] = o.astype(o_ref.dtype)


def _attn_a(proj, tabs, gq, gk, batch, seq, tq=1024):
    nq = seq // tq
    cos, slo, shi = tabs

    def q_spec(g):
        return pl.BlockSpec((tq, HEAD_DIM), lambda b, h, qi: (b * nq + qi, CB_AQ + h * A_GROUP + g))

    tab_spec = pl.BlockSpec((seq, LANES), lambda b, h, qi: (0, 0))
    gain_spec = pl.BlockSpec((1, HEAD_DIM), lambda b, h, qi: (0, 0))
    return pl.pallas_call(
        functools.partial(_attn_a_kernel, tq=tq),
        grid=(batch, A_KV_HEADS, nq),
        in_specs=[
            q_spec(0), q_spec(1), q_spec(2),
            pl.BlockSpec((seq, HEAD_DIM), lambda b, h, qi: (b, CB_AK + h)),
            pl.BlockSpec((seq, HEAD_DIM), lambda b, h, qi: (b, CB_AV + h)),
            tab_spec, tab_spec, tab_spec, gain_spec, gain_spec,
        ],
        out_specs=pl.BlockSpec((tq, A_GROUP * HEAD_DIM), lambda b, h, qi: (b * nq + qi, h)),
        out_shape=jax.ShapeDtypeStruct((batch * seq, A_WIDTH), BF16),
        scratch_shapes=[pltpu.VMEM((seq, HEAD_DIM), BF16), pltpu.VMEM((seq, 2 * HEAD_DIM), BF16)],
        compiler_params=_params(("parallel", "parallel", "arbitrary"), 48),
        name="attn_a",
    )(proj, proj, proj, proj, proj, cos, slo, shi, gq, gk)


def _attn_c_kernel(q_ref, kn_ref, v_ref, kr_ref, cos_ref, slo_ref, shi_ref, o_ref, kbuf, v1buf, *, tq):
    qi = pl.program_id(2)

    @pl.when(qi == 0)
    def _():
        kbuf[:, :C_NOPE_DIM] = kn_ref[...]
        kr = _rope(kr_ref[...].astype(F32), cos_ref[...], slo_ref[...], shi_ref[...], 32)
        kbuf[:, C_NOPE_DIM:] = kr.astype(BF16)
        v1buf[:, :C_V_DIM] = v_ref[...]
        v1buf[:, C_V_DIM:] = jnp.ones((v1buf.shape[0], C_V_DIM), BF16)

    rows = pl.ds(pl.multiple_of(qi * tq, tq), tq)
    q_rope = _rope(q_ref[:, C_NOPE_DIM:].astype(F32), cos_ref[rows, :], slo_ref[rows, :], shi_ref[rows, :], 32)
    q_all = jnp.concatenate([q_ref[:, :C_NOPE_DIM], q_rope.astype(BF16)], axis=1)
    for c, o in enumerate(_attend(q_all, kbuf, v1buf, C_V_DIM, C_ROW_CHUNK)):
        o_ref[c * C_ROW_CHUNK:(c + 1) * C_ROW_CHUNK, :] = o.astype(o_ref.dtype)


def _attn_c(qc, kvc, proj, tabs, batch, seq, tq=2048):
    nq = seq // tq
    cos, slo, shi = tabs
    tab_spec = pl.BlockSpec((seq, LANES), lambda b, h, qi: (0, 0))
    return pl.pallas_call(
        functools.partial(_attn_c_kernel, tq=tq),
        grid=(batch, C_HEADS, nq),
        in_specs=[
            pl.BlockSpec((tq, C_QK_PAD), lambda b, h, qi: (b * nq + qi, h)),
            pl.BlockSpec((seq, C_NOPE_DIM), lambda b, h, qi: (b, h)),
            pl.BlockSpec((seq, C_V_DIM), lambda b, h, qi: (b, C_HEADS + h)),
            pl.BlockSpec((seq, LANES), lambda b, h, qi: (b, CB_CKR)),
            tab_spec, tab_spec, tab_spec,
        ],
        out_specs=pl.BlockSpec((tq, C_V_DIM), lambda b, h, qi: (b * nq + qi, h)),
        out_shape=jax.ShapeDtypeStruct((batch * seq, C_WIDTH), BF16),
        scratch_shapes=[pltpu.VMEM((seq, C_QK_PAD), BF16), pltpu.VMEM((seq, 2 * C_V_DIM), BF16)],
        compiler_params=_params(("parallel", "parallel", "arbitrary"), 48),
        name="attn_c",
    )(qc, kvc, kvc, proj, cos, slo, shi)


B_QBLK = 128
B_PREP_ROWS = 256
B_UNROLL = 8


def _attn_b_kernel(q0_ref, q1_ref, q2_ref, k_ref, v_ref, cos_ref, slo_ref, shi_ref, o_ref,
                   q0f, q1f, q2f, kf, vf, m_s, l_s, acc_s, *, seq):
    q_refs = (q0_ref, q1_ref, q2_ref)
    q_bufs = (q0f, q1f, q2f)
    q_scale = HEAD_DIM ** -0.5 * LOG2E

    def prep(ci, carry):
        rows = pl.ds(pl.multiple_of(ci * B_PREP_ROWS, B_PREP_ROWS), B_PREP_ROWS)
        cos, slo, shi = cos_ref[rows, :], slo_ref[rows, :], shi_ref[rows, :]
        for q_ref, q_buf in zip(q_refs, q_bufs):
            q_buf[rows, :] = _rope(q_ref[rows, :].astype(F32), cos, slo, shi, 16) * q_scale
        kf[rows, :] = _rope(k_ref[rows, :].astype(F32), cos, slo, shi, 16)
        vf[rows, :] = v_ref[rows, :].astype(F32)
        return carry

    lax.fori_loop(0, seq // B_PREP_ROWS, prep, 0)

    for g, (window, dil) in reversed(list(enumerate(B_PATTERNS))):
        first = g == len(B_PATTERNS) - 1
        half = window // (2 * dil)
        length = seq // dil
        nblk = length // B_QBLK
        kwin = min(2 * B_QBLK, length)
        q_buf = q_bufs[g]

        def rows_of(start, size, dil=dil):
            return pl.ds(start, size) if dil == 1 else pl.ds(start, size, stride=dil)

        def block(n, carry, first=first, half=half, length=length, nblk=nblk, kwin=kwin, q_buf=q_buf, dil=dil,
                  rows_of=rows_of):
            r = n // nblk
            i = n % nblk
            k0 = jnp.clip(i * B_QBLK - half, 0, length - kwin)
            q_rows = rows_of(r + dil * B_QBLK * i, B_QBLK)
            k_rows = rows_of(r + dil * k0, kwin)
            q = q_buf[q_rows, :].astype(BF16)
            k = kf[k_rows, :].astype(BF16)
            v = vf[k_rows, :].astype(BF16)
            s = _dot_nt(q, k)
            qpos = i * B_QBLK + lax.broadcasted_iota(jnp.int32, (B_QBLK, kwin), 0)
            kpos = k0 + lax.broadcasted_iota(jnp.int32, (B_QBLK, kwin), 1)
            s = jnp.where(jnp.abs(kpos - qpos) <= half, s, -jnp.inf)
            m_b = jnp.max(s, axis=-1, keepdims=True)
            p = jnp.exp2(s - m_b)
            l_b = jnp.sum(p, axis=-1, keepdims=True)
            a_b = _dot(p.astype(BF16), v)
            full = (B_QBLK, HEAD_DIM)
            if first:
                m_s[q_rows, :] = jnp.broadcast_to(m_b, full)
                l_s[q_rows, :] = jnp.broadcast_to(l_b, full)
                acc_s[q_rows, :] = a_b
            else:
                m_o = m_s[q_rows, :]
                m_n = jnp.maximum(m_o, m_b)
                w_o = jnp.exp2(m_o - m_n)
                w_b = jnp.exp2(m_b - m_n)
                acc_s[q_rows, :] = acc_s[q_rows, :] * w_o + a_b * w_b
                l_s[q_rows, :] = l_s[q_rows, :] * w_o + l_b * w_b
                m_s[q_rows, :] = m_n
            return carry

        lax.fori_loop(0, seq // B_QBLK, block, 0, unroll=B_UNROLL)

    def finish(ci, carry):
        rows = pl.ds(pl.multiple_of(ci * B_PREP_ROWS, B_PREP_ROWS), B_PREP_ROWS)
        o_ref[rows, :] = (acc_s[rows, :] * (1.0 / l_s[rows, :])).astype(o_ref.dtype)
        return carry

    lax.fori_loop(0, seq // B_PREP_ROWS, finish, 0)


def _attn_b(proj, tabs, batch, seq):
    cos, slo, shi = tabs

    def q_spec(g):
        return pl.BlockSpec((seq, HEAD_DIM), lambda b, h: (b, CB_BQ + g * B_HEADS + h))

    tab_spec = pl.BlockSpec((seq, LANES), lambda b, h: (0, 0))
    slab = pltpu.VMEM((seq, HEAD_DIM), F32)
    return pl.pallas_call(
        functools.partial(_attn_b_kernel, seq=seq),
        grid=(batch, B_HEADS),
        in_specs=[
            q_spec(0), q_spec(1), q_spec(2),
            pl.BlockSpec((seq, HEAD_DIM), lambda b, h: (b, CB_BK + h)),
            pl.BlockSpec((seq, HEAD_DIM), lambda b, h: (b, CB_BV + h)),
            tab_spec, tab_spec, tab_spec,
        ],
        out_specs=pl.BlockSpec((seq, HEAD_DIM), lambda b, h: (b, h)),
        out_shape=jax.ShapeDtypeStruct((batch * seq, B_WIDTH), BF16),
        scratch_shapes=[slab] * 8,
        compiler_params=_params(("parallel", "parallel"), 48),
        name="attn_b",
    )(proj, proj, proj, proj, proj, cos, slo, shi)


def _out_kernel(ya_ref, yb_ref, yc_ref, ga_ref, gb_ref, gc_ref, w_ref, x_ref, gf_ref, x1_ref, xn_ref):
    y = jnp.concatenate([
        _rms(ya_ref[...].astype(F32), ga_ref[...]).astype(BF16),
        _rms(yb_ref[...].astype(F32), gb_ref[...]).astype(BF16),
        _rms(yc_ref[...].astype(F32), gc_ref[...]).astype(BF16),
    ], axis=1)
    x1 = x_ref[...] + _dot(y, w_ref[...])
    x1_ref[...] = x1
    xn_ref[...] = _rms(x1, gf_ref[...]).astype(xn_ref.dtype)


def _out_proj(ya, yb, yc, ga, gb, gc, w, layer, x, gf, tm=512):
    m, d = x.shape

    def rows(width):
        return pl.BlockSpec((tm, width), lambda i: (i, 0))

    def const(r, c):
        return pl.BlockSpec((r, c), lambda i: (0, 0))

    return pl.pallas_call(
        _out_kernel,
        grid=(m // tm,),
        in_specs=[rows(A_WIDTH), rows(B_WIDTH), rows(C_WIDTH), const(1, A_WIDTH), const(1, B_WIDTH),
                  const(1, C_WIDTH), pl.BlockSpec((None, d, d), lambda i: (layer, 0, 0)), rows(d), const(1, d)],
        out_specs=[rows(d), rows(d)],
        out_shape=[jax.ShapeDtypeStruct((m, d), F32), jax.ShapeDtypeStruct((m, d), BF16)],
        compiler_params=_params(("parallel",), 56),
        name="out_proj",
    )(ya, yb, yc, ga, gb, gc, w, x, gf)


FFN_SUBTILE = 256
FFN_ROWS = 256
FFN_GATE_ROWS = 64
FFN_INTERLEAVE = 8
FFN_GUARD = FFN_INTERLEAVE

def _ffn_up_kernel(xn_ref, wg_ref, wu_ref, cg_ref, cu_ref, bg_ref, bu_ref, o_ref, hg, hu, abuf, wg_bf, wu_bf, *,
                   seq):
    tn = o_ref.shape[1]
    il = FFN_INTERLEAVE
    seg = seq // il
    slabs = FFN_SUBTILE // LANES

    @pl.when(pl.program_id(1) == 0)
    def _():
        wg_bf[...] = wg_ref[...].astype(BF16)
        wu_bf[...] = wu_ref[...].astype(BF16)

    zeros = jnp.zeros((FFN_GUARD, LANES), F32)
    for h in (hg, hu):
        for s in range(tn // LANES):
            h[s, 0:FFN_GUARD, :] = zeros
            h[s, FFN_GUARD + seq:2 * FFN_GUARD + seq, :] = zeros

    def matmuls(c, r):
        cols = slice(c * FFN_SUBTILE, (c + 1) * FFN_SUBTILE)
        xr = xn_ref[r * FFN_ROWS:(r + 1) * FFN_ROWS, :]
        dst = slice(FFN_GUARD + r * FFN_ROWS, FFN_GUARD + (r + 1) * FFN_ROWS)
        for h, w in ((hg, wg_bf), (hu, wu_bf)):
            res = _dot(xr, w[:, cols])
            for s in range(slabs):
                h[c * slabs + s, dst, :] = res[:, s * LANES:(s + 1) * LANES]

    def gate_stage(c, r):
        base = FFN_GUARD + r * FFN_ROWS
        for s in range(slabs):
            slab = c * slabs + s
            lanes = slice(slab * LANES, (slab + 1) * LANES)

            def tap(h, t):
                if t == 1:
                    return h[slab, base:base + FFN_ROWS, :]
                return h[pl.ds(slab, 1, stride=2), pl.ds(base + t - 1, FFN_ROWS), :][0]

            def conv(h, c_ref, b_ref):
                taps = [tap(h, t) * c_ref[t:t + 1, lanes] for t in range(3)]
                return taps[0] + taps[1] + taps[2] + b_ref[:, lanes]

            gate = conv(hg, cg_ref, bg_ref)
            up = conv(hu, cu_ref, bu_ref)
            o_ref[r * FFN_ROWS:(r + 1) * FFN_ROWS, lanes] = (
                gate * (1.0 / (1.0 + jnp.exp(-gate))) * up).astype(o_ref.dtype)

    units = [(c, r) for c in range(tn // FFN_SUBTILE) for r in range(seq // FFN_ROWS)]
    for i, unit in enumerate(units):
        matmuls(*unit)
        if i > 0:
            gate_stage(*units[i - 1])
    gate_stage(*units[-1])


def _ffn_up(xn, w_up, layer, conv_w, conv_b, batch, seq, tn=512):
    d = xn.shape[1]
    nj = D_FF // tn
    return pl.pallas_call(
        functools.partial(_ffn_up_kernel, seq=seq),
        grid=(nj, batch),
        in_specs=[
            pl.BlockSpec((seq, d), lambda j, b: (b, 0)),
            pl.BlockSpec((None, d, tn), lambda j, b: (layer, 0, j)),
            pl.BlockSpec((None, d, tn), lambda j, b: (layer, 0, nj + j)),
            pl.BlockSpec((3, tn), lambda j, b: (0, j)),
            pl.BlockSpec((3, tn), lambda j, b: (0, nj + j)),
            pl.BlockSpec((1, tn), lambda j, b: (0, j)),
            pl.BlockSpec((1, tn), lambda j, b: (0, nj + j)),
        ],
        out_specs=pl.BlockSpec((seq, tn), lambda j, b: (b, j)),
        out_shape=jax.ShapeDtypeStruct((batch * seq, D_FF), BF16),
        scratch_shapes=[pltpu.VMEM((tn // LANES, seq + 2 * FFN_GUARD, LANES), F32)] * 2
        + [pltpu.VMEM((2 * FFN_SUBTILE // LANES, FFN_ROWS, LANES), F32)] + [pltpu.VMEM((d, tn), BF16)] * 2,
        compiler_params=_params(("parallel", "arbitrary"), 60),
        name="ffn_up",
    )(xn, w_up, w_up, conv_w, conv_w, conv_b, conv_b)


def _ffn_down_kernel(a_ref, w_ref, x1_ref, g_ref, x2_ref, xn_ref):
    k = pl.program_id(1)

    @pl.when(k == 0)
    def _():
        x2_ref[...] = x1_ref[...]

    x2_ref[...] += _dot(a_ref[...], w_ref[...])

    @pl.when(k == pl.num_programs(1) - 1)
    def _():
        xn_ref[...] = _rms(x2_ref[...], g_ref[...]).astype(xn_ref.dtype)


def _ffn_down_final_kernel(a_ref, w_ref, x1_ref, g_ref, xn_ref, acc_ref):
    _ffn_down_kernel(a_ref, w_ref, x1_ref, g_ref, acc_ref, xn_ref)


def _ffn_down(act, w, layer, x1, g, final, tm=1024, tk=512):
    m, kdim = act.shape
    d = w.shape[2]
    tile = pl.BlockSpec((tm, d), lambda i, k: (i, 0))
    in_specs = [
        pl.BlockSpec((tm, tk), lambda i, k: (i, k)),
        pl.BlockSpec((None, tk, d), lambda i, k: (layer, k, 0)),
        tile,
        pl.BlockSpec((1, d), lambda i, k: (0, 0)),
    ]
    common = dict(grid=(m // tm, kdim // tk), in_specs=in_specs,
                  compiler_params=_params(("parallel", "arbitrary"), 56))
    if final:
        xn = pl.pallas_call(_ffn_down_final_kernel, out_specs=tile, out_shape=jax.ShapeDtypeStruct((m, d), F32),
                            scratch_shapes=[pltpu.VMEM((tm, d), F32)], name="ffn_down_final", **common)(act, w, x1, g)
        return None, xn
    return pl.pallas_call(_ffn_down_kernel, out_specs=[tile, tile],
                          out_shape=[jax.ShapeDtypeStruct((m, d), F32), jax.ShapeDtypeStruct((m, d), BF16)],
                          name="ffn_down", **common)(act, w, x1, g)


def _rope_table(pos, dim, theta):
    inv = theta ** (-jnp.arange(0, dim, 2, dtype=F32) / dim)
    ang = pos.astype(F32)[:, None] * inv[None, :]
    return jnp.cos(ang), jnp.sin(ang)


def _lane_tables(seq):
    t = jnp.arange(seq)
    z = lambda w: jnp.zeros((seq, w), F32)
    one = lambda w: jnp.ones((seq, w), F32)
    cr, sr = _rope_table(t // GRID_W, HEAD_DIM // 2, A_ROPE_THETA)
    cc, sc = _rope_table(t % GRID_W, HEAD_DIM // 2, A_ROPE_THETA)
    tab_a = (jnp.concatenate([cr, cr, cc, cc], 1),
             jnp.concatenate([-sr, z(32), -sc, z(32)], 1),
             jnp.concatenate([z(32), sr, z(32), sc], 1))
    cp, sp = _rope_table(t, PARTIAL_ROPE_DIM, PARTIAL_ROPE_THETA)
    tab_b = (jnp.concatenate([cp, cp, one(96)], 1),
             jnp.concatenate([-sp, z(112)], 1),
             jnp.concatenate([z(16), sp, z(96)], 1))
    cm, sm = _rope_table(t, C_ROPE_DIM, C_ROPE_THETA)
    tab_c = (jnp.concatenate([cm, cm, one(64)], 1),
             jnp.concatenate([-sm, z(96)], 1),
             jnp.concatenate([z(32), sm, z(64)], 1))
    return tab_a, tab_b, tab_c


def _prep_w_ckr(w_in_t, layer):
    ckr = w_in_t[layer, IN_N_SRC * IN_WBLK:, :]
    return jnp.pad(ckr, ((0, IN_WBLK - ckr.shape[0]), (0, 0)))


def _prep_w_uq(w):
    w = w.reshape(C_Q_RANK, C_HEADS, C_NOPE_DIM + C_ROPE_DIM)
    w = jnp.pad(w, ((0, 0), (0, 0), (0, C_QK_PAD - C_NOPE_DIM - C_ROPE_DIM)))
    return w.reshape(C_Q_RANK, C_HEADS * C_QK_PAD).astype(BF16)


def _prep_w_ukv(w):
    w = w.reshape(C_KV_RANK, C_HEADS, C_NOPE_DIM + C_V_DIM)
    kn = w[:, :, :C_NOPE_DIM].reshape(C_KV_RANK, C_HEADS * C_NOPE_DIM)
    v = w[:, :, C_NOPE_DIM:].reshape(C_KV_RANK, C_HEADS * C_V_DIM)
    return jnp.concatenate([kn, v], axis=1).astype(BF16)


def kernel(x, attn_norm, w_in, a_q_norm, a_k_norm, c_q_norm, c_kv_norm, w_uq, w_ukv, out_norm, w_out,
           ffn_norm, w_up, conv_w, conv_b, w_down, final_norm):
    batch, seq, d = x.shape
    depth = w_in.shape[0]
    tab_a, tab_b, tab_c = _lane_tables(seq)
    row = lambda v: v.reshape(1, -1)

    xr = x.reshape(batch * seq, d)
    xn = _rmsnorm_rows(xr, row(attn_norm[0]))
    w_in_t = jnp.swapaxes(w_in, 1, 2)
    w_out_bf = w_out.astype(BF16)
    w_down_bf = w_down.astype(BF16)
    for l in range(depth):
        proj = _in_proj(xn, w_in_t, l, _prep_w_ckr(w_in_t, l))
        qc, kvc = _latent_up(proj, row(c_q_norm[l]), row(c_kv_norm[l]), _prep_w_uq(w_uq[l]),
                             _prep_w_ukv(w_ukv[l]))
        ya = _attn_a(proj, tab_a, row(a_q_norm[l]), row(a_k_norm[l]), batch, seq)
        yb = _attn_b(proj, tab_b, batch, seq)
        yc = _attn_c(qc, kvc, proj, tab_c, batch, seq)
        g = out_norm[l]
        x1, xn1 = _out_proj(ya, yb, yc, row(g[:A_WIDTH]), row(g[A_WIDTH:A_WIDTH + B_WIDTH]),
                            row(g[A_WIDTH + B_WIDTH:]), w_out_bf, l, xr, row(ffn_norm[l]))
        act = _ffn_up(xn1, w_up, l, conv_w[l], row(conv_b[l]), batch, seq)
        last = l == depth - 1
        g_next = final_norm if last else attn_norm[l + 1]
        xr, xn = _ffn_down(act, w_down_bf, l, x1, row(g_next), last)
    return xn.reshape(batch, seq, d)
```

```python
import functools
import math

import jax
import jax.numpy as jnp
from jax import lax
from jax.experimental import pallas as pl
from jax.experimental.pallas import tpu as pltpu

D_MODEL = 2048
HEAD_DIM = 128
A_HEADS = 6
A_KV_HEADS = 2
A_GROUP = A_HEADS // A_KV_HEADS
A_ROPE_THETA = 10000.0
B_HEADS = 4
B_PATTERNS = ((128, 1), (512, 4), (2048, 16))
B_N_GROUPS = 3
C_HEADS = 6
C_Q_RANK = 512
C_KV_RANK = 512
C_NOPE_DIM = 128
C_ROPE_DIM = 64
C_V_DIM = 128
C_ROPE_THETA = 10000.0
PARTIAL_ROPE_DIM = HEAD_DIM // 4
PARTIAL_ROPE_THETA = 500000.0
GRID_W = 64
D_FF = 5632
EPS = 1e-6

A_WIDTH = A_HEADS * HEAD_DIM
B_WIDTH = B_HEADS * HEAD_DIM
C_WIDTH = C_HEADS * C_V_DIM
C_QK_PAD = 256

LANES = 128
LOG2E = math.log2(math.e)
F32 = jnp.float32
BF16 = jnp.bfloat16

PROJ_WIDTH = 5120
CB_CQ, CB_CKV = 0, 1
CB_AQ, CB_AK, CB_AV = 8, 14, 16
CB_BQ, CB_BK, CB_BV = 18, 30, 34
CB_CKR = 38


def _params(semantics, vmem_mib):
    return pltpu.CompilerParams(dimension_semantics=semantics, vmem_limit_bytes=vmem_mib * 1024 * 1024)


def _rms(xf, g):
    return xf * lax.rsqrt(jnp.mean(xf * xf, axis=-1, keepdims=True) + EPS) * g


def _rope(xf, cos, sin_lo, sin_hi, shift):
    return xf * cos + pltpu.roll(xf, LANES - shift, 1) * sin_lo + pltpu.roll(xf, shift, 1) * sin_hi


def _dot(a, b):
    return jnp.dot(a, b, preferred_element_type=F32)


def _dot_nt(a, b):
    return lax.dot_general(a, b, (((1,), (1,)), ((), ())), preferred_element_type=F32)


def _norm_kernel(x_ref, g_ref, o_ref):
    o_ref[...] = _rms(x_ref[...], g_ref[...]).astype(o_ref.dtype)


def _rmsnorm_rows(x, g, tm=512):
    m, d = x.shape
    return pl.pallas_call(
        _norm_kernel,
        grid=(m // tm,),
        in_specs=[pl.BlockSpec((tm, d), lambda i: (i, 0)), pl.BlockSpec((1, d), lambda i: (0, 0))],
        out_specs=pl.BlockSpec((tm, d), lambda i: (i, 0)),
        out_shape=jax.ShapeDtypeStruct((m, d), BF16),
        compiler_params=_params(("parallel",), 40),
        name="rmsnorm",
    )(x, g)


IN_WBLK = 256
IN_TN = 1024
IN_SRC_CQ = 15
IN_N_SRC = 19


def _in_proj_kernel(a_ref, *refs):
    w_refs, wck_ref, o_ref, w_bf = refs[:-3], refs[-3], refs[-2], refs[-1]
    j = pl.program_id(0)
    last = pl.num_programs(0) - 1

    @pl.when(pl.program_id(1) == 0)
    def _():
        for q, w_ref in enumerate(w_refs[:-1]):
            w_bf[q * IN_WBLK:(q + 1) * IN_WBLK, :] = w_ref[...].astype(BF16)
        tail = slice((len(w_refs) - 1) * IN_WBLK, len(w_refs) * IN_WBLK)

        @pl.when(j < last)
        def _():
            w_bf[tail, :] = w_refs[-1][...].astype(BF16)

        @pl.when(j == last)
        def _():
            w_bf[tail, :] = wck_ref[...].astype(BF16)

    o_ref[...] = _dot_nt(a_ref[...], w_bf[...]).astype(o_ref.dtype)


def _in_proj(a, w_in_t, layer, w_ckr_t, tm=1024):
    m, k = a.shape
    per_step = IN_TN // IN_WBLK

    def w_spec(q):
        def index(j, i):
            n = j * per_step + q
            return layer, jnp.where(n < 4, n + IN_SRC_CQ, n - 4), 0
        return pl.BlockSpec((None, IN_WBLK, k), index)

    return pl.pallas_call(
        _in_proj_kernel,
        grid=(PROJ_WIDTH // IN_TN, m // tm),
        in_specs=[pl.BlockSpec((tm, k), lambda j, i: (i, 0))] + [w_spec(q) for q in range(per_step)]
        + [pl.BlockSpec((IN_WBLK, k), lambda j, i: (0, 0))],
        out_specs=pl.BlockSpec((tm, IN_TN), lambda j, i: (i, j)),
        out_shape=jax.ShapeDtypeStruct((m, PROJ_WIDTH), BF16),
        scratch_shapes=[pltpu.VMEM((IN_TN, k), BF16)],
        compiler_params=_params(("parallel", "arbitrary"), 48),
        name="in_proj",
    )(a, *([w_in_t] * per_step), w_ckr_t)


def _latent_up_kernel(cq_ref, ckv_ref, gq_ref, gkv_ref, wq_ref, wkv_ref, q_ref, kv_ref, *, q_scale):
    cq = _rms(cq_ref[...].astype(F32), gq_ref[...]).astype(BF16)
    q_ref[...] = (_dot(cq, wq_ref[...]) * q_scale).astype(q_ref.dtype)
    ckv = _rms(ckv_ref[...].astype(F32), gkv_ref[...]).astype(BF16)
    kv_ref[...] = _dot(ckv, wkv_ref[...]).astype(kv_ref.dtype)


def _latent_up(proj, gq, gkv, wq, wkv, tm=1024):
    m = proj.shape[0]
    nq, nkv = wq.shape[1], wkv.shape[1]
    q_scale = (C_NOPE_DIM + C_ROPE_DIM) ** -0.5 * LOG2E
    return pl.pallas_call(
        functools.partial(_latent_up_kernel, q_scale=q_scale),
        grid=(m // tm,),
        in_specs=[
            pl.BlockSpec((tm, C_Q_RANK), lambda i: (i, CB_CQ)),
            pl.BlockSpec((tm, C_KV_RANK), lambda i: (i, CB_CKV)),
            pl.BlockSpec((1, C_Q_RANK), lambda i: (0, 0)),
            pl.BlockSpec((1, C_KV_RANK), lambda i: (0, 0)),
            pl.BlockSpec((C_Q_RANK, nq), lambda i: (0, 0)),
            pl.BlockSpec((C_KV_RANK, nkv), lambda i: (0, 0)),
        ],
        out_specs=[pl.BlockSpec((tm, nq), lambda i: (i, 0)), pl.BlockSpec((tm, nkv), lambda i: (i, 0))],
        out_shape=[jax.ShapeDtypeStruct((m, nq), BF16), jax.ShapeDtypeStruct((m, nkv), BF16)],
        compiler_params=_params(("parallel",), 48),
        name="latent_up",
    )(proj, proj, gq, gkv, wq, wkv)


A_ROW_CHUNK = 128
C_ROW_CHUNK = 256


def _attend(q_all, k_ref, v1_ref, dv, chunk):
    outs = []
    for c in range(q_all.shape[0] // chunk):
        q = q_all[c * chunk:(c + 1) * chunk]
        s = _dot_nt(q, k_ref[...])
        p = jnp.exp2(s - jnp.max(s, axis=-1, keepdims=True))
        ol = _dot(p.astype(BF16), v1_ref[...])
        outs.append(ol[:, :dv] * (1.0 / ol[:, dv:]))
    return outs


def _attn_a_kernel(q0_ref, q1_ref, q2_ref, k_ref, v_ref, cos_ref, slo_ref, shi_ref, gq_ref, gk_ref,
                   o_ref, kbuf, v1buf, *, tq):
    qi = pl.program_id(2)

    @pl.when(qi == 0)
    def _():
        k = _rms(k_ref[...].astype(F32), gk_ref[...])
        kbuf[...] = _rope(k, cos_ref[...], slo_ref[...], shi_ref[...], 32).astype(BF16)
        v1buf[:, :HEAD_DIM] = v_ref[...]
        v1buf[:, HEAD_DIM:] = jnp.ones((v1buf.shape[0], HEAD_DIM), BF16)

    rows = pl.ds(pl.multiple_of(qi * tq, tq), tq)
    cos, slo, shi = cos_ref[rows, :], slo_ref[rows, :], shi_ref[rows, :]
    q_scale = HEAD_DIM ** -0.5 * LOG2E
    qs = []
    for q_ref in (q0_ref, q1_ref, q2_ref):
        q = _rms(q_ref[...].astype(F32), gq_ref[...])
        qs.append((_rope(q, cos, slo, shi, 32) * q_scale).astype(BF16))
    q_all = jnp.concatenate(qs, axis=0)
    per_head = tq // A_ROW_CHUNK
    for c, o in enumerate(_attend(q_all, kbuf, v1buf, HEAD_DIM, A_ROW_CHUNK)):
        g, r = divmod(c, per_head)
        o_ref[r * A_ROW_CHUNK:(r + 1) * A_ROW_CHUNK, g * HEAD_DIM:(g + 1) * HEAD_DIM] = o.astype(o_ref.dtype)


def _attn_a(proj, tabs, gq, gk, batch, seq, tq=1024):
    nq = seq // tq
    cos, slo, shi = tabs

    def q_spec(g):
        return pl.BlockSpec((tq, HEAD_DIM), lambda b, h, qi: (b * nq + qi, CB_AQ + h * A_GROUP + g))

    tab_spec = pl.BlockSpec((seq, LANES), lambda b, h, qi: (0, 0))
    gain_spec = pl.BlockSpec((1, HEAD_DIM), lambda b, h, qi: (0, 0))
    return pl.pallas_call(
        functools.partial(_attn_a_kernel, tq=tq),
        grid=(batch, A_KV_HEADS, nq),
        in_specs=[
            q_spec(0), q_spec(1), q_spec(2),
            pl.BlockSpec((seq, HEAD_DIM), lambda b, h, qi: (b, CB_AK + h)),
            pl.BlockSpec((seq, HEAD_DIM), lambda b, h, qi: (b, CB_AV + h)),
            tab_spec, tab_spec, tab_spec, gain_spec, gain_spec,
        ],
        out_specs=pl.BlockSpec((tq, A_GROUP * HEAD_DIM), lambda b, h, qi: (b * nq + qi, h)),
        out_shape=jax.ShapeDtypeStruct((batch * seq, A_WIDTH), BF16),
        scratch_shapes=[pltpu.VMEM((seq, HEAD_DIM), BF16), pltpu.VMEM((seq, 2 * HEAD_DIM), BF16)],
        compiler_params=_params(("parallel", "parallel", "arbitrary"), 48),
        name="attn_a",
    )(proj, proj, proj, proj, proj, cos, slo, shi, gq, gk)


def _attn_c_kernel(q_ref, kn_ref, v_ref, kr_ref, cos_ref, slo_ref, shi_ref, o_ref, kbuf, v1buf, *, tq):
    qi = pl.program_id(2)

    @pl.when(qi == 0)
    def _():
        kbuf[:, :C_NOPE_DIM] = kn_ref[...]
        kr = _rope(kr_ref[...].astype(F32), cos_ref[...], slo_ref[...], shi_ref[...], 32)
        kbuf[:, C_NOPE_DIM:] = kr.astype(BF16)
        v1buf[:, :C_V_DIM] = v_ref[...]
        v1buf[:, C_V_DIM:] = jnp.ones((v1buf.shape[0], C_V_DIM), BF16)

    rows = pl.ds(pl.multiple_of(qi * tq, tq), tq)
    q_rope = _rope(q_ref[:, C_NOPE_DIM:].astype(F32), cos_ref[rows, :], slo_ref[rows, :], shi_ref[rows, :], 32)
    q_all = jnp.concatenate([q_ref[:, :C_NOPE_DIM], q_rope.astype(BF16)], axis=1)
    for c, o in enumerate(_attend(q_all, kbuf, v1buf, C_V_DIM, C_ROW_CHUNK)):
        o_ref[c * C_ROW_CHUNK:(c + 1) * C_ROW_CHUNK, :] = o.astype(o_ref.dtype)


def _attn_c(qc, kvc, proj, tabs, batch, seq, tq=2048):
    nq = seq // tq
    cos, slo, shi = tabs
    tab_spec = pl.BlockSpec((seq, LANES), lambda b, h, qi: (0, 0))
    return pl.pallas_call(
        functools.partial(_attn_c_kernel, tq=tq),
        grid=(batch, C_HEADS, nq),
        in_specs=[
            pl.BlockSpec((tq, C_QK_PAD), lambda b, h, qi: (b * nq + qi, h)),
            pl.BlockSpec((seq, C_NOPE_DIM), lambda b, h, qi: (b, h)),
            pl.BlockSpec((seq, C_V_DIM), lambda b, h, qi: (b, C_HEADS + h)),
            pl.BlockSpec((seq, LANES), lambda b, h, qi: (b, CB_CKR)),
            tab_spec, tab_spec, tab_spec,
        ],
        out_specs=pl.BlockSpec((tq, C_V_DIM), lambda b, h, qi: (b * nq + qi, h)),
        out_shape=jax.ShapeDtypeStruct((batch * seq, C_WIDTH), BF16),
        scratch_shapes=[pltpu.VMEM((seq, C_QK_PAD), BF16), pltpu.VMEM((seq, 2 * C_V_DIM), BF16)],
        compiler_params=_params(("parallel", "parallel", "arbitrary"), 48),
        name="attn_c",
    )(qc, kvc, kvc, proj, cos, slo, shi)


B_QBLK = 128
(B_HALF,) = {w // (2 * d) for w, d in B_PATTERNS}
assert 2 * B_HALF == B_QBLK
B_PREP_ROWS = 256
B_UNROLL = 8


def _attn_b_kernel(q0_ref, q1_ref, q2_ref, k_ref, v_ref, cos_ref, slo_ref, shi_ref, o_ref,
                   q0f, q1f, q2f, kf, vf, m_s, l_s, acc_s, bias_s, *, seq):
    q_refs = (q0_ref, q1_ref, q2_ref)
    q_bufs = (q0f, q1f, q2f)
    q_scale = HEAD_DIM ** -0.5 * LOG2E

    half = B_HALF
    shape = (B_QBLK, 2 * B_QBLK)
    rel = lax.broadcasted_iota(jnp.int32, shape, 1) - lax.broadcasted_iota(jnp.int32, shape, 0)
    for case, offset in enumerate((0, -half, -B_QBLK)):
        bias_s[case] = jnp.where(jnp.abs(rel + offset) <= half, 0.0, -jnp.inf)

    def prep(ci, carry):
        rows = pl.ds(pl.multiple_of(ci * B_PREP_ROWS, B_PREP_ROWS), B_PREP_ROWS)
        cos, slo, shi = cos_ref[rows, :], slo_ref[rows, :], shi_ref[rows, :]
        for q_ref, q_buf in zip(q_refs, q_bufs):
            q_buf[rows, :] = _rope(q_ref[rows, :].astype(F32), cos, slo, shi, 16) * q_scale
        kf[rows, :] = _rope(k_ref[rows, :].astype(F32), cos, slo, shi, 16)
        vf[rows, :] = v_ref[rows, :].astype(F32)
        return carry

    lax.fori_loop(0, seq // B_PREP_ROWS, prep, 0)

    for g, (window, dil) in reversed(list(enumerate(B_PATTERNS))):
        first = g == len(B_PATTERNS) - 1
        half = window // (2 * dil)
        length = seq // dil
        nblk = length // B_QBLK
        kwin = min(2 * B_QBLK, length)
        q_buf = q_bufs[g]

        def rows_of(start, size, dil=dil):
            return pl.ds(start, size) if dil == 1 else pl.ds(start, size, stride=dil)

        def block(n, carry, first=first, half=half, length=length, nblk=nblk, kwin=kwin, q_buf=q_buf, dil=dil,
                  rows_of=rows_of):
            r = n // nblk
            i = n % nblk
            k0 = jnp.clip(i * B_QBLK - half, 0, length - kwin)
            q_rows = rows_of(r + dil * B_QBLK * i, B_QBLK)
            k_rows = rows_of(r + dil * k0, kwin)
            q = q_buf[q_rows, :].astype(BF16)
            k = kf[k_rows, :].astype(BF16)
            v = vf[k_rows, :].astype(BF16)
            case = 0 if nblk == 1 else jnp.where(i == 0, 0, jnp.where(i == nblk - 1, 2, 1))
            s = _dot_nt(q, k) + bias_s[case, :, :kwin]
            m_b = jnp.max(s, axis=-1, keepdims=True)
            p = jnp.exp2(s - m_b)
            l_b = jnp.sum(p, axis=-1, keepdims=True)
            a_b = _dot(p.astype(BF16), v)
            full = (B_QBLK, HEAD_DIM)
            if first:
                m_s[q_rows, :] = jnp.broadcast_to(m_b, full)
                l_s[q_rows, :] = jnp.broadcast_to(l_b, full)
                acc_s[q_rows, :] = a_b
            else:
                m_o = m_s[q_rows, :]
                m_n = jnp.maximum(m_o, m_b)
                w_o = jnp.exp2(m_o - m_n)
                w_b = jnp.exp2(m_b - m_n)
                acc_s[q_rows, :] = acc_s[q_rows, :] * w_o + a_b * w_b
                l_s[q_rows, :] = l_s[q_rows, :] * w_o + l_b * w_b
                m_s[q_rows, :] = m_n
            return carry

        lax.fori_loop(0, seq // B_QBLK, block, 0, unroll=B_UNROLL)

    def finish(ci, carry):
        rows = pl.ds(pl.multiple_of(ci * B_PREP_ROWS, B_PREP_ROWS), B_PREP_ROWS)
        o_ref[rows, :] = (acc_s[rows, :] * (1.0 / l_s[rows, :])).astype(o_ref.dtype)
        return carry

    lax.fori_loop(0, seq // B_PREP_ROWS, finish, 0)


def _attn_b(proj, tabs, batch, seq):
    cos, slo, shi = tabs

    def q_spec(g):
        return pl.BlockSpec((seq, HEAD_DIM), lambda b, h: (b, CB_BQ + g * B_HEADS + h))

    tab_spec = pl.BlockSpec((seq, LANES), lambda b, h: (0, 0))
    slab = pltpu.VMEM((seq, HEAD_DIM), F32)
    return pl.pallas_call(
        functools.partial(_attn_b_kernel, seq=seq),
        grid=(batch, B_HEADS),
        in_specs=[
            q_spec(0), q_spec(1), q_spec(2),
            pl.BlockSpec((seq, HEAD_DIM), lambda b, h: (b, CB_BK + h)),
            pl.BlockSpec((seq, HEAD_DIM), lambda b, h: (b, CB_BV + h)),
            tab_spec, tab_spec, tab_spec,
        ],
        out_specs=pl.BlockSpec((seq, HEAD_DIM), lambda b, h: (b, h)),
        out_shape=jax.ShapeDtypeStruct((batch * seq, B_WIDTH), BF16),
        scratch_shapes=[slab] * 8 + [pltpu.VMEM((3, B_QBLK, 2 * B_QBLK), F32)],
        compiler_params=_params(("parallel", "parallel"), 48),
        name="attn_b",
    )(proj, proj, proj, proj, proj, cos, slo, shi)


def _out_kernel(ya_ref, yb_ref, yc_ref, ga_ref, gb_ref, gc_ref, w_ref, x_ref, gf_ref, x1_ref, xn_ref):
    y = jnp.concatenate([
        _rms(ya_ref[...].astype(F32), ga_ref[...]).astype(BF16),
        _rms(yb_ref[...].astype(F32), gb_ref[...]).astype(BF16),
        _rms(yc_ref[...].astype(F32), gc_ref[...]).astype(BF16),
    ], axis=1)
    x1 = x_ref[...] + _dot(y, w_ref[...])
    x1_ref[...] = x1
    xn_ref[...] = _rms(x1, gf_ref[...]).astype(xn_ref.dtype)


def _out_proj(ya, yb, yc, ga, gb, gc, w, layer, x, gf, tm=512):
    m, d = x.shape

    def rows(width):
        return pl.BlockSpec((tm, width), lambda i: (i, 0))

    def const(r, c):
        return pl.BlockSpec((r, c), lambda i: (0, 0))

    return pl.pallas_call(
        _out_kernel,
        grid=(m // tm,),
        in_specs=[rows(A_WIDTH), rows(B_WIDTH), rows(C_WIDTH), const(1, A_WIDTH), const(1, B_WIDTH),
                  const(1, C_WIDTH), pl.BlockSpec((None, d, d), lambda i: (layer, 0, 0)), rows(d), const(1, d)],
        out_specs=[rows(d), rows(d)],
        out_shape=[jax.ShapeDtypeStruct((m, d), F32), jax.ShapeDtypeStruct((m, d), BF16)],
        compiler_params=_params(("parallel",), 56),
        name="out_proj",
    )(ya, yb, yc, ga, gb, gc, w, x, gf)


FFN_SUBTILE = 256
FFN_ROWS = 512
FFN_GATE_ROWS = 64
FFN_INTERLEAVE = 8
FFN_GUARD = FFN_INTERLEAVE

def _ffn_up_kernel(xn_ref, wg_ref, wu_ref, cg_ref, cu_ref, bg_ref, bu_ref, o_ref, hg, hu, abuf, wg_bf, wu_bf, *,
                   seq):
    tn = o_ref.shape[1]
    il = FFN_INTERLEAVE
    seg = seq // il
    slabs = FFN_SUBTILE // LANES

    @pl.when(pl.program_id(1) == 0)
    def _():
        wg_bf[...] = wg_ref[...].astype(BF16)
        wu_bf[...] = wu_ref[...].astype(BF16)

    zeros = jnp.zeros((FFN_GUARD, LANES), F32)
    for h in (hg, hu):
        for s in range(tn // LANES):
            h[s, 0:FFN_GUARD, :] = zeros
            h[s, FFN_GUARD + seq:2 * FFN_GUARD + seq, :] = zeros

    def matmuls(c, r):
        cols = slice(c * FFN_SUBTILE, (c + 1) * FFN_SUBTILE)
        xr = xn_ref[r * FFN_ROWS:(r + 1) * FFN_ROWS, :]
        dst = slice(FFN_GUARD + r * FFN_ROWS, FFN_GUARD + (r + 1) * FFN_ROWS)
        for h, w in ((hg, wg_bf), (hu, wu_bf)):
            res = _dot(xr, w[:, cols])
            for s in range(slabs):
                h[c * slabs + s, dst, :] = res[:, s * LANES:(s + 1) * LANES]

    def gate_stage(c, r):
        for s in range(slabs):
            slab = c * slabs + s
            lanes = slice(slab * LANES, (slab + 1) * LANES)
            for k in range(FFN_ROWS // FFN_GATE_ROWS):
                row0 = r * FFN_ROWS + k * FFN_GATE_ROWS
                base = FFN_GUARD + row0

                def tap(h, t):
                    if t == 1:
                        return h[slab, base:base + FFN_GATE_ROWS, :]
                    return h[pl.ds(slab, 1, stride=2), pl.ds(base + t - 1, FFN_GATE_ROWS), :][0]

                def conv(h, c_ref, b_ref):
                    taps = [tap(h, t) * c_ref[t:t + 1, lanes] for t in range(3)]
                    return taps[0] + taps[1] + taps[2] + b_ref[:, lanes]

                gate = conv(hg, cg_ref, bg_ref)
                up = conv(hu, cu_ref, bu_ref)
                o_ref[row0:row0 + FFN_GATE_ROWS, lanes] = (
                    gate * (1.0 / (1.0 + jnp.exp2(gate * -LOG2E))) * up).astype(o_ref.dtype)

    units = [(c, r) for c in range(tn // FFN_SUBTILE) for r in range(seq // FFN_ROWS)]
    for i, unit in enumerate(units):
        matmuls(*unit)
        if i > 0:
            gate_stage(*units[i - 1])
    gate_stage(*units[-1])


def _ffn_up(xn, w_up, layer, conv_w, conv_b, batch, seq, tn=512):
    d = xn.shape[1]
    nj = D_FF // tn
    return pl.pallas_call(
        functools.partial(_ffn_up_kernel, seq=seq),
        grid=(nj, batch),
        in_specs=[
            pl.BlockSpec((seq, d), lambda j, b: (b, 0)),
            pl.BlockSpec((None, d, tn), lambda j, b: (layer, 0, j)),
            pl.BlockSpec((None, d, tn), lambda j, b: (layer, 0, nj + j)),
            pl.BlockSpec((3, tn), lambda j, b: (0, j)),
            pl.BlockSpec((3, tn), lambda j, b: (0, nj + j)),
            pl.BlockSpec((1, tn), lambda j, b: (0, j)),
            pl.BlockSpec((1, tn), lambda j, b: (0, nj + j)),
        ],
        out_specs=pl.BlockSpec((seq, tn), lambda j, b: (b, j)),
        out_shape=jax.ShapeDtypeStruct((batch * seq, D_FF), BF16),
        scratch_shapes=[pltpu.VMEM((tn // LANES, seq + 2 * FFN_GUARD, LANES), F32)] * 2
        + [pltpu.VMEM((2 * FFN_SUBTILE // LANES, FFN_ROWS, LANES), F32)] + [pltpu.VMEM((d, tn), BF16)] * 2,
        compiler_params=_params(("parallel", "arbitrary"), 60),
        name="ffn_up",
    )(xn, w_up, w_up, conv_w, conv_w, conv_b, conv_b)


def _ffn_down_kernel(a_ref, w_ref, x1_ref, g_ref, x2_ref, xn_ref):
    k = pl.program_id(1)

    @pl.when(k == 0)
    def _():
        x2_ref[...] = x1_ref[...]

    x2_ref[...] += _dot(a_ref[...], w_ref[...])

    @pl.when(k == pl.num_programs(1) - 1)
    def _():
        xn_ref[...] = _rms(x2_ref[...], g_ref[...]).astype(xn_ref.dtype)


def _ffn_down_final_kernel(a_ref, w_ref, x1_ref, g_ref, xn_ref, acc_ref):
    _ffn_down_kernel(a_ref, w_ref, x1_ref, g_ref, acc_ref, xn_ref)


def _ffn_down(act, w, layer, x1, g, final, tm=1024, tk=512):
    m, kdim = act.shape
    d = w.shape[2]
    tile = pl.BlockSpec((tm, d), lambda i, k: (i, 0))
    in_specs = [
        pl.BlockSpec((tm, tk), lambda i, k: (i, k)),
        pl.BlockSpec((None, tk, d), lambda i, k: (layer, k, 0)),
        tile,
        pl.BlockSpec((1, d), lambda i, k: (0, 0)),
    ]
    common = dict(grid=(m // tm, kdim // tk), in_specs=in_specs,
                  compiler_params=_params(("parallel", "arbitrary"), 56))
    if final:
        xn = pl.pallas_call(_ffn_down_final_kernel, out_specs=tile, out_shape=jax.ShapeDtypeStruct((m, d), F32),
                            scratch_shapes=[pltpu.VMEM((tm, d), F32)], name="ffn_down_final", **common)(act, w, x1, g)
        return None, xn
    return pl.pallas_call(_ffn_down_kernel, out_specs=[tile, tile],
                          out_shape=[jax.ShapeDtypeStruct((m, d), F32), jax.ShapeDtypeStruct((m, d), BF16)],
                          name="ffn_down", **common)(act, w, x1, g)


def _rope_table(pos, dim, theta):
    inv = theta ** (-jnp.arange(0, dim, 2, dtype=F32) / dim)
    ang = pos.astype(F32)[:, None] * inv[None, :]
    return jnp.cos(ang), jnp.sin(ang)


def _lane_tables(seq):
    t = jnp.arange(seq)
    z = lambda w: jnp.zeros((seq, w), F32)
    one = lambda w: jnp.ones((seq, w), F32)
    cr, sr = _rope_table(t // GRID_W, HEAD_DIM // 2, A_ROPE_THETA)
    cc, sc = _rope_table(t % GRID_W, HEAD_DIM // 2, A_ROPE_THETA)
    tab_a = (jnp.concatenate([cr, cr, cc, cc], 1),
             jnp.concatenate([-sr, z(32), -sc, z(32)], 1),
             jnp.concatenate([z(32), sr, z(32), sc], 1))
    cp, sp = _rope_table(t, PARTIAL_ROPE_DIM, PARTIAL_ROPE_THETA)
    tab_b = (jnp.concatenate([cp, cp, one(96)], 1),
             jnp.concatenate([-sp, z(112)], 1),
             jnp.concatenate([z(16), sp, z(96)], 1))
    cm, sm = _rope_table(t, C_ROPE_DIM, C_ROPE_THETA)
    tab_c = (jnp.concatenate([cm, cm, one(64)], 1),
             jnp.concatenate([-sm, z(96)], 1),
             jnp.concatenate([z(32), sm, z(64)], 1))
    return tab_a, tab_b, tab_c


def _prep_w_ckr(w_in_t, layer):
    ckr = w_in_t[layer, IN_N_SRC * IN_WBLK:, :]
    return jnp.pad(ckr, ((0, IN_WBLK - ckr.shape[0]), (0, 0)))


def _prep_w_uq(w):
    w = w.reshape(C_Q_RANK, C_HEADS, C_NOPE_DIM + C_ROPE_DIM)
    w = jnp.pad(w, ((0, 0), (0, 0), (0, C_QK_PAD - C_NOPE_DIM - C_ROPE_DIM)))
    return w.reshape(C_Q_RANK, C_HEADS * C_QK_PAD).astype(BF16)


def _prep_w_ukv(w):
    w = w.reshape(C_KV_RANK, C_HEADS, C_NOPE_DIM + C_V_DIM)
    kn = w[:, :, :C_NOPE_DIM].reshape(C_KV_RANK, C_HEADS * C_NOPE_DIM)
    v = w[:, :, C_NOPE_DIM:].reshape(C_KV_RANK, C_HEADS * C_V_DIM)
    return jnp.concatenate([kn, v], axis=1).astype(BF16)


def kernel(x, attn_norm, w_in, a_q_norm, a_k_norm, c_q_norm, c_kv_norm, w_uq, w_ukv, out_norm, w_out,
           ffn_norm, w_up, conv_w, conv_b, w_down, final_norm):
    batch, seq, d = x.shape
    depth = w_in.shape[0]
    tab_a, tab_b, tab_c = _lane_tables(seq)
    row = lambda v: v.reshape(1, -1)

    xr = x.reshape(batch * seq, d)
    xn = _rmsnorm_rows(xr, row(attn_norm[0]))
    w_in_t = jnp.swapaxes(w_in, 1, 2)
    w_out_bf = w_out.astype(BF16)
    w_down_bf = w_down.astype(BF16)
    for l in range(depth):
        proj = _in_proj(xn, w_in_t, l, _prep_w_ckr(w_in_t, l))
        qc, kvc = _latent_up(proj, row(c_q_norm[l]), row(c_kv_norm[l]), _prep_w_uq(w_uq[l]),
                             _prep_w_ukv(w_ukv[l]))
        ya = _attn_a(proj, tab_a, row(a_q_norm[l]), row(a_k_norm[l]), batch, seq)
        yb = _attn_b(proj, tab_b, batch, seq)
        yc = _attn_c(qc, kvc, proj, tab_c, batch, seq)
        g = out_norm[l]
        x1, xn1 = _out_proj(ya, yb, yc, row(g[:A_WIDTH]), row(g[A_WIDTH:A_WIDTH + B_WIDTH]),
                            row(g[A_WIDTH + B_WIDTH:]), w_out_bf, l, xr, row(ffn_norm[l]))
        act = _ffn_up(xn1, w_up, l, conv_w[l], row(conv_b[l]), batch, seq)
        last = l == depth - 1
        g_next = final_norm if last else attn_norm[l + 1]
        xr, xn = _ffn_down(act, w_down_bf, l, x1, row(g_next), last)
    return xn.reshape(batch, seq, d)
```

```python
import functools
import math

import jax
import jax.numpy as jnp
import numpy as np
from jax import lax
from jax.experimental import pallas as pl
from jax.experimental.pallas import tpu as pltpu

D_MODEL = 2048
HEAD_DIM = 128
A_HEADS = 6
A_KV_HEADS = 2
A_GROUP = A_HEADS // A_KV_HEADS
A_ROPE_THETA = 10000.0
B_HEADS = 4
B_PATTERNS = ((128, 1), (512, 4), (2048, 16))
B_N_GROUPS = 3
C_HEADS = 6
C_Q_RANK = 512
C_KV_RANK = 512
C_NOPE_DIM = 128
C_ROPE_DIM = 64
C_V_DIM = 128
C_ROPE_THETA = 10000.0
PARTIAL_ROPE_DIM = HEAD_DIM // 4
PARTIAL_ROPE_THETA = 500000.0
GRID_W = 64
D_FF = 5632
EPS = 1e-6

A_WIDTH = A_HEADS * HEAD_DIM
B_WIDTH = B_HEADS * HEAD_DIM
C_WIDTH = C_HEADS * C_V_DIM
C_QK_PAD = 256

LANES = 128
LOG2E = math.log2(math.e)
F32 = jnp.float32
BF16 = jnp.bfloat16

PROJ_WIDTH = 5120
CB_CQ, CB_CKV = 0, 1
CB_AQ, CB_AK, CB_AV = 8, 14, 16
CB_BQ, CB_BK, CB_BV = 18, 30, 34
CB_CKR = 38


def _params(semantics, vmem_mib):
    return pltpu.CompilerParams(dimension_semantics=semantics, vmem_limit_bytes=vmem_mib * 1024 * 1024)


def _rms(xf, g):
    return xf * lax.rsqrt(jnp.mean(xf * xf, axis=-1, keepdims=True) + EPS) * g


def _rope(xf, cos, sin_lo, sin_hi, shift):
    return xf * cos + pltpu.roll(xf, LANES - shift, 1) * sin_lo + pltpu.roll(xf, shift, 1) * sin_hi


def _dot(a, b):
    return jnp.dot(a, b, preferred_element_type=F32)


def _dot_nt(a, b):
    return lax.dot_general(a, b, (((1,), (1,)), ((), ())), preferred_element_type=F32)


def _norm_kernel(x_ref, g_ref, o_ref):
    o_ref[...] = _rms(x_ref[...], g_ref[...]).astype(o_ref.dtype)


def _rmsnorm_rows(x, g, tm=512):
    m, d = x.shape
    return pl.pallas_call(
        _norm_kernel,
        grid=(m // tm,),
        in_specs=[pl.BlockSpec((tm, d), lambda i: (i, 0)), pl.BlockSpec((1, d), lambda i: (0, 0))],
        out_specs=pl.BlockSpec((tm, d), lambda i: (i, 0)),
        out_shape=jax.ShapeDtypeStruct((m, d), BF16),
        compiler_params=_params(("parallel",), 40),
        name="rmsnorm",
    )(x, g)


IN_WBLK = 256
IN_TN = 1024
IN_SRC_CQ = 15
IN_N_SRC = 19


def _in_proj_kernel(a_ref, *refs):
    w_refs, wck_ref, o_ref, w_bf = refs[:-3], refs[-3], refs[-2], refs[-1]
    j = pl.program_id(0)
    last = pl.num_programs(0) - 1

    @pl.when(pl.program_id(1) == 0)
    def _():
        for q, w_ref in enumerate(w_refs[:-1]):
            w_bf[q * IN_WBLK:(q + 1) * IN_WBLK, :] = w_ref[...].astype(BF16)
        tail = slice((len(w_refs) - 1) * IN_WBLK, len(w_refs) * IN_WBLK)

        @pl.when(j < last)
        def _():
            w_bf[tail, :] = w_refs[-1][...].astype(BF16)

        @pl.when(j == last)
        def _():
            w_bf[tail, :] = wck_ref[...].astype(BF16)

    o_ref[...] = _dot_nt(a_ref[...], w_bf[...]).astype(o_ref.dtype)


def _in_proj(a, w_in_t, layer, w_ckr_t, tm=1024):
    m, k = a.shape
    per_step = IN_TN // IN_WBLK

    def w_spec(q):
        def index(j, i):
            n = j * per_step + q
            return layer, jnp.where(n < 4, n + IN_SRC_CQ, n - 4), 0
        return pl.BlockSpec((None, IN_WBLK, k), index)

    return pl.pallas_call(
        _in_proj_kernel,
        grid=(PROJ_WIDTH // IN_TN, m // tm),
        in_specs=[pl.BlockSpec((tm, k), lambda j, i: (i, 0))] + [w_spec(q) for q in range(per_step)]
        + [pl.BlockSpec((IN_WBLK, k), lambda j, i: (0, 0))],
        out_specs=pl.BlockSpec((tm, IN_TN), lambda j, i: (i, j)),
        out_shape=jax.ShapeDtypeStruct((m, PROJ_WIDTH), BF16),
        scratch_shapes=[pltpu.VMEM((IN_TN, k), BF16)],
        compiler_params=_params(("parallel", "arbitrary"), 48),
        name="in_proj",
    )(a, *([w_in_t] * per_step), w_ckr_t)


def _latent_up_kernel(cq_ref, ckv_ref, gq_ref, gkv_ref, wq_ref, wkv_ref, q_ref, kv_ref, *, q_scale):
    cq = _rms(cq_ref[...].astype(F32), gq_ref[...]).astype(BF16)
    q_ref[...] = (_dot(cq, wq_ref[...]) * q_scale).astype(q_ref.dtype)
    ckv = _rms(ckv_ref[...].astype(F32), gkv_ref[...]).astype(BF16)
    kv_ref[...] = _dot(ckv, wkv_ref[...]).astype(kv_ref.dtype)


def _latent_up(proj, gq, gkv, wq, wkv, tm=1024):
    m = proj.shape[0]
    nq, nkv = wq.shape[1], wkv.shape[1]
    q_scale = (C_NOPE_DIM + C_ROPE_DIM) ** -0.5 * LOG2E
    return pl.pallas_call(
        functools.partial(_latent_up_kernel, q_scale=q_scale),
        grid=(m // tm,),
        in_specs=[
            pl.BlockSpec((tm, C_Q_RANK), lambda i: (i, CB_CQ)),
            pl.BlockSpec((tm, C_KV_RANK), lambda i: (i, CB_CKV)),
            pl.BlockSpec((1, C_Q_RANK), lambda i: (0, 0)),
            pl.BlockSpec((1, C_KV_RANK), lambda i: (0, 0)),
            pl.BlockSpec((C_Q_RANK, nq), lambda i: (0, 0)),
            pl.BlockSpec((C_KV_RANK, nkv), lambda i: (0, 0)),
        ],
        out_specs=[pl.BlockSpec((tm, nq), lambda i: (i, 0)), pl.BlockSpec((tm, nkv), lambda i: (i, 0))],
        out_shape=[jax.ShapeDtypeStruct((m, nq), BF16), jax.ShapeDtypeStruct((m, nkv), BF16)],
        compiler_params=_params(("parallel",), 48),
        name="latent_up",
    )(proj, proj, gq, gkv, wq, wkv)


A_ROW_CHUNK = 128
C_ROW_CHUNK = 256


def _attend(q_all, k_ref, v1_ref, dv, chunk):
    outs = []
    for c in range(q_all.shape[0] // chunk):
        q = q_all[c * chunk:(c + 1) * chunk]
        s = _dot_nt(q, k_ref[...])
        p = jnp.exp2(s - jnp.max(s, axis=-1, keepdims=True))
        ol = _dot(p.astype(BF16), v1_ref[...])
        outs.append(ol[:, :dv] * (1.0 / ol[:, dv:]))
    return outs


def _attn_a_kernel(q0_ref, q1_ref, q2_ref, k_ref, v_ref, cos_ref, slo_ref, shi_ref, gq_ref, gk_ref, wo_ref,
                   o_ref, wo_bf_ref, kbuf, v1buf, *, tq):
    qi = pl.program_id(2)
    wo_bf_ref[...] = wo_ref[...].astype(BF16)

    @pl.when(qi == 0)
    def _():
        k = _rms(k_ref[...].astype(F32), gk_ref[...])
        kbuf[...] = _rope(k, cos_ref[...], slo_ref[...], shi_ref[...], 32).astype(BF16)
        v1buf[:, :HEAD_DIM] = v_ref[...]
        v1buf[:, HEAD_DIM:] = jnp.ones((v1buf.shape[0], HEAD_DIM), BF16)

    rows = pl.ds(pl.multiple_of(qi * tq, tq), tq)
    cos, slo, shi = cos_ref[rows, :], slo_ref[rows, :], shi_ref[rows, :]
    q_scale = HEAD_DIM ** -0.5 * LOG2E
    qs = []
    for q_ref in (q0_ref, q1_ref, q2_ref):
        q = _rms(q_ref[...].astype(F32), gq_ref[...])
        qs.append((_rope(q, cos, slo, shi, 32) * q_scale).astype(BF16))
    q_all = jnp.concatenate(qs, axis=0)
    per_head = tq // A_ROW_CHUNK
    for c, o in enumerate(_attend(q_all, kbuf, v1buf, HEAD_DIM, A_ROW_CHUNK)):
        g, r = divmod(c, per_head)
        o_ref[r * A_ROW_CHUNK:(r + 1) * A_ROW_CHUNK, g * HEAD_DIM:(g + 1) * HEAD_DIM] = o.astype(o_ref.dtype)


def _attn_a(proj, tabs, gq, gk, w_out, layer, batch, seq, tq=1024):
    nq = seq // tq
    cos, slo, shi = tabs
    d = w_out.shape[2]
    wo_rows = w_out.shape[1] // (batch * A_KV_HEADS * nq)

    def wo_index(b, h, qi):
        return (b * A_KV_HEADS + h) * nq + qi

    def q_spec(g):
        return pl.BlockSpec((tq, HEAD_DIM), lambda b, h, qi: (b * nq + qi, CB_AQ + h * A_GROUP + g))

    tab_spec = pl.BlockSpec((seq, LANES), lambda b, h, qi: (0, 0))
    gain_spec = pl.BlockSpec((1, HEAD_DIM), lambda b, h, qi: (0, 0))
    return pl.pallas_call(
        functools.partial(_attn_a_kernel, tq=tq),
        grid=(batch, A_KV_HEADS, nq),
        in_specs=[
            q_spec(0), q_spec(1), q_spec(2),
            pl.BlockSpec((seq, HEAD_DIM), lambda b, h, qi: (b, CB_AK + h)),
            pl.BlockSpec((seq, HEAD_DIM), lambda b, h, qi: (b, CB_AV + h)),
            tab_spec, tab_spec, tab_spec, gain_spec, gain_spec,
            pl.BlockSpec((None, wo_rows, d), lambda b, h, qi: (layer, wo_index(b, h, qi), 0)),
        ],
        out_specs=[pl.BlockSpec((tq, A_GROUP * HEAD_DIM), lambda b, h, qi: (b * nq + qi, h)),
                   pl.BlockSpec((wo_rows, d), lambda b, h, qi: (wo_index(b, h, qi), 0))],
        out_shape=[jax.ShapeDtypeStruct((batch * seq, A_WIDTH), BF16),
                   jax.ShapeDtypeStruct(w_out.shape[1:], BF16)],
        scratch_shapes=[pltpu.VMEM((seq, HEAD_DIM), BF16), pltpu.VMEM((seq, 2 * HEAD_DIM), BF16)],
        compiler_params=_params(("parallel", "parallel", "arbitrary"), 48),
        name="attn_a",
    )(proj, proj, proj, proj, proj, cos, slo, shi, gq, gk, w_out)


def _attn_c_kernel(q_ref, kn_ref, v_ref, kr_ref, cos_ref, slo_ref, shi_ref, o_ref, kbuf, v1buf, *, tq):
    qi = pl.program_id(2)

    @pl.when(qi == 0)
    def _():
        kbuf[:, :C_NOPE_DIM] = kn_ref[...]
        kr = _rope(kr_ref[...].astype(F32), cos_ref[...], slo_ref[...], shi_ref[...], 32)
        kbuf[:, C_NOPE_DIM:] = kr.astype(BF16)
        v1buf[:, :C_V_DIM] = v_ref[...]
        v1buf[:, C_V_DIM:] = jnp.ones((v1buf.shape[0], C_V_DIM), BF16)

    rows = pl.ds(pl.multiple_of(qi * tq, tq), tq)
    q_rope = _rope(q_ref[:, C_NOPE_DIM:].astype(F32), cos_ref[rows, :], slo_ref[rows, :], shi_ref[rows, :], 32)
    q_all = jnp.concatenate([q_ref[:, :C_NOPE_DIM], q_rope.astype(BF16)], axis=1)
    for c, o in enumerate(_attend(q_all, kbuf, v1buf, C_V_DIM, C_ROW_CHUNK)):
        o_ref[c * C_ROW_CHUNK:(c + 1) * C_ROW_CHUNK, :] = o.astype(o_ref.dtype)


def _attn_c(qc, kvc, proj, tabs, batch, seq, tq=2048):
    nq = seq // tq
    cos, slo, shi = tabs
    tab_spec = pl.BlockSpec((seq, LANES), lambda b, h, qi: (0, 0))
    return pl.pallas_call(
        functools.partial(_attn_c_kernel, tq=tq),
        grid=(batch, C_HEADS, nq),
        in_specs=[
            pl.BlockSpec((tq, C_QK_PAD), lambda b, h, qi: (b * nq + qi, h)),
            pl.BlockSpec((seq, C_NOPE_DIM), lambda b, h, qi: (b, h)),
            pl.BlockSpec((seq, C_V_DIM), lambda b, h, qi: (b, C_HEADS + h)),
            pl.BlockSpec((seq, LANES), lambda b, h, qi: (b, CB_CKR)),
            tab_spec, tab_spec, tab_spec,
        ],
        out_specs=pl.BlockSpec((tq, C_V_DIM), lambda b, h, qi: (b * nq + qi, h)),
        out_shape=jax.ShapeDtypeStruct((batch * seq, C_WIDTH), BF16),
        scratch_shapes=[pltpu.VMEM((seq, C_QK_PAD), BF16), pltpu.VMEM((seq, 2 * C_V_DIM), BF16)],
        compiler_params=_params(("parallel", "parallel", "arbitrary"), 48),
        name="attn_c",
    )(qc, kvc, kvc, proj, cos, slo, shi)


B_QBLK = 128
(B_HALF,) = {w // (2 * d) for w, d in B_PATTERNS}
assert 2 * B_HALF == B_QBLK
B_PREP_ROWS = 256
B_UNROLL = 8


def _attn_b_kernel(q0_ref, q1_ref, q2_ref, k_ref, v_ref, cos_ref, slo_ref, shi_ref, o_ref,
                   q0f, q1f, q2f, kf, vf, m_s, l_s, acc_s, bias_s, *, seq):
    q_refs = (q0_ref, q1_ref, q2_ref)
    q_bufs = (q0f, q1f, q2f)
    q_scale = HEAD_DIM ** -0.5 * LOG2E

    half = B_HALF
    shape = (B_QBLK, 2 * B_QBLK)
    rel = lax.broadcasted_iota(jnp.int32, shape, 1) - lax.broadcasted_iota(jnp.int32, shape, 0)
    for case, offset in enumerate((0, -half, -B_QBLK)):
        bias_s[case] = jnp.where(jnp.abs(rel + offset) <= half, 0.0, -jnp.inf)

    def prep(ci, carry):
        rows = pl.ds(pl.multiple_of(ci * B_PREP_ROWS, B_PREP_ROWS), B_PREP_ROWS)
        cos, slo, shi = cos_ref[rows, :], slo_ref[rows, :], shi_ref[rows, :]
        for q_ref, q_buf in zip(q_refs, q_bufs):
            q_buf[rows, :] = _rope(q_ref[rows, :].astype(F32), cos, slo, shi, 16) * q_scale
        kf[rows, :] = _rope(k_ref[rows, :].astype(F32), cos, slo, shi, 16)
        vf[rows, :] = v_ref[rows, :].astype(F32)
        return carry

    lax.fori_loop(0, seq // B_PREP_ROWS, prep, 0)

    for g, (window, dil) in reversed(list(enumerate(B_PATTERNS))):
        first = g == len(B_PATTERNS) - 1
        half = window // (2 * dil)
        length = seq // dil
        nblk = length // B_QBLK
        kwin = min(2 * B_QBLK, length)
        q_buf = q_bufs[g]

        def rows_of(start, size, dil=dil):
            return pl.ds(start, size) if dil == 1 else pl.ds(start, size, stride=dil)

        def block(n, carry, first=first, half=half, length=length, nblk=nblk, kwin=kwin, q_buf=q_buf, dil=dil,
                  rows_of=rows_of):
            r = n // nblk
            i = n % nblk
            k0 = jnp.clip(i * B_QBLK - half, 0, length - kwin)
            q_rows = rows_of(r + dil * B_QBLK * i, B_QBLK)
            k_rows = rows_of(r + dil * k0, kwin)
            q = q_buf[q_rows, :].astype(BF16)
            k = kf[k_rows, :].astype(BF16)
            v = vf[k_rows, :].astype(BF16)
            case = 0 if nblk == 1 else jnp.where(i == 0, 0, jnp.where(i == nblk - 1, 2, 1))
            s = _dot_nt(q, k) + bias_s[case, :, :kwin]
            m_b = jnp.max(s, axis=-1, keepdims=True)
            p = jnp.exp2(s - m_b)
            l_b = jnp.sum(p, axis=-1, keepdims=True)
            a_b = _dot(p.astype(BF16), v)
            full = (B_QBLK, HEAD_DIM)
            if first:
                m_s[q_rows, :] = jnp.broadcast_to(m_b, full)
                l_s[q_rows, :] = jnp.broadcast_to(l_b, full)
                acc_s[q_rows, :] = a_b
            else:
                m_o = m_s[q_rows, :]
                m_n = jnp.maximum(m_o, m_b)
                w_o = jnp.exp2(m_o - m_n)
                w_b = jnp.exp2(m_b - m_n)
                acc_s[q_rows, :] = acc_s[q_rows, :] * w_o + a_b * w_b
                l_s[q_rows, :] = l_s[q_rows, :] * w_o + l_b * w_b
                m_s[q_rows, :] = m_n
            return carry

        lax.fori_loop(0, seq // B_QBLK, block, 0, unroll=B_UNROLL)

    def finish(ci, carry):
        rows = pl.ds(pl.multiple_of(ci * B_PREP_ROWS, B_PREP_ROWS), B_PREP_ROWS)
        o_ref[rows, :] = (acc_s[rows, :] * (1.0 / l_s[rows, :])).astype(o_ref.dtype)
        return carry

    lax.fori_loop(0, seq // B_PREP_ROWS, finish, 0)


def _attn_b(proj, tabs, batch, seq):
    cos, slo, shi = tabs

    def q_spec(g):
        return pl.BlockSpec((seq, HEAD_DIM), lambda b, h: (b, CB_BQ + g * B_HEADS + h))

    tab_spec = pl.BlockSpec((seq, LANES), lambda b, h: (0, 0))
    slab = pltpu.VMEM((seq, HEAD_DIM), F32)
    return pl.pallas_call(
        functools.partial(_attn_b_kernel, seq=seq),
        grid=(batch, B_HEADS),
        in_specs=[
            q_spec(0), q_spec(1), q_spec(2),
            pl.BlockSpec((seq, HEAD_DIM), lambda b, h: (b, CB_BK + h)),
            pl.BlockSpec((seq, HEAD_DIM), lambda b, h: (b, CB_BV + h)),
            tab_spec, tab_spec, tab_spec,
        ],
        out_specs=pl.BlockSpec((seq, HEAD_DIM), lambda b, h: (b, h)),
        out_shape=jax.ShapeDtypeStruct((batch * seq, B_WIDTH), BF16),
        scratch_shapes=[slab] * 8 + [pltpu.VMEM((3, B_QBLK, 2 * B_QBLK), F32)],
        compiler_params=_params(("parallel", "parallel"), 48),
        name="attn_b",
    )(proj, proj, proj, proj, proj, cos, slo, shi)


def _out_kernel(ya_ref, yb_ref, yc_ref, ga_ref, gb_ref, gc_ref, w_ref, x_ref, gf_ref, x1_ref, xn_ref):
    y = jnp.concatenate([
        _rms(ya_ref[...].astype(F32), ga_ref[...]).astype(BF16),
        _rms(yb_ref[...].astype(F32), gb_ref[...]).astype(BF16),
        _rms(yc_ref[...].astype(F32), gc_ref[...]).astype(BF16),
    ], axis=1)
    x1 = x_ref[...] + _dot(y, w_ref[...])
    x1_ref[...] = x1
    xn_ref[...] = _rms(x1, gf_ref[...]).astype(xn_ref.dtype)


def _out_proj(ya, yb, yc, ga, gb, gc, w, x, gf, tm=512):
    m, d = x.shape

    def rows(width):
        return pl.BlockSpec((tm, width), lambda i: (i, 0))

    def const(r, c):
        return pl.BlockSpec((r, c), lambda i: (0, 0))

    return pl.pallas_call(
        _out_kernel,
        grid=(m // tm,),
        in_specs=[rows(A_WIDTH), rows(B_WIDTH), rows(C_WIDTH), const(1, A_WIDTH), const(1, B_WIDTH),
                  const(1, C_WIDTH), const(d, d), rows(d), const(1, d)],
        out_specs=[rows(d), rows(d)],
        out_shape=[jax.ShapeDtypeStruct((m, d), F32), jax.ShapeDtypeStruct((m, d), BF16)],
        compiler_params=_params(("parallel",), 56),
        name="out_proj",
    )(ya, yb, yc, ga, gb, gc, w, x, gf)


FFN_SUBTILE = 256
FFN_ROWS = 512
FFN_GATE_ROWS = 64
FFN_INTERLEAVE = 8
FFN_GUARD = FFN_INTERLEAVE

def _ffn_up_kernel(xn_ref, wg_ref, wu_ref, cg_ref, cu_ref, bg_ref, bu_ref, wd_ref, o_ref, wd_bf_ref,
                   hg, hu, wg_bf, wu_bf, *, seq):
    tn = o_ref.shape[1]
    slabs = FFN_SUBTILE // LANES
    wd_bf_ref[...] = wd_ref[...].astype(BF16)

    @pl.when(pl.program_id(1) == 0)
    def _():
        wg_bf[...] = wg_ref[...].astype(BF16)
        wu_bf[...] = wu_ref[...].astype(BF16)

    zeros = jnp.zeros((FFN_GUARD, LANES), F32)
    for h in (hg, hu):
        for s in range(tn // LANES):
            h[s, 0:FFN_GUARD, :] = zeros
            h[s, FFN_GUARD + seq:2 * FFN_GUARD + seq, :] = zeros

    def matmuls(c, r):
        cols = slice(c * FFN_SUBTILE, (c + 1) * FFN_SUBTILE)
        xr = xn_ref[r * FFN_ROWS:(r + 1) * FFN_ROWS, :]
        dst = slice(FFN_GUARD + r * FFN_ROWS, FFN_GUARD + (r + 1) * FFN_ROWS)
        for h, w in ((hg, wg_bf), (hu, wu_bf)):
            res = _dot(xr, w[:, cols])
            for s in range(slabs):
                h[c * slabs + s, dst, :] = res[:, s * LANES:(s + 1) * LANES]

    def gate_stage(c, r):
        for s in range(slabs):
            slab = c * slabs + s
            lanes = slice(slab * LANES, (slab + 1) * LANES)
            for k in range(FFN_ROWS // FFN_GATE_ROWS):
                row0 = r * FFN_ROWS + k * FFN_GATE_ROWS
                base = FFN_GUARD + row0

                def tap(h, t):
                    if t == 1:
                        return h[slab, base:base + FFN_GATE_ROWS, :]
                    return h[pl.ds(slab, 1, stride=2), pl.ds(base + t - 1, FFN_GATE_ROWS), :][0]

                def conv(h, c_ref, b_ref):
                    taps = [tap(h, t) * c_ref[t:t + 1, lanes] for t in range(3)]
                    return taps[0] + taps[1] + taps[2] + b_ref[:, lanes]

                gate = conv(hg, cg_ref, bg_ref)
                up = conv(hu, cu_ref, bu_ref)
                o_ref[row0:row0 + FFN_GATE_ROWS, lanes] = (
                    gate * (1.0 / (1.0 + jnp.exp2(gate * -LOG2E))) * up).astype(o_ref.dtype)

    units = [(c, r) for c in range(tn // FFN_SUBTILE) for r in range(seq // FFN_ROWS)]
    for i, unit in enumerate(units):
        matmuls(*unit)
        if i > 0:
            gate_stage(*units[i - 1])
    gate_stage(*units[-1])


def _ffn_up(xn, w_up, w_down, layer, conv_w, conv_b, batch, seq, tn=512):
    d = xn.shape[1]
    nj = D_FF // tn
    wd_rows = w_down.shape[1] // (nj * batch)
    return pl.pallas_call(
        functools.partial(_ffn_up_kernel, seq=seq),
        grid=(nj, batch),
        in_specs=[
            pl.BlockSpec((seq, d), lambda j, b: (b, 0)),
            pl.BlockSpec((None, d, tn), lambda j, b: (layer, 0, j)),
            pl.BlockSpec((None, d, tn), lambda j, b: (layer, 0, nj + j)),
            pl.BlockSpec((3, tn), lambda j, b: (0, j)),
            pl.BlockSpec((3, tn), lambda j, b: (0, nj + j)),
            pl.BlockSpec((1, tn), lambda j, b: (0, j)),
            pl.BlockSpec((1, tn), lambda j, b: (0, nj + j)),
            pl.BlockSpec((None, wd_rows, d), lambda j, b: (layer, j * batch + b, 0)),
        ],
        out_specs=[pl.BlockSpec((seq, tn), lambda j, b: (b, j)),
                   pl.BlockSpec((wd_rows, d), lambda j, b: (j * batch + b, 0))],
        out_shape=[jax.ShapeDtypeStruct((batch * seq, D_FF), BF16),
                   jax.ShapeDtypeStruct(w_down.shape[1:], BF16)],
        scratch_shapes=[pltpu.VMEM((tn // LANES, seq + 2 * FFN_GUARD, LANES), F32)] * 2
        + [pltpu.VMEM((d, tn), BF16)] * 2,
        compiler_params=_params(("parallel", "arbitrary"), 60),
        name="ffn_up",
    )(xn, w_up, w_up, conv_w, conv_w, conv_b, conv_b, w_down)


def _ffn_down_kernel(a_ref, w_ref, x1_ref, g_ref, x2_ref, xn_ref):
    k = pl.program_id(1)

    @pl.when(k == 0)
    def _():
        x2_ref[...] = x1_ref[...]

    x2_ref[...] += _dot(a_ref[...], w_ref[...])

    @pl.when(k == pl.num_programs(1) - 1)
    def _():
        xn_ref[...] = _rms(x2_ref[...], g_ref[...]).astype(xn_ref.dtype)


def _ffn_down_final_kernel(a_ref, w_ref, x1_ref, g_ref, xn_ref, acc_ref):
    _ffn_down_kernel(a_ref, w_ref, x1_ref, g_ref, acc_ref, xn_ref)


def _ffn_down(act, w, x1, g, final, tm=1024, tk=512):
    m, kdim = act.shape
    d = w.shape[1]
    tile = pl.BlockSpec((tm, d), lambda i, k: (i, 0))
    in_specs = [
        pl.BlockSpec((tm, tk), lambda i, k: (i, k)),
        pl.BlockSpec((tk, d), lambda i, k: (k, 0)),
        tile,
        pl.BlockSpec((1, d), lambda i, k: (0, 0)),
    ]
    common = dict(grid=(m // tm, kdim // tk), in_specs=in_specs,
                  compiler_params=_params(("parallel", "arbitrary"), 56))
    if final:
        xn = pl.pallas_call(_ffn_down_final_kernel, out_specs=tile, out_shape=jax.ShapeDtypeStruct((m, d), F32),
                            scratch_shapes=[pltpu.VMEM((tm, d), F32)], name="ffn_down_final", **common)(act, w, x1, g)
        return None, xn
    return pl.pallas_call(_ffn_down_kernel, out_specs=[tile, tile],
                          out_shape=[jax.ShapeDtypeStruct((m, d), F32), jax.ShapeDtypeStruct((m, d), BF16)],
                          name="ffn_down", **common)(act, w, x1, g)


def _rope_table(pos, dim, theta):
    f32 = np.float32
    inv = f32(theta) ** (-np.arange(0, dim, 2, dtype=f32) / f32(dim))
    ang = (pos.astype(f32)[:, None] * inv[None, :]).astype(np.float64)
    return np.cos(ang).astype(f32), np.sin(ang).astype(f32)


def _lane_tables(seq):
    t = np.arange(seq)
    z = lambda w: np.zeros((seq, w), np.float32)
    one = lambda w: np.ones((seq, w), np.float32)
    cat = lambda parts: jnp.asarray(np.concatenate(parts, 1))
    cr, sr = _rope_table(t // GRID_W, HEAD_DIM // 2, A_ROPE_THETA)
    cc, sc = _rope_table(t % GRID_W, HEAD_DIM // 2, A_ROPE_THETA)
    tab_a = (cat([cr, cr, cc, cc]), cat([-sr, z(32), -sc, z(32)]), cat([z(32), sr, z(32), sc]))
    cp, sp = _rope_table(t, PARTIAL_ROPE_DIM, PARTIAL_ROPE_THETA)
    tab_b = (cat([cp, cp, one(96)]), cat([-sp, z(112)]), cat([z(16), sp, z(96)]))
    cm, sm = _rope_table(t, C_ROPE_DIM, C_ROPE_THETA)
    tab_c = (cat([cm, cm, one(64)]), cat([-sm, z(96)]), cat([z(32), sm, z(64)]))
    return tab_a, tab_b, tab_c


def _prep_w_ckr(w_in_t, layer):
    ckr = w_in_t[layer, IN_N_SRC * IN_WBLK:, :]
    return jnp.pad(ckr, ((0, IN_WBLK - ckr.shape[0]), (0, 0)))


def _prep_w_uq(w):
    w = w.reshape(C_Q_RANK, C_HEADS, C_NOPE_DIM + C_ROPE_DIM)
    w = jnp.pad(w, ((0, 0), (0, 0), (0, C_QK_PAD - C_NOPE_DIM - C_ROPE_DIM)))
    return w.reshape(C_Q_RANK, C_HEADS * C_QK_PAD).astype(BF16)


def _prep_w_ukv(w):
    w = w.reshape(C_KV_RANK, C_HEADS, C_NOPE_DIM + C_V_DIM)
    kn = w[:, :, :C_NOPE_DIM].reshape(C_KV_RANK, C_HEADS * C_NOPE_DIM)
    v = w[:, :, C_NOPE_DIM:].reshape(C_KV_RANK, C_HEADS * C_V_DIM)
    return jnp.concatenate([kn, v], axis=1).astype(BF16)


def kernel(x, attn_norm, w_in, a_q_norm, a_k_norm, c_q_norm, c_kv_norm, w_uq, w_ukv, out_norm, w_out,
           ffn_norm, w_up, conv_w, conv_b, w_down, final_norm):
    batch, seq, d = x.shape
    depth = w_in.shape[0]
    tab_a, tab_b, tab_c = _lane_tables(seq)
    row = lambda v: v.reshape(1, -1)

    xr = x.reshape(batch * seq, d)
    xn = _rmsnorm_rows(xr, row(attn_norm[0]))
    w_in_t = jnp.swapaxes(w_in, 1, 2)
    for l in range(depth):
        proj = _in_proj(xn, w_in_t, l, _prep_w_ckr(w_in_t, l))
        qc, kvc = _latent_up(proj, row(c_q_norm[l]), row(c_kv_norm[l]), _prep_w_uq(w_uq[l]),
                             _prep_w_ukv(w_ukv[l]))
        ya, w_out_bf = _attn_a(proj, tab_a, row(a_q_norm[l]), row(a_k_norm[l]), w_out, l, batch, seq)
        yb = _attn_b(proj, tab_b, batch, seq)
        yc = _attn_c(qc, kvc, proj, tab_c, batch, seq)
        g = out_norm[l]
        x1, xn1 = _out_proj(ya, yb, yc, row(g[:A_WIDTH]), row(g[A_WIDTH:A_WIDTH + B_WIDTH]),
                            row(g[A_WIDTH + B_WIDTH:]), w_out_bf, xr, row(ffn_norm[l]))
        act, w_down_bf = _ffn_up(xn1, w_up, w_down, l, conv_w[l], row(conv_b[l]), batch, seq)
        last = l == depth - 1
        g_next = final_norm if last else attn_norm[l + 1]
        xr, xn = _ffn_down(act, w_down_bf, x1, row(g_next), last)
    return xn.reshape(batch, seq, d)
```

```python
import functools
import math

import jax
import jax.numpy as jnp
import numpy as np
from jax import lax
from jax.experimental import pallas as pl
from jax.experimental.pallas import tpu as pltpu

D_MODEL = 2048
HEAD_DIM = 128
A_HEADS = 6
A_KV_HEADS = 2
A_GROUP = A_HEADS // A_KV_HEADS
A_ROPE_THETA = 10000.0
B_HEADS = 4
B_PATTERNS = ((128, 1), (512, 4), (2048, 16))
B_N_GROUPS = 3
C_HEADS = 6
C_Q_RANK = 512
C_KV_RANK = 512
C_NOPE_DIM = 128
C_ROPE_DIM = 64
C_V_DIM = 128
C_ROPE_THETA = 10000.0
PARTIAL_ROPE_DIM = HEAD_DIM // 4
PARTIAL_ROPE_THETA = 500000.0
GRID_W = 64
D_FF = 5632
EPS = 1e-6

A_WIDTH = A_HEADS * HEAD_DIM
B_WIDTH = B_HEADS * HEAD_DIM
C_WIDTH = C_HEADS * C_V_DIM
C_QK_PAD = 256

LANES = 128
LOG2E = math.log2(math.e)
F32 = jnp.float32
BF16 = jnp.bfloat16

PROJ_WIDTH = 5120
CB_CQ, CB_CKV = 0, 1
CB_AQ, CB_AK, CB_AV = 8, 14, 16
CB_BQ, CB_BK, CB_BV = 18, 30, 34
CB_CKR = 38


def _params(semantics, vmem_mib):
    return pltpu.CompilerParams(dimension_semantics=semantics, vmem_limit_bytes=vmem_mib * 1024 * 1024)


def _rms(xf, g):
    return xf * lax.rsqrt(jnp.mean(xf * xf, axis=-1, keepdims=True) + EPS) * g


def _rope(xf, cos, sin_lo, sin_hi, shift):
    return xf * cos + pltpu.roll(xf, LANES - shift, 1) * sin_lo + pltpu.roll(xf, shift, 1) * sin_hi


def _dot(a, b):
    return jnp.dot(a, b, preferred_element_type=F32)


def _dot_nt(a, b):
    return lax.dot_general(a, b, (((1,), (1,)), ((), ())), preferred_element_type=F32)


def _norm_kernel(x_ref, g_ref, o_ref):
    o_ref[...] = _rms(x_ref[...], g_ref[...]).astype(o_ref.dtype)


def _rmsnorm_rows(x, g, tm=512):
    m, d = x.shape
    return pl.pallas_call(
        _norm_kernel,
        grid=(m // tm,),
        in_specs=[pl.BlockSpec((tm, d), lambda i: (i, 0)), pl.BlockSpec((1, d), lambda i: (0, 0))],
        out_specs=pl.BlockSpec((tm, d), lambda i: (i, 0)),
        out_shape=jax.ShapeDtypeStruct((m, d), BF16),
        compiler_params=_params(("parallel",), 40),
        name="rmsnorm",
    )(x, g)


IN_WBLK = 256
IN_TN = 1024
IN_SRC_CQ = 15
IN_N_SRC = 19


def _in_proj_kernel(a_ref, *refs):
    w_refs, wck_ref, o_ref, w_bf = refs[:-3], refs[-3], refs[-2], refs[-1]
    j = pl.program_id(0)
    last = pl.num_programs(0) - 1

    @pl.when(pl.program_id(1) == 0)
    def _():
        for q, w_ref in enumerate(w_refs[:-1]):
            w_bf[q * IN_WBLK:(q + 1) * IN_WBLK, :] = w_ref[...].astype(BF16)
        tail = slice((len(w_refs) - 1) * IN_WBLK, len(w_refs) * IN_WBLK)

        @pl.when(j < last)
        def _():
            w_bf[tail, :] = w_refs[-1][...].astype(BF16)

        @pl.when(j == last)
        def _():
            w_bf[tail, :] = wck_ref[...].astype(BF16)

    o_ref[...] = _dot_nt(a_ref[...], w_bf[...]).astype(o_ref.dtype)


def _in_proj(a, w_in_t, layer, w_ckr_t, tm=2048):
    m, k = a.shape
    per_step = IN_TN // IN_WBLK

    def w_spec(q):
        def index(j, i):
            n = j * per_step + q
            return layer, jnp.where(n < 4, n + IN_SRC_CQ, n - 4), 0
        return pl.BlockSpec((None, IN_WBLK, k), index)

    return pl.pallas_call(
        _in_proj_kernel,
        grid=(PROJ_WIDTH // IN_TN, m // tm),
        in_specs=[pl.BlockSpec((tm, k), lambda j, i: (i, 0))] + [w_spec(q) for q in range(per_step)]
        + [pl.BlockSpec((IN_WBLK, k), lambda j, i: (0, 0))],
        out_specs=pl.BlockSpec((tm, IN_TN), lambda j, i: (i, j)),
        out_shape=jax.ShapeDtypeStruct((m, PROJ_WIDTH), BF16),
        scratch_shapes=[pltpu.VMEM((IN_TN, k), BF16)],
        compiler_params=_params(("parallel", "arbitrary"), 60),
        name="in_proj",
    )(a, *([w_in_t] * per_step), w_ckr_t)


def _latent_up_kernel(cq_ref, ckv_ref, gq_ref, gkv_ref, wq_ref, wkv_ref, q_ref, kv_ref, *, q_scale):
    cq = _rms(cq_ref[...].astype(F32), gq_ref[...]).astype(BF16)
    q_ref[...] = (_dot(cq, wq_ref[...]) * q_scale).astype(q_ref.dtype)
    ckv = _rms(ckv_ref[...].astype(F32), gkv_ref[...]).astype(BF16)
    kv_ref[...] = _dot(ckv, wkv_ref[...]).astype(kv_ref.dtype)


def _latent_up(proj, gq, gkv, wq, wkv, tm=1024):
    m = proj.shape[0]
    nq, nkv = wq.shape[1], wkv.shape[1]
    q_scale = (C_NOPE_DIM + C_ROPE_DIM) ** -0.5 * LOG2E
    return pl.pallas_call(
        functools.partial(_latent_up_kernel, q_scale=q_scale),
        grid=(m // tm,),
        in_specs=[
            pl.BlockSpec((tm, C_Q_RANK), lambda i: (i, CB_CQ)),
            pl.BlockSpec((tm, C_KV_RANK), lambda i: (i, CB_CKV)),
            pl.BlockSpec((1, C_Q_RANK), lambda i: (0, 0)),
            pl.BlockSpec((1, C_KV_RANK), lambda i: (0, 0)),
            pl.BlockSpec((C_Q_RANK, nq), lambda i: (0, 0)),
            pl.BlockSpec((C_KV_RANK, nkv), lambda i: (0, 0)),
        ],
        out_specs=[pl.BlockSpec((tm, nq), lambda i: (i, 0)), pl.BlockSpec((tm, nkv), lambda i: (i, 0))],
        out_shape=[jax.ShapeDtypeStruct((m, nq), BF16), jax.ShapeDtypeStruct((m, nkv), BF16)],
        compiler_params=_params(("parallel",), 48),
        name="latent_up",
    )(proj, proj, gq, gkv, wq, wkv)


A_ROW_CHUNK = 128
C_ROW_CHUNK = 256


def _attend(q_all, k_ref, v1_ref, dv, chunk):
    outs = []
    for c in range(q_all.shape[0] // chunk):
        q = q_all[c * chunk:(c + 1) * chunk]
        s = _dot_nt(q, k_ref[...])
        p = jnp.exp2(s - jnp.max(s, axis=-1, keepdims=True))
        ol = _dot(p.astype(BF16), v1_ref[...])
        outs.append(ol[:, :dv] * (1.0 / ol[:, dv:]))
    return outs


def _attn_a_kernel(q0_ref, q1_ref, q2_ref, k_ref, v_ref, cos_ref, slo_ref, shi_ref, gq_ref, gk_ref, wo_ref,
                   o_ref, wo_bf_ref, kbuf, v1buf, *, tq):
    qi = pl.program_id(2)
    wo_bf_ref[...] = wo_ref[...].astype(BF16)

    @pl.when(qi == 0)
    def _():
        k = _rms(k_ref[...].astype(F32), gk_ref[...])
        kbuf[...] = _rope(k, cos_ref[...], slo_ref[...], shi_ref[...], 32).astype(BF16)
        v1buf[:, :HEAD_DIM] = v_ref[...]
        v1buf[:, HEAD_DIM:] = jnp.ones((v1buf.shape[0], HEAD_DIM), BF16)

    rows = pl.ds(pl.multiple_of(qi * tq, tq), tq)
    cos, slo, shi = cos_ref[rows, :], slo_ref[rows, :], shi_ref[rows, :]
    q_scale = HEAD_DIM ** -0.5 * LOG2E
    qs = []
    for q_ref in (q0_ref, q1_ref, q2_ref):
        q = _rms(q_ref[...].astype(F32), gq_ref[...])
        qs.append((_rope(q, cos, slo, shi, 32) * q_scale).astype(BF16))
    q_all = jnp.concatenate(qs, axis=0)
    per_head = tq // A_ROW_CHUNK
    for c, o in enumerate(_attend(q_all, kbuf, v1buf, HEAD_DIM, A_ROW_CHUNK)):
        g, r = divmod(c, per_head)
        o_ref[r * A_ROW_CHUNK:(r + 1) * A_ROW_CHUNK, g * HEAD_DIM:(g + 1) * HEAD_DIM] = o.astype(o_ref.dtype)


def _attn_a(proj, tabs, gq, gk, w_out, layer, batch, seq, tq=1024):
    nq = seq // tq
    cos, slo, shi = tabs
    d = w_out.shape[2]
    wo_rows = w_out.shape[1] // (batch * A_KV_HEADS * nq)

    def wo_index(b, h, qi):
        return (b * A_KV_HEADS + h) * nq + qi

    def q_spec(g):
        return pl.BlockSpec((tq, HEAD_DIM), lambda b, h, qi: (b * nq + qi, CB_AQ + h * A_GROUP + g))

    tab_spec = pl.BlockSpec((seq, LANES), lambda b, h, qi: (0, 0))
    gain_spec = pl.BlockSpec((1, HEAD_DIM), lambda b, h, qi: (0, 0))
    return pl.pallas_call(
        functools.partial(_attn_a_kernel, tq=tq),
        grid=(batch, A_KV_HEADS, nq),
        in_specs=[
            q_spec(0), q_spec(1), q_spec(2),
            pl.BlockSpec((seq, HEAD_DIM), lambda b, h, qi: (b, CB_AK + h)),
            pl.BlockSpec((seq, HEAD_DIM), lambda b, h, qi: (b, CB_AV + h)),
            tab_spec, tab_spec, tab_spec, gain_spec, gain_spec,
            pl.BlockSpec((None, wo_rows, d), lambda b, h, qi: (layer, wo_index(b, h, qi), 0)),
        ],
        out_specs=[pl.BlockSpec((tq, A_GROUP * HEAD_DIM), lambda b, h, qi: (b * nq + qi, h)),
                   pl.BlockSpec((wo_rows, d), lambda b, h, qi: (wo_index(b, h, qi), 0))],
        out_shape=[jax.ShapeDtypeStruct((batch * seq, A_WIDTH), BF16),
                   jax.ShapeDtypeStruct(w_out.shape[1:], BF16)],
        scratch_shapes=[pltpu.VMEM((seq, HEAD_DIM), BF16), pltpu.VMEM((seq, 2 * HEAD_DIM), BF16)],
        compiler_params=_params(("parallel", "parallel", "arbitrary"), 48),
        name="attn_a",
    )(proj, proj, proj, proj, proj, cos, slo, shi, gq, gk, w_out)


def _attn_c_kernel(q_ref, kn_ref, v_ref, kr_ref, cos_ref, slo_ref, shi_ref, o_ref, kbuf, v1buf, *, tq):
    qi = pl.program_id(2)

    @pl.when(qi == 0)
    def _():
        kbuf[:, :C_NOPE_DIM] = kn_ref[...]
        kr = _rope(kr_ref[...].astype(F32), cos_ref[...], slo_ref[...], shi_ref[...], 32)
        kbuf[:, C_NOPE_DIM:] = kr.astype(BF16)
        v1buf[:, :C_V_DIM] = v_ref[...]
        v1buf[:, C_V_DIM:] = jnp.ones((v1buf.shape[0], C_V_DIM), BF16)

    rows = pl.ds(pl.multiple_of(qi * tq, tq), tq)
    q_rope = _rope(q_ref[:, C_NOPE_DIM:].astype(F32), cos_ref[rows, :], slo_ref[rows, :], shi_ref[rows, :], 32)
    q_all = jnp.concatenate([q_ref[:, :C_NOPE_DIM], q_rope.astype(BF16)], axis=1)
    for c, o in enumerate(_attend(q_all, kbuf, v1buf, C_V_DIM, C_ROW_CHUNK)):
        o_ref[c * C_ROW_CHUNK:(c + 1) * C_ROW_CHUNK, :] = o.astype(o_ref.dtype)


def _attn_c(qc, kvc, proj, tabs, batch, seq, tq=2048):
    nq = seq // tq
    cos, slo, shi = tabs
    tab_spec = pl.BlockSpec((seq, LANES), lambda b, h, qi: (0, 0))
    return pl.pallas_call(
        functools.partial(_attn_c_kernel, tq=tq),
        grid=(batch, C_HEADS, nq),
        in_specs=[
            pl.BlockSpec((tq, C_QK_PAD), lambda b, h, qi: (b * nq + qi, h)),
            pl.BlockSpec((seq, C_NOPE_DIM), lambda b, h, qi: (b, h)),
            pl.BlockSpec((seq, C_V_DIM), lambda b, h, qi: (b, C_HEADS + h)),
            pl.BlockSpec((seq, LANES), lambda b, h, qi: (b, CB_CKR)),
            tab_spec, tab_spec, tab_spec,
        ],
        out_specs=pl.BlockSpec((tq, C_V_DIM), lambda b, h, qi: (b * nq + qi, h)),
        out_shape=jax.ShapeDtypeStruct((batch * seq, C_WIDTH), BF16),
        scratch_shapes=[pltpu.VMEM((seq, C_QK_PAD), BF16), pltpu.VMEM((seq, 2 * C_V_DIM), BF16)],
        compiler_params=_params(("parallel", "parallel", "arbitrary"), 48),
        name="attn_c",
    )(qc, kvc, kvc, proj, cos, slo, shi)


B_QBLK = 128
(B_HALF,) = {w // (2 * d) for w, d in B_PATTERNS}
assert 2 * B_HALF == B_QBLK
B_PREP_ROWS = 256
B_UNROLL = 8


def _attn_b_kernel(q0_ref, q1_ref, q2_ref, k_ref, v_ref, cos_ref, slo_ref, shi_ref, o_ref,
                   q0f, q1f, q2f, kf, vf, m_s, l_s, acc_s, bias_s, *, seq):
    q_refs = (q0_ref, q1_ref, q2_ref)
    q_bufs = (q0f, q1f, q2f)
    q_scale = HEAD_DIM ** -0.5 * LOG2E

    half = B_HALF
    shape = (B_QBLK, 2 * B_QBLK)
    rel = lax.broadcasted_iota(jnp.int32, shape, 1) - lax.broadcasted_iota(jnp.int32, shape, 0)
    for case, offset in enumerate((0, -half, -B_QBLK)):
        bias_s[case] = jnp.where(jnp.abs(rel + offset) <= half, 0.0, -jnp.inf)

    def prep(ci, carry):
        rows = pl.ds(pl.multiple_of(ci * B_PREP_ROWS, B_PREP_ROWS), B_PREP_ROWS)
        cos, slo, shi = cos_ref[rows, :], slo_ref[rows, :], shi_ref[rows, :]
        for q_ref, q_buf in zip(q_refs, q_bufs):
            q_buf[rows, :] = _rope(q_ref[rows, :].astype(F32), cos, slo, shi, 16) * q_scale
        kf[rows, :] = _rope(k_ref[rows, :].astype(F32), cos, slo, shi, 16)
        vf[rows, :] = v_ref[rows, :].astype(F32)
        return carry

    lax.fori_loop(0, seq // B_PREP_ROWS, prep, 0)

    for g, (window, dil) in reversed(list(enumerate(B_PATTERNS))):
        first = g == len(B_PATTERNS) - 1
        half = window // (2 * dil)
        length = seq // dil
        nblk = length // B_QBLK
        kwin = min(2 * B_QBLK, length)
        q_buf = q_bufs[g]

        def rows_of(start, size, dil=dil):
            return pl.ds(start, size) if dil == 1 else pl.ds(start, size, stride=dil)

        def block(n, carry, first=first, half=half, length=length, nblk=nblk, kwin=kwin, q_buf=q_buf, dil=dil,
                  rows_of=rows_of):
            r = n // nblk
            i = n % nblk
            k0 = jnp.clip(i * B_QBLK - half, 0, length - kwin)
            q_rows = rows_of(r + dil * B_QBLK * i, B_QBLK)
            k_rows = rows_of(r + dil * k0, kwin)
            q = q_buf[q_rows, :].astype(BF16)
            k = kf[k_rows, :].astype(BF16)
            v = vf[k_rows, :].astype(BF16)
            case = 0 if nblk == 1 else jnp.where(i == 0, 0, jnp.where(i == nblk - 1, 2, 1))
            s = _dot_nt(q, k) + bias_s[case, :, :kwin]
            m_b = jnp.max(s, axis=-1, keepdims=True)
            p = jnp.exp2(s - m_b)
            l_b = jnp.sum(p, axis=-1, keepdims=True)
            a_b = _dot(p.astype(BF16), v)
            full = (B_QBLK, HEAD_DIM)
            if first:
                m_s[q_rows, :] = jnp.broadcast_to(m_b, full)
                l_s[q_rows, :] = jnp.broadcast_to(l_b, full)
                acc_s[q_rows, :] = a_b
            else:
                m_o = m_s[q_rows, :]
                m_n = jnp.maximum(m_o, m_b)
                w_o = jnp.exp2(m_o - m_n)
                w_b = jnp.exp2(m_b - m_n)
                acc_s[q_rows, :] = acc_s[q_rows, :] * w_o + a_b * w_b
                l_s[q_rows, :] = l_s[q_rows, :] * w_o + l_b * w_b
                m_s[q_rows, :] = m_n
            return carry

        lax.fori_loop(0, seq // B_QBLK, block, 0, unroll=B_UNROLL)

    def finish(ci, carry):
        rows = pl.ds(pl.multiple_of(ci * B_PREP_ROWS, B_PREP_ROWS), B_PREP_ROWS)
        o_ref[rows, :] = (acc_s[rows, :] * (1.0 / l_s[rows, :])).astype(o_ref.dtype)
        return carry

    lax.fori_loop(0, seq // B_PREP_ROWS, finish, 0)


def _attn_b(proj, tabs, batch, seq):
    cos, slo, shi = tabs

    def q_spec(g):
        return pl.BlockSpec((seq, HEAD_DIM), lambda b, h: (b, CB_BQ + g * B_HEADS + h))

    tab_spec = pl.BlockSpec((seq, LANES), lambda b, h: (0, 0))
    slab = pltpu.VMEM((seq, HEAD_DIM), F32)
    return pl.pallas_call(
        functools.partial(_attn_b_kernel, seq=seq),
        grid=(batch, B_HEADS),
        in_specs=[
            q_spec(0), q_spec(1), q_spec(2),
            pl.BlockSpec((seq, HEAD_DIM), lambda b, h: (b, CB_BK + h)),
            pl.BlockSpec((seq, HEAD_DIM), lambda b, h: (b, CB_BV + h)),
            tab_spec, tab_spec, tab_spec,
        ],
        out_specs=pl.BlockSpec((seq, HEAD_DIM), lambda b, h: (b, h)),
        out_shape=jax.ShapeDtypeStruct((batch * seq, B_WIDTH), BF16),
        scratch_shapes=[slab] * 8 + [pltpu.VMEM((3, B_QBLK, 2 * B_QBLK), F32)],
        compiler_params=_params(("parallel", "parallel"), 48),
        name="attn_b",
    )(proj, proj, proj, proj, proj, cos, slo, shi)


def _out_kernel(ya_ref, yb_ref, yc_ref, ga_ref, gb_ref, gc_ref, w_ref, x_ref, gf_ref, x1_ref, xn_ref):
    y = jnp.concatenate([
        _rms(ya_ref[...].astype(F32), ga_ref[...]).astype(BF16),
        _rms(yb_ref[...].astype(F32), gb_ref[...]).astype(BF16),
        _rms(yc_ref[...].astype(F32), gc_ref[...]).astype(BF16),
    ], axis=1)
    x1 = x_ref[...] + _dot(y, w_ref[...])
    x1_ref[...] = x1
    xn_ref[...] = _rms(x1, gf_ref[...]).astype(xn_ref.dtype)


def _out_proj(ya, yb, yc, ga, gb, gc, w, x, gf, tm=512):
    m, d = x.shape

    def rows(width):
        return pl.BlockSpec((tm, width), lambda i: (i, 0))

    def const(r, c):
        return pl.BlockSpec((r, c), lambda i: (0, 0))

    return pl.pallas_call(
        _out_kernel,
        grid=(m // tm,),
        in_specs=[rows(A_WIDTH), rows(B_WIDTH), rows(C_WIDTH), const(1, A_WIDTH), const(1, B_WIDTH),
                  const(1, C_WIDTH), const(d, d), rows(d), const(1, d)],
        out_specs=[rows(d), rows(d)],
        out_shape=[jax.ShapeDtypeStruct((m, d), F32), jax.ShapeDtypeStruct((m, d), BF16)],
        compiler_params=_params(("parallel",), 56),
        name="out_proj",
    )(ya, yb, yc, ga, gb, gc, w, x, gf)


FFN_SUBTILE = 256
FFN_ROWS = 512
FFN_GATE_ROWS = 32
FFN_INTERLEAVE = 8
FFN_GUARD = FFN_INTERLEAVE

def _ffn_up_kernel(xn_ref, wg_ref, wu_ref, cg_ref, cu_ref, bg_ref, bu_ref, wd_ref, o_ref, wd_bf_ref,
                   hg, hu, wg_bf, wu_bf, *, seq):
    tn = o_ref.shape[1]
    slabs = FFN_SUBTILE // LANES
    wd_bf_ref[...] = wd_ref[...].astype(BF16)

    @pl.when(pl.program_id(1) == 0)
    def _():
        wg_bf[...] = wg_ref[...].astype(BF16)
        wu_bf[...] = wu_ref[...].astype(BF16)

    zeros = jnp.zeros((FFN_GUARD, LANES), F32)
    for h in (hg, hu):
        for s in range(tn // LANES):
            h[s, 0:FFN_GUARD, :] = zeros
            h[s, FFN_GUARD + seq:2 * FFN_GUARD + seq, :] = zeros

    def matmuls(c, r):
        cols = slice(c * FFN_SUBTILE, (c + 1) * FFN_SUBTILE)
        xr = xn_ref[r * FFN_ROWS:(r + 1) * FFN_ROWS, :]
        dst = slice(FFN_GUARD + r * FFN_ROWS, FFN_GUARD + (r + 1) * FFN_ROWS)
        for h, w in ((hg, wg_bf), (hu, wu_bf)):
            res = _dot(xr, w[:, cols])
            for s in range(slabs):
                h[c * slabs + s, dst, :] = res[:, s * LANES:(s + 1) * LANES]

    def gate_stage(c, r):
        for s in range(slabs):
            slab = c * slabs + s
            lanes = slice(slab * LANES, (slab + 1) * LANES)
            for k in range(FFN_ROWS // FFN_GATE_ROWS):
                row0 = r * FFN_ROWS + k * FFN_GATE_ROWS
                base = FFN_GUARD + row0

                def tap(h, t):
                    if t == 1:
                        return h[slab, base:base + FFN_GATE_ROWS, :]
                    return h[pl.ds(slab, 1, stride=2), pl.ds(base + t - 1, FFN_GATE_ROWS), :][0]

                def conv(h, c_ref, b_ref):
                    taps = [tap(h, t) * c_ref[t:t + 1, lanes] for t in range(3)]
                    return taps[0] + taps[1] + taps[2] + b_ref[:, lanes]

                gate = conv(hg, cg_ref, bg_ref)
                up = conv(hu, cu_ref, bu_ref)
                o_ref[row0:row0 + FFN_GATE_ROWS, lanes] = (
                    gate * (1.0 / (1.0 + jnp.exp2(gate * -LOG2E))) * up).astype(o_ref.dtype)

    units = [(c, r) for c in range(tn // FFN_SUBTILE) for r in range(seq // FFN_ROWS)]
    for i, unit in enumerate(units):
        matmuls(*unit)
        if i > 0:
            gate_stage(*units[i - 1])
    gate_stage(*units[-1])


def _ffn_up(xn, w_up, w_down, layer, conv_w, conv_b, batch, seq, tn=512):
    d = xn.shape[1]
    nj = D_FF // tn
    wd_rows = w_down.shape[1] // (nj * batch)
    return pl.pallas_call(
        functools.partial(_ffn_up_kernel, seq=seq),
        grid=(nj, batch),
        in_specs=[
            pl.BlockSpec((seq, d), lambda j, b: (b, 0)),
            pl.BlockSpec((None, d, tn), lambda j, b: (layer, 0, j)),
            pl.BlockSpec((None, d, tn), lambda j, b: (layer, 0, nj + j)),
            pl.BlockSpec((3, tn), lambda j, b: (0, j)),
            pl.BlockSpec((3, tn), lambda j, b: (0, nj + j)),
            pl.BlockSpec((1, tn), lambda j, b: (0, j)),
            pl.BlockSpec((1, tn), lambda j, b: (0, nj + j)),
            pl.BlockSpec((None, wd_rows, d), lambda j, b: (layer, j * batch + b, 0)),
        ],
        out_specs=[pl.BlockSpec((seq, tn), lambda j, b: (b, j)),
                   pl.BlockSpec((wd_rows, d), lambda j, b: (j * batch + b, 0))],
        out_shape=[jax.ShapeDtypeStruct((batch * seq, D_FF), BF16),
                   jax.ShapeDtypeStruct(w_down.shape[1:], BF16)],
        scratch_shapes=[pltpu.VMEM((tn // LANES, seq + 2 * FFN_GUARD, LANES), F32)] * 2
        + [pltpu.VMEM((d, tn), BF16)] * 2,
        compiler_params=_params(("parallel", "arbitrary"), 60),
        name="ffn_up",
    )(xn, w_up, w_up, conv_w, conv_w, conv_b, conv_b, w_down)


def _ffn_down_kernel(a_ref, w_ref, x1_ref, g_ref, x2_ref, xn_ref):
    k = pl.program_id(1)

    @pl.when(k == 0)
    def _():
        x2_ref[...] = x1_ref[...]

    x2_ref[...] += _dot(a_ref[...], w_ref[...])

    @pl.when(k == pl.num_programs(1) - 1)
    def _():
        xn_ref[...] = _rms(x2_ref[...], g_ref[...]).astype(xn_ref.dtype)


def _ffn_down_final_kernel(a_ref, w_ref, x1_ref, g_ref, xn_ref, acc_ref):
    _ffn_down_kernel(a_ref, w_ref, x1_ref, g_ref, acc_ref, xn_ref)


def _ffn_down(act, w, x1, g, final, tm=1024, tk=512):
    m, kdim = act.shape
    d = w.shape[1]
    tile = pl.BlockSpec((tm, d), lambda i, k: (i, 0))
    in_specs = [
        pl.BlockSpec((tm, tk), lambda i, k: (i, k)),
        pl.BlockSpec((tk, d), lambda i, k: (k, 0)),
        tile,
        pl.BlockSpec((1, d), lambda i, k: (0, 0)),
    ]
    common = dict(grid=(m // tm, kdim // tk), in_specs=in_specs,
                  compiler_params=_params(("parallel", "arbitrary"), 56))
    if final:
        xn = pl.pallas_call(_ffn_down_final_kernel, out_specs=tile, out_shape=jax.ShapeDtypeStruct((m, d), F32),
                            scratch_shapes=[pltpu.VMEM((tm, d), F32)], name="ffn_down_final", **common)(act, w, x1, g)
        return None, xn
    return pl.pallas_call(_ffn_down_kernel, out_specs=[tile, tile],
                          out_shape=[jax.ShapeDtypeStruct((m, d), F32), jax.ShapeDtypeStruct((m, d), BF16)],
                          name="ffn_down", **common)(act, w, x1, g)


def _rope_table(pos, dim, theta):
    f32 = np.float32
    inv = f32(theta) ** (-np.arange(0, dim, 2, dtype=f32) / f32(dim))
    ang = (pos.astype(f32)[:, None] * inv[None, :]).astype(np.float64)
    return np.cos(ang).astype(f32), np.sin(ang).astype(f32)


def _lane_tables(seq):
    t = np.arange(seq)
    z = lambda w: np.zeros((seq, w), np.float32)
    one = lambda w: np.ones((seq, w), np.float32)
    cat = lambda parts: jnp.asarray(np.concatenate(parts, 1))
    cr, sr = _rope_table(t // GRID_W, HEAD_DIM // 2, A_ROPE_THETA)
    cc, sc = _rope_table(t % GRID_W, HEAD_DIM // 2, A_ROPE_THETA)
    tab_a = (cat([cr, cr, cc, cc]), cat([-sr, z(32), -sc, z(32)]), cat([z(32), sr, z(32), sc]))
    cp, sp = _rope_table(t, PARTIAL_ROPE_DIM, PARTIAL_ROPE_THETA)
    tab_b = (cat([cp, cp, one(96)]), cat([-sp, z(112)]), cat([z(16), sp, z(96)]))
    cm, sm = _rope_table(t, C_ROPE_DIM, C_ROPE_THETA)
    tab_c = (cat([cm, cm, one(64)]), cat([-sm, z(96)]), cat([z(32), sm, z(64)]))
    return tab_a, tab_b, tab_c


def _prep_w_ckr(w_in_t, layer):
    ckr = w_in_t[layer, IN_N_SRC * IN_WBLK:, :]
    return jnp.pad(ckr, ((0, IN_WBLK - ckr.shape[0]), (0, 0)))


def _prep_w_uq(w):
    w = w.reshape(C_Q_RANK, C_HEADS, C_NOPE_DIM + C_ROPE_DIM)
    w = jnp.pad(w, ((0, 0), (0, 0), (0, C_QK_PAD - C_NOPE_DIM - C_ROPE_DIM)))
    return w.reshape(C_Q_RANK, C_HEADS * C_QK_PAD).astype(BF16)


def _prep_w_ukv(w):
    w = w.reshape(C_KV_RANK, C_HEADS, C_NOPE_DIM + C_V_DIM)
    kn = w[:, :, :C_NOPE_DIM].reshape(C_KV_RANK, C_HEADS * C_NOPE_DIM)
    v = w[:, :, C_NOPE_DIM:].reshape(C_KV_RANK, C_HEADS * C_V_DIM)
    return jnp.concatenate([kn, v], axis=1).astype(BF16)


def kernel(x, attn_norm, w_in, a_q_norm, a_k_norm, c_q_norm, c_kv_norm, w_uq, w_ukv, out_norm, w_out,
           ffn_norm, w_up, conv_w, conv_b, w_down, final_norm):
    batch, seq, d = x.shape
    depth = w_in.shape[0]
    tab_a, tab_b, tab_c = _lane_tables(seq)
    row = lambda v: v.reshape(1, -1)

    xr = x.reshape(batch * seq, d)
    xn = _rmsnorm_rows(xr, row(attn_norm[0]))
    w_in_t = jnp.swapaxes(w_in, 1, 2)
    for l in range(depth):
        proj = _in_proj(xn, w_in_t, l, _prep_w_ckr(w_in_t, l))
        qc, kvc = _latent_up(proj, row(c_q_norm[l]), row(c_kv_norm[l]), _prep_w_uq(w_uq[l]),
                             _prep_w_ukv(w_ukv[l]))
        ya, w_out_bf = _attn_a(proj, tab_a, row(a_q_norm[l]), row(a_k_norm[l]), w_out, l, batch, seq)
        yb = _attn_b(proj, tab_b, batch, seq)
        yc = _attn_c(qc, kvc, proj, tab_c, batch, seq)
        g = out_norm[l]
        x1, xn1 = _out_proj(ya, yb, yc, row(g[:A_WIDTH]), row(g[A_WIDTH:A_WIDTH + B_WIDTH]),
                            row(g[A_WIDTH + B_WIDTH:]), w_out_bf, xr, row(ffn_norm[l]))
        act, w_down_bf = _ffn_up(xn1, w_up, w_down, l, conv_w[l], row(conv_b[l]), batch, seq)
        last = l == depth - 1
        g_next = final_norm if last else attn_norm[l + 1]
        xr, xn = _ffn_down(act, w_down_bf, x1, row(g_next), last)
    return xn.reshape(batch, seq, d)
```

```python
import functools
import math

import jax
import jax.numpy as jnp
import numpy as np
from jax import lax
from jax.experimental import pallas as pl
from jax.experimental.pallas import tpu as pltpu

D_MODEL = 2048
HEAD_DIM = 128
A_HEADS = 6
A_KV_HEADS = 2
A_GROUP = A_HEADS // A_KV_HEADS
A_ROPE_THETA = 10000.0
B_HEADS = 4
B_PATTERNS = ((128, 1), (512, 4), (2048, 16))
B_N_GROUPS = 3
C_HEADS = 6
C_Q_RANK = 512
C_KV_RANK = 512
C_NOPE_DIM = 128
C_ROPE_DIM = 64
C_V_DIM = 128
C_ROPE_THETA = 10000.0
PARTIAL_ROPE_DIM = HEAD_DIM // 4
PARTIAL_ROPE_THETA = 500000.0
GRID_W = 64
D_FF = 5632
EPS = 1e-6

A_WIDTH = A_HEADS * HEAD_DIM
B_WIDTH = B_HEADS * HEAD_DIM
C_WIDTH = C_HEADS * C_V_DIM
C_QK_PAD = 256

LANES = 128
LOG2E = math.log2(math.e)
F32 = jnp.float32
BF16 = jnp.bfloat16

PROJ_WIDTH = 5120
CB_CQ, CB_CKV = 0, 1
CB_AQ, CB_AK, CB_AV = 8, 14, 16
CB_BQ, CB_BK, CB_BV = 18, 30, 34
CB_CKR = 38


def _params(semantics, vmem_mib):
    return pltpu.CompilerParams(dimension_semantics=semantics, vmem_limit_bytes=vmem_mib * 1024 * 1024)


def _rms(xf, g):
    return xf * lax.rsqrt(jnp.mean(xf * xf, axis=-1, keepdims=True) + EPS) * g


def _rope(xf, cos, sin_lo, sin_hi, shift):
    return xf * cos + pltpu.roll(xf, LANES - shift, 1) * sin_lo + pltpu.roll(xf, shift, 1) * sin_hi


def _dot(a, b):
    return jnp.dot(a, b, preferred_element_type=F32)


def _dot_nt(a, b):
    return lax.dot_general(a, b, (((1,), (1,)), ((), ())), preferred_element_type=F32)


def _norm_kernel(x_ref, g_ref, o_ref):
    o_ref[...] = _rms(x_ref[...], g_ref[...]).astype(o_ref.dtype)


def _rmsnorm_rows(x, g, tm=512):
    m, d = x.shape
    return pl.pallas_call(
        _norm_kernel,
        grid=(m // tm,),
        in_specs=[pl.BlockSpec((tm, d), lambda i: (i, 0)), pl.BlockSpec((1, d), lambda i: (0, 0))],
        out_specs=pl.BlockSpec((tm, d), lambda i: (i, 0)),
        out_shape=jax.ShapeDtypeStruct((m, d), BF16),
        compiler_params=_params(("parallel",), 40),
        name="rmsnorm",
    )(x, g)


IN_WBLK = 256
IN_TN = 1024
IN_SRC_CQ = 15
IN_N_SRC = 19


def _in_proj_kernel(a_ref, *refs):
    w_refs, wck_ref, o_ref, w_bf = refs[:-3], refs[-3], refs[-2], refs[-1]
    j = pl.program_id(0)
    last = pl.num_programs(0) - 1

    @pl.when(pl.program_id(1) == 0)
    def _():
        for q, w_ref in enumerate(w_refs[:-1]):
            w_bf[q * IN_WBLK:(q + 1) * IN_WBLK, :] = w_ref[...].astype(BF16)
        tail = slice((len(w_refs) - 1) * IN_WBLK, len(w_refs) * IN_WBLK)

        @pl.when(j < last)
        def _():
            w_bf[tail, :] = w_refs[-1][...].astype(BF16)

        @pl.when(j == last)
        def _():
            w_bf[tail, :] = wck_ref[...].astype(BF16)

    o_ref[...] = _dot_nt(a_ref[...], w_bf[...]).astype(o_ref.dtype)


def _in_proj(a, w_in_t, layer, w_ckr_t, tm=2048):
    m, k = a.shape
    per_step = IN_TN // IN_WBLK

    def w_spec(q):
        def index(j, i):
            n = j * per_step + q
            return layer, jnp.where(n < 4, n + IN_SRC_CQ, n - 4), 0
        return pl.BlockSpec((None, IN_WBLK, k), index)

    return pl.pallas_call(
        _in_proj_kernel,
        grid=(PROJ_WIDTH // IN_TN, m // tm),
        in_specs=[pl.BlockSpec((tm, k), lambda j, i: (i, 0))] + [w_spec(q) for q in range(per_step)]
        + [pl.BlockSpec((IN_WBLK, k), lambda j, i: (0, 0))],
        out_specs=pl.BlockSpec((tm, IN_TN), lambda j, i: (i, j)),
        out_shape=jax.ShapeDtypeStruct((m, PROJ_WIDTH), BF16),
        scratch_shapes=[pltpu.VMEM((IN_TN, k), BF16)],
        compiler_params=_params(("parallel", "arbitrary"), 60),
        name="in_proj",
    )(a, *([w_in_t] * per_step), w_ckr_t)


def _latent_up_kernel(cq_ref, ckv_ref, gq_ref, gkv_ref, wq_ref, wkv_ref, q_ref, kv_ref, *, q_scale):
    cq = _rms(cq_ref[...].astype(F32), gq_ref[...]).astype(BF16)
    q_ref[...] = (_dot(cq, wq_ref[...]) * q_scale).astype(q_ref.dtype)
    ckv = _rms(ckv_ref[...].astype(F32), gkv_ref[...]).astype(BF16)
    kv_ref[...] = _dot(ckv, wkv_ref[...]).astype(kv_ref.dtype)


def _latent_up(proj, gq, gkv, wq, wkv, tm=1024):
    m = proj.shape[0]
    nq, nkv = wq.shape[1], wkv.shape[1]
    q_scale = (C_NOPE_DIM + C_ROPE_DIM) ** -0.5 * LOG2E
    return pl.pallas_call(
        functools.partial(_latent_up_kernel, q_scale=q_scale),
        grid=(m // tm,),
        in_specs=[
            pl.BlockSpec((tm, C_Q_RANK), lambda i: (i, CB_CQ)),
            pl.BlockSpec((tm, C_KV_RANK), lambda i: (i, CB_CKV)),
            pl.BlockSpec((1, C_Q_RANK), lambda i: (0, 0)),
            pl.BlockSpec((1, C_KV_RANK), lambda i: (0, 0)),
            pl.BlockSpec((C_Q_RANK, nq), lambda i: (0, 0)),
            pl.BlockSpec((C_KV_RANK, nkv), lambda i: (0, 0)),
        ],
        out_specs=[pl.BlockSpec((tm, nq), lambda i: (i, 0)), pl.BlockSpec((tm, nkv), lambda i: (i, 0))],
        out_shape=[jax.ShapeDtypeStruct((m, nq), BF16), jax.ShapeDtypeStruct((m, nkv), BF16)],
        compiler_params=_params(("parallel",), 48),
        name="latent_up",
    )(proj, proj, gq, gkv, wq, wkv)


A_ROW_CHUNK = 128
C_ROW_CHUNK = 256


def _attend(q_all, k_ref, v1_ref, dv, chunk):
    outs = []
    for c in range(q_all.shape[0] // chunk):
        q = q_all[c * chunk:(c + 1) * chunk]
        s = _dot_nt(q, k_ref[...])
        p = jnp.exp2(s - jnp.max(s, axis=-1, keepdims=True))
        ol = _dot(p.astype(BF16), v1_ref[...])
        outs.append(ol[:, :dv] * (1.0 / ol[:, dv:]))
    return outs


def _attn_a_kernel(q0_ref, q1_ref, q2_ref, k_ref, v_ref, cos_ref, slo_ref, shi_ref, gq_ref, gk_ref, wo_ref,
                   o_ref, wo_bf_ref, kbuf, v1buf, *, tq):
    qi = pl.program_id(2)
    wo_bf_ref[...] = wo_ref[...].astype(BF16)

    @pl.when(qi == 0)
    def _():
        k = _rms(k_ref[...].astype(F32), gk_ref[...])
        kbuf[...] = _rope(k, cos_ref[...], slo_ref[...], shi_ref[...], 32).astype(BF16)
        v1buf[:, :HEAD_DIM] = v_ref[...]
        v1buf[:, HEAD_DIM:] = jnp.ones((v1buf.shape[0], HEAD_DIM), BF16)

    rows = pl.ds(pl.multiple_of(qi * tq, tq), tq)
    cos, slo, shi = cos_ref[rows, :], slo_ref[rows, :], shi_ref[rows, :]
    q_scale = HEAD_DIM ** -0.5 * LOG2E
    qs = []
    for q_ref in (q0_ref, q1_ref, q2_ref):
        q = _rms(q_ref[...].astype(F32), gq_ref[...])
        qs.append((_rope(q, cos, slo, shi, 32) * q_scale).astype(BF16))
    q_all = jnp.concatenate(qs, axis=0)
    per_head = tq // A_ROW_CHUNK
    for c, o in enumerate(_attend(q_all, kbuf, v1buf, HEAD_DIM, A_ROW_CHUNK)):
        g, r = divmod(c, per_head)
        o_ref[r * A_ROW_CHUNK:(r + 1) * A_ROW_CHUNK, g * HEAD_DIM:(g + 1) * HEAD_DIM] = o.astype(o_ref.dtype)


def _attn_a(proj, tabs, gq, gk, w_out, layer, batch, seq, tq=1024):
    nq = seq // tq
    cos, slo, shi = tabs
    d = w_out.shape[2]
    wo_rows = w_out.shape[1] // (batch * A_KV_HEADS * nq)

    def wo_index(b, h, qi):
        return (b * A_KV_HEADS + h) * nq + qi

    def q_spec(g):
        return pl.BlockSpec((tq, HEAD_DIM), lambda b, h, qi: (b * nq + qi, CB_AQ + h * A_GROUP + g))

    tab_spec = pl.BlockSpec((seq, LANES), lambda b, h, qi: (0, 0))
    gain_spec = pl.BlockSpec((1, HEAD_DIM), lambda b, h, qi: (0, 0))
    return pl.pallas_call(
        functools.partial(_attn_a_kernel, tq=tq),
        grid=(batch, A_KV_HEADS, nq),
        in_specs=[
            q_spec(0), q_spec(1), q_spec(2),
            pl.BlockSpec((seq, HEAD_DIM), lambda b, h, qi: (b, CB_AK + h)),
            pl.BlockSpec((seq, HEAD_DIM), lambda b, h, qi: (b, CB_AV + h)),
            tab_spec, tab_spec, tab_spec, gain_spec, gain_spec,
            pl.BlockSpec((None, wo_rows, d), lambda b, h, qi: (layer, wo_index(b, h, qi), 0)),
        ],
        out_specs=[pl.BlockSpec((tq, A_GROUP * HEAD_DIM), lambda b, h, qi: (b * nq + qi, h)),
                   pl.BlockSpec((wo_rows, d), lambda b, h, qi: (wo_index(b, h, qi), 0))],
        out_shape=[jax.ShapeDtypeStruct((batch * seq, A_WIDTH), BF16),
                   jax.ShapeDtypeStruct(w_out.shape[1:], BF16)],
        scratch_shapes=[pltpu.VMEM((seq, HEAD_DIM), BF16), pltpu.VMEM((seq, 2 * HEAD_DIM), BF16)],
        compiler_params=_params(("parallel", "parallel", "arbitrary"), 48),
        name="attn_a",
    )(proj, proj, proj, proj, proj, cos, slo, shi, gq, gk, w_out)


def _attn_c_kernel(q_ref, kn_ref, v_ref, kr_ref, cos_ref, slo_ref, shi_ref, o_ref, kbuf, v1buf, *, tq):
    qi = pl.program_id(2)

    @pl.when(qi == 0)
    def _():
        kbuf[:, :C_NOPE_DIM] = kn_ref[...]
        kr = _rope(kr_ref[...].astype(F32), cos_ref[...], slo_ref[...], shi_ref[...], 32)
        kbuf[:, C_NOPE_DIM:] = kr.astype(BF16)
        v1buf[:, :C_V_DIM] = v_ref[...]
        v1buf[:, C_V_DIM:] = jnp.ones((v1buf.shape[0], C_V_DIM), BF16)

    rows = pl.ds(pl.multiple_of(qi * tq, tq), tq)
    q_rope = _rope(q_ref[:, C_NOPE_DIM:].astype(F32), cos_ref[rows, :], slo_ref[rows, :], shi_ref[rows, :], 32)
    q_all = jnp.concatenate([q_ref[:, :C_NOPE_DIM], q_rope.astype(BF16)], axis=1)
    for c, o in enumerate(_attend(q_all, kbuf, v1buf, C_V_DIM, C_ROW_CHUNK)):
        o_ref[c * C_ROW_CHUNK:(c + 1) * C_ROW_CHUNK, :] = o.astype(o_ref.dtype)


def _attn_c(qc, kvc, proj, tabs, batch, seq, tq=2048):
    nq = seq // tq
    cos, slo, shi = tabs
    tab_spec = pl.BlockSpec((seq, LANES), lambda b, h, qi: (0, 0))
    return pl.pallas_call(
        functools.partial(_attn_c_kernel, tq=tq),
        grid=(batch, C_HEADS, nq),
        in_specs=[
            pl.BlockSpec((tq, C_QK_PAD), lambda b, h, qi: (b * nq + qi, h)),
            pl.BlockSpec((seq, C_NOPE_DIM), lambda b, h, qi: (b, h)),
            pl.BlockSpec((seq, C_V_DIM), lambda b, h, qi: (b, C_HEADS + h)),
            pl.BlockSpec((seq, LANES), lambda b, h, qi: (b, CB_CKR)),
            tab_spec, tab_spec, tab_spec,
        ],
        out_specs=pl.BlockSpec((tq, C_V_DIM), lambda b, h, qi: (b * nq + qi, h)),
        out_shape=jax.ShapeDtypeStruct((batch * seq, C_WIDTH), BF16),
        scratch_shapes=[pltpu.VMEM((seq, C_QK_PAD), BF16), pltpu.VMEM((seq, 2 * C_V_DIM), BF16)],
        compiler_params=_params(("parallel", "parallel", "arbitrary"), 48),
        name="attn_c",
    )(qc, kvc, kvc, proj, cos, slo, shi)


B_QBLK = 128
(B_HALF,) = {w // (2 * d) for w, d in B_PATTERNS}
assert 2 * B_HALF == B_QBLK
B_PREP_ROWS = 256
B_UNROLL = 16


def _attn_b_kernel(q0_ref, q1_ref, q2_ref, k_ref, v_ref, cos_ref, slo_ref, shi_ref, o_ref,
                   q0f, q1f, q2f, kf, vf, m_s, l_s, acc_s, bias_s, *, seq):
    q_refs = (q0_ref, q1_ref, q2_ref)
    q_bufs = (q0f, q1f, q2f)
    q_scale = HEAD_DIM ** -0.5 * LOG2E

    half = B_HALF
    shape = (B_QBLK, 2 * B_QBLK)
    rel = lax.broadcasted_iota(jnp.int32, shape, 1) - lax.broadcasted_iota(jnp.int32, shape, 0)
    for case, offset in enumerate((0, -half, -B_QBLK)):
        bias_s[case] = jnp.where(jnp.abs(rel + offset) <= half, 0.0, -jnp.inf)

    def prep(ci, carry):
        rows = pl.ds(pl.multiple_of(ci * B_PREP_ROWS, B_PREP_ROWS), B_PREP_ROWS)
        cos, slo, shi = cos_ref[rows, :], slo_ref[rows, :], shi_ref[rows, :]
        for q_ref, q_buf in zip(q_refs, q_bufs):
            q_buf[rows, :] = _rope(q_ref[rows, :].astype(F32), cos, slo, shi, 16) * q_scale
        kf[rows, :] = _rope(k_ref[rows, :].astype(F32), cos, slo, shi, 16)
        vf[rows, :] = v_ref[rows, :].astype(F32)
        return carry

    lax.fori_loop(0, seq // B_PREP_ROWS, prep, 0)

    for g, (window, dil) in reversed(list(enumerate(B_PATTERNS))):
        first = g == len(B_PATTERNS) - 1
        half = window // (2 * dil)
        length = seq // dil
        nblk = length // B_QBLK
        kwin = min(2 * B_QBLK, length)
        q_buf = q_bufs[g]

        def rows_of(start, size, dil=dil):
            return pl.ds(start, size) if dil == 1 else pl.ds(start, size, stride=dil)

        def block(n, carry, first=first, half=half, length=length, nblk=nblk, kwin=kwin, q_buf=q_buf, dil=dil,
                  rows_of=rows_of):
            r = n // nblk
            i = n % nblk
            k0 = jnp.clip(i * B_QBLK - half, 0, length - kwin)
            q_rows = rows_of(r + dil * B_QBLK * i, B_QBLK)
            k_rows = rows_of(r + dil * k0, kwin)
            q = q_buf[q_rows, :].astype(BF16)
            k = kf[k_rows, :].astype(BF16)
            v = vf[k_rows, :].astype(BF16)
            case = 0 if nblk == 1 else jnp.where(i == 0, 0, jnp.where(i == nblk - 1, 2, 1))
            s = _dot_nt(q, k) + bias_s[case, :, :kwin]
            m_b = jnp.max(s, axis=-1, keepdims=True)
            p = jnp.exp2(s - m_b)
            l_b = jnp.sum(p, axis=-1, keepdims=True)
            a_b = _dot(p.astype(BF16), v)
            full = (B_QBLK, HEAD_DIM)
            if first:
                m_s[q_rows, :] = jnp.broadcast_to(m_b, full)
                l_s[q_rows, :] = jnp.broadcast_to(l_b, full)
                acc_s[q_rows, :] = a_b
            else:
                m_o = m_s[q_rows, :]
                m_n = jnp.maximum(m_o, m_b)
                w_o = jnp.exp2(m_o - m_n)
                w_b = jnp.exp2(m_b - m_n)
                acc_s[q_rows, :] = acc_s[q_rows, :] * w_o + a_b * w_b
                l_s[q_rows, :] = l_s[q_rows, :] * w_o + l_b * w_b
                m_s[q_rows, :] = m_n
            return carry

        lax.fori_loop(0, seq // B_QBLK, block, 0, unroll=B_UNROLL)

    def finish(ci, carry):
        rows = pl.ds(pl.multiple_of(ci * B_PREP_ROWS, B_PREP_ROWS), B_PREP_ROWS)
        o_ref[rows, :] = (acc_s[rows, :] * (1.0 / l_s[rows, :])).astype(o_ref.dtype)
        return carry

    lax.fori_loop(0, seq // B_PREP_ROWS, finish, 0)


def _attn_b(proj, tabs, batch, seq):
    cos, slo, shi = tabs

    def q_spec(g):
        return pl.BlockSpec((seq, HEAD_DIM), lambda b, h: (b, CB_BQ + g * B_HEADS + h))

    tab_spec = pl.BlockSpec((seq, LANES), lambda b, h: (0, 0))
    slab = pltpu.VMEM((seq, HEAD_DIM), F32)
    return pl.pallas_call(
        functools.partial(_attn_b_kernel, seq=seq),
        grid=(batch, B_HEADS),
        in_specs=[
            q_spec(0), q_spec(1), q_spec(2),
            pl.BlockSpec((seq, HEAD_DIM), lambda b, h: (b, CB_BK + h)),
            pl.BlockSpec((seq, HEAD_DIM), lambda b, h: (b, CB_BV + h)),
            tab_spec, tab_spec, tab_spec,
        ],
        out_specs=pl.BlockSpec((seq, HEAD_DIM), lambda b, h: (b, h)),
        out_shape=jax.ShapeDtypeStruct((batch * seq, B_WIDTH), BF16),
        scratch_shapes=[slab] * 8 + [pltpu.VMEM((3, B_QBLK, 2 * B_QBLK), F32)],
        compiler_params=_params(("parallel", "parallel"), 48),
        name="attn_b",
    )(proj, proj, proj, proj, proj, cos, slo, shi)


def _out_kernel(ya_ref, yb_ref, yc_ref, ga_ref, gb_ref, gc_ref, w_ref, x_ref, gf_ref, x1_ref, xn_ref):
    y = jnp.concatenate([
        _rms(ya_ref[...].astype(F32), ga_ref[...]).astype(BF16),
        _rms(yb_ref[...].astype(F32), gb_ref[...]).astype(BF16),
        _rms(yc_ref[...].astype(F32), gc_ref[...]).astype(BF16),
    ], axis=1)
    x1 = x_ref[...] + _dot(y, w_ref[...])
    x1_ref[...] = x1
    xn_ref[...] = _rms(x1, gf_ref[...]).astype(xn_ref.dtype)


def _out_proj(ya, yb, yc, ga, gb, gc, w, x, gf, tm=512):
    m, d = x.shape

    def rows(width):
        return pl.BlockSpec((tm, width), lambda i: (i, 0))

    def const(r, c):
        return pl.BlockSpec((r, c), lambda i: (0, 0))

    return pl.pallas_call(
        _out_kernel,
        grid=(m // tm,),
        in_specs=[rows(A_WIDTH), rows(B_WIDTH), rows(C_WIDTH), const(1, A_WIDTH), const(1, B_WIDTH),
                  const(1, C_WIDTH), const(d, d), rows(d), const(1, d)],
        out_specs=[rows(d), rows(d)],
        out_shape=[jax.ShapeDtypeStruct((m, d), F32), jax.ShapeDtypeStruct((m, d), BF16)],
        compiler_params=_params(("parallel",), 56),
        name="out_proj",
    )(ya, yb, yc, ga, gb, gc, w, x, gf)


FFN_SUBTILE = 256
FFN_ROWS = 256
FFN_GATE_ROWS = 32
FFN_INTERLEAVE = 8
FFN_GUARD = FFN_INTERLEAVE

def _ffn_up_kernel(xn_ref, wg_ref, wu_ref, cg_ref, cu_ref, bg_ref, bu_ref, wd_ref, o_ref, wd_bf_ref,
                   hg, hu, wg_bf, wu_bf, *, seq):
    tn = o_ref.shape[1]
    slabs = FFN_SUBTILE // LANES
    wd_bf_ref[...] = wd_ref[...].astype(BF16)

    @pl.when(pl.program_id(1) == 0)
    def _():
        wg_bf[...] = wg_ref[...].astype(BF16)
        wu_bf[...] = wu_ref[...].astype(BF16)

    zeros = jnp.zeros((FFN_GUARD, LANES), F32)
    for h in (hg, hu):
        for s in range(tn // LANES):
            h[s, 0:FFN_GUARD, :] = zeros
            h[s, FFN_GUARD + seq:2 * FFN_GUARD + seq, :] = zeros

    def matmuls(c, r):
        cols = slice(c * FFN_SUBTILE, (c + 1) * FFN_SUBTILE)
        xr = xn_ref[r * FFN_ROWS:(r + 1) * FFN_ROWS, :]
        dst = slice(FFN_GUARD + r * FFN_ROWS, FFN_GUARD + (r + 1) * FFN_ROWS)
        for h, w in ((hg, wg_bf), (hu, wu_bf)):
            res = _dot(xr, w[:, cols])
            for s in range(slabs):
                h[c * slabs + s, dst, :] = res[:, s * LANES:(s + 1) * LANES]

    def gate_stage(c, r):
        for s in range(slabs):
            slab = c * slabs + s
            lanes = slice(slab * LANES, (slab + 1) * LANES)
            for k in range(FFN_ROWS // FFN_GATE_ROWS):
                row0 = r * FFN_ROWS + k * FFN_GATE_ROWS
                base = FFN_GUARD + row0

                def tap(h, t):
                    if t == 1:
                        return h[slab, base:base + FFN_GATE_ROWS, :]
                    return h[pl.ds(slab, 1, stride=2), pl.ds(base + t - 1, FFN_GATE_ROWS), :][0]

                def conv(h, c_ref, b_ref):
                    taps = [tap(h, t) * c_ref[t:t + 1, lanes] for t in range(3)]
                    return taps[0] + taps[1] + taps[2] + b_ref[:, lanes]

                gate = conv(hg, cg_ref, bg_ref)
                up = conv(hu, cu_ref, bu_ref)
                o_ref[row0:row0 + FFN_GATE_ROWS, lanes] = (
                    gate * (1.0 / (1.0 + jnp.exp2(gate * -LOG2E))) * up).astype(o_ref.dtype)

    units = [(c, r) for c in range(tn // FFN_SUBTILE) for r in range(seq // FFN_ROWS)]
    for i, unit in enumerate(units):
        matmuls(*unit)
        if i > 0:
            gate_stage(*units[i - 1])
    gate_stage(*units[-1])


def _ffn_up(xn, w_up, w_down, layer, conv_w, conv_b, batch, seq, tn=512):
    d = xn.shape[1]
    nj = D_FF // tn
    wd_rows = w_down.shape[1] // (nj * batch)
    return pl.pallas_call(
        functools.partial(_ffn_up_kernel, seq=seq),
        grid=(nj, batch),
        in_specs=[
            pl.BlockSpec((seq, d), lambda j, b: (b, 0)),
            pl.BlockSpec((None, d, tn), lambda j, b: (layer, 0, j)),
            pl.BlockSpec((None, d, tn), lambda j, b: (layer, 0, nj + j)),
            pl.BlockSpec((3, tn), lambda j, b: (0, j)),
            pl.BlockSpec((3, tn), lambda j, b: (0, nj + j)),
            pl.BlockSpec((1, tn), lambda j, b: (0, j)),
            pl.BlockSpec((1, tn), lambda j, b: (0, nj + j)),
            pl.BlockSpec((None, wd_rows, d), lambda j, b: (layer, j * batch + b, 0)),
        ],
        out_specs=[pl.BlockSpec((seq, tn), lambda j, b: (b, j)),
                   pl.BlockSpec((wd_rows, d), lambda j, b: (j * batch + b, 0))],
        out_shape=[jax.ShapeDtypeStruct((batch * seq, D_FF), BF16),
                   jax.ShapeDtypeStruct(w_down.shape[1:], BF16)],
        scratch_shapes=[pltpu.VMEM((tn // LANES, seq + 2 * FFN_GUARD, LANES), F32)] * 2
        + [pltpu.VMEM((d, tn), BF16)] * 2,
        compiler_params=_params(("parallel", "arbitrary"), 60),
        name="ffn_up",
    )(xn, w_up, w_up, conv_w, conv_w, conv_b, conv_b, w_down)


def _ffn_down_kernel(a_ref, w_ref, x1_ref, g_ref, x2_ref, xn_ref):
    k = pl.program_id(1)

    @pl.when(k == 0)
    def _():
        x2_ref[...] = x1_ref[...]

    x2_ref[...] += _dot(a_ref[...], w_ref[...])

    @pl.when(k == pl.num_programs(1) - 1)
    def _():
        xn_ref[...] = _rms(x2_ref[...], g_ref[...]).astype(xn_ref.dtype)


def _ffn_down_final_kernel(a_ref, w_ref, x1_ref, g_ref, xn_ref, acc_ref):
    _ffn_down_kernel(a_ref, w_ref, x1_ref, g_ref, acc_ref, xn_ref)


def _ffn_down(act, w, x1, g, final, tm=1024, tk=512):
    m, kdim = act.shape
    d = w.shape[1]
    tile = pl.BlockSpec((tm, d), lambda i, k: (i, 0))
    in_specs = [
        pl.BlockSpec((tm, tk), lambda i, k: (i, k)),
        pl.BlockSpec((tk, d), lambda i, k: (k, 0)),
        tile,
        pl.BlockSpec((1, d), lambda i, k: (0, 0)),
    ]
    common = dict(grid=(m // tm, kdim // tk), in_specs=in_specs,
                  compiler_params=_params(("parallel", "arbitrary"), 56))
    if final:
        xn = pl.pallas_call(_ffn_down_final_kernel, out_specs=tile, out_shape=jax.ShapeDtypeStruct((m, d), F32),
                            scratch_shapes=[pltpu.VMEM((tm, d), F32)], name="ffn_down_final", **common)(act, w, x1, g)
        return None, xn
    return pl.pallas_call(_ffn_down_kernel, out_specs=[tile, tile],
                          out_shape=[jax.ShapeDtypeStruct((m, d), F32), jax.ShapeDtypeStruct((m, d), BF16)],
                          name="ffn_down", **common)(act, w, x1, g)


def _rope_table(pos, dim, theta):
    f32 = np.float32
    inv = f32(theta) ** (-np.arange(0, dim, 2, dtype=f32) / f32(dim))
    ang = (pos.astype(f32)[:, None] * inv[None, :]).astype(np.float64)
    return np.cos(ang).astype(f32), np.sin(ang).astype(f32)


def _lane_tables(seq):
    t = np.arange(seq)
    z = lambda w: np.zeros((seq, w), np.float32)
    one = lambda w: np.ones((seq, w), np.float32)
    cat = lambda parts: jnp.asarray(np.concatenate(parts, 1))
    cr, sr = _rope_table(t // GRID_W, HEAD_DIM // 2, A_ROPE_THETA)
    cc, sc = _rope_table(t % GRID_W, HEAD_DIM // 2, A_ROPE_THETA)
    tab_a = (cat([cr, cr, cc, cc]), cat([-sr, z(32), -sc, z(32)]), cat([z(32), sr, z(32), sc]))
    cp, sp = _rope_table(t, PARTIAL_ROPE_DIM, PARTIAL_ROPE_THETA)
    tab_b = (cat([cp, cp, one(96)]), cat([-sp, z(112)]), cat([z(16), sp, z(96)]))
    cm, sm = _rope_table(t, C_ROPE_DIM, C_ROPE_THETA)
    tab_c = (cat([cm, cm, one(64)]), cat([-sm, z(96)]), cat([z(32), sm, z(64)]))
    return tab_a, tab_b, tab_c


def _prep_w_ckr(w_in_t, layer):
    ckr = w_in_t[layer, IN_N_SRC * IN_WBLK:, :]
    return jnp.pad(ckr, ((0, IN_WBLK - ckr.shape[0]), (0, 0)))


def _prep_w_uq(w):
    w = w.reshape(C_Q_RANK, C_HEADS, C_NOPE_DIM + C_ROPE_DIM)
    w = jnp.pad(w, ((0, 0), (0, 0), (0, C_QK_PAD - C_NOPE_DIM - C_ROPE_DIM)))
    return w.reshape(C_Q_RANK, C_HEADS * C_QK_PAD).astype(BF16)


def _prep_w_ukv(w):
    w = w.reshape(C_KV_RANK, C_HEADS, C_NOPE_DIM + C_V_DIM)
    kn = w[:, :, :C_NOPE_DIM].reshape(C_KV_RANK, C_HEADS * C_NOPE_DIM)
    v = w[:, :, C_NOPE_DIM:].reshape(C_KV_RANK, C_HEADS * C_V_DIM)
    return jnp.concatenate([kn, v], axis=1).astype(BF16)


def kernel(x, attn_norm, w_in, a_q_norm, a_k_norm, c_q_norm, c_kv_norm, w_uq, w_ukv, out_norm, w_out,
           ffn_norm, w_up, conv_w, conv_b, w_down, final_norm):
    batch, seq, d = x.shape
    depth = w_in.shape[0]
    tab_a, tab_b, tab_c = _lane_tables(seq)
    row = lambda v: v.reshape(1, -1)

    xr = x.reshape(batch * seq, d)
    xn = _rmsnorm_rows(xr, row(attn_norm[0]))
    w_in_t = jnp.swapaxes(w_in, 1, 2)
    for l in range(depth):
        proj = _in_proj(xn, w_in_t, l, _prep_w_ckr(w_in_t, l))
        qc, kvc = _latent_up(proj, row(c_q_norm[l]), row(c_kv_norm[l]), _prep_w_uq(w_uq[l]),
                             _prep_w_ukv(w_ukv[l]))
        ya, w_out_bf = _attn_a(proj, tab_a, row(a_q_norm[l]), row(a_k_norm[l]), w_out, l, batch, seq)
        yb = _attn_b(proj, tab_b, batch, seq)
        yc = _attn_c(qc, kvc, proj, tab_c, batch, seq)
        g = out_norm[l]
        x1, xn1 = _out_proj(ya, yb, yc, row(g[:A_WIDTH]), row(g[A_WIDTH:A_WIDTH + B_WIDTH]),
                            row(g[A_WIDTH + B_WIDTH:]), w_out_bf, xr, row(ffn_norm[l]))
        act, w_down_bf = _ffn_up(xn1, w_up, w_down, l, conv_w[l], row(conv_b[l]), batch, seq)
        last = l == depth - 1
        g_next = final_norm if last else attn_norm[l + 1]
        xr, xn = _ffn_down(act, w_down_bf, x1, row(g_next), last)
    return xn.reshape(batch, seq, d)
```

```python
import functools
import math

import jax
import jax.numpy as jnp
import numpy as np
from jax import lax
from jax.experimental import pallas as pl
from jax.experimental.pallas import tpu as pltpu

D_MODEL = 2048
HEAD_DIM = 128
A_HEADS = 6
A_KV_HEADS = 2
A_GROUP = A_HEADS // A_KV_HEADS
A_ROPE_THETA = 10000.0
B_HEADS = 4
B_PATTERNS = ((128, 1), (512, 4), (2048, 16))
B_N_GROUPS = 3
C_HEADS = 6
C_Q_RANK = 512
C_KV_RANK = 512
C_NOPE_DIM = 128
C_ROPE_DIM = 64
C_V_DIM = 128
C_ROPE_THETA = 10000.0
PARTIAL_ROPE_DIM = HEAD_DIM // 4
PARTIAL_ROPE_THETA = 500000.0
GRID_W = 64
D_FF = 5632
EPS = 1e-6

A_WIDTH = A_HEADS * HEAD_DIM
B_WIDTH = B_HEADS * HEAD_DIM
C_WIDTH = C_HEADS * C_V_DIM
C_QK_PAD = 256

LANES = 128
LOG2E = math.log2(math.e)
F32 = jnp.float32
BF16 = jnp.bfloat16

PROJ_WIDTH = 5120
CB_CQ, CB_CKV = 0, 1
CB_AQ, CB_AK, CB_AV = 8, 14, 16
CB_BQ, CB_BK, CB_BV = 18, 30, 34
CB_CKR = 38


def _params(semantics, vmem_mib):
    return pltpu.CompilerParams(dimension_semantics=semantics, vmem_limit_bytes=vmem_mib * 1024 * 1024)


def _rms(xf, g):
    return xf * lax.rsqrt(jnp.mean(xf * xf, axis=-1, keepdims=True) + EPS) * g


def _rope(xf, cos, sin_lo, sin_hi, shift):
    return xf * cos + pltpu.roll(xf, LANES - shift, 1) * sin_lo + pltpu.roll(xf, shift, 1) * sin_hi


def _dot(a, b):
    return jnp.dot(a, b, preferred_element_type=F32)


def _dot_nt(a, b):
    return lax.dot_general(a, b, (((1,), (1,)), ((), ())), preferred_element_type=F32)


def _norm_kernel(x_ref, g_ref, o_ref):
    o_ref[...] = _rms(x_ref[...], g_ref[...]).astype(o_ref.dtype)


def _rmsnorm_rows(x, g, tm=512):
    m, d = x.shape
    return pl.pallas_call(
        _norm_kernel,
        grid=(m // tm,),
        in_specs=[pl.BlockSpec((tm, d), lambda i: (i, 0)), pl.BlockSpec((1, d), lambda i: (0, 0))],
        out_specs=pl.BlockSpec((tm, d), lambda i: (i, 0)),
        out_shape=jax.ShapeDtypeStruct((m, d), BF16),
        compiler_params=_params(("parallel",), 40),
        name="rmsnorm",
    )(x, g)


IN_WBLK = 256
IN_TN = 1024
IN_SRC_CQ = 15
IN_N_SRC = 19


def _in_proj_kernel(a_ref, *refs):
    w_refs, wck_ref, o_ref, w_bf = refs[:-3], refs[-3], refs[-2], refs[-1]
    j = pl.program_id(0)
    last = pl.num_programs(0) - 1

    @pl.when(pl.program_id(1) == 0)
    def _():
        for q, w_ref in enumerate(w_refs[:-1]):
            w_bf[q * IN_WBLK:(q + 1) * IN_WBLK, :] = w_ref[...].astype(BF16)
        tail = slice((len(w_refs) - 1) * IN_WBLK, len(w_refs) * IN_WBLK)

        @pl.when(j < last)
        def _():
            w_bf[tail, :] = w_refs[-1][...].astype(BF16)

        @pl.when(j == last)
        def _():
            w_bf[tail, :] = wck_ref[...].astype(BF16)

    o_ref[...] = _dot_nt(a_ref[...], w_bf[...]).astype(o_ref.dtype)


def _in_proj(a, w_in_t, layer, w_ckr_t, tm=2048):
    m, k = a.shape
    per_step = IN_TN // IN_WBLK

    def w_spec(q):
        def index(j, i):
            n = j * per_step + q
            return layer, jnp.where(n < 4, n + IN_SRC_CQ, n - 4), 0
        return pl.BlockSpec((None, IN_WBLK, k), index)

    return pl.pallas_call(
        _in_proj_kernel,
        grid=(PROJ_WIDTH // IN_TN, m // tm),
        in_specs=[pl.BlockSpec((tm, k), lambda j, i: (i, 0))] + [w_spec(q) for q in range(per_step)]
        + [pl.BlockSpec((IN_WBLK, k), lambda j, i: (0, 0))],
        out_specs=pl.BlockSpec((tm, IN_TN), lambda j, i: (i, j)),
        out_shape=jax.ShapeDtypeStruct((m, PROJ_WIDTH), BF16),
        scratch_shapes=[pltpu.VMEM((IN_TN, k), BF16)],
        compiler_params=_params(("parallel", "arbitrary"), 60),
        name="in_proj",
    )(a, *([w_in_t] * per_step), w_ckr_t)


def _latent_up_kernel(cq_ref, ckv_ref, gq_ref, gkv_ref, wq_ref, wkv_ref, q_ref, kv_ref, *, q_scale):
    cq = _rms(cq_ref[...].astype(F32), gq_ref[...]).astype(BF16)
    q_ref[...] = (_dot(cq, wq_ref[...]) * q_scale).astype(q_ref.dtype)
    ckv = _rms(ckv_ref[...].astype(F32), gkv_ref[...]).astype(BF16)
    kv_ref[...] = _dot(ckv, wkv_ref[...]).astype(kv_ref.dtype)


def _latent_up(proj, gq, gkv, wq, wkv, tm=1024):
    m = proj.shape[0]
    nq, nkv = wq.shape[1], wkv.shape[1]
    q_scale = (C_NOPE_DIM + C_ROPE_DIM) ** -0.5 * LOG2E
    return pl.pallas_call(
        functools.partial(_latent_up_kernel, q_scale=q_scale),
        grid=(m // tm,),
        in_specs=[
            pl.BlockSpec((tm, C_Q_RANK), lambda i: (i, CB_CQ)),
            pl.BlockSpec((tm, C_KV_RANK), lambda i: (i, CB_CKV)),
            pl.BlockSpec((1, C_Q_RANK), lambda i: (0, 0)),
            pl.BlockSpec((1, C_KV_RANK), lambda i: (0, 0)),
            pl.BlockSpec((C_Q_RANK, nq), lambda i: (0, 0)),
            pl.BlockSpec((C_KV_RANK, nkv), lambda i: (0, 0)),
        ],
        out_specs=[pl.BlockSpec((tm, nq), lambda i: (i, 0)), pl.BlockSpec((tm, nkv), lambda i: (i, 0))],
        out_shape=[jax.ShapeDtypeStruct((m, nq), BF16), jax.ShapeDtypeStruct((m, nkv), BF16)],
        compiler_params=_params(("parallel",), 48),
        name="latent_up",
    )(proj, proj, gq, gkv, wq, wkv)


A_ROW_CHUNK = 128
C_ROW_CHUNK = 256


def _attend(q_all, k_ref, v1_ref, dv, chunk):
    outs = []
    for c in range(q_all.shape[0] // chunk):
        q = q_all[c * chunk:(c + 1) * chunk]
        s = _dot_nt(q, k_ref[...])
        p = jnp.exp2(s - jnp.max(s, axis=-1, keepdims=True))
        ol = _dot(p.astype(BF16), v1_ref[...])
        outs.append(ol[:, :dv] * (1.0 / ol[:, dv:]))
    return outs


def _attn_a_kernel(q0_ref, q1_ref, q2_ref, k_ref, v_ref, cos_ref, slo_ref, shi_ref, gq_ref, gk_ref, wo_ref,
                   o_ref, wo_bf_ref, kbuf, v1buf, *, tq):
    qi = pl.program_id(2)
    wo_bf_ref[...] = wo_ref[...].astype(BF16)

    @pl.when(qi == 0)
    def _():
        k = _rms(k_ref[...].astype(F32), gk_ref[...])
        kbuf[...] = _rope(k, cos_ref[...], slo_ref[...], shi_ref[...], 32).astype(BF16)
        v1buf[:, :HEAD_DIM] = v_ref[...]
        v1buf[:, HEAD_DIM:] = jnp.ones((v1buf.shape[0], HEAD_DIM), BF16)

    rows = pl.ds(pl.multiple_of(qi * tq, tq), tq)
    cos, slo, shi = cos_ref[rows, :], slo_ref[rows, :], shi_ref[rows, :]
    q_scale = HEAD_DIM ** -0.5 * LOG2E
    qs = []
    for q_ref in (q0_ref, q1_ref, q2_ref):
        q = _rms(q_ref[...].astype(F32), gq_ref[...])
        qs.append((_rope(q, cos, slo, shi, 32) * q_scale).astype(BF16))
    q_all = jnp.concatenate(qs, axis=0)
    per_head = tq // A_ROW_CHUNK
    for c, o in enumerate(_attend(q_all, kbuf, v1buf, HEAD_DIM, A_ROW_CHUNK)):
        g, r = divmod(c, per_head)
        o_ref[r * A_ROW_CHUNK:(r + 1) * A_ROW_CHUNK, g * HEAD_DIM:(g + 1) * HEAD_DIM] = o.astype(o_ref.dtype)


def _attn_a(proj, tabs, gq, gk, w_out, layer, batch, seq, tq=1024):
    nq = seq // tq
    cos, slo, shi = tabs
    d = w_out.shape[2]
    wo_rows = w_out.shape[1] // (batch * A_KV_HEADS * nq)

    def wo_index(b, h, qi):
        return (b * A_KV_HEADS + h) * nq + qi

    def q_spec(g):
        return pl.BlockSpec((tq, HEAD_DIM), lambda b, h, qi: (b * nq + qi, CB_AQ + h * A_GROUP + g))

    tab_spec = pl.BlockSpec((seq, LANES), lambda b, h, qi: (0, 0))
    gain_spec = pl.BlockSpec((1, HEAD_DIM), lambda b, h, qi: (0, 0))
    return pl.pallas_call(
        functools.partial(_attn_a_kernel, tq=tq),
        grid=(batch, A_KV_HEADS, nq),
        in_specs=[
            q_spec(0), q_spec(1), q_spec(2),
            pl.BlockSpec((seq, HEAD_DIM), lambda b, h, qi: (b, CB_AK + h)),
            pl.BlockSpec((seq, HEAD_DIM), lambda b, h, qi: (b, CB_AV + h)),
            tab_spec, tab_spec, tab_spec, gain_spec, gain_spec,
            pl.BlockSpec((None, wo_rows, d), lambda b, h, qi: (layer, wo_index(b, h, qi), 0)),
        ],
        out_specs=[pl.BlockSpec((tq, A_GROUP * HEAD_DIM), lambda b, h, qi: (b * nq + qi, h)),
                   pl.BlockSpec((wo_rows, d), lambda b, h, qi: (wo_index(b, h, qi), 0))],
        out_shape=[jax.ShapeDtypeStruct((batch * seq, A_WIDTH), BF16),
                   jax.ShapeDtypeStruct(w_out.shape[1:], BF16)],
        scratch_shapes=[pltpu.VMEM((seq, HEAD_DIM), BF16), pltpu.VMEM((seq, 2 * HEAD_DIM), BF16)],
        compiler_params=_params(("parallel", "parallel", "arbitrary"), 48),
        name="attn_a",
    )(proj, proj, proj, proj, proj, cos, slo, shi, gq, gk, w_out)


def _attn_c_kernel(q_ref, kn_ref, v_ref, kr_ref, cos_ref, slo_ref, shi_ref, o_ref, kbuf, v1buf, *, tq):
    qi = pl.program_id(2)

    @pl.when(qi == 0)
    def _():
        kbuf[:, :C_NOPE_DIM] = kn_ref[...]
        kr = _rope(kr_ref[...].astype(F32), cos_ref[...], slo_ref[...], shi_ref[...], 32)
        kbuf[:, C_NOPE_DIM:] = kr.astype(BF16)
        v1buf[:, :C_V_DIM] = v_ref[...]
        v1buf[:, C_V_DIM:] = jnp.ones((v1buf.shape[0], C_V_DIM), BF16)

    rows = pl.ds(pl.multiple_of(qi * tq, tq), tq)
    q_rope = _rope(q_ref[:, C_NOPE_DIM:].astype(F32), cos_ref[rows, :], slo_ref[rows, :], shi_ref[rows, :], 32)
    q_all = jnp.concatenate([q_ref[:, :C_NOPE_DIM], q_rope.astype(BF16)], axis=1)
    for c, o in enumerate(_attend(q_all, kbuf, v1buf, C_V_DIM, C_ROW_CHUNK)):
        o_ref[c * C_ROW_CHUNK:(c + 1) * C_ROW_CHUNK, :] = o.astype(o_ref.dtype)


def _attn_c(qc, kvc, proj, tabs, batch, seq, tq=2048):
    nq = seq // tq
    cos, slo, shi = tabs
    tab_spec = pl.BlockSpec((seq, LANES), lambda b, h, qi: (0, 0))
    return pl.pallas_call(
        functools.partial(_attn_c_kernel, tq=tq),
        grid=(batch, C_HEADS, nq),
        in_specs=[
            pl.BlockSpec((tq, C_QK_PAD), lambda b, h, qi: (b * nq + qi, h)),
            pl.BlockSpec((seq, C_NOPE_DIM), lambda b, h, qi: (b, h)),
            pl.BlockSpec((seq, C_V_DIM), lambda b, h, qi: (b, C_HEADS + h)),
            pl.BlockSpec((seq, LANES), lambda b, h, qi: (b, CB_CKR)),
            tab_spec, tab_spec, tab_spec,
        ],
        out_specs=pl.BlockSpec((tq, C_V_DIM), lambda b, h, qi: (b * nq + qi, h)),
        out_shape=jax.ShapeDtypeStruct((batch * seq, C_WIDTH), BF16),
        scratch_shapes=[pltpu.VMEM((seq, C_QK_PAD), BF16), pltpu.VMEM((seq, 2 * C_V_DIM), BF16)],
        compiler_params=_params(("parallel", "parallel", "arbitrary"), 48),
        name="attn_c",
    )(qc, kvc, kvc, proj, cos, slo, shi)


B_QBLK = 128
(B_HALF,) = {w // (2 * d) for w, d in B_PATTERNS}
assert 2 * B_HALF == B_QBLK
B_PREP_ROWS = 256
B_UNROLL = 16


def _attn_b_kernel(q0_ref, q1_ref, q2_ref, k_ref, v_ref, cos_ref, slo_ref, shi_ref, o_ref,
                   q0f, q1f, q2f, kf, vf, m_s, l_s, acc_s, bias_s, *, seq):
    q_refs = (q0_ref, q1_ref, q2_ref)
    q_bufs = (q0f, q1f, q2f)
    q_scale = HEAD_DIM ** -0.5 * LOG2E

    half = B_HALF
    shape = (B_QBLK, 2 * B_QBLK)
    rel = lax.broadcasted_iota(jnp.int32, shape, 1) - lax.broadcasted_iota(jnp.int32, shape, 0)
    for case, offset in enumerate((0, -half, -B_QBLK)):
        bias_s[case] = jnp.where(jnp.abs(rel + offset) <= half, 0.0, -jnp.inf)

    def prep(ci, carry):
        rows = pl.ds(pl.multiple_of(ci * B_PREP_ROWS, B_PREP_ROWS), B_PREP_ROWS)
        cos, slo, shi = cos_ref[rows, :], slo_ref[rows, :], shi_ref[rows, :]
        for q_ref, q_buf in zip(q_refs, q_bufs):
            q_buf[rows, :] = _rope(q_ref[rows, :].astype(F32), cos, slo, shi, 16) * q_scale
        kf[rows, :] = _rope(k_ref[rows, :].astype(F32), cos, slo, shi, 16)
        vf[rows, :] = v_ref[rows, :].astype(F32)
        return carry

    lax.fori_loop(0, seq // B_PREP_ROWS, prep, 0, unroll=2)

    for g, (window, dil) in reversed(list(enumerate(B_PATTERNS))):
        first = g == len(B_PATTERNS) - 1
        half = window // (2 * dil)
        length = seq // dil
        nblk = length // B_QBLK
        kwin = min(2 * B_QBLK, length)
        q_buf = q_bufs[g]

        def rows_of(start, size, dil=dil):
            return pl.ds(start, size) if dil == 1 else pl.ds(start, size, stride=dil)

        def block(n, carry, first=first, half=half, length=length, nblk=nblk, kwin=kwin, q_buf=q_buf, dil=dil,
                  rows_of=rows_of):
            r = n // nblk
            i = n % nblk
            k0 = jnp.clip(i * B_QBLK - half, 0, length - kwin)
            q_rows = rows_of(r + dil * B_QBLK * i, B_QBLK)
            k_rows = rows_of(r + dil * k0, kwin)
            q = q_buf[q_rows, :].astype(BF16)
            k = kf[k_rows, :].astype(BF16)
            v = vf[k_rows, :].astype(BF16)
            case = 0 if nblk == 1 else jnp.where(i == 0, 0, jnp.where(i == nblk - 1, 2, 1))
            s = _dot_nt(q, k) + bias_s[case, :, :kwin]
            m_b = jnp.max(s, axis=-1, keepdims=True)
            p = jnp.exp2(s - m_b)
            l_b = jnp.sum(p, axis=-1, keepdims=True)
            a_b = _dot(p.astype(BF16), v)
            full = (B_QBLK, HEAD_DIM)
            if first:
                m_s[q_rows, :] = jnp.broadcast_to(m_b, full)
                l_s[q_rows, :] = jnp.broadcast_to(l_b, full)
                acc_s[q_rows, :] = a_b
            else:
                m_o = m_s[q_rows, :]
                m_n = jnp.maximum(m_o, m_b)
                w_o = jnp.exp2(m_o - m_n)
                w_b = jnp.exp2(m_b - m_n)
                acc_s[q_rows, :] = acc_s[q_rows, :] * w_o + a_b * w_b
                l_s[q_rows, :] = l_s[q_rows, :] * w_o + l_b * w_b
                m_s[q_rows, :] = m_n
            return carry

        lax.fori_loop(0, seq // B_QBLK, block, 0, unroll=B_UNROLL)

    def finish(ci, carry):
        rows = pl.ds(pl.multiple_of(ci * B_PREP_ROWS, B_PREP_ROWS), B_PREP_ROWS)
        o_ref[rows, :] = (acc_s[rows, :] * (1.0 / l_s[rows, :])).astype(o_ref.dtype)
        return carry

    lax.fori_loop(0, seq // B_PREP_ROWS, finish, 0)


def _attn_b(proj, tabs, batch, seq):
    cos, slo, shi = tabs

    def q_spec(g):
        return pl.BlockSpec((seq, HEAD_DIM), lambda b, h: (b, CB_BQ + g * B_HEADS + h))

    tab_spec = pl.BlockSpec((seq, LANES), lambda b, h: (0, 0))
    slab = pltpu.VMEM((seq, HEAD_DIM), F32)
    return pl.pallas_call(
        functools.partial(_attn_b_kernel, seq=seq),
        grid=(batch, B_HEADS),
        in_specs=[
            q_spec(0), q_spec(1), q_spec(2),
            pl.BlockSpec((seq, HEAD_DIM), lambda b, h: (b, CB_BK + h)),
            pl.BlockSpec((seq, HEAD_DIM), lambda b, h: (b, CB_BV + h)),
            tab_spec, tab_spec, tab_spec,
        ],
        out_specs=pl.BlockSpec((seq, HEAD_DIM), lambda b, h: (b, h)),
        out_shape=jax.ShapeDtypeStruct((batch * seq, B_WIDTH), BF16),
        scratch_shapes=[slab] * 8 + [pltpu.VMEM((3, B_QBLK, 2 * B_QBLK), F32)],
        compiler_params=_params(("parallel", "parallel"), 48),
        name="attn_b",
    )(proj, proj, proj, proj, proj, cos, slo, shi)


def _out_kernel(ya_ref, yb_ref, yc_ref, ga_ref, gb_ref, gc_ref, w_ref, x_ref, gf_ref, x1_ref, xn_ref):
    y = jnp.concatenate([
        _rms(ya_ref[...].astype(F32), ga_ref[...]).astype(BF16),
        _rms(yb_ref[...].astype(F32), gb_ref[...]).astype(BF16),
        _rms(yc_ref[...].astype(F32), gc_ref[...]).astype(BF16),
    ], axis=1)
    x1 = x_ref[...] + _dot(y, w_ref[...])
    x1_ref[...] = x1
    xn_ref[...] = _rms(x1, gf_ref[...]).astype(xn_ref.dtype)


def _out_proj(ya, yb, yc, ga, gb, gc, w, x, gf, tm=512):
    m, d = x.shape

    def rows(width):
        return pl.BlockSpec((tm, width), lambda i: (i, 0))

    def const(r, c):
        return pl.BlockSpec((r, c), lambda i: (0, 0))

    return pl.pallas_call(
        _out_kernel,
        grid=(m // tm,),
        in_specs=[rows(A_WIDTH), rows(B_WIDTH), rows(C_WIDTH), const(1, A_WIDTH), const(1, B_WIDTH),
                  const(1, C_WIDTH), const(d, d), rows(d), const(1, d)],
        out_specs=[rows(d), rows(d)],
        out_shape=[jax.ShapeDtypeStruct((m, d), F32), jax.ShapeDtypeStruct((m, d), BF16)],
        compiler_params=_params(("parallel",), 56),
        name="out_proj",
    )(ya, yb, yc, ga, gb, gc, w, x, gf)


FFN_SUBTILE = 256
FFN_ROWS = 1024
FFN_GATE_ROWS = 32
FFN_INTERLEAVE = 8
FFN_GUARD = FFN_INTERLEAVE

def _ffn_up_kernel(xn_ref, wg_ref, wu_ref, cg_ref, cu_ref, bg_ref, bu_ref, wd_ref, o_ref, wd_bf_ref,
                   hg, hu, wg_bf, wu_bf, *, seq):
    tn = o_ref.shape[1]
    slabs = FFN_SUBTILE // LANES
    wd_bf_ref[...] = wd_ref[...].astype(BF16)

    @pl.when(pl.program_id(1) == 0)
    def _():
        wg_bf[...] = wg_ref[...].astype(BF16)
        wu_bf[...] = wu_ref[...].astype(BF16)

    zeros = jnp.zeros((FFN_GUARD, LANES), F32)
    for h in (hg, hu):
        for s in range(tn // LANES):
            h[s, 0:FFN_GUARD, :] = zeros
            h[s, FFN_GUARD + seq:2 * FFN_GUARD + seq, :] = zeros

    def matmuls(c, r):
        cols = slice(c * FFN_SUBTILE, (c + 1) * FFN_SUBTILE)
        xr = xn_ref[r * FFN_ROWS:(r + 1) * FFN_ROWS, :]
        dst = slice(FFN_GUARD + r * FFN_ROWS, FFN_GUARD + (r + 1) * FFN_ROWS)
        for h, w in ((hg, wg_bf), (hu, wu_bf)):
            res = _dot(xr, w[:, cols])
            for s in range(slabs):
                h[c * slabs + s, dst, :] = res[:, s * LANES:(s + 1) * LANES]

    def gate_stage(c, r):
        for s in range(slabs):
            slab = c * slabs + s
            lanes = slice(slab * LANES, (slab + 1) * LANES)
            for k in range(FFN_ROWS // FFN_GATE_ROWS):
                row0 = r * FFN_ROWS + k * FFN_GATE_ROWS
                base = FFN_GUARD + row0

                def tap(h, t):
                    if t == 1:
                        return h[slab, base:base + FFN_GATE_ROWS, :]
                    return h[pl.ds(slab, 1, stride=2), pl.ds(base + t - 1, FFN_GATE_ROWS), :][0]

                def conv(h, c_ref, b_ref):
                    taps = [tap(h, t) * c_ref[t:t + 1, lanes] for t in range(3)]
                    return taps[0] + taps[1] + taps[2] + b_ref[:, lanes]

                gate = conv(hg, cg_ref, bg_ref)
                up = conv(hu, cu_ref, bu_ref)
                o_ref[row0:row0 + FFN_GATE_ROWS, lanes] = (
                    gate * (1.0 / (1.0 + jnp.exp2(gate * -LOG2E))) * up).astype(o_ref.dtype)

    units = [(c, r) for c in range(tn // FFN_SUBTILE) for r in range(seq // FFN_ROWS)]
    for i, unit in enumerate(units):
        matmuls(*unit)
        if i > 0:
            gate_stage(*units[i - 1])
    gate_stage(*units[-1])


def _ffn_up(xn, w_up, w_down, layer, conv_w, conv_b, batch, seq, tn=512):
    d = xn.shape[1]
    nj = D_FF // tn
    wd_rows = w_down.shape[1] // (nj * batch)
    return pl.pallas_call(
        functools.partial(_ffn_up_kernel, seq=seq),
        grid=(nj, batch),
        in_specs=[
            pl.BlockSpec((seq, d), lambda j, b: (b, 0)),
            pl.BlockSpec((None, d, tn), lambda j, b: (layer, 0, j)),
            pl.BlockSpec((None, d, tn), lambda j, b: (layer, 0, nj + j)),
            pl.BlockSpec((3, tn), lambda j, b: (0, j)),
            pl.BlockSpec((3, tn), lambda j, b: (0, nj + j)),
            pl.BlockSpec((1, tn), lambda j, b: (0, j)),
            pl.BlockSpec((1, tn), lambda j, b: (0, nj + j)),
            pl.BlockSpec((None, wd_rows, d), lambda j, b: (layer, j * batch + b, 0)),
        ],
        out_specs=[pl.BlockSpec((seq, tn), lambda j, b: (b, j)),
                   pl.BlockSpec((wd_rows, d), lambda j, b: (j * batch + b, 0))],
        out_shape=[jax.ShapeDtypeStruct((batch * seq, D_FF), BF16),
                   jax.ShapeDtypeStruct(w_down.shape[1:], BF16)],
        scratch_shapes=[pltpu.VMEM((tn // LANES, seq + 2 * FFN_GUARD, LANES), F32)] * 2
        + [pltpu.VMEM((d, tn), BF16)] * 2,
        compiler_params=_params(("parallel", "arbitrary"), 60),
        name="ffn_up",
    )(xn, w_up, w_up, conv_w, conv_w, conv_b, conv_b, w_down)


def _ffn_down_kernel(a_ref, w_ref, x1_ref, g_ref, x2_ref, xn_ref):
    k = pl.program_id(1)

    @pl.when(k == 0)
    def _():
        x2_ref[...] = x1_ref[...]

    x2_ref[...] += _dot(a_ref[...], w_ref[...])

    @pl.when(k == pl.num_programs(1) - 1)
    def _():
        xn_ref[...] = _rms(x2_ref[...], g_ref[...]).astype(xn_ref.dtype)


def _ffn_down_final_kernel(a_ref, w_ref, x1_ref, g_ref, xn_ref, acc_ref):
    _ffn_down_kernel(a_ref, w_ref, x1_ref, g_ref, acc_ref, xn_ref)


def _ffn_down(act, w, x1, g, final, tm=1024, tk=512):
    m, kdim = act.shape
    d = w.shape[1]
    tile = pl.BlockSpec((tm, d), lambda i, k: (i, 0))
    in_specs = [
        pl.BlockSpec((tm, tk), lambda i, k: (i, k)),
        pl.BlockSpec((tk, d), lambda i, k: (k, 0)),
        tile,
        pl.BlockSpec((1, d), lambda i, k: (0, 0)),
    ]
    common = dict(grid=(m // tm, kdim // tk), in_specs=in_specs,
                  compiler_params=_params(("parallel", "arbitrary"), 56))
    if final:
        xn = pl.pallas_call(_ffn_down_final_kernel, out_specs=tile, out_shape=jax.ShapeDtypeStruct((m, d), F32),
                            scratch_shapes=[pltpu.VMEM((tm, d), F32)], name="ffn_down_final", **common)(act, w, x1, g)
        return None, xn
    return pl.pallas_call(_ffn_down_kernel, out_specs=[tile, tile],
                          out_shape=[jax.ShapeDtypeStruct((m, d), F32), jax.ShapeDtypeStruct((m, d), BF16)],
                          name="ffn_down", **common)(act, w, x1, g)


def _rope_table(pos, dim, theta):
    f32 = np.float32
    inv = f32(theta) ** (-np.arange(0, dim, 2, dtype=f32) / f32(dim))
    ang = (pos.astype(f32)[:, None] * inv[None, :]).astype(np.float64)
    return np.cos(ang).astype(f32), np.sin(ang).astype(f32)


def _lane_tables(seq):
    t = np.arange(seq)
    z = lambda w: np.zeros((seq, w), np.float32)
    one = lambda w: np.ones((seq, w), np.float32)
    cat = lambda parts: jnp.asarray(np.concatenate(parts, 1))
    cr, sr = _rope_table(t // GRID_W, HEAD_DIM // 2, A_ROPE_THETA)
    cc, sc = _rope_table(t % GRID_W, HEAD_DIM // 2, A_ROPE_THETA)
    tab_a = (cat([cr, cr, cc, cc]), cat([-sr, z(32), -sc, z(32)]), cat([z(32), sr, z(32), sc]))
    cp, sp = _rope_table(t, PARTIAL_ROPE_DIM, PARTIAL_ROPE_THETA)
    tab_b = (cat([cp, cp, one(96)]), cat([-sp, z(112)]), cat([z(16), sp, z(96)]))
    cm, sm = _rope_table(t, C_ROPE_DIM, C_ROPE_THETA)
    tab_c = (cat([cm, cm, one(64)]), cat([-sm, z(96)]), cat([z(32), sm, z(64)]))
    return tab_a, tab_b, tab_c


def _prep_w_ckr(w_in_t, layer):
    ckr = w_in_t[layer, IN_N_SRC * IN_WBLK:, :]
    return jnp.pad(ckr, ((0, IN_WBLK - ckr.shape[0]), (0, 0)))


def _prep_w_uq(w):
    w = w.reshape(C_Q_RANK, C_HEADS, C_NOPE_DIM + C_ROPE_DIM)
    w = jnp.pad(w, ((0, 0), (0, 0), (0, C_QK_PAD - C_NOPE_DIM - C_ROPE_DIM)))
    return w.reshape(C_Q_RANK, C_HEADS * C_QK_PAD).astype(BF16)


def _prep_w_ukv(w):
    w = w.reshape(C_KV_RANK, C_HEADS, C_NOPE_DIM + C_V_DIM)
    kn = w[:, :, :C_NOPE_DIM].reshape(C_KV_RANK, C_HEADS * C_NOPE_DIM)
    v = w[:, :, C_NOPE_DIM:].reshape(C_KV_RANK, C_HEADS * C_V_DIM)
    return jnp.concatenate([kn, v], axis=1).astype(BF16)


def kernel(x, attn_norm, w_in, a_q_norm, a_k_norm, c_q_norm, c_kv_norm, w_uq, w_ukv, out_norm, w_out,
           ffn_norm, w_up, conv_w, conv_b, w_down, final_norm):
    batch, seq, d = x.shape
    depth = w_in.shape[0]
    tab_a, tab_b, tab_c = _lane_tables(seq)
    row = lambda v: v.reshape(1, -1)

    xr = x.reshape(batch * seq, d)
    xn = _rmsnorm_rows(xr, row(attn_norm[0]))
    w_in_t = jnp.swapaxes(w_in, 1, 2)
    for l in range(depth):
        proj = _in_proj(xn, w_in_t, l, _prep_w_ckr(w_in_t, l))
        qc, kvc = _latent_up(proj, row(c_q_norm[l]), row(c_kv_norm[l]), _prep_w_uq(w_uq[l]),
                             _prep_w_ukv(w_ukv[l]))
        ya, w_out_bf = _attn_a(proj, tab_a, row(a_q_norm[l]), row(a_k_norm[l]), w_out, l, batch, seq)
        yb = _attn_b(proj, tab_b, batch, seq)
        yc = _attn_c(qc, kvc, proj, tab_c, batch, seq)
        g = out_norm[l]
        x1, xn1 = _out_proj(ya, yb, yc, row(g[:A_WIDTH]), row(g[A_WIDTH:A_WIDTH + B_WIDTH]),
                            row(g[A_WIDTH + B_WIDTH:]), w_out_bf, xr, row(ffn_norm[l]))
        act, w_down_bf = _ffn_up(xn1, w_up, w_down, l, conv_w[l], row(conv_b[l]), batch, seq)
        last = l == depth - 1
        g_next = final_norm if last else attn_norm[l + 1]
        xr, xn = _ffn_down(act, w_down_bf, x1, row(g_next), last)
    return xn.reshape(batch, seq, d)
```

```python
import functools
import math

import jax
import jax.numpy as jnp
import numpy as np
from jax import lax
from jax.experimental import pallas as pl
from jax.experimental.pallas import tpu as pltpu

D_MODEL = 2048
HEAD_DIM = 128
A_HEADS = 6
A_KV_HEADS = 2
A_GROUP = A_HEADS // A_KV_HEADS
A_ROPE_THETA = 10000.0
B_HEADS = 4
B_PATTERNS = ((128, 1), (512, 4), (2048, 16))
B_N_GROUPS = 3
C_HEADS = 6
C_Q_RANK = 512
C_KV_RANK = 512
C_NOPE_DIM = 128
C_ROPE_DIM = 64
C_V_DIM = 128
C_ROPE_THETA = 10000.0
PARTIAL_ROPE_DIM = HEAD_DIM // 4
PARTIAL_ROPE_THETA = 500000.0
GRID_W = 64
D_FF = 5632
EPS = 1e-6

A_WIDTH = A_HEADS * HEAD_DIM
B_WIDTH = B_HEADS * HEAD_DIM
C_WIDTH = C_HEADS * C_V_DIM
C_QK_PAD = 256

LANES = 128
LOG2E = math.log2(math.e)
F32 = jnp.float32
BF16 = jnp.bfloat16

PROJ_WIDTH = 5120
CB_CQ, CB_CKV = 0, 1
CB_AQ, CB_AK, CB_AV = 8, 14, 16
CB_BQ, CB_BK, CB_BV = 18, 30, 34
CB_CKR = 38


def _params(semantics, vmem_mib):
    return pltpu.CompilerParams(dimension_semantics=semantics, vmem_limit_bytes=vmem_mib * 1024 * 1024)


def _rms(xf, g):
    return xf * lax.rsqrt(jnp.mean(xf * xf, axis=-1, keepdims=True) + EPS) * g


def _rope(xf, cos, sin_lo, sin_hi, shift):
    return xf * cos + pltpu.roll(xf, LANES - shift, 1) * sin_lo + pltpu.roll(xf, shift, 1) * sin_hi


def _dot(a, b):
    return jnp.dot(a, b, preferred_element_type=F32)


def _dot_nt(a, b):
    return lax.dot_general(a, b, (((1,), (1,)), ((), ())), preferred_element_type=F32)


def _norm_kernel(x_ref, g_ref, o_ref):
    o_ref[...] = _rms(x_ref[...], g_ref[...]).astype(o_ref.dtype)


def _rmsnorm_rows(x, g, tm=512):
    m, d = x.shape
    return pl.pallas_call(
        _norm_kernel,
        grid=(m // tm,),
        in_specs=[pl.BlockSpec((tm, d), lambda i: (i, 0)), pl.BlockSpec((1, d), lambda i: (0, 0))],
        out_specs=pl.BlockSpec((tm, d), lambda i: (i, 0)),
        out_shape=jax.ShapeDtypeStruct((m, d), BF16),
        compiler_params=_params(("parallel",), 40),
        name="rmsnorm",
    )(x, g)


IN_WBLK = 256
IN_TN = 1024
IN_SRC_CQ = 15
IN_N_SRC = 19


def _in_proj_kernel(a_ref, *refs):
    w_refs, wck_ref, o_ref, w_bf = refs[:-3], refs[-3], refs[-2], refs[-1]
    j = pl.program_id(0)
    last = pl.num_programs(0) - 1

    @pl.when(pl.program_id(1) == 0)
    def _():
        for q, w_ref in enumerate(w_refs[:-1]):
            w_bf[q * IN_WBLK:(q + 1) * IN_WBLK, :] = w_ref[...].astype(BF16)
        tail = slice((len(w_refs) - 1) * IN_WBLK, len(w_refs) * IN_WBLK)

        @pl.when(j < last)
        def _():
            w_bf[tail, :] = w_refs[-1][...].astype(BF16)

        @pl.when(j == last)
        def _():
            w_bf[tail, :] = wck_ref[...].astype(BF16)

    o_ref[...] = _dot_nt(a_ref[...], w_bf[...]).astype(o_ref.dtype)


def _in_proj(a, w_in_t, layer, w_ckr_t, tm=2048):
    m, k = a.shape
    per_step = IN_TN // IN_WBLK

    def w_spec(q):
        def index(j, i):
            n = j * per_step + q
            return layer, jnp.where(n < 4, n + IN_SRC_CQ, n - 4), 0
        return pl.BlockSpec((None, IN_WBLK, k), index)

    return pl.pallas_call(
        _in_proj_kernel,
        grid=(PROJ_WIDTH // IN_TN, m // tm),
        in_specs=[pl.BlockSpec((tm, k), lambda j, i: (i, 0))] + [w_spec(q) for q in range(per_step)]
        + [pl.BlockSpec((IN_WBLK, k), lambda j, i: (0, 0))],
        out_specs=pl.BlockSpec((tm, IN_TN), lambda j, i: (i, j)),
        out_shape=jax.ShapeDtypeStruct((m, PROJ_WIDTH), BF16),
        scratch_shapes=[pltpu.VMEM((IN_TN, k), BF16)],
        compiler_params=_params(("parallel", "arbitrary"), 60),
        name="in_proj",
    )(a, *([w_in_t] * per_step), w_ckr_t)


def _latent_up_kernel(cq_ref, ckv_ref, gq_ref, gkv_ref, wq_ref, wkv_ref, q_ref, kv_ref, *, q_scale):
    cq = _rms(cq_ref[...].astype(F32), gq_ref[...]).astype(BF16)
    q_ref[...] = (_dot(cq, wq_ref[...]) * q_scale).astype(q_ref.dtype)
    ckv = _rms(ckv_ref[...].astype(F32), gkv_ref[...]).astype(BF16)
    kv_ref[...] = _dot(ckv, wkv_ref[...]).astype(kv_ref.dtype)


def _latent_up(proj, gq, gkv, wq, wkv, tm=1024):
    m = proj.shape[0]
    nq, nkv = wq.shape[1], wkv.shape[1]
    q_scale = (C_NOPE_DIM + C_ROPE_DIM) ** -0.5 * LOG2E
    return pl.pallas_call(
        functools.partial(_latent_up_kernel, q_scale=q_scale),
        grid=(m // tm,),
        in_specs=[
            pl.BlockSpec((tm, C_Q_RANK), lambda i: (i, CB_CQ)),
            pl.BlockSpec((tm, C_KV_RANK), lambda i: (i, CB_CKV)),
            pl.BlockSpec((1, C_Q_RANK), lambda i: (0, 0)),
            pl.BlockSpec((1, C_KV_RANK), lambda i: (0, 0)),
            pl.BlockSpec((C_Q_RANK, nq), lambda i: (0, 0)),
            pl.BlockSpec((C_KV_RANK, nkv), lambda i: (0, 0)),
        ],
        out_specs=[pl.BlockSpec((tm, nq), lambda i: (i, 0)), pl.BlockSpec((tm, nkv), lambda i: (i, 0))],
        out_shape=[jax.ShapeDtypeStruct((m, nq), BF16), jax.ShapeDtypeStruct((m, nkv), BF16)],
        compiler_params=_params(("parallel",), 48),
        name="latent_up",
    )(proj, proj, gq, gkv, wq, wkv)


A_ROW_CHUNK = 128
C_ROW_CHUNK = 256


def _attend(q_all, k_ref, v1_ref, dv, chunk):
    outs = []
    for c in range(q_all.shape[0] // chunk):
        q = q_all[c * chunk:(c + 1) * chunk]
        s = _dot_nt(q, k_ref[...])
        p = jnp.exp2(s - jnp.max(s, axis=-1, keepdims=True))
        ol = _dot(p.astype(BF16), v1_ref[...])
        outs.append(ol[:, :dv] * (1.0 / ol[:, dv:]))
    return outs


def _attn_a_kernel(q0_ref, q1_ref, q2_ref, k_ref, v_ref, cos_ref, slo_ref, shi_ref, gq_ref, gk_ref, wo_ref,
                   o_ref, wo_bf_ref, kbuf, v1buf, *, tq):
    qi = pl.program_id(2)
    wo_bf_ref[...] = wo_ref[...].astype(BF16)

    @pl.when(qi == 0)
    def _():
        k = _rms(k_ref[...].astype(F32), gk_ref[...])
        kbuf[...] = _rope(k, cos_ref[...], slo_ref[...], shi_ref[...], 32).astype(BF16)
        v1buf[:, :HEAD_DIM] = v_ref[...]
        v1buf[:, HEAD_DIM:] = jnp.ones((v1buf.shape[0], HEAD_DIM), BF16)

    rows = pl.ds(pl.multiple_of(qi * tq, tq), tq)
    cos, slo, shi = cos_ref[rows, :], slo_ref[rows, :], shi_ref[rows, :]
    q_scale = HEAD_DIM ** -0.5 * LOG2E
    qs = []
    for q_ref in (q0_ref, q1_ref, q2_ref):
        q = _rms(q_ref[...].astype(F32), gq_ref[...])
        qs.append((_rope(q, cos, slo, shi, 32) * q_scale).astype(BF16))
    q_all = jnp.concatenate(qs, axis=0)
    per_head = tq // A_ROW_CHUNK
    for c, o in enumerate(_attend(q_all, kbuf, v1buf, HEAD_DIM, A_ROW_CHUNK)):
        g, r = divmod(c, per_head)
        o_ref[r * A_ROW_CHUNK:(r + 1) * A_ROW_CHUNK, g * HEAD_DIM:(g + 1) * HEAD_DIM] = o.astype(o_ref.dtype)


def _attn_a(proj, tabs, gq, gk, w_out, layer, batch, seq, tq=1024):
    nq = seq // tq
    cos, slo, shi = tabs
    d = w_out.shape[2]
    wo_rows = w_out.shape[1] // (batch * A_KV_HEADS * nq)

    def wo_index(b, h, qi):
        return (b * A_KV_HEADS + h) * nq + qi

    def q_spec(g):
        return pl.BlockSpec((tq, HEAD_DIM), lambda b, h, qi: (b * nq + qi, CB_AQ + h * A_GROUP + g))

    tab_spec = pl.BlockSpec((seq, LANES), lambda b, h, qi: (0, 0))
    gain_spec = pl.BlockSpec((1, HEAD_DIM), lambda b, h, qi: (0, 0))
    return pl.pallas_call(
        functools.partial(_attn_a_kernel, tq=tq),
        grid=(batch, A_KV_HEADS, nq),
        in_specs=[
            q_spec(0), q_spec(1), q_spec(2),
            pl.BlockSpec((seq, HEAD_DIM), lambda b, h, qi: (b, CB_AK + h)),
            pl.BlockSpec((seq, HEAD_DIM), lambda b, h, qi: (b, CB_AV + h)),
            tab_spec, tab_spec, tab_spec, gain_spec, gain_spec,
            pl.BlockSpec((None, wo_rows, d), lambda b, h, qi: (layer, wo_index(b, h, qi), 0)),
        ],
        out_specs=[pl.BlockSpec((tq, A_GROUP * HEAD_DIM), lambda b, h, qi: (b * nq + qi, h)),
                   pl.BlockSpec((wo_rows, d), lambda b, h, qi: (wo_index(b, h, qi), 0))],
        out_shape=[jax.ShapeDtypeStruct((batch * seq, A_WIDTH), BF16),
                   jax.ShapeDtypeStruct(w_out.shape[1:], BF16)],
        scratch_shapes=[pltpu.VMEM((seq, HEAD_DIM), BF16), pltpu.VMEM((seq, 2 * HEAD_DIM), BF16)],
        compiler_params=_params(("parallel", "parallel", "arbitrary"), 48),
        name="attn_a",
    )(proj, proj, proj, proj, proj, cos, slo, shi, gq, gk, w_out)


def _attn_c_kernel(q_ref, kn_ref, v_ref, kr_ref, cos_ref, slo_ref, shi_ref, o_ref, kbuf, v1buf, *, tq):
    qi = pl.program_id(2)

    @pl.when(qi == 0)
    def _():
        kbuf[:, :C_NOPE_DIM] = kn_ref[...]
        kr = _rope(kr_ref[...].astype(F32), cos_ref[...], slo_ref[...], shi_ref[...], 32)
        kbuf[:, C_NOPE_DIM:] = kr.astype(BF16)
        v1buf[:, :C_V_DIM] = v_ref[...]
        v1buf[:, C_V_DIM:] = jnp.ones((v1buf.shape[0], C_V_DIM), BF16)

    rows = pl.ds(pl.multiple_of(qi * tq, tq), tq)
    q_rope = _rope(q_ref[:, C_NOPE_DIM:].astype(F32), cos_ref[rows, :], slo_ref[rows, :], shi_ref[rows, :], 32)
    q_all = jnp.concatenate([q_ref[:, :C_NOPE_DIM], q_rope.astype(BF16)], axis=1)
    for c, o in enumerate(_attend(q_all, kbuf, v1buf, C_V_DIM, C_ROW_CHUNK)):
        o_ref[c * C_ROW_CHUNK:(c + 1) * C_ROW_CHUNK, :] = o.astype(o_ref.dtype)


def _attn_c(qc, kvc, proj, tabs, batch, seq, tq=2048):
    nq = seq // tq
    cos, slo, shi = tabs
    tab_spec = pl.BlockSpec((seq, LANES), lambda b, h, qi: (0, 0))
    return pl.pallas_call(
        functools.partial(_attn_c_kernel, tq=tq),
        grid=(batch, C_HEADS, nq),
        in_specs=[
            pl.BlockSpec((tq, C_QK_PAD), lambda b, h, qi: (b * nq + qi, h)),
            pl.BlockSpec((seq, C_NOPE_DIM), lambda b, h, qi: (b, h)),
            pl.BlockSpec((seq, C_V_DIM), lambda b, h, qi: (b, C_HEADS + h)),
            pl.BlockSpec((seq, LANES), lambda b, h, qi: (b, CB_CKR)),
            tab_spec, tab_spec, tab_spec,
        ],
        out_specs=pl.BlockSpec((tq, C_V_DIM), lambda b, h, qi: (b * nq + qi, h)),
        out_shape=jax.ShapeDtypeStruct((batch * seq, C_WIDTH), BF16),
        scratch_shapes=[pltpu.VMEM((seq, C_QK_PAD), BF16), pltpu.VMEM((seq, 2 * C_V_DIM), BF16)],
        compiler_params=_params(("parallel", "parallel", "arbitrary"), 48),
        name="attn_c",
    )(qc, kvc, kvc, proj, cos, slo, shi)


B_QBLK = 128
(B_HALF,) = {w // (2 * d) for w, d in B_PATTERNS}
assert 2 * B_HALF == B_QBLK
B_PREP_ROWS = 256
B_UNROLL = 16


def _attn_b_kernel(q0_ref, q1_ref, q2_ref, k_ref, v_ref, cos_ref, slo_ref, shi_ref, o_ref,
                   q0f, q1f, q2f, kf, vf, m_s, l_s, acc_s, bias_s, *, seq):
    q_refs = (q0_ref, q1_ref, q2_ref)
    q_bufs = (q0f, q1f, q2f)
    q_scale = HEAD_DIM ** -0.5 * LOG2E

    half = B_HALF
    shape = (B_QBLK, 2 * B_QBLK)
    rel = lax.broadcasted_iota(jnp.int32, shape, 1) - lax.broadcasted_iota(jnp.int32, shape, 0)
    for case, offset in enumerate((0, -half, -B_QBLK)):
        bias_s[case] = jnp.where(jnp.abs(rel + offset) <= half, 0.0, -jnp.inf)

    def prep(ci, carry):
        rows = pl.ds(pl.multiple_of(ci * B_PREP_ROWS, B_PREP_ROWS), B_PREP_ROWS)
        cos, slo, shi = cos_ref[rows, :], slo_ref[rows, :], shi_ref[rows, :]
        for q_ref, q_buf in zip(q_refs, q_bufs):
            q_buf[rows, :] = _rope(q_ref[rows, :].astype(F32), cos, slo, shi, 16) * q_scale
        kf[rows, :] = _rope(k_ref[rows, :].astype(F32), cos, slo, shi, 16)
        vf[rows, :] = v_ref[rows, :].astype(F32)
        return carry

    lax.fori_loop(0, seq // B_PREP_ROWS, prep, 0, unroll=2)

    for g, (window, dil) in reversed(list(enumerate(B_PATTERNS))):
        first = g == len(B_PATTERNS) - 1
        half = window // (2 * dil)
        length = seq // dil
        nblk = length // B_QBLK
        kwin = min(2 * B_QBLK, length)
        q_buf = q_bufs[g]

        def rows_of(start, size, dil=dil):
            return pl.ds(start, size) if dil == 1 else pl.ds(start, size, stride=dil)

        def block(n, carry, first=first, half=half, length=length, nblk=nblk, kwin=kwin, q_buf=q_buf, dil=dil,
                  rows_of=rows_of):
            r = n // nblk
            i = n % nblk
            k0 = jnp.clip(i * B_QBLK - half, 0, length - kwin)
            q_rows = rows_of(r + dil * B_QBLK * i, B_QBLK)
            k_rows = rows_of(r + dil * k0, kwin)
            q = q_buf[q_rows, :].astype(BF16)
            k = kf[k_rows, :].astype(BF16)
            v = vf[k_rows, :].astype(BF16)
            case = 0 if nblk == 1 else jnp.where(i == 0, 0, jnp.where(i == nblk - 1, 2, 1))
            s = _dot_nt(q, k) + bias_s[case, :, :kwin]
            m_b = jnp.max(s, axis=-1, keepdims=True)
            p = jnp.exp2(s - m_b)
            l_b = jnp.sum(p, axis=-1, keepdims=True)
            a_b = _dot(p.astype(BF16), v)
            full = (B_QBLK, HEAD_DIM)
            if first:
                m_s[q_rows, :] = jnp.broadcast_to(m_b, full)
                l_s[q_rows, :] = jnp.broadcast_to(l_b, full)
                acc_s[q_rows, :] = a_b
            else:
                m_o = m_s[q_rows, :]
                m_n = jnp.maximum(m_o, m_b)
                w_o = jnp.exp2(m_o - m_n)
                w_b = jnp.exp2(m_b - m_n)
                acc_s[q_rows, :] = acc_s[q_rows, :] * w_o + a_b * w_b
                l_s[q_rows, :] = l_s[q_rows, :] * w_o + l_b * w_b
                m_s[q_rows, :] = m_n
            return carry

        lax.fori_loop(0, seq // B_QBLK, block, 0, unroll=B_UNROLL)

    def finish(ci, carry):
        rows = pl.ds(pl.multiple_of(ci * B_PREP_ROWS, B_PREP_ROWS), B_PREP_ROWS)
        o_ref[rows, :] = (acc_s[rows, :] * (1.0 / l_s[rows, :])).astype(o_ref.dtype)
        return carry

    lax.fori_loop(0, seq // B_PREP_ROWS, finish, 0)


def _attn_b(proj, tabs, batch, seq):
    cos, slo, shi = tabs

    def q_spec(g):
        return pl.BlockSpec((seq, HEAD_DIM), lambda b, h: (b, CB_BQ + g * B_HEADS + h))

    tab_spec = pl.BlockSpec((seq, LANES), lambda b, h: (0, 0))
    slab = pltpu.VMEM((seq, HEAD_DIM), F32)
    return pl.pallas_call(
        functools.partial(_attn_b_kernel, seq=seq),
        grid=(batch, B_HEADS),
        in_specs=[
            q_spec(0), q_spec(1), q_spec(2),
            pl.BlockSpec((seq, HEAD_DIM), lambda b, h: (b, CB_BK + h)),
            pl.BlockSpec((seq, HEAD_DIM), lambda b, h: (b, CB_BV + h)),
            tab_spec, tab_spec, tab_spec,
        ],
        out_specs=pl.BlockSpec((seq, HEAD_DIM), lambda b, h: (b, h)),
        out_shape=jax.ShapeDtypeStruct((batch * seq, B_WIDTH), BF16),
        scratch_shapes=[slab] * 8 + [pltpu.VMEM((3, B_QBLK, 2 * B_QBLK), F32)],
        compiler_params=_params(("parallel", "parallel"), 48),
        name="attn_b",
    )(proj, proj, proj, proj, proj, cos, slo, shi)


def _out_kernel(ya_ref, yb_ref, yc_ref, ga_ref, gb_ref, gc_ref, w_ref, x_ref, gf_ref, x1_ref, xn_ref):
    y = jnp.concatenate([
        _rms(ya_ref[...].astype(F32), ga_ref[...]).astype(BF16),
        _rms(yb_ref[...].astype(F32), gb_ref[...]).astype(BF16),
        _rms(yc_ref[...].astype(F32), gc_ref[...]).astype(BF16),
    ], axis=1)
    x1 = x_ref[...] + _dot(y, w_ref[...])
    x1_ref[...] = x1
    xn_ref[...] = _rms(x1, gf_ref[...]).astype(xn_ref.dtype)


def _out_proj(ya, yb, yc, ga, gb, gc, w, x, gf, tm=512):
    m, d = x.shape

    def rows(width):
        return pl.BlockSpec((tm, width), lambda i: (i, 0))

    def const(r, c):
        return pl.BlockSpec((r, c), lambda i: (0, 0))

    return pl.pallas_call(
        _out_kernel,
        grid=(m // tm,),
        in_specs=[rows(A_WIDTH), rows(B_WIDTH), rows(C_WIDTH), const(1, A_WIDTH), const(1, B_WIDTH),
                  const(1, C_WIDTH), const(d, d), rows(d), const(1, d)],
        out_specs=[rows(d), rows(d)],
        out_shape=[jax.ShapeDtypeStruct((m, d), F32), jax.ShapeDtypeStruct((m, d), BF16)],
        compiler_params=_params(("parallel",), 56),
        name="out_proj",
    )(ya, yb, yc, ga, gb, gc, w, x, gf)


FFN_SUBTILE = 256
FFN_ROWS = 1024
FFN_GATE_ROWS = 32
FFN_INTERLEAVE = 8
FFN_GUARD = FFN_INTERLEAVE

def _ffn_up_kernel(xn_ref, wg_ref, wu_ref, cg_ref, cu_ref, bg_ref, bu_ref, wd_ref, o_ref, wd_bf_ref,
                   hg, hu, wg_bf, wu_bf, *, seq):
    tn = o_ref.shape[1]
    slabs = FFN_SUBTILE // LANES
    wd_bf_ref[...] = wd_ref[...].astype(BF16)

    @pl.when(pl.program_id(1) == 0)
    def _():
        wg_bf[...] = wg_ref[...].astype(BF16)
        wu_bf[...] = wu_ref[...].astype(BF16)

    zeros = jnp.zeros((FFN_GUARD, LANES), F32)
    for h in (hg, hu):
        for s in range(tn // LANES):
            h[s, 0:FFN_GUARD, :] = zeros
            h[s, FFN_GUARD + seq:2 * FFN_GUARD + seq, :] = zeros

    def matmuls(c, r):
        cols = slice(c * FFN_SUBTILE, (c + 1) * FFN_SUBTILE)
        xr = xn_ref[r * FFN_ROWS:(r + 1) * FFN_ROWS, :]
        dst = slice(FFN_GUARD + r * FFN_ROWS, FFN_GUARD + (r + 1) * FFN_ROWS)
        for h, w in ((hg, wg_bf), (hu, wu_bf)):
            res = _dot(xr, w[:, cols])
            for s in range(slabs):
                h[c * slabs + s, dst, :] = res[:, s * LANES:(s + 1) * LANES]

    def gate_stage(c, r):
        for s in range(slabs):
            slab = c * slabs + s
            lanes = slice(slab * LANES, (slab + 1) * LANES)
            for k in range(FFN_ROWS // FFN_GATE_ROWS):
                row0 = r * FFN_ROWS + k * FFN_GATE_ROWS
                base = FFN_GUARD + row0

                def tap(h, t):
                    if t == 1:
                        return h[slab, base:base + FFN_GATE_ROWS, :]
                    return h[pl.ds(slab, 1, stride=2), pl.ds(base + t - 1, FFN_GATE_ROWS), :][0]

                def conv(h, c_ref, b_ref):
                    taps = [tap(h, t) * c_ref[t:t + 1, lanes] for t in range(3)]
                    return taps[0] + taps[1] + taps[2] + b_ref[:, lanes]

                gate = conv(hg, cg_ref, bg_ref)
                up = conv(hu, cu_ref, bu_ref)
                o_ref[row0:row0 + FFN_GATE_ROWS, lanes] = (
                    gate * (1.0 / (1.0 + jnp.exp2(gate * -LOG2E))) * up).astype(o_ref.dtype)

    units = [(c, r) for c in range(tn // FFN_SUBTILE) for r in range(seq // FFN_ROWS)]
    for i, unit in enumerate(units):
        matmuls(*unit)
        if i > 0:
            gate_stage(*units[i - 1])
    gate_stage(*units[-1])


def _ffn_up(xn, w_up, w_down, layer, conv_w, conv_b, batch, seq, tn=512):
    d = xn.shape[1]
    nj = D_FF // tn
    wd_rows = w_down.shape[1] // (nj * batch)
    return pl.pallas_call(
        functools.partial(_ffn_up_kernel, seq=seq),
        grid=(nj, batch),
        in_specs=[
            pl.BlockSpec((seq, d), lambda j, b: (b, 0)),
            pl.BlockSpec((None, d, tn), lambda j, b: (layer, 0, j)),
            pl.BlockSpec((None, d, tn), lambda j, b: (layer, 0, nj + j)),
            pl.BlockSpec((3, tn), lambda j, b: (0, j)),
            pl.BlockSpec((3, tn), lambda j, b: (0, nj + j)),
            pl.BlockSpec((1, tn), lambda j, b: (0, j)),
            pl.BlockSpec((1, tn), lambda j, b: (0, nj + j)),
            pl.BlockSpec((None, wd_rows, d), lambda j, b: (layer, j * batch + b, 0)),
        ],
        out_specs=[pl.BlockSpec((seq, tn), lambda j, b: (b, j)),
                   pl.BlockSpec((wd_rows, d), lambda j, b: (j * batch + b, 0))],
        out_shape=[jax.ShapeDtypeStruct((batch * seq, D_FF), BF16),
                   jax.ShapeDtypeStruct(w_down.shape[1:], BF16)],
        scratch_shapes=[pltpu.VMEM((tn // LANES, seq + 2 * FFN_GUARD, LANES), F32)] * 2
        + [pltpu.VMEM((d, tn), BF16)] * 2,
        compiler_params=_params(("parallel", "arbitrary"), 60),
        name="ffn_up",
    )(xn, w_up, w_up, conv_w, conv_w, conv_b, conv_b, w_down)


def _ffn_down_kernel(a_ref, w_ref, x1_ref, g_ref, x2_ref, xn_ref):
    x2 = x1_ref[...] + _dot(a_ref[...], w_ref[...])
    x2_ref[...] = x2
    xn_ref[...] = _rms(x2, g_ref[...]).astype(xn_ref.dtype)


def _ffn_down_final_kernel(a_ref, w_ref, x1_ref, g_ref, xn_ref):
    x2 = x1_ref[...] + _dot(a_ref[...], w_ref[...])
    xn_ref[...] = _rms(x2, g_ref[...]).astype(xn_ref.dtype)


def _ffn_down(act, w, x1, g, final, tm=512):
    m, kdim = act.shape
    d = w.shape[1]
    tile = pl.BlockSpec((tm, d), lambda i: (i, 0))
    in_specs = [
        pl.BlockSpec((tm, kdim), lambda i: (i, 0)),
        pl.BlockSpec((kdim, d), lambda i: (0, 0), pipeline_mode=pl.Buffered(1)),
        tile,
        pl.BlockSpec((1, d), lambda i: (0, 0)),
    ]
    common = dict(grid=(m // tm,), in_specs=in_specs, compiler_params=_params(("parallel",), 60))
    if final:
        xn = pl.pallas_call(_ffn_down_final_kernel, out_specs=tile, out_shape=jax.ShapeDtypeStruct((m, d), F32),
                            name="ffn_down_final", **common)(act, w, x1, g)
        return None, xn
    return pl.pallas_call(_ffn_down_kernel, out_specs=[tile, tile],
                          out_shape=[jax.ShapeDtypeStruct((m, d), F32), jax.ShapeDtypeStruct((m, d), BF16)],
                          name="ffn_down", **common)(act, w, x1, g)


def _rope_table(pos, dim, theta):
    f32 = np.float32
    inv = f32(theta) ** (-np.arange(0, dim, 2, dtype=f32) / f32(dim))
    ang = (pos.astype(f32)[:, None] * inv[None, :]).astype(np.float64)
    return np.cos(ang).astype(f32), np.sin(ang).astype(f32)


def _lane_tables(seq):
    t = np.arange(seq)
    z = lambda w: np.zeros((seq, w), np.float32)
    one = lambda w: np.ones((seq, w), np.float32)
    cat = lambda parts: jnp.asarray(np.concatenate(parts, 1))
    cr, sr = _rope_table(t // GRID_W, HEAD_DIM // 2, A_ROPE_THETA)
    cc, sc = _rope_table(t % GRID_W, HEAD_DIM // 2, A_ROPE_THETA)
    tab_a = (cat([cr, cr, cc, cc]), cat([-sr, z(32), -sc, z(32)]), cat([z(32), sr, z(32), sc]))
    cp, sp = _rope_table(t, PARTIAL_ROPE_DIM, PARTIAL_ROPE_THETA)
    tab_b = (cat([cp, cp, one(96)]), cat([-sp, z(112)]), cat([z(16), sp, z(96)]))
    cm, sm = _rope_table(t, C_ROPE_DIM, C_ROPE_THETA)
    tab_c = (cat([cm, cm, one(64)]), cat([-sm, z(96)]), cat([z(32), sm, z(64)]))
    return tab_a, tab_b, tab_c


def _prep_w_ckr(w_in_t, layer):
    ckr = w_in_t[layer, IN_N_SRC * IN_WBLK:, :]
    return jnp.pad(ckr, ((0, IN_WBLK - ckr.shape[0]), (0, 0)))


def _prep_w_uq(w):
    w = w.reshape(C_Q_RANK, C_HEADS, C_NOPE_DIM + C_ROPE_DIM)
    w = jnp.pad(w, ((0, 0), (0, 0), (0, C_QK_PAD - C_NOPE_DIM - C_ROPE_DIM)))
    return w.reshape(C_Q_RANK, C_HEADS * C_QK_PAD).astype(BF16)


def _prep_w_ukv(w):
    w = w.reshape(C_KV_RANK, C_HEADS, C_NOPE_DIM + C_V_DIM)
    kn = w[:, :, :C_NOPE_DIM].reshape(C_KV_RANK, C_HEADS * C_NOPE_DIM)
    v = w[:, :, C_NOPE_DIM:].reshape(C_KV_RANK, C_HEADS * C_V_DIM)
    return jnp.concatenate([kn, v], axis=1).astype(BF16)


def kernel(x, attn_norm, w_in, a_q_norm, a_k_norm, c_q_norm, c_kv_norm, w_uq, w_ukv, out_norm, w_out,
           ffn_norm, w_up, conv_w, conv_b, w_down, final_norm):
    batch, seq, d = x.shape
    depth = w_in.shape[0]
    tab_a, tab_b, tab_c = _lane_tables(seq)
    row = lambda v: v.reshape(1, -1)

    xr = x.reshape(batch * seq, d)
    xn = _rmsnorm_rows(xr, row(attn_norm[0]))
    w_in_t = jnp.swapaxes(w_in, 1, 2)
    for l in range(depth):
        proj = _in_proj(xn, w_in_t, l, _prep_w_ckr(w_in_t, l))
        qc, kvc = _latent_up(proj, row(c_q_norm[l]), row(c_kv_norm[l]), _prep_w_uq(w_uq[l]),
                             _prep_w_ukv(w_ukv[l]))
        ya, w_out_bf = _attn_a(proj, tab_a, row(a_q_norm[l]), row(a_k_norm[l]), w_out, l, batch, seq)
        yb = _attn_b(proj, tab_b, batch, seq)
        yc = _attn_c(qc, kvc, proj, tab_c, batch, seq)
        g = out_norm[l]
        x1, xn1 = _out_proj(ya, yb, yc, row(g[:A_WIDTH]), row(g[A_WIDTH:A_WIDTH + B_WIDTH]),
                            row(g[A_WIDTH + B_WIDTH:]), w_out_bf, xr, row(ffn_norm[l]))
        act, w_down_bf = _ffn_up(xn1, w_up, w_down, l, conv_w[l], row(conv_b[l]), batch, seq)
        last = l == depth - 1
        g_next = final_norm if last else attn_norm[l + 1]
        xr, xn = _ffn_down(act, w_down_bf, x1, row(g_next), last)
    return xn.reshape(batch, seq, d)
```

```python
import functools
import math

import jax
import jax.numpy as jnp
import numpy as np
from jax import lax
from jax.experimental import pallas as pl
from jax.experimental.pallas import tpu as pltpu

D_MODEL = 2048
HEAD_DIM = 128
A_HEADS = 6
A_KV_HEADS = 2
A_GROUP = A_HEADS // A_KV_HEADS
A_ROPE_THETA = 10000.0
B_HEADS = 4
B_PATTERNS = ((128, 1), (512, 4), (2048, 16))
B_N_GROUPS = 3
C_HEADS = 6
C_Q_RANK = 512
C_KV_RANK = 512
C_NOPE_DIM = 128
C_ROPE_DIM = 64
C_V_DIM = 128
C_ROPE_THETA = 10000.0
PARTIAL_ROPE_DIM = HEAD_DIM // 4
PARTIAL_ROPE_THETA = 500000.0
GRID_W = 64
D_FF = 5632
EPS = 1e-6

A_WIDTH = A_HEADS * HEAD_DIM
B_WIDTH = B_HEADS * HEAD_DIM
C_WIDTH = C_HEADS * C_V_DIM
C_QK_PAD = 256

LANES = 128
LOG2E = math.log2(math.e)
F32 = jnp.float32
BF16 = jnp.bfloat16

PROJ_WIDTH = 5120
CB_CQ, CB_CKV = 0, 1
CB_AQ, CB_AK, CB_AV = 8, 14, 16
CB_BQ, CB_BK, CB_BV = 18, 30, 34
CB_CKR = 38


TILE = dict(rmsnorm=512, in_proj=2048, latent_up=2048, attn_a=1024, attn_c=2048, out_proj=512, ffn_up=512,
            ffn_down=512)
VMEM_MIB = dict(rmsnorm=40, in_proj=60, latent_up=48, attn_a=48, attn_b=48, attn_c=48, out_proj=56, ffn_up=60,
                ffn_down=60)


def _params(semantics, call):
    return pltpu.CompilerParams(dimension_semantics=semantics, vmem_limit_bytes=VMEM_MIB[call] * 1024 * 1024)


def _rms(xf, g):
    return xf * lax.rsqrt(jnp.mean(xf * xf, axis=-1, keepdims=True) + EPS) * g


def _rope(xf, cos, sin_lo, sin_hi, shift):
    return xf * cos + pltpu.roll(xf, LANES - shift, 1) * sin_lo + pltpu.roll(xf, shift, 1) * sin_hi


def _dot(a, b):
    return jnp.dot(a, b, preferred_element_type=F32)


def _dot_nt(a, b):
    return lax.dot_general(a, b, (((1,), (1,)), ((), ())), preferred_element_type=F32)


def _norm_kernel(x_ref, g_ref, o_ref):
    o_ref[...] = _rms(x_ref[...], g_ref[...]).astype(o_ref.dtype)


def _rmsnorm_rows(x, g, tm=TILE["rmsnorm"]):
    m, d = x.shape
    return pl.pallas_call(
        _norm_kernel,
        grid=(m // tm,),
        in_specs=[pl.BlockSpec((tm, d), lambda i: (i, 0)), pl.BlockSpec((1, d), lambda i: (0, 0))],
        out_specs=pl.BlockSpec((tm, d), lambda i: (i, 0)),
        out_shape=jax.ShapeDtypeStruct((m, d), BF16),
        compiler_params=_params(("parallel",), "rmsnorm"),
        name="rmsnorm",
    )(x, g)


IN_WBLK = 256
IN_TN = 1024
IN_SRC_CQ = 15
IN_N_SRC = 19


def _in_proj_kernel(a_ref, *refs):
    w_refs, wck_ref, o_ref, w_bf = refs[:-3], refs[-3], refs[-2], refs[-1]
    j = pl.program_id(0)
    last = pl.num_programs(0) - 1

    @pl.when(pl.program_id(1) == 0)
    def _():
        for q, w_ref in enumerate(w_refs[:-1]):
            w_bf[q * IN_WBLK:(q + 1) * IN_WBLK, :] = w_ref[...].astype(BF16)
        tail = slice((len(w_refs) - 1) * IN_WBLK, len(w_refs) * IN_WBLK)

        @pl.when(j < last)
        def _():
            w_bf[tail, :] = w_refs[-1][...].astype(BF16)

        @pl.when(j == last)
        def _():
            w_bf[tail, :] = wck_ref[...].astype(BF16)

    o_ref[...] = _dot_nt(a_ref[...], w_bf[...]).astype(o_ref.dtype)


def _in_proj(a, w_in_t, layer, w_ckr_t, tm=TILE["in_proj"]):
    m, k = a.shape
    per_step = IN_TN // IN_WBLK

    def w_spec(q):
        def index(j, i):
            n = j * per_step + q
            return layer, jnp.where(n < 4, n + IN_SRC_CQ, n - 4), 0
        return pl.BlockSpec((None, IN_WBLK, k), index)

    return pl.pallas_call(
        _in_proj_kernel,
        grid=(PROJ_WIDTH // IN_TN, m // tm),
        in_specs=[pl.BlockSpec((tm, k), lambda j, i: (i, 0))] + [w_spec(q) for q in range(per_step)]
        + [pl.BlockSpec((IN_WBLK, k), lambda j, i: (0, 0))],
        out_specs=pl.BlockSpec((tm, IN_TN), lambda j, i: (i, j)),
        out_shape=jax.ShapeDtypeStruct((m, PROJ_WIDTH), BF16),
        scratch_shapes=[pltpu.VMEM((IN_TN, k), BF16)],
        compiler_params=_params(("parallel", "arbitrary"), "in_proj"),
        name="in_proj",
    )(a, *([w_in_t] * per_step), w_ckr_t)


def _latent_up_kernel(cq_ref, ckv_ref, gq_ref, gkv_ref, wq_ref, wkv_ref, q_ref, kv_ref, *, q_scale):
    cq = _rms(cq_ref[...].astype(F32), gq_ref[...]).astype(BF16)
    q_ref[...] = (_dot(cq, wq_ref[...]) * q_scale).astype(q_ref.dtype)
    ckv = _rms(ckv_ref[...].astype(F32), gkv_ref[...]).astype(BF16)
    kv_ref[...] = _dot(ckv, wkv_ref[...]).astype(kv_ref.dtype)


def _latent_up(proj, gq, gkv, wq, wkv, tm=TILE["latent_up"]):
    m = proj.shape[0]
    nq, nkv = wq.shape[1], wkv.shape[1]
    q_scale = (C_NOPE_DIM + C_ROPE_DIM) ** -0.5 * LOG2E
    return pl.pallas_call(
        functools.partial(_latent_up_kernel, q_scale=q_scale),
        grid=(m // tm,),
        in_specs=[
            pl.BlockSpec((tm, C_Q_RANK), lambda i: (i, CB_CQ)),
            pl.BlockSpec((tm, C_KV_RANK), lambda i: (i, CB_CKV)),
            pl.BlockSpec((1, C_Q_RANK), lambda i: (0, 0)),
            pl.BlockSpec((1, C_KV_RANK), lambda i: (0, 0)),
            pl.BlockSpec((C_Q_RANK, nq), lambda i: (0, 0)),
            pl.BlockSpec((C_KV_RANK, nkv), lambda i: (0, 0)),
        ],
        out_specs=[pl.BlockSpec((tm, nq), lambda i: (i, 0)), pl.BlockSpec((tm, nkv), lambda i: (i, 0))],
        out_shape=[jax.ShapeDtypeStruct((m, nq), BF16), jax.ShapeDtypeStruct((m, nkv), BF16)],
        compiler_params=_params(("parallel",), "latent_up"),
        name="latent_up",
    )(proj, proj, gq, gkv, wq, wkv)


A_ROW_CHUNK = 128
C_ROW_CHUNK = 256


def _attend(q_all, k_ref, v1_ref, dv, chunk):
    outs = []
    for c in range(q_all.shape[0] // chunk):
        q = q_all[c * chunk:(c + 1) * chunk]
        s = _dot_nt(q, k_ref[...])
        p = jnp.exp2(s - jnp.max(s, axis=-1, keepdims=True))
        ol = _dot(p.astype(BF16), v1_ref[...])
        outs.append(ol[:, :dv] * (1.0 / ol[:, dv:]))
    return outs


def _attn_a_kernel(q0_ref, q1_ref, q2_ref, k_ref, v_ref, cos_ref, slo_ref, shi_ref, gq_ref, gk_ref, wo_ref,
                   o_ref, wo_bf_ref, kbuf, v1buf, *, tq):
    qi = pl.program_id(2)
    wo_bf_ref[...] = wo_ref[...].astype(BF16)

    @pl.when(qi == 0)
    def _():
        k = _rms(k_ref[...].astype(F32), gk_ref[...])
        kbuf[...] = _rope(k, cos_ref[...], slo_ref[...], shi_ref[...], 32).astype(BF16)
        v1buf[:, :HEAD_DIM] = v_ref[...]
        v1buf[:, HEAD_DIM:] = jnp.ones((v1buf.shape[0], HEAD_DIM), BF16)

    rows = pl.ds(pl.multiple_of(qi * tq, tq), tq)
    cos, slo, shi = cos_ref[rows, :], slo_ref[rows, :], shi_ref[rows, :]
    q_scale = HEAD_DIM ** -0.5 * LOG2E
    qs = []
    for q_ref in (q0_ref, q1_ref, q2_ref):
        q = _rms(q_ref[...].astype(F32), gq_ref[...])
        qs.append((_rope(q, cos, slo, shi, 32) * q_scale).astype(BF16))
    q_all = jnp.concatenate(qs, axis=0)
    per_head = tq // A_ROW_CHUNK
    for c, o in enumerate(_attend(q_all, kbuf, v1buf, HEAD_DIM, A_ROW_CHUNK)):
        g, r = divmod(c, per_head)
        o_ref[r * A_ROW_CHUNK:(r + 1) * A_ROW_CHUNK, g * HEAD_DIM:(g + 1) * HEAD_DIM] = o.astype(o_ref.dtype)


def _attn_a(proj, tabs, gq, gk, w_out, layer, batch, seq, tq=TILE["attn_a"]):
    nq = seq // tq
    cos, slo, shi = tabs
    d = w_out.shape[2]
    wo_rows = w_out.shape[1] // (batch * A_KV_HEADS * nq)

    def wo_index(b, h, qi):
        return (b * A_KV_HEADS + h) * nq + qi

    def q_spec(g):
        return pl.BlockSpec((tq, HEAD_DIM), lambda b, h, qi: (b * nq + qi, CB_AQ + h * A_GROUP + g))

    tab_spec = pl.BlockSpec((seq, LANES), lambda b, h, qi: (0, 0))
    gain_spec = pl.BlockSpec((1, HEAD_DIM), lambda b, h, qi: (0, 0))
    return pl.pallas_call(
        functools.partial(_attn_a_kernel, tq=tq),
        grid=(batch, A_KV_HEADS, nq),
        in_specs=[
            q_spec(0), q_spec(1), q_spec(2),
            pl.BlockSpec((seq, HEAD_DIM), lambda b, h, qi: (b, CB_AK + h)),
            pl.BlockSpec((seq, HEAD_DIM), lambda b, h, qi: (b, CB_AV + h)),
            tab_spec, tab_spec, tab_spec, gain_spec, gain_spec,
            pl.BlockSpec((None, wo_rows, d), lambda b, h, qi: (layer, wo_index(b, h, qi), 0)),
        ],
        out_specs=[pl.BlockSpec((tq, A_GROUP * HEAD_DIM), lambda b, h, qi: (b * nq + qi, h)),
                   pl.BlockSpec((wo_rows, d), lambda b, h, qi: (wo_index(b, h, qi), 0))],
        out_shape=[jax.ShapeDtypeStruct((batch * seq, A_WIDTH), BF16),
                   jax.ShapeDtypeStruct(w_out.shape[1:], BF16)],
        scratch_shapes=[pltpu.VMEM((seq, HEAD_DIM), BF16), pltpu.VMEM((seq, 2 * HEAD_DIM), BF16)],
        compiler_params=_params(("parallel", "parallel", "arbitrary"), "attn_a"),
        name="attn_a",
    )(proj, proj, proj, proj, proj, cos, slo, shi, gq, gk, w_out)


def _attn_c_kernel(q_ref, kn_ref, v_ref, kr_ref, cos_ref, slo_ref, shi_ref, o_ref, kbuf, v1buf, *, tq):
    qi = pl.program_id(2)

    @pl.when(qi == 0)
    def _():
        kbuf[:, :C_NOPE_DIM] = kn_ref[...]
        kr = _rope(kr_ref[...].astype(F32), cos_ref[...], slo_ref[...], shi_ref[...], 32)
        kbuf[:, C_NOPE_DIM:] = kr.astype(BF16)
        v1buf[:, :C_V_DIM] = v_ref[...]
        v1buf[:, C_V_DIM:] = jnp.ones((v1buf.shape[0], C_V_DIM), BF16)

    rows = pl.ds(pl.multiple_of(qi * tq, tq), tq)
    q_rope = _rope(q_ref[:, C_NOPE_DIM:].astype(F32), cos_ref[rows, :], slo_ref[rows, :], shi_ref[rows, :], 32)
    q_all = jnp.concatenate([q_ref[:, :C_NOPE_DIM], q_rope.astype(BF16)], axis=1)
    for c, o in enumerate(_attend(q_all, kbuf, v1buf, C_V_DIM, C_ROW_CHUNK)):
        o_ref[c * C_ROW_CHUNK:(c + 1) * C_ROW_CHUNK, :] = o.astype(o_ref.dtype)


def _attn_c(qc, kvc, proj, tabs, batch, seq, tq=TILE["attn_c"]):
    nq = seq // tq
    cos, slo, shi = tabs
    tab_spec = pl.BlockSpec((seq, LANES), lambda b, h, qi: (0, 0))
    return pl.pallas_call(
        functools.partial(_attn_c_kernel, tq=tq),
        grid=(batch, C_HEADS, nq),
        in_specs=[
            pl.BlockSpec((tq, C_QK_PAD), lambda b, h, qi: (b * nq + qi, h)),
            pl.BlockSpec((seq, C_NOPE_DIM), lambda b, h, qi: (b, h)),
            pl.BlockSpec((seq, C_V_DIM), lambda b, h, qi: (b, C_HEADS + h)),
            pl.BlockSpec((seq, LANES), lambda b, h, qi: (b, CB_CKR)),
            tab_spec, tab_spec, tab_spec,
        ],
        out_specs=pl.BlockSpec((tq, C_V_DIM), lambda b, h, qi: (b * nq + qi, h)),
        out_shape=jax.ShapeDtypeStruct((batch * seq, C_WIDTH), BF16),
        scratch_shapes=[pltpu.VMEM((seq, C_QK_PAD), BF16), pltpu.VMEM((seq, 2 * C_V_DIM), BF16)],
        compiler_params=_params(("parallel", "parallel", "arbitrary"), "attn_c"),
        name="attn_c",
    )(qc, kvc, kvc, proj, cos, slo, shi)


B_QBLK = 128
(B_HALF,) = {w // (2 * d) for w, d in B_PATTERNS}
assert 2 * B_HALF == B_QBLK
B_PREP_ROWS = 256
B_UNROLL = 16


def _attn_b_kernel(q0_ref, q1_ref, q2_ref, k_ref, v_ref, cos_ref, slo_ref, shi_ref, o_ref,
                   q0f, q1f, q2f, kf, vf, m_s, l_s, acc_s, bias_s, *, seq):
    q_refs = (q0_ref, q1_ref, q2_ref)
    q_bufs = (q0f, q1f, q2f)
    q_scale = HEAD_DIM ** -0.5 * LOG2E

    half = B_HALF
    shape = (B_QBLK, 2 * B_QBLK)
    rel = lax.broadcasted_iota(jnp.int32, shape, 1) - lax.broadcasted_iota(jnp.int32, shape, 0)
    for case, offset in enumerate((0, -half, -B_QBLK)):
        bias_s[case] = jnp.where(jnp.abs(rel + offset) <= half, 0.0, -jnp.inf)

    def prep(ci, carry):
        rows = pl.ds(pl.multiple_of(ci * B_PREP_ROWS, B_PREP_ROWS), B_PREP_ROWS)
        cos, slo, shi = cos_ref[rows, :], slo_ref[rows, :], shi_ref[rows, :]
        for q_ref, q_buf in zip(q_refs, q_bufs):
            q_buf[rows, :] = _rope(q_ref[rows, :].astype(F32), cos, slo, shi, 16) * q_scale
        kf[rows, :] = _rope(k_ref[rows, :].astype(F32), cos, slo, shi, 16)
        vf[rows, :] = v_ref[rows, :].astype(F32)
        return carry

    lax.fori_loop(0, seq // B_PREP_ROWS, prep, 0, unroll=2)

    for g, (window, dil) in reversed(list(enumerate(B_PATTERNS))):
        first = g == len(B_PATTERNS) - 1
        half = window // (2 * dil)
        length = seq // dil
        nblk = length // B_QBLK
        kwin = min(2 * B_QBLK, length)
        q_buf = q_bufs[g]

        def rows_of(start, size, dil=dil):
            return pl.ds(start, size) if dil == 1 else pl.ds(start, size, stride=dil)

        def block(n, carry, first=first, half=half, length=length, nblk=nblk, kwin=kwin, q_buf=q_buf, dil=dil,
                  rows_of=rows_of):
            r = n // nblk
            i = n % nblk
            k0 = jnp.clip(i * B_QBLK - half, 0, length - kwin)
            q_rows = rows_of(r + dil * B_QBLK * i, B_QBLK)
            k_rows = rows_of(r + dil * k0, kwin)
            q = q_buf[q_rows, :].astype(BF16)
            k = kf[k_rows, :].astype(BF16)
            v = vf[k_rows, :].astype(BF16)
            case = 0 if nblk == 1 else jnp.where(i == 0, 0, jnp.where(i == nblk - 1, 2, 1))
            s = _dot_nt(q, k) + bias_s[case, :, :kwin]
            m_b = jnp.max(s, axis=-1, keepdims=True)
            p = jnp.exp2(s - m_b)
            l_b = jnp.sum(p, axis=-1, keepdims=True)
            a_b = _dot(p.astype(BF16), v)
            full = (B_QBLK, HEAD_DIM)
            if first:
                m_s[q_rows, :] = jnp.broadcast_to(m_b, full)
                l_s[q_rows, :] = jnp.broadcast_to(l_b, full)
                acc_s[q_rows, :] = a_b
            else:
                m_o = m_s[q_rows, :]
                m_n = jnp.maximum(m_o, m_b)
                w_o = jnp.exp2(m_o - m_n)
                w_b = jnp.exp2(m_b - m_n)
                acc_s[q_rows, :] = acc_s[q_rows, :] * w_o + a_b * w_b
                l_s[q_rows, :] = l_s[q_rows, :] * w_o + l_b * w_b
                m_s[q_rows, :] = m_n
            return carry

        lax.fori_loop(0, seq // B_QBLK, block, 0, unroll=B_UNROLL)

    def finish(ci, carry):
        rows = pl.ds(pl.multiple_of(ci * B_PREP_ROWS, B_PREP_ROWS), B_PREP_ROWS)
        o_ref[rows, :] = (acc_s[rows, :] * (1.0 / l_s[rows, :])).astype(o_ref.dtype)
        return carry

    lax.fori_loop(0, seq // B_PREP_ROWS, finish, 0)


def _attn_b(proj, tabs, batch, seq):
    cos, slo, shi = tabs

    def q_spec(g):
        return pl.BlockSpec((seq, HEAD_DIM), lambda b, h: (b, CB_BQ + g * B_HEADS + h))

    tab_spec = pl.BlockSpec((seq, LANES), lambda b, h: (0, 0))
    slab = pltpu.VMEM((seq, HEAD_DIM), F32)
    return pl.pallas_call(
        functools.partial(_attn_b_kernel, seq=seq),
        grid=(batch, B_HEADS),
        in_specs=[
            q_spec(0), q_spec(1), q_spec(2),
            pl.BlockSpec((seq, HEAD_DIM), lambda b, h: (b, CB_BK + h)),
            pl.BlockSpec((seq, HEAD_DIM), lambda b, h: (b, CB_BV + h)),
            tab_spec, tab_spec, tab_spec,
        ],
        out_specs=pl.BlockSpec((seq, HEAD_DIM), lambda b, h: (b, h)),
        out_shape=jax.ShapeDtypeStruct((batch * seq, B_WIDTH), BF16),
        scratch_shapes=[slab] * 8 + [pltpu.VMEM((3, B_QBLK, 2 * B_QBLK), F32)],
        compiler_params=_params(("parallel", "parallel"), "attn_b"),
        name="attn_b",
    )(proj, proj, proj, proj, proj, cos, slo, shi)


def _out_kernel(ya_ref, yb_ref, yc_ref, ga_ref, gb_ref, gc_ref, w_ref, x_ref, gf_ref, x1_ref, xn_ref):
    y = jnp.concatenate([
        _rms(ya_ref[...].astype(F32), ga_ref[...]).astype(BF16),
        _rms(yb_ref[...].astype(F32), gb_ref[...]).astype(BF16),
        _rms(yc_ref[...].astype(F32), gc_ref[...]).astype(BF16),
    ], axis=1)
    x1 = x_ref[...] + _dot(y, w_ref[...])
    x1_ref[...] = x1
    xn_ref[...] = _rms(x1, gf_ref[...]).astype(xn_ref.dtype)


def _out_proj(ya, yb, yc, ga, gb, gc, w, x, gf, tm=TILE["out_proj"]):
    m, d = x.shape

    def rows(width):
        return pl.BlockSpec((tm, width), lambda i: (i, 0))

    def const(r, c):
        return pl.BlockSpec((r, c), lambda i: (0, 0))

    return pl.pallas_call(
        _out_kernel,
        grid=(m // tm,),
        in_specs=[rows(A_WIDTH), rows(B_WIDTH), rows(C_WIDTH), const(1, A_WIDTH), const(1, B_WIDTH),
                  const(1, C_WIDTH), const(d, d), rows(d), const(1, d)],
        out_specs=[rows(d), rows(d)],
        out_shape=[jax.ShapeDtypeStruct((m, d), F32), jax.ShapeDtypeStruct((m, d), BF16)],
        compiler_params=_params(("parallel",), "out_proj"),
        name="out_proj",
    )(ya, yb, yc, ga, gb, gc, w, x, gf)


FFN_SUBTILE = 256
FFN_ROWS = 1024
FFN_GATE_ROWS = 32
FFN_GUARD = 8

def _ffn_up_kernel(xn_ref, wg_ref, wu_ref, cg_ref, cu_ref, bg_ref, bu_ref, wd_ref, o_ref, wd_bf_ref,
                   hg, hu, wg_bf, wu_bf, *, seq):
    tn = o_ref.shape[1]
    slabs = FFN_SUBTILE // LANES
    wd_bf_ref[...] = wd_ref[...].astype(BF16)

    @pl.when(pl.program_id(1) == 0)
    def _():
        wg_bf[...] = wg_ref[...].astype(BF16)
        wu_bf[...] = wu_ref[...].astype(BF16)

    zeros = jnp.zeros((FFN_GUARD, LANES), F32)
    for h in (hg, hu):
        for s in range(tn // LANES):
            h[s, 0:FFN_GUARD, :] = zeros
            h[s, FFN_GUARD + seq:2 * FFN_GUARD + seq, :] = zeros

    def matmuls(c, r):
        cols = slice(c * FFN_SUBTILE, (c + 1) * FFN_SUBTILE)
        xr = xn_ref[r * FFN_ROWS:(r + 1) * FFN_ROWS, :]
        dst = slice(FFN_GUARD + r * FFN_ROWS, FFN_GUARD + (r + 1) * FFN_ROWS)
        for h, w in ((hg, wg_bf), (hu, wu_bf)):
            res = _dot(xr, w[:, cols])
            for s in range(slabs):
                h[c * slabs + s, dst, :] = res[:, s * LANES:(s + 1) * LANES]

    def gate_stage(c, r):
        for s in range(slabs):
            slab = c * slabs + s
            lanes = slice(slab * LANES, (slab + 1) * LANES)
            for k in range(FFN_ROWS // FFN_GATE_ROWS):
                row0 = r * FFN_ROWS + k * FFN_GATE_ROWS
                base = FFN_GUARD + row0

                def tap(h, t):
                    if t == 1:
                        return h[slab, base:base + FFN_GATE_ROWS, :]
                    return h[pl.ds(slab, 1, stride=2), pl.ds(base + t - 1, FFN_GATE_ROWS), :][0]

                def conv(h, c_ref, b_ref):
                    taps = [tap(h, t) * c_ref[t:t + 1, lanes] for t in range(3)]
                    return taps[0] + taps[1] + taps[2] + b_ref[:, lanes]

                gate = conv(hg, cg_ref, bg_ref)
                up = conv(hu, cu_ref, bu_ref)
                o_ref[row0:row0 + FFN_GATE_ROWS, lanes] = (
                    gate * (1.0 / (1.0 + jnp.exp2(gate * -LOG2E))) * up).astype(o_ref.dtype)

    units = [(c, r) for c in range(tn // FFN_SUBTILE) for r in range(seq // FFN_ROWS)]
    for i, unit in enumerate(units):
        matmuls(*unit)
        if i > 0:
            gate_stage(*units[i - 1])
    gate_stage(*units[-1])


def _ffn_up(xn, w_up, w_down, layer, conv_w, conv_b, batch, seq, tn=TILE["ffn_up"]):
    d = xn.shape[1]
    nj = D_FF // tn
    wd_rows = w_down.shape[1] // (nj * batch)
    return pl.pallas_call(
        functools.partial(_ffn_up_kernel, seq=seq),
        grid=(nj, batch),
        in_specs=[
            pl.BlockSpec((seq, d), lambda j, b: (b, 0)),
            pl.BlockSpec((None, d, tn), lambda j, b: (layer, 0, j)),
            pl.BlockSpec((None, d, tn), lambda j, b: (layer, 0, nj + j)),
            pl.BlockSpec((3, tn), lambda j, b: (0, j)),
            pl.BlockSpec((3, tn), lambda j, b: (0, nj + j)),
            pl.BlockSpec((1, tn), lambda j, b: (0, j)),
            pl.BlockSpec((1, tn), lambda j, b: (0, nj + j)),
            pl.BlockSpec((None, wd_rows, d), lambda j, b: (layer, j * batch + b, 0)),
        ],
        out_specs=[pl.BlockSpec((seq, tn), lambda j, b: (b, j)),
                   pl.BlockSpec((wd_rows, d), lambda j, b: (j * batch + b, 0))],
        out_shape=[jax.ShapeDtypeStruct((batch * seq, D_FF), BF16),
                   jax.ShapeDtypeStruct(w_down.shape[1:], BF16)],
        scratch_shapes=[pltpu.VMEM((tn // LANES, seq + 2 * FFN_GUARD, LANES), F32)] * 2
        + [pltpu.VMEM((d, tn), BF16)] * 2,
        compiler_params=_params(("parallel", "arbitrary"), "ffn_up"),
        name="ffn_up",
    )(xn, w_up, w_up, conv_w, conv_w, conv_b, conv_b, w_down)


def _ffn_down_kernel(a_ref, w_ref, x1_ref, g_ref, x2_ref, xn_ref):
    x2 = x1_ref[...] + _dot(a_ref[...], w_ref[...])
    x2_ref[...] = x2
    xn_ref[...] = _rms(x2, g_ref[...]).astype(xn_ref.dtype)


def _ffn_down_final_kernel(a_ref, w_ref, x1_ref, g_ref, xn_ref):
    x2 = x1_ref[...] + _dot(a_ref[...], w_ref[...])
    xn_ref[...] = _rms(x2, g_ref[...]).astype(xn_ref.dtype)


def _ffn_down(act, w, x1, g, final, tm=TILE["ffn_down"]):
    m, kdim = act.shape
    d = w.shape[1]
    tile = pl.BlockSpec((tm, d), lambda i: (i, 0))
    in_specs = [
        pl.BlockSpec((tm, kdim), lambda i: (i, 0)),
        pl.BlockSpec((kdim, d), lambda i: (0, 0), pipeline_mode=pl.Buffered(1)),
        tile,
        pl.BlockSpec((1, d), lambda i: (0, 0)),
    ]
    common = dict(grid=(m // tm,), in_specs=in_specs, compiler_params=_params(("parallel",), "ffn_down"))
    if final:
        xn = pl.pallas_call(_ffn_down_final_kernel, out_specs=tile, out_shape=jax.ShapeDtypeStruct((m, d), F32),
                            name="ffn_down_final", **common)(act, w, x1, g)
        return None, xn
    return pl.pallas_call(_ffn_down_kernel, out_specs=[tile, tile],
                          out_shape=[jax.ShapeDtypeStruct((m, d), F32), jax.ShapeDtypeStruct((m, d), BF16)],
                          name="ffn_down", **common)(act, w, x1, g)


def _rope_table(pos, dim, theta):
    f32 = np.float32
    inv = f32(theta) ** (-np.arange(0, dim, 2, dtype=f32) / f32(dim))
    ang = (pos.astype(f32)[:, None] * inv[None, :]).astype(np.float64)
    return np.cos(ang).astype(f32), np.sin(ang).astype(f32)


def _lane_tables(seq):
    t = np.arange(seq)
    z = lambda w: np.zeros((seq, w), np.float32)
    one = lambda w: np.ones((seq, w), np.float32)
    cat = lambda parts: jnp.asarray(np.concatenate(parts, 1))
    cr, sr = _rope_table(t // GRID_W, HEAD_DIM // 2, A_ROPE_THETA)
    cc, sc = _rope_table(t % GRID_W, HEAD_DIM // 2, A_ROPE_THETA)
    tab_a = (cat([cr, cr, cc, cc]), cat([-sr, z(32), -sc, z(32)]), cat([z(32), sr, z(32), sc]))
    cp, sp = _rope_table(t, PARTIAL_ROPE_DIM, PARTIAL_ROPE_THETA)
    tab_b = (cat([cp, cp, one(96)]), cat([-sp, z(112)]), cat([z(16), sp, z(96)]))
    cm, sm = _rope_table(t, C_ROPE_DIM, C_ROPE_THETA)
    tab_c = (cat([cm, cm, one(64)]), cat([-sm, z(96)]), cat([z(32), sm, z(64)]))
    return tab_a, tab_b, tab_c


def _prep_w_ckr(w_in_t, layer):
    ckr = w_in_t[layer, IN_N_SRC * IN_WBLK:, :]
    return jnp.pad(ckr, ((0, IN_WBLK - ckr.shape[0]), (0, 0)))


def _prep_w_uq(w):
    w = w.reshape(C_Q_RANK, C_HEADS, C_NOPE_DIM + C_ROPE_DIM)
    w = jnp.pad(w, ((0, 0), (0, 0), (0, C_QK_PAD - C_NOPE_DIM - C_ROPE_DIM)))
    return w.reshape(C_Q_RANK, C_HEADS * C_QK_PAD).astype(BF16)


def _prep_w_ukv(w):
    w = w.reshape(C_KV_RANK, C_HEADS, C_NOPE_DIM + C_V_DIM)
    kn = w[:, :, :C_NOPE_DIM].reshape(C_KV_RANK, C_HEADS * C_NOPE_DIM)
    v = w[:, :, C_NOPE_DIM:].reshape(C_KV_RANK, C_HEADS * C_V_DIM)
    return jnp.concatenate([kn, v], axis=1).astype(BF16)


def kernel(x, attn_norm, w_in, a_q_norm, a_k_norm, c_q_norm, c_kv_norm, w_uq, w_ukv, out_norm, w_out,
           ffn_norm, w_up, conv_w, conv_b, w_down, final_norm):
    batch, seq, d = x.shape
    depth = w_in.shape[0]
    tab_a, tab_b, tab_c = _lane_tables(seq)
    row = lambda v: v.reshape(1, -1)

    xr = x.reshape(batch * seq, d)
    xn = _rmsnorm_rows(xr, row(attn_norm[0]))
    w_in_t = jnp.swapaxes(w_in, 1, 2)
    for l in range(depth):
        proj = _in_proj(xn, w_in_t, l, _prep_w_ckr(w_in_t, l))
        qc, kvc = _latent_up(proj, row(c_q_norm[l]), row(c_kv_norm[l]), _prep_w_uq(w_uq[l]),
                             _prep_w_ukv(w_ukv[l]))
        ya, w_out_bf = _attn_a(proj, tab_a, row(a_q_norm[l]), row(a_k_norm[l]), w_out, l, batch, seq)
        yb = _attn_b(proj, tab_b, batch, seq)
        yc = _attn_c(qc, kvc, proj, tab_c, batch, seq)
        g = out_norm[l]
        x1, xn1 = _out_proj(ya, yb, yc, row(g[:A_WIDTH]), row(g[A_WIDTH:A_WIDTH + B_WIDTH]),
                            row(g[A_WIDTH + B_WIDTH:]), w_out_bf, xr, row(ffn_norm[l]))
        act, w_down_bf = _ffn_up(xn1, w_up, w_down, l, conv_w[l], row(conv_b[l]), batch, seq)
        last = l == depth - 1
        g_next = final_norm if last else attn_norm[l + 1]
        xr, xn = _ffn_down(act, w_down_bf, x1, row(g_next), last)
    return xn.reshape(batch, seq, d)
```

```python
import functools
import math

import jax
import jax.numpy as jnp
import numpy as np
from jax import lax
from jax.experimental import pallas as pl
from jax.experimental.pallas import tpu as pltpu

D_MODEL = 2048
HEAD_DIM = 128
A_HEADS = 6
A_KV_HEADS = 2
A_GROUP = A_HEADS // A_KV_HEADS
A_ROPE_THETA = 10000.0
B_HEADS = 4
B_PATTERNS = ((128, 1), (512, 4), (2048, 16))
B_N_GROUPS = 3
C_HEADS = 6
C_Q_RANK = 512
C_KV_RANK = 512
C_NOPE_DIM = 128
C_ROPE_DIM = 64
C_V_DIM = 128
C_ROPE_THETA = 10000.0
PARTIAL_ROPE_DIM = HEAD_DIM // 4
PARTIAL_ROPE_THETA = 500000.0
GRID_W = 64
D_FF = 5632
EPS = 1e-6

A_WIDTH = A_HEADS * HEAD_DIM
B_WIDTH = B_HEADS * HEAD_DIM
C_WIDTH = C_HEADS * C_V_DIM
C_QK_PAD = 256

LANES = 128
LOG2E = math.log2(math.e)
F32 = jnp.float32
BF16 = jnp.bfloat16

PROJ_WIDTH = 5120
CB_CQ, CB_CKV = 0, 1
CB_AQ, CB_AK, CB_AV = 8, 14, 16
CB_BQ, CB_BK, CB_BV = 18, 30, 34
CB_CKR = 38


TILE = dict(rmsnorm=512, in_proj=2048, latent_up=2048, attn_a=1024, attn_c=2048, out_proj=512, ffn_up=512,
            ffn_down=512)
VMEM_MIB = dict(rmsnorm=40, in_proj=60, latent_up=48, attn_a=48, attn_b=48, attn_c=48, out_proj=56, ffn_up=60,
                ffn_down=60)


def _params(semantics, call):
    return pltpu.CompilerParams(dimension_semantics=semantics, vmem_limit_bytes=VMEM_MIB[call] * 1024 * 1024)


def _rms(xf, g):
    return xf * lax.rsqrt(jnp.mean(xf * xf, axis=-1, keepdims=True) + EPS) * g


def _rope(xf, cos, sin_lo, sin_hi, shift):
    return xf * cos + pltpu.roll(xf, LANES - shift, 1) * sin_lo + pltpu.roll(xf, shift, 1) * sin_hi


def _dot(a, b):
    return jnp.dot(a, b, preferred_element_type=F32)


def _dot_nt(a, b):
    return lax.dot_general(a, b, (((1,), (1,)), ((), ())), preferred_element_type=F32)


def _norm_kernel(x_ref, g_ref, o_ref):
    o_ref[...] = _rms(x_ref[...], g_ref[...]).astype(o_ref.dtype)


def _rmsnorm_rows(x, g, tm=TILE["rmsnorm"]):
    m, d = x.shape
    return pl.pallas_call(
        _norm_kernel,
        grid=(m // tm,),
        in_specs=[pl.BlockSpec((tm, d), lambda i: (i, 0)), pl.BlockSpec((1, d), lambda i: (0, 0))],
        out_specs=pl.BlockSpec((tm, d), lambda i: (i, 0)),
        out_shape=jax.ShapeDtypeStruct((m, d), BF16),
        compiler_params=_params(("parallel",), "rmsnorm"),
        name="rmsnorm",
    )(x, g)


IN_WBLK = 256
IN_TN = 1024
IN_SRC_CQ = 15
IN_N_SRC = 19


def _in_proj_kernel(a_ref, *refs):
    w_refs, wck_ref, o_ref, w_bf = refs[:-3], refs[-3], refs[-2], refs[-1]
    j = pl.program_id(0)
    last = pl.num_programs(0) - 1

    @pl.when(pl.program_id(1) == 0)
    def _():
        for q, w_ref in enumerate(w_refs[:-1]):
            w_bf[q * IN_WBLK:(q + 1) * IN_WBLK, :] = w_ref[...].astype(BF16)
        tail = slice((len(w_refs) - 1) * IN_WBLK, len(w_refs) * IN_WBLK)

        @pl.when(j < last)
        def _():
            w_bf[tail, :] = w_refs[-1][...].astype(BF16)

        @pl.when(j == last)
        def _():
            w_bf[tail, :] = wck_ref[...].astype(BF16)

    o_ref[...] = _dot_nt(a_ref[...], w_bf[...]).astype(o_ref.dtype)


def _in_proj(a, w_in_t, layer, w_ckr_t, tm=TILE["in_proj"]):
    m, k = a.shape
    per_step = IN_TN // IN_WBLK

    def w_spec(q):
        def index(j, i):
            n = j * per_step + q
            return layer, jnp.where(n < 4, n + IN_SRC_CQ, n - 4), 0
        return pl.BlockSpec((None, IN_WBLK, k), index)

    return pl.pallas_call(
        _in_proj_kernel,
        grid=(PROJ_WIDTH // IN_TN, m // tm),
        in_specs=[pl.BlockSpec((tm, k), lambda j, i: (i, 0))] + [w_spec(q) for q in range(per_step)]
        + [pl.BlockSpec((IN_WBLK, k), lambda j, i: (0, 0))],
        out_specs=pl.BlockSpec((tm, IN_TN), lambda j, i: (i, j)),
        out_shape=jax.ShapeDtypeStruct((m, PROJ_WIDTH), BF16),
        scratch_shapes=[pltpu.VMEM((IN_TN, k), BF16)],
        compiler_params=_params(("parallel", "arbitrary"), "in_proj"),
        name="in_proj",
    )(a, *([w_in_t] * per_step), w_ckr_t)


def _latent_up_kernel(cq_ref, ckv_ref, gq_ref, gkv_ref, wq_ref, wkv_ref, q_ref, kv_ref, *, q_scale):
    cq = _rms(cq_ref[...].astype(F32), gq_ref[...]).astype(BF16)
    q_ref[...] = (_dot(cq, wq_ref[...]) * q_scale).astype(q_ref.dtype)
    ckv = _rms(ckv_ref[...].astype(F32), gkv_ref[...]).astype(BF16)
    kv_ref[...] = _dot(ckv, wkv_ref[...]).astype(kv_ref.dtype)


def _latent_up(proj, gq, gkv, wq, wkv, tm=TILE["latent_up"]):
    m = proj.shape[0]
    nq, nkv = wq.shape[1], wkv.shape[1]
    q_scale = (C_NOPE_DIM + C_ROPE_DIM) ** -0.5 * LOG2E
    return pl.pallas_call(
        functools.partial(_latent_up_kernel, q_scale=q_scale),
        grid=(m // tm,),
        in_specs=[
            pl.BlockSpec((tm, C_Q_RANK), lambda i: (i, CB_CQ)),
            pl.BlockSpec((tm, C_KV_RANK), lambda i: (i, CB_CKV)),
            pl.BlockSpec((1, C_Q_RANK), lambda i: (0, 0)),
            pl.BlockSpec((1, C_KV_RANK), lambda i: (0, 0)),
            pl.BlockSpec((C_Q_RANK, nq), lambda i: (0, 0)),
            pl.BlockSpec((C_KV_RANK, nkv), lambda i: (0, 0)),
        ],
        out_specs=[pl.BlockSpec((tm, nq), lambda i: (i, 0)), pl.BlockSpec((tm, nkv), lambda i: (i, 0))],
        out_shape=[jax.ShapeDtypeStruct((m, nq), BF16), jax.ShapeDtypeStruct((m, nkv), BF16)],
        compiler_params=_params(("parallel",), "latent_up"),
        name="latent_up",
    )(proj, proj, gq, gkv, wq, wkv)


A_ROW_CHUNK = 128
C_ROW_CHUNK = 256


def _attend(q_all, k_ref, v1_ref, dv, chunk):
    outs = []
    for c in range(q_all.shape[0] // chunk):
        q = q_all[c * chunk:(c + 1) * chunk]
        s = _dot_nt(q, k_ref[...])
        p = jnp.exp2(s - jnp.max(s, axis=-1, keepdims=True))
        ol = _dot(p.astype(BF16), v1_ref[...])
        outs.append(ol[:, :dv] * (1.0 / ol[:, dv:]))
    return outs


def _attn_a_kernel(q0_ref, q1_ref, q2_ref, k_ref, v_ref, cos_ref, slo_ref, shi_ref, gq_ref, gk_ref, wo_ref,
                   o_ref, wo_bf_ref, kbuf, v1buf, *, tq):
    qi = pl.program_id(2)
    wo_bf_ref[...] = wo_ref[...].astype(BF16)

    @pl.when(qi == 0)
    def _():
        k = _rms(k_ref[...].astype(F32), gk_ref[...])
        kbuf[...] = _rope(k, cos_ref[...], slo_ref[...], shi_ref[...], 32).astype(BF16)
        v1buf[:, :HEAD_DIM] = v_ref[...]
        v1buf[:, HEAD_DIM:] = jnp.ones((v1buf.shape[0], HEAD_DIM), BF16)

    rows = pl.ds(pl.multiple_of(qi * tq, tq), tq)
    cos, slo, shi = cos_ref[rows, :], slo_ref[rows, :], shi_ref[rows, :]
    q_scale = HEAD_DIM ** -0.5 * LOG2E
    qs = []
    for q_ref in (q0_ref, q1_ref, q2_ref):
        q = _rms(q_ref[...].astype(F32), gq_ref[...])
        qs.append((_rope(q, cos, slo, shi, 32) * q_scale).astype(BF16))
    q_all = jnp.concatenate(qs, axis=0)
    per_head = tq // A_ROW_CHUNK
    for c, o in enumerate(_attend(q_all, kbuf, v1buf, HEAD_DIM, A_ROW_CHUNK)):
        g, r = divmod(c, per_head)
        o_ref[r * A_ROW_CHUNK:(r + 1) * A_ROW_CHUNK, g * HEAD_DIM:(g + 1) * HEAD_DIM] = o.astype(o_ref.dtype)


def _attn_a(proj, tabs, gq, gk, w_out, layer, batch, seq, tq=TILE["attn_a"]):
    nq = seq // tq
    cos, slo, shi = tabs
    d = w_out.shape[2]
    wo_rows = w_out.shape[1] // (batch * A_KV_HEADS * nq)

    def wo_index(b, h, qi):
        return (b * A_KV_HEADS + h) * nq + qi

    def q_spec(g):
        return pl.BlockSpec((tq, HEAD_DIM), lambda b, h, qi: (b * nq + qi, CB_AQ + h * A_GROUP + g))

    tab_spec = pl.BlockSpec((seq, LANES), lambda b, h, qi: (0, 0))
    gain_spec = pl.BlockSpec((1, HEAD_DIM), lambda b, h, qi: (0, 0))
    return pl.pallas_call(
        functools.partial(_attn_a_kernel, tq=tq),
        grid=(batch, A_KV_HEADS, nq),
        in_specs=[
            q_spec(0), q_spec(1), q_spec(2),
            pl.BlockSpec((seq, HEAD_DIM), lambda b, h, qi: (b, CB_AK + h)),
            pl.BlockSpec((seq, HEAD_DIM), lambda b, h, qi: (b, CB_AV + h)),
            tab_spec, tab_spec, tab_spec, gain_spec, gain_spec,
            pl.BlockSpec((None, wo_rows, d), lambda b, h, qi: (layer, wo_index(b, h, qi), 0)),
        ],
        out_specs=[pl.BlockSpec((tq, A_GROUP * HEAD_DIM), lambda b, h, qi: (b * nq + qi, h)),
                   pl.BlockSpec((wo_rows, d), lambda b, h, qi: (wo_index(b, h, qi), 0))],
        out_shape=[jax.ShapeDtypeStruct((batch * seq, A_WIDTH), BF16),
                   jax.ShapeDtypeStruct(w_out.shape[1:], BF16)],
        scratch_shapes=[pltpu.VMEM((seq, HEAD_DIM), BF16), pltpu.VMEM((seq, 2 * HEAD_DIM), BF16)],
        compiler_params=_params(("parallel", "parallel", "arbitrary"), "attn_a"),
        name="attn_a",
    )(proj, proj, proj, proj, proj, cos, slo, shi, gq, gk, w_out)


def _attn_c_kernel(q_ref, kn_ref, v_ref, kr_ref, cos_ref, slo_ref, shi_ref, o_ref, kbuf, v1buf, *, tq):
    qi = pl.program_id(2)

    @pl.when(qi == 0)
    def _():
        kbuf[:, :C_NOPE_DIM] = kn_ref[...]
        kr = _rope(kr_ref[...].astype(F32), cos_ref[...], slo_ref[...], shi_ref[...], 32)
        kbuf[:, C_NOPE_DIM:] = kr.astype(BF16)
        v1buf[:, :C_V_DIM] = v_ref[...]
        v1buf[:, C_V_DIM:] = jnp.ones((v1buf.shape[0], C_V_DIM), BF16)

    rows = pl.ds(pl.multiple_of(qi * tq, tq), tq)
    q_rope = _rope(q_ref[:, C_NOPE_DIM:].astype(F32), cos_ref[rows, :], slo_ref[rows, :], shi_ref[rows, :], 32)
    q_all = jnp.concatenate([q_ref[:, :C_NOPE_DIM], q_rope.astype(BF16)], axis=1)
    for c, o in enumerate(_attend(q_all, kbuf, v1buf, C_V_DIM, C_ROW_CHUNK)):
        o_ref[c * C_ROW_CHUNK:(c + 1) * C_ROW_CHUNK, :] = o.astype(o_ref.dtype)


def _attn_c(qc, kvc, proj, tabs, batch, seq, tq=TILE["attn_c"]):
    nq = seq // tq
    cos, slo, shi = tabs
    tab_spec = pl.BlockSpec((seq, LANES), lambda b, h, qi: (0, 0))
    return pl.pallas_call(
        functools.partial(_attn_c_kernel, tq=tq),
        grid=(batch, C_HEADS, nq),
        in_specs=[
            pl.BlockSpec((tq, C_QK_PAD), lambda b, h, qi: (b * nq + qi, h)),
            pl.BlockSpec((seq, C_NOPE_DIM), lambda b, h, qi: (b, h)),
            pl.BlockSpec((seq, C_V_DIM), lambda b, h, qi: (b, C_HEADS + h)),
            pl.BlockSpec((seq, LANES), lambda b, h, qi: (b, CB_CKR)),
            tab_spec, tab_spec, tab_spec,
        ],
        out_specs=pl.BlockSpec((tq, C_V_DIM), lambda b, h, qi: (b * nq + qi, h)),
        out_shape=jax.ShapeDtypeStruct((batch * seq, C_WIDTH), BF16),
        scratch_shapes=[pltpu.VMEM((seq, C_QK_PAD), BF16), pltpu.VMEM((seq, 2 * C_V_DIM), BF16)],
        compiler_params=_params(("parallel", "parallel", "arbitrary"), "attn_c"),
        name="attn_c",
    )(qc, kvc, kvc, proj, cos, slo, shi)


B_QBLK = 128
(B_HALF,) = {w // (2 * d) for w, d in B_PATTERNS}
assert 2 * B_HALF == B_QBLK
B_PREP_ROWS = 256
B_UNROLL = 16


def _attn_b_kernel(q0_ref, q1_ref, q2_ref, k_ref, v_ref, cos_ref, slo_ref, shi_ref, o_ref,
                   q0f, q1f, q2f, kf, vf, m_s, l_s, acc_s, bias_s, *, seq):
    q_refs = (q0_ref, q1_ref, q2_ref)
    q_bufs = (q0f, q1f, q2f)
    q_scale = HEAD_DIM ** -0.5 * LOG2E

    half = B_HALF
    shape = (B_QBLK, 2 * B_QBLK)
    rel = lax.broadcasted_iota(jnp.int32, shape, 1) - lax.broadcasted_iota(jnp.int32, shape, 0)
    for case, offset in enumerate((0, -half, -B_QBLK)):
        bias_s[case] = jnp.where(jnp.abs(rel + offset) <= half, 0.0, -jnp.inf)

    def prep(ci, carry):
        rows = pl.ds(pl.multiple_of(ci * B_PREP_ROWS, B_PREP_ROWS), B_PREP_ROWS)
        cos, slo, shi = cos_ref[rows, :], slo_ref[rows, :], shi_ref[rows, :]
        for q_ref, q_buf in zip(q_refs, q_bufs):
            q_buf[rows, :] = _rope(q_ref[rows, :].astype(F32), cos, slo, shi, 16) * q_scale
        kf[rows, :] = _rope(k_ref[rows, :].astype(F32), cos, slo, shi, 16)
        vf[rows, :] = v_ref[rows, :].astype(F32)
        return carry

    lax.fori_loop(0, seq // B_PREP_ROWS, prep, 0, unroll=2)

    for g, (window, dil) in reversed(list(enumerate(B_PATTERNS))):
        first = g == len(B_PATTERNS) - 1
        half = window // (2 * dil)
        length = seq // dil
        nblk = length // B_QBLK
        kwin = min(2 * B_QBLK, length)
        q_buf = q_bufs[g]

        def rows_of(start, size, dil=dil):
            return pl.ds(start, size) if dil == 1 else pl.ds(start, size, stride=dil)

        def block(n, carry, first=first, half=half, length=length, nblk=nblk, kwin=kwin, q_buf=q_buf, dil=dil,
                  rows_of=rows_of):
            r = n // nblk
            i = n % nblk
            k0 = jnp.clip(i * B_QBLK - half, 0, length - kwin)
            q_rows = rows_of(r + dil * B_QBLK * i, B_QBLK)
            k_rows = rows_of(r + dil * k0, kwin)
            q = q_buf[q_rows, :].astype(BF16)
            k = kf[k_rows, :].astype(BF16)
            v = vf[k_rows, :].astype(BF16)
            case = 0 if nblk == 1 else jnp.where(i == 0, 0, jnp.where(i == nblk - 1, 2, 1))
            s = _dot_nt(q, k) + bias_s[case, :, :kwin]
            m_b = jnp.max(s, axis=-1, keepdims=True)
            p = jnp.exp2(s - m_b)
            l_b = jnp.sum(p, axis=-1, keepdims=True)
            a_b = _dot(p.astype(BF16), v)
            full = (B_QBLK, HEAD_DIM)
            if first:
                m_s[q_rows, :] = jnp.broadcast_to(m_b, full)
                l_s[q_rows, :] = jnp.broadcast_to(l_b, full)
                acc_s[q_rows, :] = a_b
            else:
                m_o = m_s[q_rows, :]
                m_n = jnp.maximum(m_o, m_b)
                w_o = jnp.exp2(m_o - m_n)
                w_b = jnp.exp2(m_b - m_n)
                acc_s[q_rows, :] = acc_s[q_rows, :] * w_o + a_b * w_b
                l_s[q_rows, :] = l_s[q_rows, :] * w_o + l_b * w_b
                m_s[q_rows, :] = m_n
            return carry

        lax.fori_loop(0, seq // B_QBLK, block, 0, unroll=B_UNROLL)

    def finish(ci, carry):
        rows = pl.ds(pl.multiple_of(ci * B_PREP_ROWS, B_PREP_ROWS), B_PREP_ROWS)
        o_ref[rows, :] = (acc_s[rows, :] * (1.0 / l_s[rows, :])).astype(o_ref.dtype)
        return carry

    lax.fori_loop(0, seq // B_PREP_ROWS, finish, 0)


def _attn_b(proj, tabs, batch, seq):
    cos, slo, shi = tabs

    def q_spec(g):
        return pl.BlockSpec((seq, HEAD_DIM), lambda b, h: (b, CB_BQ + g * B_HEADS + h))

    tab_spec = pl.BlockSpec((seq, LANES), lambda b, h: (0, 0))
    slab = pltpu.VMEM((seq, HEAD_DIM), F32)
    return pl.pallas_call(
        functools.partial(_attn_b_kernel, seq=seq),
        grid=(batch, B_HEADS),
        in_specs=[
            q_spec(0), q_spec(1), q_spec(2),
            pl.BlockSpec((seq, HEAD_DIM), lambda b, h: (b, CB_BK + h)),
            pl.BlockSpec((seq, HEAD_DIM), lambda b, h: (b, CB_BV + h)),
            tab_spec, tab_spec, tab_spec,
        ],
        out_specs=pl.BlockSpec((seq, HEAD_DIM), lambda b, h: (b, h)),
        out_shape=jax.ShapeDtypeStruct((batch * seq, B_WIDTH), BF16),
        scratch_shapes=[slab] * 8 + [pltpu.VMEM((3, B_QBLK, 2 * B_QBLK), F32)],
        compiler_params=_params(("parallel", "parallel"), "attn_b"),
        name="attn_b",
    )(proj, proj, proj, proj, proj, cos, slo, shi)


def _out_kernel(ya_ref, yb_ref, yc_ref, ga_ref, gb_ref, gc_ref, w_ref, x_ref, gf_ref, x1_ref, xn_ref):
    y = jnp.concatenate([
        _rms(ya_ref[...].astype(F32), ga_ref[...]).astype(BF16),
        _rms(yb_ref[...].astype(F32), gb_ref[...]).astype(BF16),
        _rms(yc_ref[...].astype(F32), gc_ref[...]).astype(BF16),
    ], axis=1)
    x1 = x_ref[...] + _dot(y, w_ref[...])
    x1_ref[...] = x1
    xn_ref[...] = _rms(x1, gf_ref[...]).astype(xn_ref.dtype)


def _out_proj(ya, yb, yc, ga, gb, gc, w, x, gf, tm=TILE["out_proj"]):
    m, d = x.shape

    def rows(width):
        return pl.BlockSpec((tm, width), lambda i: (i, 0))

    def const(r, c):
        return pl.BlockSpec((r, c), lambda i: (0, 0))

    return pl.pallas_call(
        _out_kernel,
        grid=(m // tm,),
        in_specs=[rows(A_WIDTH), rows(B_WIDTH), rows(C_WIDTH), const(1, A_WIDTH), const(1, B_WIDTH),
                  const(1, C_WIDTH), const(d, d), rows(d), const(1, d)],
        out_specs=[rows(d), rows(d)],
        out_shape=[jax.ShapeDtypeStruct((m, d), F32), jax.ShapeDtypeStruct((m, d), BF16)],
        compiler_params=_params(("parallel",), "out_proj"),
        name="out_proj",
    )(ya, yb, yc, ga, gb, gc, w, x, gf)


FFN_SUBTILE = 256
FFN_ROWS = 1024
FFN_GATE_ROWS = 32
FFN_GUARD = 8

def _ffn_up_kernel(xn_ref, wg_ref, wu_ref, cg_ref, cu_ref, bg_ref, bu_ref, wd_ref, o_ref, wd_bf_ref,
                   hg, hu, wg_bf, wu_bf, *, seq):
    tn = o_ref.shape[1]
    slabs = FFN_SUBTILE // LANES
    wd_bf_ref[...] = wd_ref[...].astype(BF16)

    @pl.when(pl.program_id(1) == 0)
    def _():
        wg_bf[...] = wg_ref[...].astype(BF16)
        wu_bf[...] = wu_ref[...].astype(BF16)

    zeros = jnp.zeros((FFN_GUARD, LANES), F32)
    for h in (hg, hu):
        for s in range(tn // LANES):
            h[s, 0:FFN_GUARD, :] = zeros
            h[s, FFN_GUARD + seq:2 * FFN_GUARD + seq, :] = zeros

    def matmuls(c, r):
        cols = slice(c * FFN_SUBTILE, (c + 1) * FFN_SUBTILE)
        xr = xn_ref[r * FFN_ROWS:(r + 1) * FFN_ROWS, :]
        dst = slice(FFN_GUARD + r * FFN_ROWS, FFN_GUARD + (r + 1) * FFN_ROWS)
        for h, w in ((hg, wg_bf), (hu, wu_bf)):
            res = _dot(xr, w[:, cols])
            for s in range(slabs):
                h[c * slabs + s, dst, :] = res[:, s * LANES:(s + 1) * LANES]

    def gate_stage(c, r):
        for k in range(FFN_ROWS // FFN_GATE_ROWS):
            row0 = r * FFN_ROWS + k * FFN_GATE_ROWS
            base = FFN_GUARD + row0
            acts = []
            for s in range(slabs):
                slab = c * slabs + s
                lanes = slice(slab * LANES, (slab + 1) * LANES)

                def tap(h, t):
                    if t == 1:
                        return h[slab, base:base + FFN_GATE_ROWS, :]
                    return h[pl.ds(slab, 1, stride=2), pl.ds(base + t - 1, FFN_GATE_ROWS), :][0]

                def conv(h, c_ref, b_ref):
                    taps = [tap(h, t) * c_ref[t:t + 1, lanes] for t in range(3)]
                    return taps[0] + taps[1] + taps[2] + b_ref[:, lanes]

                gate = conv(hg, cg_ref, bg_ref)
                up = conv(hu, cu_ref, bu_ref)
                acts.append(gate * (1.0 / (1.0 + jnp.exp2(gate * -LOG2E))) * up)
            o_ref[row0:row0 + FFN_GATE_ROWS, c * FFN_SUBTILE:(c + 1) * FFN_SUBTILE] = (
                jnp.concatenate(acts, axis=1).astype(o_ref.dtype))

    units = [(c, r) for c in range(tn // FFN_SUBTILE) for r in range(seq // FFN_ROWS)]
    for i, unit in enumerate(units):
        matmuls(*unit)
        if i > 0:
            gate_stage(*units[i - 1])
    gate_stage(*units[-1])


def _ffn_up(xn, w_up, w_down, layer, conv_w, conv_b, batch, seq, tn=TILE["ffn_up"]):
    d = xn.shape[1]
    nj = D_FF // tn
    wd_rows = w_down.shape[1] // (nj * batch)
    return pl.pallas_call(
        functools.partial(_ffn_up_kernel, seq=seq),
        grid=(nj, batch),
        in_specs=[
            pl.BlockSpec((seq, d), lambda j, b: (b, 0)),
            pl.BlockSpec((None, d, tn), lambda j, b: (layer, 0, j)),
            pl.BlockSpec((None, d, tn), lambda j, b: (layer, 0, nj + j)),
            pl.BlockSpec((3, tn), lambda j, b: (0, j)),
            pl.BlockSpec((3, tn), lambda j, b: (0, nj + j)),
            pl.BlockSpec((1, tn), lambda j, b: (0, j)),
            pl.BlockSpec((1, tn), lambda j, b: (0, nj + j)),
            pl.BlockSpec((None, wd_rows, d), lambda j, b: (layer, j * batch + b, 0)),
        ],
        out_specs=[pl.BlockSpec((seq, tn), lambda j, b: (b, j)),
                   pl.BlockSpec((wd_rows, d), lambda j, b: (j * batch + b, 0))],
        out_shape=[jax.ShapeDtypeStruct((batch * seq, D_FF), BF16),
                   jax.ShapeDtypeStruct(w_down.shape[1:], BF16)],
        scratch_shapes=[pltpu.VMEM((tn // LANES, seq + 2 * FFN_GUARD, LANES), F32)] * 2
        + [pltpu.VMEM((d, tn), BF16)] * 2,
        compiler_params=_params(("parallel", "arbitrary"), "ffn_up"),
        name="ffn_up",
    )(xn, w_up, w_up, conv_w, conv_w, conv_b, conv_b, w_down)


def _ffn_down_kernel(a_ref, w_ref, x1_ref, g_ref, x2_ref, xn_ref):
    x2 = x1_ref[...] + _dot(a_ref[...], w_ref[...])
    x2_ref[...] = x2
    xn_ref[...] = _rms(x2, g_ref[...]).astype(xn_ref.dtype)


def _ffn_down_final_kernel(a_ref, w_ref, x1_ref, g_ref, xn_ref):
    x2 = x1_ref[...] + _dot(a_ref[...], w_ref[...])
    xn_ref[...] = _rms(x2, g_ref[...]).astype(xn_ref.dtype)


def _ffn_down(act, w, x1, g, final, tm=TILE["ffn_down"]):
    m, kdim = act.shape
    d = w.shape[1]
    tile = pl.BlockSpec((tm, d), lambda i: (i, 0))
    in_specs = [
        pl.BlockSpec((tm, kdim), lambda i: (i, 0)),
        pl.BlockSpec((kdim, d), lambda i: (0, 0), pipeline_mode=pl.Buffered(1)),
        tile,
        pl.BlockSpec((1, d), lambda i: (0, 0)),
    ]
    common = dict(grid=(m // tm,), in_specs=in_specs, compiler_params=_params(("parallel",), "ffn_down"))
    if final:
        xn = pl.pallas_call(_ffn_down_final_kernel, out_specs=tile, out_shape=jax.ShapeDtypeStruct((m, d), F32),
                            name="ffn_down_final", **common)(act, w, x1, g)
        return None, xn
    return pl.pallas_call(_ffn_down_kernel, out_specs=[tile, tile],
                          out_shape=[jax.ShapeDtypeStruct((m, d), F32), jax.ShapeDtypeStruct((m, d), BF16)],
                          name="ffn_down", **common)(act, w, x1, g)


def _rope_table(pos, dim, theta):
    f32 = np.float32
    inv = f32(theta) ** (-np.arange(0, dim, 2, dtype=f32) / f32(dim))
    ang = (pos.astype(f32)[:, None] * inv[None, :]).astype(np.float64)
    return np.cos(ang).astype(f32), np.sin(ang).astype(f32)


def _lane_tables(seq):
    t = np.arange(seq)
    z = lambda w: np.zeros((seq, w), np.float32)
    one = lambda w: np.ones((seq, w), np.float32)
    cat = lambda parts: jnp.asarray(np.concatenate(parts, 1))
    cr, sr = _rope_table(t // GRID_W, HEAD_DIM // 2, A_ROPE_THETA)
    cc, sc = _rope_table(t % GRID_W, HEAD_DIM // 2, A_ROPE_THETA)
    tab_a = (cat([cr, cr, cc, cc]), cat([-sr, z(32), -sc, z(32)]), cat([z(32), sr, z(32), sc]))
    cp, sp = _rope_table(t, PARTIAL_ROPE_DIM, PARTIAL_ROPE_THETA)
    tab_b = (cat([cp, cp, one(96)]), cat([-sp, z(112)]), cat([z(16), sp, z(96)]))
    cm, sm = _rope_table(t, C_ROPE_DIM, C_ROPE_THETA)
    tab_c = (cat([cm, cm, one(64)]), cat([-sm, z(96)]), cat([z(32), sm, z(64)]))
    return tab_a, tab_b, tab_c


def _prep_w_ckr(w_in_t, layer):
    ckr = w_in_t[layer, IN_N_SRC * IN_WBLK:, :]
    return jnp.pad(ckr, ((0, IN_WBLK - ckr.shape[0]), (0, 0)))


def _prep_w_uq(w):
    w = w.reshape(C_Q_RANK, C_HEADS, C_NOPE_DIM + C_ROPE_DIM)
    w = jnp.pad(w, ((0, 0), (0, 0), (0, C_QK_PAD - C_NOPE_DIM - C_ROPE_DIM)))
    return w.reshape(C_Q_RANK, C_HEADS * C_QK_PAD).astype(BF16)


def _prep_w_ukv(w):
    w = w.reshape(C_KV_RANK, C_HEADS, C_NOPE_DIM + C_V_DIM)
    kn = w[:, :, :C_NOPE_DIM].reshape(C_KV_RANK, C_HEADS * C_NOPE_DIM)
    v = w[:, :, C_NOPE_DIM:].reshape(C_KV_RANK, C_HEADS * C_V_DIM)
    return jnp.concatenate([kn, v], axis=1).astype(BF16)


def kernel(x, attn_norm, w_in, a_q_norm, a_k_norm, c_q_norm, c_kv_norm, w_uq, w_ukv, out_norm, w_out,
           ffn_norm, w_up, conv_w, conv_b, w_down, final_norm):
    batch, seq, d = x.shape
    depth = w_in.shape[0]
    tab_a, tab_b, tab_c = _lane_tables(seq)
    row = lambda v: v.reshape(1, -1)

    xr = x.reshape(batch * seq, d)
    xn = _rmsnorm_rows(xr, row(attn_norm[0]))
    w_in_t = jnp.swapaxes(w_in, 1, 2)
    for l in range(depth):
        proj = _in_proj(xn, w_in_t, l, _prep_w_ckr(w_in_t, l))
        qc, kvc = _latent_up(proj, row(c_q_norm[l]), row(c_kv_norm[l]), _prep_w_uq(w_uq[l]),
                             _prep_w_ukv(w_ukv[l]))
        ya, w_out_bf = _attn_a(proj, tab_a, row(a_q_norm[l]), row(a_k_norm[l]), w_out, l, batch, seq)
        yb = _attn_b(proj, tab_b, batch, seq)
        yc = _attn_c(qc, kvc, proj, tab_c, batch, seq)
        g = out_norm[l]
        x1, xn1 = _out_proj(ya, yb, yc, row(g[:A_WIDTH]), row(g[A_WIDTH:A_WIDTH + B_WIDTH]),
                            row(g[A_WIDTH + B_WIDTH:]), w_out_bf, xr, row(ffn_norm[l]))
        act, w_down_bf = _ffn_up(xn1, w_up, w_down, l, conv_w[l], row(conv_b[l]), batch, seq)
        last = l == depth - 1
        g_next = final_norm if last else attn_norm[l + 1]
        xr, xn = _ffn_down(act, w_down_bf, x1, row(g_next), last)
    return xn.reshape(batch, seq, d)
```

```python
import functools
import math

import jax
import jax.numpy as jnp
import numpy as np
from jax import lax
from jax.experimental import pallas as pl
from jax.experimental.pallas import tpu as pltpu

D_MODEL = 2048
HEAD_DIM = 128
A_HEADS = 6
A_KV_HEADS = 2
A_GROUP = A_HEADS // A_KV_HEADS
A_ROPE_THETA = 10000.0
B_HEADS = 4
B_PATTERNS = ((128, 1), (512, 4), (2048, 16))
B_N_GROUPS = 3
C_HEADS = 6
C_Q_RANK = 512
C_KV_RANK = 512
C_NOPE_DIM = 128
C_ROPE_DIM = 64
C_V_DIM = 128
C_ROPE_THETA = 10000.0
PARTIAL_ROPE_DIM = HEAD_DIM // 4
PARTIAL_ROPE_THETA = 500000.0
GRID_W = 64
D_FF = 5632
EPS = 1e-6

A_WIDTH = A_HEADS * HEAD_DIM
B_WIDTH = B_HEADS * HEAD_DIM
C_WIDTH = C_HEADS * C_V_DIM
C_QK_PAD = 256

LANES = 128
LOG2E = math.log2(math.e)
F32 = jnp.float32
BF16 = jnp.bfloat16

PROJ_WIDTH = 5120
CB_CQ, CB_CKV = 0, 1
CB_AQ, CB_AK, CB_AV = 8, 14, 16
CB_BQ, CB_BK, CB_BV = 18, 30, 34
CB_CKR = 38


TILE = dict(rmsnorm=512, in_proj=2048, latent_up=2048, attn_a=1024, attn_c=2048, out_proj=512, ffn_up=512,
            ffn_down=512)
VMEM_MIB = dict(rmsnorm=40, in_proj=60, latent_up=48, attn_a=48, attn_b=48, attn_c=48, out_proj=56, ffn_up=60,
                ffn_down=60)


def _params(semantics, call):
    return pltpu.CompilerParams(dimension_semantics=semantics, vmem_limit_bytes=VMEM_MIB[call] * 1024 * 1024)


def _rms(xf, g):
    return xf * lax.rsqrt(jnp.mean(xf * xf, axis=-1, keepdims=True) + EPS) * g


def _rope(xf, cos, sin_lo, sin_hi, shift):
    return xf * cos + pltpu.roll(xf, LANES - shift, 1) * sin_lo + pltpu.roll(xf, shift, 1) * sin_hi


def _dot(a, b):
    return jnp.dot(a, b, preferred_element_type=F32)


def _dot_nt(a, b):
    return lax.dot_general(a, b, (((1,), (1,)), ((), ())), preferred_element_type=F32)


def _norm_kernel(x_ref, g_ref, o_ref):
    o_ref[...] = _rms(x_ref[...], g_ref[...]).astype(o_ref.dtype)


def _rmsnorm_rows(x, g, tm=TILE["rmsnorm"]):
    m, d = x.shape
    return pl.pallas_call(
        _norm_kernel,
        grid=(m // tm,),
        in_specs=[pl.BlockSpec((tm, d), lambda i: (i, 0)), pl.BlockSpec((1, d), lambda i: (0, 0))],
        out_specs=pl.BlockSpec((tm, d), lambda i: (i, 0)),
        out_shape=jax.ShapeDtypeStruct((m, d), BF16),
        compiler_params=_params(("parallel",), "rmsnorm"),
        name="rmsnorm",
    )(x, g)


IN_WBLK = 256
IN_TN = 1024
IN_SRC_CQ = 15
IN_N_SRC = 19


def _in_proj_kernel(a_ref, *refs):
    w_refs, wck_ref, o_ref, w_bf = refs[:-3], refs[-3], refs[-2], refs[-1]
    j = pl.program_id(0)
    last = pl.num_programs(0) - 1

    @pl.when(pl.program_id(1) == 0)
    def _():
        for q, w_ref in enumerate(w_refs[:-1]):
            w_bf[q * IN_WBLK:(q + 1) * IN_WBLK, :] = w_ref[...].astype(BF16)
        tail = slice((len(w_refs) - 1) * IN_WBLK, len(w_refs) * IN_WBLK)

        @pl.when(j < last)
        def _():
            w_bf[tail, :] = w_refs[-1][...].astype(BF16)

        @pl.when(j == last)
        def _():
            w_bf[tail, :] = wck_ref[...].astype(BF16)

    o_ref[...] = _dot_nt(a_ref[...], w_bf[...]).astype(o_ref.dtype)


def _in_proj(a, w_in_t, layer, w_ckr_t, tm=TILE["in_proj"]):
    m, k = a.shape
    per_step = IN_TN // IN_WBLK

    def w_spec(q):
        def index(j, i):
            n = j * per_step + q
            return layer, jnp.where(n < 4, n + IN_SRC_CQ, n - 4), 0
        return pl.BlockSpec((None, IN_WBLK, k), index)

    return pl.pallas_call(
        _in_proj_kernel,
        grid=(PROJ_WIDTH // IN_TN, m // tm),
        in_specs=[pl.BlockSpec((tm, k), lambda j, i: (i, 0))] + [w_spec(q) for q in range(per_step)]
        + [pl.BlockSpec((IN_WBLK, k), lambda j, i: (0, 0))],
        out_specs=pl.BlockSpec((tm, IN_TN), lambda j, i: (i, j)),
        out_shape=jax.ShapeDtypeStruct((m, PROJ_WIDTH), BF16),
        scratch_shapes=[pltpu.VMEM((IN_TN, k), BF16)],
        compiler_params=_params(("parallel", "arbitrary"), "in_proj"),
        name="in_proj",
    )(a, *([w_in_t] * per_step), w_ckr_t)


def _latent_up_kernel(cq_ref, ckv_ref, gq_ref, gkv_ref, wq_ref, wkv_ref, q_ref, kv_ref, *, q_scale):
    cq = _rms(cq_ref[...].astype(F32), gq_ref[...]).astype(BF16)
    q_ref[...] = (_dot(cq, wq_ref[...]) * q_scale).astype(q_ref.dtype)
    ckv = _rms(ckv_ref[...].astype(F32), gkv_ref[...]).astype(BF16)
    kv_ref[...] = _dot(ckv, wkv_ref[...]).astype(kv_ref.dtype)


def _latent_up(proj, gq, gkv, wq, wkv, tm=TILE["latent_up"]):
    m = proj.shape[0]
    nq, nkv = wq.shape[1], wkv.shape[1]
    q_scale = (C_NOPE_DIM + C_ROPE_DIM) ** -0.5 * LOG2E
    return pl.pallas_call(
        functools.partial(_latent_up_kernel, q_scale=q_scale),
        grid=(m // tm,),
        in_specs=[
            pl.BlockSpec((tm, C_Q_RANK), lambda i: (i, CB_CQ)),
            pl.BlockSpec((tm, C_KV_RANK), lambda i: (i, CB_CKV)),
            pl.BlockSpec((1, C_Q_RANK), lambda i: (0, 0)),
            pl.BlockSpec((1, C_KV_RANK), lambda i: (0, 0)),
            pl.BlockSpec((C_Q_RANK, nq), lambda i: (0, 0)),
            pl.BlockSpec((C_KV_RANK, nkv), lambda i: (0, 0)),
        ],
        out_specs=[pl.BlockSpec((tm, nq), lambda i: (i, 0)), pl.BlockSpec((tm, nkv), lambda i: (i, 0))],
        out_shape=[jax.ShapeDtypeStruct((m, nq), BF16), jax.ShapeDtypeStruct((m, nkv), BF16)],
        compiler_params=_params(("parallel",), "latent_up"),
        name="latent_up",
    )(proj, proj, gq, gkv, wq, wkv)


A_ROW_CHUNK = 256
C_ROW_CHUNK = 256


def _attend(q_all, k_ref, v1_ref, dv, chunk):
    outs = []
    for c in range(q_all.shape[0] // chunk):
        q = q_all[c * chunk:(c + 1) * chunk]
        s = _dot_nt(q, k_ref[...])
        p = jnp.exp2(s - jnp.max(s, axis=-1, keepdims=True))
        ol = _dot(p.astype(BF16), v1_ref[...])
        outs.append(ol[:, :dv] * (1.0 / ol[:, dv:]))
    return outs


def _attn_a_kernel(q0_ref, q1_ref, q2_ref, k_ref, v_ref, cos_ref, slo_ref, shi_ref, gq_ref, gk_ref, wo_ref,
                   o_ref, wo_bf_ref, kbuf, v1buf, *, tq):
    qi = pl.program_id(2)
    wo_bf_ref[...] = wo_ref[...].astype(BF16)

    @pl.when(qi == 0)
    def _():
        k = _rms(k_ref[...].astype(F32), gk_ref[...])
        kbuf[...] = _rope(k, cos_ref[...], slo_ref[...], shi_ref[...], 32).astype(BF16)
        v1buf[:, :HEAD_DIM] = v_ref[...]
        v1buf[:, HEAD_DIM:] = jnp.ones((v1buf.shape[0], HEAD_DIM), BF16)

    rows = pl.ds(pl.multiple_of(qi * tq, tq), tq)
    cos, slo, shi = cos_ref[rows, :], slo_ref[rows, :], shi_ref[rows, :]
    q_scale = HEAD_DIM ** -0.5 * LOG2E
    qs = []
    for q_ref in (q0_ref, q1_ref, q2_ref):
        q = _rms(q_ref[...].astype(F32), gq_ref[...])
        qs.append((_rope(q, cos, slo, shi, 32) * q_scale).astype(BF16))
    q_all = jnp.concatenate(qs, axis=0)
    per_head = tq // A_ROW_CHUNK
    for c, o in enumerate(_attend(q_all, kbuf, v1buf, HEAD_DIM, A_ROW_CHUNK)):
        g, r = divmod(c, per_head)
        o_ref[r * A_ROW_CHUNK:(r + 1) * A_ROW_CHUNK, g * HEAD_DIM:(g + 1) * HEAD_DIM] = o.astype(o_ref.dtype)


def _attn_a(proj, tabs, gq, gk, w_out, layer, batch, seq, tq=TILE["attn_a"]):
    nq = seq // tq
    cos, slo, shi = tabs
    d = w_out.shape[2]
    wo_rows = w_out.shape[1] // (batch * A_KV_HEADS * nq)

    def wo_index(b, h, qi):
        return (b * A_KV_HEADS + h) * nq + qi

    def q_spec(g):
        return pl.BlockSpec((tq, HEAD_DIM), lambda b, h, qi: (b * nq + qi, CB_AQ + h * A_GROUP + g))

    tab_spec = pl.BlockSpec((seq, LANES), lambda b, h, qi: (0, 0))
    gain_spec = pl.BlockSpec((1, HEAD_DIM), lambda b, h, qi: (0, 0))
    return pl.pallas_call(
        functools.partial(_attn_a_kernel, tq=tq),
        grid=(batch, A_KV_HEADS, nq),
        in_specs=[
            q_spec(0), q_spec(1), q_spec(2),
            pl.BlockSpec((seq, HEAD_DIM), lambda b, h, qi: (b, CB_AK + h)),
            pl.BlockSpec((seq, HEAD_DIM), lambda b, h, qi: (b, CB_AV + h)),
            tab_spec, tab_spec, tab_spec, gain_spec, gain_spec,
            pl.BlockSpec((None, wo_rows, d), lambda b, h, qi: (layer, wo_index(b, h, qi), 0)),
        ],
        out_specs=[pl.BlockSpec((tq, A_GROUP * HEAD_DIM), lambda b, h, qi: (b * nq + qi, h)),
                   pl.BlockSpec((wo_rows, d), lambda b, h, qi: (wo_index(b, h, qi), 0))],
        out_shape=[jax.ShapeDtypeStruct((batch * seq, A_WIDTH), BF16),
                   jax.ShapeDtypeStruct(w_out.shape[1:], BF16)],
        scratch_shapes=[pltpu.VMEM((seq, HEAD_DIM), BF16), pltpu.VMEM((seq, 2 * HEAD_DIM), BF16)],
        compiler_params=_params(("parallel", "parallel", "arbitrary"), "attn_a"),
        name="attn_a",
    )(proj, proj, proj, proj, proj, cos, slo, shi, gq, gk, w_out)


def _attn_c_kernel(q_ref, kn_ref, v_ref, kr_ref, cos_ref, slo_ref, shi_ref, o_ref, kbuf, v1buf, *, tq):
    qi = pl.program_id(2)

    @pl.when(qi == 0)
    def _():
        kbuf[:, :C_NOPE_DIM] = kn_ref[...]
        kr = _rope(kr_ref[...].astype(F32), cos_ref[...], slo_ref[...], shi_ref[...], 32)
        kbuf[:, C_NOPE_DIM:] = kr.astype(BF16)
        v1buf[:, :C_V_DIM] = v_ref[...]
        v1buf[:, C_V_DIM:] = jnp.ones((v1buf.shape[0], C_V_DIM), BF16)

    rows = pl.ds(pl.multiple_of(qi * tq, tq), tq)
    q_rope = _rope(q_ref[:, C_NOPE_DIM:].astype(F32), cos_ref[rows, :], slo_ref[rows, :], shi_ref[rows, :], 32)
    q_all = jnp.concatenate([q_ref[:, :C_NOPE_DIM], q_rope.astype(BF16)], axis=1)
    for c, o in enumerate(_attend(q_all, kbuf, v1buf, C_V_DIM, C_ROW_CHUNK)):
        o_ref[c * C_ROW_CHUNK:(c + 1) * C_ROW_CHUNK, :] = o.astype(o_ref.dtype)


def _attn_c(qc, kvc, proj, tabs, batch, seq, tq=TILE["attn_c"]):
    nq = seq // tq
    cos, slo, shi = tabs
    tab_spec = pl.BlockSpec((seq, LANES), lambda b, h, qi: (0, 0))
    return pl.pallas_call(
        functools.partial(_attn_c_kernel, tq=tq),
        grid=(batch, C_HEADS, nq),
        in_specs=[
            pl.BlockSpec((tq, C_QK_PAD), lambda b, h, qi: (b * nq + qi, h)),
            pl.BlockSpec((seq, C_NOPE_DIM), lambda b, h, qi: (b, h)),
            pl.BlockSpec((seq, C_V_DIM), lambda b, h, qi: (b, C_HEADS + h)),
            pl.BlockSpec((seq, LANES), lambda b, h, qi: (b, CB_CKR)),
            tab_spec, tab_spec, tab_spec,
        ],
        out_specs=pl.BlockSpec((tq, C_V_DIM), lambda b, h, qi: (b * nq + qi, h)),
        out_shape=jax.ShapeDtypeStruct((batch * seq, C_WIDTH), BF16),
        scratch_shapes=[pltpu.VMEM((seq, C_QK_PAD), BF16), pltpu.VMEM((seq, 2 * C_V_DIM), BF16)],
        compiler_params=_params(("parallel", "parallel", "arbitrary"), "attn_c"),
        name="attn_c",
    )(qc, kvc, kvc, proj, cos, slo, shi)


B_QBLK = 128
(B_HALF,) = {w // (2 * d) for w, d in B_PATTERNS}
assert 2 * B_HALF == B_QBLK
B_PREP_ROWS = 256
B_UNROLL = 16


def _attn_b_kernel(q0_ref, q1_ref, q2_ref, k_ref, v_ref, cos_ref, slo_ref, shi_ref, o_ref,
                   q0f, q1f, q2f, kf, vf, m_s, l_s, acc_s, bias_s, *, seq):
    q_refs = (q0_ref, q1_ref, q2_ref)
    q_bufs = (q0f, q1f, q2f)
    q_scale = HEAD_DIM ** -0.5 * LOG2E

    half = B_HALF
    shape = (B_QBLK, 2 * B_QBLK)
    rel = lax.broadcasted_iota(jnp.int32, shape, 1) - lax.broadcasted_iota(jnp.int32, shape, 0)
    for case, offset in enumerate((0, -half, -B_QBLK)):
        bias_s[case] = jnp.where(jnp.abs(rel + offset) <= half, 0.0, -jnp.inf)

    def prep(ci, carry):
        rows = pl.ds(pl.multiple_of(ci * B_PREP_ROWS, B_PREP_ROWS), B_PREP_ROWS)
        cos, slo, shi = cos_ref[rows, :], slo_ref[rows, :], shi_ref[rows, :]
        for q_ref, q_buf in zip(q_refs, q_bufs):
            q_buf[rows, :] = _rope(q_ref[rows, :].astype(F32), cos, slo, shi, 16) * q_scale
        kf[rows, :] = _rope(k_ref[rows, :].astype(F32), cos, slo, shi, 16)
        vf[rows, :] = v_ref[rows, :].astype(F32)
        return carry

    lax.fori_loop(0, seq // B_PREP_ROWS, prep, 0, unroll=2)

    for g, (window, dil) in reversed(list(enumerate(B_PATTERNS))):
        first = g == len(B_PATTERNS) - 1
        half = window // (2 * dil)
        length = seq // dil
        nblk = length // B_QBLK
        kwin = min(2 * B_QBLK, length)
        q_buf = q_bufs[g]

        def rows_of(start, size, dil=dil):
            return pl.ds(start, size) if dil == 1 else pl.ds(start, size, stride=dil)

        def block(n, carry, first=first, half=half, length=length, nblk=nblk, kwin=kwin, q_buf=q_buf, dil=dil,
                  rows_of=rows_of):
            r = n // nblk
            i = n % nblk
            k0 = jnp.clip(i * B_QBLK - half, 0, length - kwin)
            q_rows = rows_of(r + dil * B_QBLK * i, B_QBLK)
            k_rows = rows_of(r + dil * k0, kwin)
            q = q_buf[q_rows, :].astype(BF16)
            k = kf[k_rows, :].astype(BF16)
            v = vf[k_rows, :].astype(BF16)
            case = 0 if nblk == 1 else jnp.where(i == 0, 0, jnp.where(i == nblk - 1, 2, 1))
            s = _dot_nt(q, k) + bias_s[case, :, :kwin]
            m_b = jnp.max(s, axis=-1, keepdims=True)
            p = jnp.exp2(s - m_b)
            l_b = jnp.sum(p, axis=-1, keepdims=True)
            a_b = _dot(p.astype(BF16), v)
            full = (B_QBLK, HEAD_DIM)
            if first:
                m_s[q_rows, :] = jnp.broadcast_to(m_b, full)
                l_s[q_rows, :] = jnp.broadcast_to(l_b, full)
                acc_s[q_rows, :] = a_b
            else:
                m_o = m_s[q_rows, :]
                m_n = jnp.maximum(m_o, m_b)
                w_o = jnp.exp2(m_o - m_n)
                w_b = jnp.exp2(m_b - m_n)
                acc_s[q_rows, :] = acc_s[q_rows, :] * w_o + a_b * w_b
                l_s[q_rows, :] = l_s[q_rows, :] * w_o + l_b * w_b
                m_s[q_rows, :] = m_n
            return carry

        lax.fori_loop(0, seq // B_QBLK, block, 0, unroll=B_UNROLL)

    def finish(ci, carry):
        rows = pl.ds(pl.multiple_of(ci * B_PREP_ROWS, B_PREP_ROWS), B_PREP_ROWS)
        o_ref[rows, :] = (acc_s[rows, :] * (1.0 / l_s[rows, :])).astype(o_ref.dtype)
        return carry

    lax.fori_loop(0, seq // B_PREP_ROWS, finish, 0)


def _attn_b(proj, tabs, batch, seq):
    cos, slo, shi = tabs

    def q_spec(g):
        return pl.BlockSpec((seq, HEAD_DIM), lambda b, h: (b, CB_BQ + g * B_HEADS + h))

    tab_spec = pl.BlockSpec((seq, LANES), lambda b, h: (0, 0))
    slab = pltpu.VMEM((seq, HEAD_DIM), F32)
    return pl.pallas_call(
        functools.partial(_attn_b_kernel, seq=seq),
        grid=(batch, B_HEADS),
        in_specs=[
            q_spec(0), q_spec(1), q_spec(2),
            pl.BlockSpec((seq, HEAD_DIM), lambda b, h: (b, CB_BK + h)),
            pl.BlockSpec((seq, HEAD_DIM), lambda b, h: (b, CB_BV + h)),
            tab_spec, tab_spec, tab_spec,
        ],
        out_specs=pl.BlockSpec((seq, HEAD_DIM), lambda b, h: (b, h)),
        out_shape=jax.ShapeDtypeStruct((batch * seq, B_WIDTH), BF16),
        scratch_shapes=[slab] * 8 + [pltpu.VMEM((3, B_QBLK, 2 * B_QBLK), F32)],
        compiler_params=_params(("parallel", "parallel"), "attn_b"),
        name="attn_b",
    )(proj, proj, proj, proj, proj, cos, slo, shi)


def _out_kernel(ya_ref, yb_ref, yc_ref, ga_ref, gb_ref, gc_ref, w_ref, x_ref, gf_ref, x1_ref, xn_ref):
    y = jnp.concatenate([
        _rms(ya_ref[...].astype(F32), ga_ref[...]).astype(BF16),
        _rms(yb_ref[...].astype(F32), gb_ref[...]).astype(BF16),
        _rms(yc_ref[...].astype(F32), gc_ref[...]).astype(BF16),
    ], axis=1)
    x1 = x_ref[...] + _dot(y, w_ref[...])
    x1_ref[...] = x1
    xn_ref[...] = _rms(x1, gf_ref[...]).astype(xn_ref.dtype)


def _out_proj(ya, yb, yc, ga, gb, gc, w, x, gf, tm=TILE["out_proj"]):
    m, d = x.shape

    def rows(width):
        return pl.BlockSpec((tm, width), lambda i: (i, 0))

    def const(r, c):
        return pl.BlockSpec((r, c), lambda i: (0, 0))

    return pl.pallas_call(
        _out_kernel,
        grid=(m // tm,),
        in_specs=[rows(A_WIDTH), rows(B_WIDTH), rows(C_WIDTH), const(1, A_WIDTH), const(1, B_WIDTH),
                  const(1, C_WIDTH), const(d, d), rows(d), const(1, d)],
        out_specs=[rows(d), rows(d)],
        out_shape=[jax.ShapeDtypeStruct((m, d), F32), jax.ShapeDtypeStruct((m, d), BF16)],
        compiler_params=_params(("parallel",), "out_proj"),
        name="out_proj",
    )(ya, yb, yc, ga, gb, gc, w, x, gf)


FFN_SUBTILE = 256
FFN_ROWS = 1024
FFN_GATE_ROWS = 64
FFN_GUARD = 8

def _ffn_up_kernel(xn_ref, wg_ref, wu_ref, cg_ref, cu_ref, bg_ref, bu_ref, wd_ref, o_ref, wd_bf_ref,
                   hg, hu, wg_bf, wu_bf, *, seq):
    tn = o_ref.shape[1]
    slabs = FFN_SUBTILE // LANES
    wd_bf_ref[...] = wd_ref[...].astype(BF16)

    @pl.when(pl.program_id(1) == 0)
    def _():
        wg_bf[...] = wg_ref[...].astype(BF16)
        wu_bf[...] = wu_ref[...].astype(BF16)

    zeros = jnp.zeros((FFN_GUARD, LANES), F32)
    for h in (hg, hu):
        for s in range(tn // LANES):
            h[s, 0:FFN_GUARD, :] = zeros
            h[s, FFN_GUARD + seq:2 * FFN_GUARD + seq, :] = zeros

    def matmuls(c, r):
        cols = slice(c * FFN_SUBTILE, (c + 1) * FFN_SUBTILE)
        xr = xn_ref[r * FFN_ROWS:(r + 1) * FFN_ROWS, :]
        dst = slice(FFN_GUARD + r * FFN_ROWS, FFN_GUARD + (r + 1) * FFN_ROWS)
        for h, w in ((hg, wg_bf), (hu, wu_bf)):
            res = _dot(xr, w[:, cols])
            for s in range(slabs):
                h[c * slabs + s, dst, :] = res[:, s * LANES:(s + 1) * LANES]

    def gate_stage(c, r):
        for k in range(FFN_ROWS // FFN_GATE_ROWS):
            row0 = r * FFN_ROWS + k * FFN_GATE_ROWS
            base = FFN_GUARD + row0
            acts = []
            for s in range(slabs):
                slab = c * slabs + s
                lanes = slice(slab * LANES, (slab + 1) * LANES)

                def tap(h, t):
                    if t == 1:
                        return h[slab, base:base + FFN_GATE_ROWS, :]
                    return h[pl.ds(slab, 1, stride=2), pl.ds(base + t - 1, FFN_GATE_ROWS), :][0]

                def conv(h, c_ref, b_ref):
                    taps = [tap(h, t) * c_ref[t:t + 1, lanes] for t in range(3)]
                    return taps[0] + taps[1] + taps[2] + b_ref[:, lanes]

                gate = conv(hg, cg_ref, bg_ref)
                up = conv(hu, cu_ref, bu_ref)
                acts.append(gate * (1.0 / (1.0 + jnp.exp2(gate * -LOG2E))) * up)
            o_ref[row0:row0 + FFN_GATE_ROWS, c * FFN_SUBTILE:(c + 1) * FFN_SUBTILE] = (
                jnp.concatenate(acts, axis=1).astype(o_ref.dtype))

    units = [(c, r) for c in range(tn // FFN_SUBTILE) for r in range(seq // FFN_ROWS)]
    for i, unit in enumerate(units):
        matmuls(*unit)
        if i > 0:
            gate_stage(*units[i - 1])
    gate_stage(*units[-1])


def _ffn_up(xn, w_up, w_down, layer, conv_w, conv_b, batch, seq, tn=TILE["ffn_up"]):
    d = xn.shape[1]
    nj = D_FF // tn
    wd_rows = w_down.shape[1] // (nj * batch)
    return pl.pallas_call(
        functools.partial(_ffn_up_kernel, seq=seq),
        grid=(nj, batch),
        in_specs=[
            pl.BlockSpec((seq, d), lambda j, b: (b, 0)),
            pl.BlockSpec((None, d, tn), lambda j, b: (layer, 0, j)),
            pl.BlockSpec((None, d, tn), lambda j, b: (layer, 0, nj + j)),
            pl.BlockSpec((3, tn), lambda j, b: (0, j)),
            pl.BlockSpec((3, tn), lambda j, b: (0, nj + j)),
            pl.BlockSpec((1, tn), lambda j, b: (0, j)),
            pl.BlockSpec((1, tn), lambda j, b: (0, nj + j)),
            pl.BlockSpec((None, wd_rows, d), lambda j, b: (layer, j * batch + b, 0)),
        ],
        out_specs=[pl.BlockSpec((seq, tn), lambda j, b: (b, j)),
                   pl.BlockSpec((wd_rows, d), lambda j, b: (j * batch + b, 0))],
        out_shape=[jax.ShapeDtypeStruct((batch * seq, D_FF), BF16),
                   jax.ShapeDtypeStruct(w_down.shape[1:], BF16)],
        scratch_shapes=[pltpu.VMEM((tn // LANES, seq + 2 * FFN_GUARD, LANES), F32)] * 2
        + [pltpu.VMEM((d, tn), BF16)] * 2,
        compiler_params=_params(("parallel", "arbitrary"), "ffn_up"),
        name="ffn_up",
    )(xn, w_up, w_up, conv_w, conv_w, conv_b, conv_b, w_down)


def _ffn_down_kernel(a_ref, w_ref, x1_ref, g_ref, x2_ref, xn_ref):
    x2 = x1_ref[...] + _dot(a_ref[...], w_ref[...])
    x2_ref[...] = x2
    xn_ref[...] = _rms(x2, g_ref[...]).astype(xn_ref.dtype)


def _ffn_down_final_kernel(a_ref, w_ref, x1_ref, g_ref, xn_ref):
    x2 = x1_ref[...] + _dot(a_ref[...], w_ref[...])
    xn_ref[...] = _rms(x2, g_ref[...]).astype(xn_ref.dtype)


def _ffn_down(act, w, x1, g, final, tm=TILE["ffn_down"]):
    m, kdim = act.shape
    d = w.shape[1]
    tile = pl.BlockSpec((tm, d), lambda i: (i, 0))
    in_specs = [
        pl.BlockSpec((tm, kdim), lambda i: (i, 0)),
        pl.BlockSpec((kdim, d), lambda i: (0, 0), pipeline_mode=pl.Buffered(1)),
        tile,
        pl.BlockSpec((1, d), lambda i: (0, 0)),
    ]
    common = dict(grid=(m // tm,), in_specs=in_specs, compiler_params=_params(("parallel",), "ffn_down"))
    if final:
        xn = pl.pallas_call(_ffn_down_final_kernel, out_specs=tile, out_shape=jax.ShapeDtypeStruct((m, d), F32),
                            name="ffn_down_final", **common)(act, w, x1, g)
        return None, xn
    return pl.pallas_call(_ffn_down_kernel, out_specs=[tile, tile],
                          out_shape=[jax.ShapeDtypeStruct((m, d), F32), jax.ShapeDtypeStruct((m, d), BF16)],
                          name="ffn_down", **common)(act, w, x1, g)


def _rope_table(pos, dim, theta):
    f32 = np.float32
    inv = f32(theta) ** (-np.arange(0, dim, 2, dtype=f32) / f32(dim))
    ang = (pos.astype(f32)[:, None] * inv[None, :]).astype(np.float64)
    return np.cos(ang).astype(f32), np.sin(ang).astype(f32)


def _lane_tables(seq):
    t = np.arange(seq)
    z = lambda w: np.zeros((seq, w), np.float32)
    one = lambda w: np.ones((seq, w), np.float32)
    cat = lambda parts: jnp.asarray(np.concatenate(parts, 1))
    cr, sr = _rope_table(t // GRID_W, HEAD_DIM // 2, A_ROPE_THETA)
    cc, sc = _rope_table(t % GRID_W, HEAD_DIM // 2, A_ROPE_THETA)
    tab_a = (cat([cr, cr, cc, cc]), cat([-sr, z(32), -sc, z(32)]), cat([z(32), sr, z(32), sc]))
    cp, sp = _rope_table(t, PARTIAL_ROPE_DIM, PARTIAL_ROPE_THETA)
    tab_b = (cat([cp, cp, one(96)]), cat([-sp, z(112)]), cat([z(16), sp, z(96)]))
    cm, sm = _rope_table(t, C_ROPE_DIM, C_ROPE_THETA)
    tab_c = (cat([cm, cm, one(64)]), cat([-sm, z(96)]), cat([z(32), sm, z(64)]))
    return tab_a, tab_b, tab_c


def _prep_w_ckr(w_in_t, layer):
    ckr = w_in_t[layer, IN_N_SRC * IN_WBLK:, :]
    return jnp.pad(ckr, ((0, IN_WBLK - ckr.shape[0]), (0, 0)))


def _prep_w_uq(w):
    w = w.reshape(C_Q_RANK, C_HEADS, C_NOPE_DIM + C_ROPE_DIM)
    w = jnp.pad(w, ((0, 0), (0, 0), (0, C_QK_PAD - C_NOPE_DIM - C_ROPE_DIM)))
    return w.reshape(C_Q_RANK, C_HEADS * C_QK_PAD).astype(BF16)


def _prep_w_ukv(w):
    w = w.reshape(C_KV_RANK, C_HEADS, C_NOPE_DIM + C_V_DIM)
    kn = w[:, :, :C_NOPE_DIM].reshape(C_KV_RANK, C_HEADS * C_NOPE_DIM)
    v = w[:, :, C_NOPE_DIM:].reshape(C_KV_RANK, C_HEADS * C_V_DIM)
    return jnp.concatenate([kn, v], axis=1).astype(BF16)


def kernel(x, attn_norm, w_in, a_q_norm, a_k_norm, c_q_norm, c_kv_norm, w_uq, w_ukv, out_norm, w_out,
           ffn_norm, w_up, conv_w, conv_b, w_down, final_norm):
    batch, seq, d = x.shape
    depth = w_in.shape[0]
    tab_a, tab_b, tab_c = _lane_tables(seq)
    row = lambda v: v.reshape(1, -1)

    xr = x.reshape(batch * seq, d)
    xn = _rmsnorm_rows(xr, row(attn_norm[0]))
    w_in_t = jnp.swapaxes(w_in, 1, 2)
    for l in range(depth):
        proj = _in_proj(xn, w_in_t, l, _prep_w_ckr(w_in_t, l))
        qc, kvc = _latent_up(proj, row(c_q_norm[l]), row(c_kv_norm[l]), _prep_w_uq(w_uq[l]),
                             _prep_w_ukv(w_ukv[l]))
        ya, w_out_bf = _attn_a(proj, tab_a, row(a_q_norm[l]), row(a_k_norm[l]), w_out, l, batch, seq)
        yb = _attn_b(proj, tab_b, batch, seq)
        yc = _attn_c(qc, kvc, proj, tab_c, batch, seq)
        g = out_norm[l]
        x1, xn1 = _out_proj(ya, yb, yc, row(g[:A_WIDTH]), row(g[A_WIDTH:A_WIDTH + B_WIDTH]),
                            row(g[A_WIDTH + B_WIDTH:]), w_out_bf, xr, row(ffn_norm[l]))
        act, w_down_bf = _ffn_up(xn1, w_up, w_down, l, conv_w[l], row(conv_b[l]), batch, seq)
        last = l == depth - 1
        g_next = final_norm if last else attn_norm[l + 1]
        xr, xn = _ffn_down(act, w_down_bf, x1, row(g_next), last)
    return xn.reshape(batch, seq, d)
```

```python
import functools
import math

import jax
import jax.numpy as jnp
import numpy as np
from jax import lax
from jax.experimental import pallas as pl
from jax.experimental.pallas import tpu as pltpu

D_MODEL = 2048
HEAD_DIM = 128
A_HEADS = 6
A_KV_HEADS = 2
A_GROUP = A_HEADS // A_KV_HEADS
A_ROPE_THETA = 10000.0
B_HEADS = 4
B_PATTERNS = ((128, 1), (512, 4), (2048, 16))
B_N_GROUPS = 3
C_HEADS = 6
C_Q_RANK = 512
C_KV_RANK = 512
C_NOPE_DIM = 128
C_ROPE_DIM = 64
C_V_DIM = 128
C_ROPE_THETA = 10000.0
PARTIAL_ROPE_DIM = HEAD_DIM // 4
PARTIAL_ROPE_THETA = 500000.0
GRID_W = 64
D_FF = 5632
EPS = 1e-6

A_WIDTH = A_HEADS * HEAD_DIM
B_WIDTH = B_HEADS * HEAD_DIM
C_WIDTH = C_HEADS * C_V_DIM
C_QK_PAD = 256

LANES = 128
LOG2E = math.log2(math.e)
F32 = jnp.float32
BF16 = jnp.bfloat16

PROJ_WIDTH = 5120
CB_CQ, CB_CKV = 0, 1
CB_AQ, CB_AK, CB_AV = 8, 14, 16
CB_BQ, CB_BK, CB_BV = 18, 30, 34
CB_CKR = 38


TILE = dict(rmsnorm=512, in_proj=2048, latent_up=2048, attn_a=1024, attn_c=2048, out_proj=512, ffn_up=512,
            ffn_down=512)
VMEM_MIB = dict(rmsnorm=40, in_proj=60, latent_up=48, attn_a=48, attn_b=48, attn_c=48, out_proj=56, ffn_up=60,
                ffn_down=60)


def _params(semantics, call):
    return pltpu.CompilerParams(dimension_semantics=semantics, vmem_limit_bytes=VMEM_MIB[call] * 1024 * 1024)


def _rms(xf, g):
    return xf * lax.rsqrt(jnp.mean(xf * xf, axis=-1, keepdims=True) + EPS) * g


def _rope(xf, cos, sin_lo, sin_hi, shift):
    return xf * cos + pltpu.roll(xf, LANES - shift, 1) * sin_lo + pltpu.roll(xf, shift, 1) * sin_hi


def _dot(a, b):
    return jnp.dot(a, b, preferred_element_type=F32)


def _dot_nt(a, b):
    return lax.dot_general(a, b, (((1,), (1,)), ((), ())), preferred_element_type=F32)


def _norm_kernel(x_ref, g_ref, o_ref):
    o_ref[...] = _rms(x_ref[...], g_ref[...]).astype(o_ref.dtype)


def _rmsnorm_rows(x, g, tm=TILE["rmsnorm"]):
    m, d = x.shape
    return pl.pallas_call(
        _norm_kernel,
        grid=(m // tm,),
        in_specs=[pl.BlockSpec((tm, d), lambda i: (i, 0)), pl.BlockSpec((1, d), lambda i: (0, 0))],
        out_specs=pl.BlockSpec((tm, d), lambda i: (i, 0)),
        out_shape=jax.ShapeDtypeStruct((m, d), BF16),
        compiler_params=_params(("parallel",), "rmsnorm"),
        name="rmsnorm",
    )(x, g)


IN_WBLK = 256
IN_TN = 1024
IN_SRC_CQ = 15
IN_N_SRC = 19


def _in_proj_kernel(a_ref, *refs):
    w_refs, wck_ref, o_ref, w_bf = refs[:-3], refs[-3], refs[-2], refs[-1]
    j = pl.program_id(0)
    last = pl.num_programs(0) - 1

    @pl.when(pl.program_id(1) == 0)
    def _():
        for q, w_ref in enumerate(w_refs[:-1]):
            w_bf[q * IN_WBLK:(q + 1) * IN_WBLK, :] = w_ref[...].astype(BF16)
        tail = slice((len(w_refs) - 1) * IN_WBLK, len(w_refs) * IN_WBLK)

        @pl.when(j < last)
        def _():
            w_bf[tail, :] = w_refs[-1][...].astype(BF16)

        @pl.when(j == last)
        def _():
            w_bf[tail, :] = wck_ref[...].astype(BF16)

    o_ref[...] = _dot_nt(a_ref[...], w_bf[...]).astype(o_ref.dtype)


def _in_proj(a, w_in_t, layer, w_ckr_t, tm=TILE["in_proj"]):
    m, k = a.shape
    per_step = IN_TN // IN_WBLK

    def w_spec(q):
        def index(j, i):
            n = j * per_step + q
            return layer, jnp.where(n < 4, n + IN_SRC_CQ, n - 4), 0
        return pl.BlockSpec((None, IN_WBLK, k), index)

    return pl.pallas_call(
        _in_proj_kernel,
        grid=(PROJ_WIDTH // IN_TN, m // tm),
        in_specs=[pl.BlockSpec((tm, k), lambda j, i: (i, 0))] + [w_spec(q) for q in range(per_step)]
        + [pl.BlockSpec((IN_WBLK, k), lambda j, i: (0, 0))],
        out_specs=pl.BlockSpec((tm, IN_TN), lambda j, i: (i, j)),
        out_shape=jax.ShapeDtypeStruct((m, PROJ_WIDTH), BF16),
        scratch_shapes=[pltpu.VMEM((IN_TN, k), BF16)],
        compiler_params=_params(("parallel", "arbitrary"), "in_proj"),
        name="in_proj",
    )(a, *([w_in_t] * per_step), w_ckr_t)


def _latent_up_kernel(cq_ref, ckv_ref, gq_ref, gkv_ref, wq_ref, wkv_ref, q_ref, kv_ref, *, q_scale):
    cq = _rms(cq_ref[...].astype(F32), gq_ref[...]).astype(BF16)
    q_ref[...] = (_dot(cq, wq_ref[...]) * q_scale).astype(q_ref.dtype)
    ckv = _rms(ckv_ref[...].astype(F32), gkv_ref[...]).astype(BF16)
    kv_ref[...] = _dot(ckv, wkv_ref[...]).astype(kv_ref.dtype)


def _latent_up(proj, gq, gkv, wq, wkv, tm=TILE["latent_up"]):
    m = proj.shape[0]
    nq, nkv = wq.shape[1], wkv.shape[1]
    q_scale = (C_NOPE_DIM + C_ROPE_DIM) ** -0.5 * LOG2E
    return pl.pallas_call(
        functools.partial(_latent_up_kernel, q_scale=q_scale),
        grid=(m // tm,),
        in_specs=[
            pl.BlockSpec((tm, C_Q_RANK), lambda i: (i, CB_CQ)),
            pl.BlockSpec((tm, C_KV_RANK), lambda i: (i, CB_CKV)),
            pl.BlockSpec((1, C_Q_RANK), lambda i: (0, 0)),
            pl.BlockSpec((1, C_KV_RANK), lambda i: (0, 0)),
            pl.BlockSpec((C_Q_RANK, nq), lambda i: (0, 0)),
            pl.BlockSpec((C_KV_RANK, nkv), lambda i: (0, 0)),
        ],
        out_specs=[pl.BlockSpec((tm, nq), lambda i: (i, 0)), pl.BlockSpec((tm, nkv), lambda i: (i, 0))],
        out_shape=[jax.ShapeDtypeStruct((m, nq), BF16), jax.ShapeDtypeStruct((m, nkv), BF16)],
        compiler_params=_params(("parallel",), "latent_up"),
        name="latent_up",
    )(proj, proj, gq, gkv, wq, wkv)


A_ROW_CHUNK = 256
C_ROW_CHUNK = 256


def _attend(q_all, k_ref, v1_ref, dv, chunk):
    outs = []
    for c in range(q_all.shape[0] // chunk):
        q = q_all[c * chunk:(c + 1) * chunk]
        s = _dot_nt(q, k_ref[...])
        p = jnp.exp2(s - jnp.max(s, axis=-1, keepdims=True))
        ol = _dot(p.astype(BF16), v1_ref[...])
        outs.append(ol[:, :dv] * (1.0 / ol[:, dv:]))
    return outs


def _attn_a_kernel(q0_ref, q1_ref, q2_ref, k_ref, v_ref, cos_ref, slo_ref, shi_ref, gq_ref, gk_ref, wo_ref,
                   o_ref, wo_bf_ref, kbuf, v1buf, *, tq):
    qi = pl.program_id(2)
    wo_bf_ref[...] = wo_ref[...].astype(BF16)

    @pl.when(qi == 0)
    def _():
        k = _rms(k_ref[...].astype(F32), gk_ref[...])
        kbuf[...] = _rope(k, cos_ref[...], slo_ref[...], shi_ref[...], 32).astype(BF16)
        v1buf[:, :HEAD_DIM] = v_ref[...]
        v1buf[:, HEAD_DIM:] = jnp.ones((v1buf.shape[0], HEAD_DIM), BF16)

    rows = pl.ds(pl.multiple_of(qi * tq, tq), tq)
    cos, slo, shi = cos_ref[rows, :], slo_ref[rows, :], shi_ref[rows, :]
    q_scale = HEAD_DIM ** -0.5 * LOG2E
    qs = []
    for q_ref in (q0_ref, q1_ref, q2_ref):
        q = _rms(q_ref[...].astype(F32), gq_ref[...])
        qs.append((_rope(q, cos, slo, shi, 32) * q_scale).astype(BF16))
    q_all = jnp.concatenate(qs, axis=0)
    per_head = tq // A_ROW_CHUNK
    for c, o in enumerate(_attend(q_all, kbuf, v1buf, HEAD_DIM, A_ROW_CHUNK)):
        g, r = divmod(c, per_head)
        o_ref[r * A_ROW_CHUNK:(r + 1) * A_ROW_CHUNK, g * HEAD_DIM:(g + 1) * HEAD_DIM] = o.astype(o_ref.dtype)


def _attn_a(proj, tabs, gq, gk, w_out, layer, batch, seq, tq=TILE["attn_a"]):
    nq = seq // tq
    cos, slo, shi = tabs
    d = w_out.shape[2]
    wo_rows = w_out.shape[1] // (batch * A_KV_HEADS * nq)

    def wo_index(b, h, qi):
        return (b * A_KV_HEADS + h) * nq + qi

    def q_spec(g):
        return pl.BlockSpec((tq, HEAD_DIM), lambda b, h, qi: (b * nq + qi, CB_AQ + h * A_GROUP + g))

    tab_spec = pl.BlockSpec((seq, LANES), lambda b, h, qi: (0, 0))
    gain_spec = pl.BlockSpec((1, HEAD_DIM), lambda b, h, qi: (0, 0))
    return pl.pallas_call(
        functools.partial(_attn_a_kernel, tq=tq),
        grid=(batch, A_KV_HEADS, nq),
        in_specs=[
            q_spec(0), q_spec(1), q_spec(2),
            pl.BlockSpec((seq, HEAD_DIM), lambda b, h, qi: (b, CB_AK + h)),
            pl.BlockSpec((seq, HEAD_DIM), lambda b, h, qi: (b, CB_AV + h)),
            tab_spec, tab_spec, tab_spec, gain_spec, gain_spec,
            pl.BlockSpec((None, wo_rows, d), lambda b, h, qi: (layer, wo_index(b, h, qi), 0)),
        ],
        out_specs=[pl.BlockSpec((tq, A_GROUP * HEAD_DIM), lambda b, h, qi: (b * nq + qi, h)),
                   pl.BlockSpec((wo_rows, d), lambda b, h, qi: (wo_index(b, h, qi), 0))],
        out_shape=[jax.ShapeDtypeStruct((batch * seq, A_WIDTH), BF16),
                   jax.ShapeDtypeStruct(w_out.shape[1:], BF16)],
        scratch_shapes=[pltpu.VMEM((seq, HEAD_DIM), BF16), pltpu.VMEM((seq, 2 * HEAD_DIM), BF16)],
        compiler_params=_params(("parallel", "parallel", "arbitrary"), "attn_a"),
        name="attn_a",
    )(proj, proj, proj, proj, proj, cos, slo, shi, gq, gk, w_out)


def _attn_c_kernel(q_ref, kn_ref, v_ref, kr_ref, cos_ref, slo_ref, shi_ref, o_ref, kbuf, v1buf, *, tq):
    qi = pl.program_id(2)

    @pl.when(qi == 0)
    def _():
        kbuf[:, :C_NOPE_DIM] = kn_ref[...]
        kr = _rope(kr_ref[...].astype(F32), cos_ref[...], slo_ref[...], shi_ref[...], 32)
        kbuf[:, C_NOPE_DIM:] = kr.astype(BF16)
        v1buf[:, :C_V_DIM] = v_ref[...]
        v1buf[:, C_V_DIM:] = jnp.ones((v1buf.shape[0], C_V_DIM), BF16)

    rows = pl.ds(pl.multiple_of(qi * tq, tq), tq)
    q_rope = _rope(q_ref[:, C_NOPE_DIM:].astype(F32), cos_ref[rows, :], slo_ref[rows, :], shi_ref[rows, :], 32)
    q_all = jnp.concatenate([q_ref[:, :C_NOPE_DIM], q_rope.astype(BF16)], axis=1)
    for c, o in enumerate(_attend(q_all, kbuf, v1buf, C_V_DIM, C_ROW_CHUNK)):
        o_ref[c * C_ROW_CHUNK:(c + 1) * C_ROW_CHUNK, :] = o.astype(o_ref.dtype)


def _attn_c(qc, kvc, proj, tabs, batch, seq, tq=TILE["attn_c"]):
    nq = seq // tq
    cos, slo, shi = tabs
    tab_spec = pl.BlockSpec((seq, LANES), lambda b, h, qi: (0, 0))
    return pl.pallas_call(
        functools.partial(_attn_c_kernel, tq=tq),
        grid=(batch, C_HEADS, nq),
        in_specs=[
            pl.BlockSpec((tq, C_QK_PAD), lambda b, h, qi: (b * nq + qi, h)),
            pl.BlockSpec((seq, C_NOPE_DIM), lambda b, h, qi: (b, h)),
            pl.BlockSpec((seq, C_V_DIM), lambda b, h, qi: (b, C_HEADS + h)),
            pl.BlockSpec((seq, LANES), lambda b, h, qi: (b, CB_CKR)),
            tab_spec, tab_spec, tab_spec,
        ],
        out_specs=pl.BlockSpec((tq, C_V_DIM), lambda b, h, qi: (b * nq + qi, h)),
        out_shape=jax.ShapeDtypeStruct((batch * seq, C_WIDTH), BF16),
        scratch_shapes=[pltpu.VMEM((seq, C_QK_PAD), BF16), pltpu.VMEM((seq, 2 * C_V_DIM), BF16)],
        compiler_params=_params(("parallel", "parallel", "arbitrary"), "attn_c"),
        name="attn_c",
    )(qc, kvc, kvc, proj, cos, slo, shi)


B_QBLK = 128
(B_HALF,) = {w // (2 * d) for w, d in B_PATTERNS}
assert 2 * B_HALF == B_QBLK
B_PREP_ROWS = 256
B_UNROLL = 16


def _attn_b_kernel(q0_ref, q1_ref, q2_ref, k_ref, v_ref, cos_ref, slo_ref, shi_ref, o_ref,
                   q0f, q1f, q2f, kf, vf, m_s, l_s, acc_s, bias_s, *, seq):
    q_refs = (q0_ref, q1_ref, q2_ref)
    q_bufs = (q0f, q1f, q2f)
    q_scale = HEAD_DIM ** -0.5 * LOG2E

    half = B_HALF
    shape = (B_QBLK, 2 * B_QBLK)
    rel = lax.broadcasted_iota(jnp.int32, shape, 1) - lax.broadcasted_iota(jnp.int32, shape, 0)
    for case, offset in enumerate((0, -half, -B_QBLK)):
        bias_s[case] = jnp.where(jnp.abs(rel + offset) <= half, 0.0, -jnp.inf)

    def prep(ci, carry):
        rows = pl.ds(pl.multiple_of(ci * B_PREP_ROWS, B_PREP_ROWS), B_PREP_ROWS)
        cos, slo, shi = cos_ref[rows, :], slo_ref[rows, :], shi_ref[rows, :]
        for q_ref, q_buf in zip(q_refs, q_bufs):
            q_buf[rows, :] = _rope(q_ref[rows, :].astype(F32), cos, slo, shi, 16) * q_scale
        kf[rows, :] = _rope(k_ref[rows, :].astype(F32), cos, slo, shi, 16)
        vf[rows, :] = v_ref[rows, :].astype(F32)
        return carry

    lax.fori_loop(0, seq // B_PREP_ROWS, prep, 0, unroll=4)

    for g, (window, dil) in reversed(list(enumerate(B_PATTERNS))):
        first = g == len(B_PATTERNS) - 1
        half = window // (2 * dil)
        length = seq // dil
        nblk = length // B_QBLK
        kwin = min(2 * B_QBLK, length)
        q_buf = q_bufs[g]

        def rows_of(start, size, dil=dil):
            return pl.ds(start, size) if dil == 1 else pl.ds(start, size, stride=dil)

        def block(n, carry, first=first, half=half, length=length, nblk=nblk, kwin=kwin, q_buf=q_buf, dil=dil,
                  rows_of=rows_of):
            r = n // nblk
            i = n % nblk
            k0 = jnp.clip(i * B_QBLK - half, 0, length - kwin)
            q_rows = rows_of(r + dil * B_QBLK * i, B_QBLK)
            k_rows = rows_of(r + dil * k0, kwin)
            q = q_buf[q_rows, :].astype(BF16)
            k = kf[k_rows, :].astype(BF16)
            v = vf[k_rows, :].astype(BF16)
            case = 0 if nblk == 1 else jnp.where(i == 0, 0, jnp.where(i == nblk - 1, 2, 1))
            s = _dot_nt(q, k) + bias_s[case, :, :kwin]
            m_b = jnp.max(s, axis=-1, keepdims=True)
            p = jnp.exp2(s - m_b)
            l_b = jnp.sum(p, axis=-1, keepdims=True)
            a_b = _dot(p.astype(BF16), v)
            full = (B_QBLK, HEAD_DIM)
            if first:
                m_s[q_rows, :] = jnp.broadcast_to(m_b, full)
                l_s[q_rows, :] = jnp.broadcast_to(l_b, full)
                acc_s[q_rows, :] = a_b
            else:
                m_o = m_s[q_rows, :]
                m_n = jnp.maximum(m_o, m_b)
                w_o = jnp.exp2(m_o - m_n)
                w_b = jnp.exp2(m_b - m_n)
                acc_s[q_rows, :] = acc_s[q_rows, :] * w_o + a_b * w_b
                l_s[q_rows, :] = l_s[q_rows, :] * w_o + l_b * w_b
                m_s[q_rows, :] = m_n
            return carry

        lax.fori_loop(0, seq // B_QBLK, block, 0, unroll=B_UNROLL)

    def finish(ci, carry):
        rows = pl.ds(pl.multiple_of(ci * B_PREP_ROWS, B_PREP_ROWS), B_PREP_ROWS)
        o_ref[rows, :] = (acc_s[rows, :] * (1.0 / l_s[rows, :])).astype(o_ref.dtype)
        return carry

    lax.fori_loop(0, seq // B_PREP_ROWS, finish, 0)


def _attn_b(proj, tabs, batch, seq):
    cos, slo, shi = tabs

    def q_spec(g):
        return pl.BlockSpec((seq, HEAD_DIM), lambda b, h: (b, CB_BQ + g * B_HEADS + h))

    tab_spec = pl.BlockSpec((seq, LANES), lambda b, h: (0, 0))
    slab = pltpu.VMEM((seq, HEAD_DIM), F32)
    return pl.pallas_call(
        functools.partial(_attn_b_kernel, seq=seq),
        grid=(batch, B_HEADS),
        in_specs=[
            q_spec(0), q_spec(1), q_spec(2),
            pl.BlockSpec((seq, HEAD_DIM), lambda b, h: (b, CB_BK + h)),
            pl.BlockSpec((seq, HEAD_DIM), lambda b, h: (b, CB_BV + h)),
            tab_spec, tab_spec, tab_spec,
        ],
        out_specs=pl.BlockSpec((seq, HEAD_DIM), lambda b, h: (b, h)),
        out_shape=jax.ShapeDtypeStruct((batch * seq, B_WIDTH), BF16),
        scratch_shapes=[slab] * 8 + [pltpu.VMEM((3, B_QBLK, 2 * B_QBLK), F32)],
        compiler_params=_params(("parallel", "parallel"), "attn_b"),
        name="attn_b",
    )(proj, proj, proj, proj, proj, cos, slo, shi)


def _out_kernel(ya_ref, yb_ref, yc_ref, ga_ref, gb_ref, gc_ref, w_ref, x_ref, gf_ref, x1_ref, xn_ref):
    y = jnp.concatenate([
        _rms(ya_ref[...].astype(F32), ga_ref[...]).astype(BF16),
        _rms(yb_ref[...].astype(F32), gb_ref[...]).astype(BF16),
        _rms(yc_ref[...].astype(F32), gc_ref[...]).astype(BF16),
    ], axis=1)
    x1 = x_ref[...] + _dot(y, w_ref[...])
    x1_ref[...] = x1
    xn_ref[...] = _rms(x1, gf_ref[...]).astype(xn_ref.dtype)


def _out_proj(ya, yb, yc, ga, gb, gc, w, x, gf, tm=TILE["out_proj"]):
    m, d = x.shape

    def rows(width):
        return pl.BlockSpec((tm, width), lambda i: (i, 0))

    def const(r, c):
        return pl.BlockSpec((r, c), lambda i: (0, 0))

    return pl.pallas_call(
        _out_kernel,
        grid=(m // tm,),
        in_specs=[rows(A_WIDTH), rows(B_WIDTH), rows(C_WIDTH), const(1, A_WIDTH), const(1, B_WIDTH),
                  const(1, C_WIDTH), const(d, d), rows(d), const(1, d)],
        out_specs=[rows(d), rows(d)],
        out_shape=[jax.ShapeDtypeStruct((m, d), F32), jax.ShapeDtypeStruct((m, d), BF16)],
        compiler_params=_params(("parallel",), "out_proj"),
        name="out_proj",
    )(ya, yb, yc, ga, gb, gc, w, x, gf)


FFN_SUBTILE = 256
FFN_ROWS = 1024
FFN_GATE_ROWS = 128
FFN_GUARD = 8

def _ffn_up_kernel(xn_ref, wg_ref, wu_ref, cg_ref, cu_ref, bg_ref, bu_ref, wd_ref, o_ref, wd_bf_ref,
                   hg, hu, wg_bf, wu_bf, *, seq):
    tn = o_ref.shape[1]
    slabs = FFN_SUBTILE // LANES
    wd_bf_ref[...] = wd_ref[...].astype(BF16)

    @pl.when(pl.program_id(1) == 0)
    def _():
        wg_bf[...] = wg_ref[...].astype(BF16)
        wu_bf[...] = wu_ref[...].astype(BF16)

    zeros = jnp.zeros((FFN_GUARD, LANES), F32)
    for h in (hg, hu):
        for s in range(tn // LANES):
            h[s, 0:FFN_GUARD, :] = zeros
            h[s, FFN_GUARD + seq:2 * FFN_GUARD + seq, :] = zeros

    def matmuls(c, r):
        cols = slice(c * FFN_SUBTILE, (c + 1) * FFN_SUBTILE)
        xr = xn_ref[r * FFN_ROWS:(r + 1) * FFN_ROWS, :]
        dst = slice(FFN_GUARD + r * FFN_ROWS, FFN_GUARD + (r + 1) * FFN_ROWS)
        for h, w in ((hg, wg_bf), (hu, wu_bf)):
            res = _dot(xr, w[:, cols])
            for s in range(slabs):
                h[c * slabs + s, dst, :] = res[:, s * LANES:(s + 1) * LANES]

    def gate_stage(c, r):
        for k in range(FFN_ROWS // FFN_GATE_ROWS):
            row0 = r * FFN_ROWS + k * FFN_GATE_ROWS
            base = FFN_GUARD + row0
            acts = []
            for s in range(slabs):
                slab = c * slabs + s
                lanes = slice(slab * LANES, (slab + 1) * LANES)

                def tap(h, t):
                    if t == 1:
                        return h[slab, base:base + FFN_GATE_ROWS, :]
                    return h[pl.ds(slab, 1, stride=2), pl.ds(base + t - 1, FFN_GATE_ROWS), :][0]

                def conv(h, c_ref, b_ref):
                    taps = [tap(h, t) * c_ref[t:t + 1, lanes] for t in range(3)]
                    return taps[0] + taps[1] + taps[2] + b_ref[:, lanes]

                gate = conv(hg, cg_ref, bg_ref)
                up = conv(hu, cu_ref, bu_ref)
                acts.append(gate * (1.0 / (1.0 + jnp.exp2(gate * -LOG2E))) * up)
            o_ref[row0:row0 + FFN_GATE_ROWS, c * FFN_SUBTILE:(c + 1) * FFN_SUBTILE] = (
                jnp.concatenate(acts, axis=1).astype(o_ref.dtype))

    units = [(c, r) for c in range(tn // FFN_SUBTILE) for r in range(seq // FFN_ROWS)]
    for i, unit in enumerate(units):
        matmuls(*unit)
        if i > 0:
            gate_stage(*units[i - 1])
    gate_stage(*units[-1])


def _ffn_up(xn, w_up, w_down, layer, conv_w, conv_b, batch, seq, tn=TILE["ffn_up"]):
    d = xn.shape[1]
    nj = D_FF // tn
    wd_rows = w_down.shape[1] // (nj * batch)
    return pl.pallas_call(
        functools.partial(_ffn_up_kernel, seq=seq),
        grid=(nj, batch),
        in_specs=[
            pl.BlockSpec((seq, d), lambda j, b: (b, 0)),
            pl.BlockSpec((None, d, tn), lambda j, b: (layer, 0, j)),
            pl.BlockSpec((None, d, tn), lambda j, b: (layer, 0, nj + j)),
            pl.BlockSpec((3, tn), lambda j, b: (0, j)),
            pl.BlockSpec((3, tn), lambda j, b: (0, nj + j)),
            pl.BlockSpec((1, tn), lambda j, b: (0, j)),
            pl.BlockSpec((1, tn), lambda j, b: (0, nj + j)),
            pl.BlockSpec((None, wd_rows, d), lambda j, b: (layer, j * batch + b, 0)),
        ],
        out_specs=[pl.BlockSpec((seq, tn), lambda j, b: (b, j)),
                   pl.BlockSpec((wd_rows, d), lambda j, b: (j * batch + b, 0))],
        out_shape=[jax.ShapeDtypeStruct((batch * seq, D_FF), BF16),
                   jax.ShapeDtypeStruct(w_down.shape[1:], BF16)],
        scratch_shapes=[pltpu.VMEM((tn // LANES, seq + 2 * FFN_GUARD, LANES), F32)] * 2
        + [pltpu.VMEM((d, tn), BF16)] * 2,
        compiler_params=_params(("parallel", "arbitrary"), "ffn_up"),
        name="ffn_up",
    )(xn, w_up, w_up, conv_w, conv_w, conv_b, conv_b, w_down)


def _ffn_down_kernel(a_ref, w_ref, x1_ref, g_ref, x2_ref, xn_ref):
    x2 = x1_ref[...] + _dot(a_ref[...], w_ref[...])
    x2_ref[...] = x2
    xn_ref[...] = _rms(x2, g_ref[...]).astype(xn_ref.dtype)


def _ffn_down_final_kernel(a_ref, w_ref, x1_ref, g_ref, xn_ref):
    x2 = x1_ref[...] + _dot(a_ref[...], w_ref[...])
    xn_ref[...] = _rms(x2, g_ref[...]).astype(xn_ref.dtype)


def _ffn_down(act, w, x1, g, final, tm=TILE["ffn_down"]):
    m, kdim = act.shape
    d = w.shape[1]
    tile = pl.BlockSpec((tm, d), lambda i: (i, 0))
    in_specs = [
        pl.BlockSpec((tm, kdim), lambda i: (i, 0)),
        pl.BlockSpec((kdim, d), lambda i: (0, 0), pipeline_mode=pl.Buffered(1)),
        tile,
        pl.BlockSpec((1, d), lambda i: (0, 0)),
    ]
    common = dict(grid=(m // tm,), in_specs=in_specs, compiler_params=_params(("parallel",), "ffn_down"))
    if final:
        xn = pl.pallas_call(_ffn_down_final_kernel, out_specs=tile, out_shape=jax.ShapeDtypeStruct((m, d), F32),
                            name="ffn_down_final", **common)(act, w, x1, g)
        return None, xn
    return pl.pallas_call(_ffn_down_kernel, out_specs=[tile, tile],
                          out_shape=[jax.ShapeDtypeStruct((m, d), F32), jax.ShapeDtypeStruct((m, d), BF16)],
                          name="ffn_down", **common)(act, w, x1, g)


def _rope_table(pos, dim, theta):
    f32 = np.float32
    inv = f32(theta) ** (-np.arange(0, dim, 2, dtype=f32) / f32(dim))
    ang = (pos.astype(f32)[:, None] * inv[None, :]).astype(np.float64)
    return np.cos(ang).astype(f32), np.sin(ang).astype(f32)


def _lane_tables(seq):
    t = np.arange(seq)
    z = lambda w: np.zeros((seq, w), np.float32)
    one = lambda w: np.ones((seq, w), np.float32)
    cat = lambda parts: jnp.asarray(np.concatenate(parts, 1))
    cr, sr = _rope_table(t // GRID_W, HEAD_DIM // 2, A_ROPE_THETA)
    cc, sc = _rope_table(t % GRID_W, HEAD_DIM // 2, A_ROPE_THETA)
    tab_a = (cat([cr, cr, cc, cc]), cat([-sr, z(32), -sc, z(32)]), cat([z(32), sr, z(32), sc]))
    cp, sp = _rope_table(t, PARTIAL_ROPE_DIM, PARTIAL_ROPE_THETA)
    tab_b = (cat([cp, cp, one(96)]), cat([-sp, z(112)]), cat([z(16), sp, z(96)]))
    cm, sm = _rope_table(t, C_ROPE_DIM, C_ROPE_THETA)
    tab_c = (cat([cm, cm, one(64)]), cat([-sm, z(96)]), cat([z(32), sm, z(64)]))
    return tab_a, tab_b, tab_c


def _prep_w_ckr(w_in_t, layer):
    ckr = w_in_t[layer, IN_N_SRC * IN_WBLK:, :]
    return jnp.pad(ckr, ((0, IN_WBLK - ckr.shape[0]), (0, 0)))


def _prep_w_uq(w):
    w = w.reshape(C_Q_RANK, C_HEADS, C_NOPE_DIM + C_ROPE_DIM)
    w = jnp.pad(w, ((0, 0), (0, 0), (0, C_QK_PAD - C_NOPE_DIM - C_ROPE_DIM)))
    return w.reshape(C_Q_RANK, C_HEADS * C_QK_PAD).astype(BF16)


def _prep_w_ukv(w):
    w = w.reshape(C_KV_RANK, C_HEADS, C_NOPE_DIM + C_V_DIM)
    kn = w[:, :, :C_NOPE_DIM].reshape(C_KV_RANK, C_HEADS * C_NOPE_DIM)
    v = w[:, :, C_NOPE_DIM:].reshape(C_KV_RANK, C_HEADS * C_V_DIM)
    return jnp.concatenate([kn, v], axis=1).astype(BF16)


def kernel(x, attn_norm, w_in, a_q_norm, a_k_norm, c_q_norm, c_kv_norm, w_uq, w_ukv, out_norm, w_out,
           ffn_norm, w_up, conv_w, conv_b, w_down, final_norm):
    batch, seq, d = x.shape
    depth = w_in.shape[0]
    tab_a, tab_b, tab_c = _lane_tables(seq)
    row = lambda v: v.reshape(1, -1)

    xr = x.reshape(batch * seq, d)
    xn = _rmsnorm_rows(xr, row(attn_norm[0]))
    w_in_t = jnp.swapaxes(w_in, 1, 2)
    for l in range(depth):
        proj = _in_proj(xn, w_in_t, l, _prep_w_ckr(w_in_t, l))
        qc, kvc = _latent_up(proj, row(c_q_norm[l]), row(c_kv_norm[l]), _prep_w_uq(w_uq[l]),
                             _prep_w_ukv(w_ukv[l]))
        ya, w_out_bf = _attn_a(proj, tab_a, row(a_q_norm[l]), row(a_k_norm[l]), w_out, l, batch, seq)
        yb = _attn_b(proj, tab_b, batch, seq)
        yc = _attn_c(qc, kvc, proj, tab_c, batch, seq)
        g = out_norm[l]
        x1, xn1 = _out_proj(ya, yb, yc, row(g[:A_WIDTH]), row(g[A_WIDTH:A_WIDTH + B_WIDTH]),
                            row(g[A_WIDTH + B_WIDTH:]), w_out_bf, xr, row(ffn_norm[l]))
        act, w_down_bf = _ffn_up(xn1, w_up, w_down, l, conv_w[l], row(conv_b[l]), batch, seq)
        last = l == depth - 1
        g_next = final_norm if last else attn_norm[l + 1]
        xr, xn = _ffn_down(act, w_down_bf, x1, row(g_next), last)
    return xn.reshape(batch, seq, d)
```

```python
import functools
import math

import jax
import jax.numpy as jnp
import numpy as np
from jax import lax
from jax.experimental import pallas as pl
from jax.experimental.pallas import tpu as pltpu

D_MODEL = 2048
HEAD_DIM = 128
A_HEADS = 6
A_KV_HEADS = 2
A_GROUP = A_HEADS // A_KV_HEADS
A_ROPE_THETA = 10000.0
B_HEADS = 4
B_PATTERNS = ((128, 1), (512, 4), (2048, 16))
B_N_GROUPS = 3
C_HEADS = 6
C_Q_RANK = 512
C_KV_RANK = 512
C_NOPE_DIM = 128
C_ROPE_DIM = 64
C_V_DIM = 128
C_ROPE_THETA = 10000.0
PARTIAL_ROPE_DIM = HEAD_DIM // 4
PARTIAL_ROPE_THETA = 500000.0
GRID_W = 64
D_FF = 5632
EPS = 1e-6

A_WIDTH = A_HEADS * HEAD_DIM
B_WIDTH = B_HEADS * HEAD_DIM
C_WIDTH = C_HEADS * C_V_DIM
C_QK_PAD = 256

LANES = 128
LOG2E = math.log2(math.e)
F32 = jnp.float32
BF16 = jnp.bfloat16

PROJ_WIDTH = 5120
CB_CQ, CB_CKV = 0, 1
CB_AQ, CB_AK, CB_AV = 8, 14, 16
CB_BQ, CB_BK, CB_BV = 18, 30, 34
CB_CKR = 38


TILE = dict(rmsnorm=512, in_proj=2048, latent_up=2048, attn_a=1024, attn_c=2048, out_proj=512, ffn_up=512,
            ffn_down=512)
VMEM_MIB = dict(rmsnorm=40, in_proj=60, latent_up=48, attn_a=48, attn_b=48, attn_c=48, out_proj=56, ffn_up=60,
                ffn_down=60)


def _params(semantics, call):
    return pltpu.CompilerParams(dimension_semantics=semantics, vmem_limit_bytes=VMEM_MIB[call] * 1024 * 1024)


def _rms(xf, g):
    return xf * lax.rsqrt(jnp.mean(xf * xf, axis=-1, keepdims=True) + EPS) * g


def _rope(xf, cos, sin_lo, sin_hi, shift):
    return xf * cos + pltpu.roll(xf, LANES - shift, 1) * sin_lo + pltpu.roll(xf, shift, 1) * sin_hi


def _dot(a, b):
    return jnp.dot(a, b, preferred_element_type=F32)


def _dot_nt(a, b):
    return lax.dot_general(a, b, (((1,), (1,)), ((), ())), preferred_element_type=F32)


def _norm_kernel(x_ref, g_ref, o_ref):
    o_ref[...] = _rms(x_ref[...], g_ref[...]).astype(o_ref.dtype)


def _rmsnorm_rows(x, g, tm=TILE["rmsnorm"]):
    m, d = x.shape
    return pl.pallas_call(
        _norm_kernel,
        grid=(m // tm,),
        in_specs=[pl.BlockSpec((tm, d), lambda i: (i, 0)), pl.BlockSpec((1, d), lambda i: (0, 0))],
        out_specs=pl.BlockSpec((tm, d), lambda i: (i, 0)),
        out_shape=jax.ShapeDtypeStruct((m, d), BF16),
        compiler_params=_params(("parallel",), "rmsnorm"),
        name="rmsnorm",
    )(x, g)


IN_WBLK = 256
IN_TN = 1024
IN_SRC_CQ = 15
IN_N_SRC = 19


def _in_proj_kernel(a_ref, *refs):
    w_refs, wck_ref, o_ref, w_bf = refs[:-3], refs[-3], refs[-2], refs[-1]
    j = pl.program_id(0)
    last = pl.num_programs(0) - 1

    @pl.when(pl.program_id(1) == 0)
    def _():
        for q, w_ref in enumerate(w_refs[:-1]):
            w_bf[q * IN_WBLK:(q + 1) * IN_WBLK, :] = w_ref[...].astype(BF16)
        tail = slice((len(w_refs) - 1) * IN_WBLK, len(w_refs) * IN_WBLK)

        @pl.when(j < last)
        def _():
            w_bf[tail, :] = w_refs[-1][...].astype(BF16)

        @pl.when(j == last)
        def _():
            w_bf[tail, :] = wck_ref[...].astype(BF16)

    o_ref[...] = _dot_nt(a_ref[...], w_bf[...]).astype(o_ref.dtype)


def _in_proj(a, w_in_t, layer, w_ckr_t, tm=TILE["in_proj"]):
    m, k = a.shape
    per_step = IN_TN // IN_WBLK

    def w_spec(q):
        def index(j, i):
            n = j * per_step + q
            return layer, jnp.where(n < 4, n + IN_SRC_CQ, n - 4), 0
        return pl.BlockSpec((None, IN_WBLK, k), index)

    return pl.pallas_call(
        _in_proj_kernel,
        grid=(PROJ_WIDTH // IN_TN, m // tm),
        in_specs=[pl.BlockSpec((tm, k), lambda j, i: (i, 0))] + [w_spec(q) for q in range(per_step)]
        + [pl.BlockSpec((IN_WBLK, k), lambda j, i: (0, 0))],
        out_specs=pl.BlockSpec((tm, IN_TN), lambda j, i: (i, j)),
        out_shape=jax.ShapeDtypeStruct((m, PROJ_WIDTH), BF16),
        scratch_shapes=[pltpu.VMEM((IN_TN, k), BF16)],
        compiler_params=_params(("parallel", "arbitrary"), "in_proj"),
        name="in_proj",
    )(a, *([w_in_t] * per_step), w_ckr_t)


def _latent_up_kernel(cq_ref, ckv_ref, gq_ref, gkv_ref, wq_ref, wkv_ref, q_ref, kv_ref, *, q_scale):
    cq = _rms(cq_ref[...].astype(F32), gq_ref[...]).astype(BF16)
    q_ref[...] = (_dot(cq, wq_ref[...]) * q_scale).astype(q_ref.dtype)
    ckv = _rms(ckv_ref[...].astype(F32), gkv_ref[...]).astype(BF16)
    kv_ref[...] = _dot(ckv, wkv_ref[...]).astype(kv_ref.dtype)


def _latent_up(proj, gq, gkv, wq, wkv, tm=TILE["latent_up"]):
    m = proj.shape[0]
    nq, nkv = wq.shape[1], wkv.shape[1]
    q_scale = (C_NOPE_DIM + C_ROPE_DIM) ** -0.5 * LOG2E
    return pl.pallas_call(
        functools.partial(_latent_up_kernel, q_scale=q_scale),
        grid=(m // tm,),
        in_specs=[
            pl.BlockSpec((tm, C_Q_RANK), lambda i: (i, CB_CQ)),
            pl.BlockSpec((tm, C_KV_RANK), lambda i: (i, CB_CKV)),
            pl.BlockSpec((1, C_Q_RANK), lambda i: (0, 0)),
            pl.BlockSpec((1, C_KV_RANK), lambda i: (0, 0)),
            pl.BlockSpec((C_Q_RANK, nq), lambda i: (0, 0)),
            pl.BlockSpec((C_KV_RANK, nkv), lambda i: (0, 0)),
        ],
        out_specs=[pl.BlockSpec((tm, nq), lambda i: (i, 0)), pl.BlockSpec((tm, nkv), lambda i: (i, 0))],
        out_shape=[jax.ShapeDtypeStruct((m, nq), BF16), jax.ShapeDtypeStruct((m, nkv), BF16)],
        compiler_params=_params(("parallel",), "latent_up"),
        name="latent_up",
    )(proj, proj, gq, gkv, wq, wkv)


A_ROW_CHUNK = 256
C_ROW_CHUNK = 256


def _attend(q_all, k_ref, v1_ref, dv, chunk):
    outs = []
    for c in range(q_all.shape[0] // chunk):
        q = q_all[c * chunk:(c + 1) * chunk]
        s = _dot_nt(q, k_ref[...])
        p = jnp.exp2(s - jnp.max(s, axis=-1, keepdims=True))
        ol = _dot(p.astype(BF16), v1_ref[...])
        outs.append(ol[:, :dv] * (1.0 / ol[:, dv:]))
    return outs


def _attn_a_kernel(q0_ref, q1_ref, q2_ref, k_ref, v_ref, cos_ref, slo_ref, shi_ref, gq_ref, gk_ref, wo_ref,
                   o_ref, wo_bf_ref, kbuf, v1buf, *, tq):
    qi = pl.program_id(2)
    wo_bf_ref[...] = wo_ref[...].astype(BF16)

    @pl.when(qi == 0)
    def _():
        k = _rms(k_ref[...].astype(F32), gk_ref[...])
        kbuf[...] = _rope(k, cos_ref[...], slo_ref[...], shi_ref[...], 32).astype(BF16)
        v1buf[:, :HEAD_DIM] = v_ref[...]
        v1buf[:, HEAD_DIM:] = jnp.ones((v1buf.shape[0], HEAD_DIM), BF16)

    rows = pl.ds(pl.multiple_of(qi * tq, tq), tq)
    cos, slo, shi = cos_ref[rows, :], slo_ref[rows, :], shi_ref[rows, :]
    q_scale = HEAD_DIM ** -0.5 * LOG2E
    qs = []
    for q_ref in (q0_ref, q1_ref, q2_ref):
        q = _rms(q_ref[...].astype(F32), gq_ref[...])
        qs.append((_rope(q, cos, slo, shi, 32) * q_scale).astype(BF16))
    q_all = jnp.concatenate(qs, axis=0)
    per_head = tq // A_ROW_CHUNK
    for c, o in enumerate(_attend(q_all, kbuf, v1buf, HEAD_DIM, A_ROW_CHUNK)):
        g, r = divmod(c, per_head)
        o_ref[r * A_ROW_CHUNK:(r + 1) * A_ROW_CHUNK, g * HEAD_DIM:(g + 1) * HEAD_DIM] = o.astype(o_ref.dtype)


def _attn_a(proj, tabs, gq, gk, w_out, layer, batch, seq, tq=TILE["attn_a"]):
    nq = seq // tq
    cos, slo, shi = tabs
    d = w_out.shape[2]
    wo_rows = w_out.shape[1] // (batch * A_KV_HEADS * nq)

    def wo_index(b, h, qi):
        return (b * A_KV_HEADS + h) * nq + qi

    def q_spec(g):
        return pl.BlockSpec((tq, HEAD_DIM), lambda b, h, qi: (b * nq + qi, CB_AQ + h * A_GROUP + g))

    tab_spec = pl.BlockSpec((seq, LANES), lambda b, h, qi: (0, 0))
    gain_spec = pl.BlockSpec((1, HEAD_DIM), lambda b, h, qi: (0, 0))
    return pl.pallas_call(
        functools.partial(_attn_a_kernel, tq=tq),
        grid=(batch, A_KV_HEADS, nq),
        in_specs=[
            q_spec(0), q_spec(1), q_spec(2),
            pl.BlockSpec((seq, HEAD_DIM), lambda b, h, qi: (b, CB_AK + h)),
            pl.BlockSpec((seq, HEAD_DIM), lambda b, h, qi: (b, CB_AV + h)),
            tab_spec, tab_spec, tab_spec, gain_spec, gain_spec,
            pl.BlockSpec((None, wo_rows, d), lambda b, h, qi: (layer, wo_index(b, h, qi), 0)),
        ],
        out_specs=[pl.BlockSpec((tq, A_GROUP * HEAD_DIM), lambda b, h, qi: (b * nq + qi, h)),
                   pl.BlockSpec((wo_rows, d), lambda b, h, qi: (wo_index(b, h, qi), 0))],
        out_shape=[jax.ShapeDtypeStruct((batch * seq, A_WIDTH), BF16),
                   jax.ShapeDtypeStruct(w_out.shape[1:], BF16)],
        scratch_shapes=[pltpu.VMEM((seq, HEAD_DIM), BF16), pltpu.VMEM((seq, 2 * HEAD_DIM), BF16)],
        compiler_params=_params(("parallel", "parallel", "arbitrary"), "attn_a"),
        name="attn_a",
    )(proj, proj, proj, proj, proj, cos, slo, shi, gq, gk, w_out)


def _attn_c_kernel(q_ref, kn_ref, v_ref, kr_ref, cos_ref, slo_ref, shi_ref, o_ref, kbuf, v1buf, *, tq):
    h, qi = pl.program_id(1), pl.program_id(2)

    @pl.when((h == 0) & (qi == 0))
    def _():
        kr = _rope(kr_ref[...].astype(F32), cos_ref[...], slo_ref[...], shi_ref[...], 32)
        kbuf[:, C_NOPE_DIM:] = kr.astype(BF16)
        v1buf[:, C_V_DIM:] = jnp.ones((v1buf.shape[0], C_V_DIM), BF16)

    @pl.when(qi == 0)
    def _():
        kbuf[:, :C_NOPE_DIM] = kn_ref[...]
        v1buf[:, :C_V_DIM] = v_ref[...]

    rows = pl.ds(pl.multiple_of(qi * tq, tq), tq)
    q_rope = _rope(q_ref[:, C_NOPE_DIM:].astype(F32), cos_ref[rows, :], slo_ref[rows, :], shi_ref[rows, :], 32)
    q_all = jnp.concatenate([q_ref[:, :C_NOPE_DIM], q_rope.astype(BF16)], axis=1)
    for c, o in enumerate(_attend(q_all, kbuf, v1buf, C_V_DIM, C_ROW_CHUNK)):
        o_ref[c * C_ROW_CHUNK:(c + 1) * C_ROW_CHUNK, :] = o.astype(o_ref.dtype)


def _attn_c(qc, kvc, proj, tabs, batch, seq, tq=TILE["attn_c"]):
    nq = seq // tq
    cos, slo, shi = tabs
    tab_spec = pl.BlockSpec((seq, LANES), lambda b, h, qi: (0, 0))
    return pl.pallas_call(
        functools.partial(_attn_c_kernel, tq=tq),
        grid=(batch, C_HEADS, nq),
        in_specs=[
            pl.BlockSpec((tq, C_QK_PAD), lambda b, h, qi: (b * nq + qi, h)),
            pl.BlockSpec((seq, C_NOPE_DIM), lambda b, h, qi: (b, h)),
            pl.BlockSpec((seq, C_V_DIM), lambda b, h, qi: (b, C_HEADS + h)),
            pl.BlockSpec((seq, LANES), lambda b, h, qi: (b, CB_CKR)),
            tab_spec, tab_spec, tab_spec,
        ],
        out_specs=pl.BlockSpec((tq, C_V_DIM), lambda b, h, qi: (b * nq + qi, h)),
        out_shape=jax.ShapeDtypeStruct((batch * seq, C_WIDTH), BF16),
        scratch_shapes=[pltpu.VMEM((seq, C_QK_PAD), BF16), pltpu.VMEM((seq, 2 * C_V_DIM), BF16)],
        compiler_params=_params(("parallel", "arbitrary", "arbitrary"), "attn_c"),
        name="attn_c",
    )(qc, kvc, kvc, proj, cos, slo, shi)


B_QBLK = 128
(B_HALF,) = {w // (2 * d) for w, d in B_PATTERNS}
assert 2 * B_HALF == B_QBLK
B_PREP_ROWS = 256
B_UNROLL = 16


def _attn_b_kernel(q0_ref, q1_ref, q2_ref, k_ref, v_ref, cos_ref, slo_ref, shi_ref, o_ref,
                   q0f, q1f, q2f, kf, vf, m_s, l_s, acc_s, bias_s, *, seq):
    q_refs = (q0_ref, q1_ref, q2_ref)
    q_bufs = (q0f, q1f, q2f)
    q_scale = HEAD_DIM ** -0.5 * LOG2E

    half = B_HALF
    shape = (B_QBLK, 2 * B_QBLK)
    rel = lax.broadcasted_iota(jnp.int32, shape, 1) - lax.broadcasted_iota(jnp.int32, shape, 0)
    for case, offset in enumerate((0, -half, -B_QBLK)):
        bias_s[case] = jnp.where(jnp.abs(rel + offset) <= half, 0.0, -jnp.inf)

    def prep(ci, carry):
        rows = pl.ds(pl.multiple_of(ci * B_PREP_ROWS, B_PREP_ROWS), B_PREP_ROWS)
        cos, slo, shi = cos_ref[rows, :], slo_ref[rows, :], shi_ref[rows, :]
        for q_ref, q_buf in zip(q_refs, q_bufs):
            q_buf[rows, :] = _rope(q_ref[rows, :].astype(F32), cos, slo, shi, 16) * q_scale
        kf[rows, :] = _rope(k_ref[rows, :].astype(F32), cos, slo, shi, 16)
        vf[rows, :] = v_ref[rows, :].astype(F32)
        return carry

    lax.fori_loop(0, seq // B_PREP_ROWS, prep, 0, unroll=4)

    for g, (window, dil) in reversed(list(enumerate(B_PATTERNS))):
        first = g == len(B_PATTERNS) - 1
        half = window // (2 * dil)
        length = seq // dil
        nblk = length // B_QBLK
        kwin = min(2 * B_QBLK, length)
        q_buf = q_bufs[g]

        def rows_of(start, size, dil=dil):
            return pl.ds(start, size) if dil == 1 else pl.ds(start, size, stride=dil)

        def block(n, carry, first=first, half=half, length=length, nblk=nblk, kwin=kwin, q_buf=q_buf, dil=dil,
                  rows_of=rows_of):
            r = n // nblk
            i = n % nblk
            k0 = jnp.clip(i * B_QBLK - half, 0, length - kwin)
            q_rows = rows_of(r + dil * B_QBLK * i, B_QBLK)
            k_rows = rows_of(r + dil * k0, kwin)
            q = q_buf[q_rows, :].astype(BF16)
            k = kf[k_rows, :].astype(BF16)
            v = vf[k_rows, :].astype(BF16)
            case = 0 if nblk == 1 else jnp.where(i == 0, 0, jnp.where(i == nblk - 1, 2, 1))
            s = _dot_nt(q, k) + bias_s[case, :, :kwin]
            m_b = jnp.max(s, axis=-1, keepdims=True)
            p = jnp.exp2(s - m_b)
            l_b = jnp.sum(p, axis=-1, keepdims=True)
            a_b = _dot(p.astype(BF16), v)
            full = (B_QBLK, HEAD_DIM)
            if first:
                m_s[q_rows, :] = jnp.broadcast_to(m_b, full)
                l_s[q_rows, :] = jnp.broadcast_to(l_b, full)
                acc_s[q_rows, :] = a_b
            else:
                m_o = m_s[q_rows, :]
                m_n = jnp.maximum(m_o, m_b)
                w_o = jnp.exp2(m_o - m_n)
                w_b = jnp.exp2(m_b - m_n)
                acc_s[q_rows, :] = acc_s[q_rows, :] * w_o + a_b * w_b
                l_s[q_rows, :] = l_s[q_rows, :] * w_o + l_b * w_b
                m_s[q_rows, :] = m_n
            return carry

        lax.fori_loop(0, seq // B_QBLK, block, 0, unroll=B_UNROLL)

    def finish(ci, carry):
        rows = pl.ds(pl.multiple_of(ci * B_PREP_ROWS, B_PREP_ROWS), B_PREP_ROWS)
        o_ref[rows, :] = (acc_s[rows, :] * (1.0 / l_s[rows, :])).astype(o_ref.dtype)
        return carry

    lax.fori_loop(0, seq // B_PREP_ROWS, finish, 0)


def _attn_b(proj, tabs, batch, seq):
    cos, slo, shi = tabs

    def q_spec(g):
        return pl.BlockSpec((seq, HEAD_DIM), lambda b, h: (b, CB_BQ + g * B_HEADS + h))

    tab_spec = pl.BlockSpec((seq, LANES), lambda b, h: (0, 0))
    slab = pltpu.VMEM((seq, HEAD_DIM), F32)
    return pl.pallas_call(
        functools.partial(_attn_b_kernel, seq=seq),
        grid=(batch, B_HEADS),
        in_specs=[
            q_spec(0), q_spec(1), q_spec(2),
            pl.BlockSpec((seq, HEAD_DIM), lambda b, h: (b, CB_BK + h)),
            pl.BlockSpec((seq, HEAD_DIM), lambda b, h: (b, CB_BV + h)),
            tab_spec, tab_spec, tab_spec,
        ],
        out_specs=pl.BlockSpec((seq, HEAD_DIM), lambda b, h: (b, h)),
        out_shape=jax.ShapeDtypeStruct((batch * seq, B_WIDTH), BF16),
        scratch_shapes=[slab] * 8 + [pltpu.VMEM((3, B_QBLK, 2 * B_QBLK), F32)],
        compiler_params=_params(("parallel", "parallel"), "attn_b"),
        name="attn_b",
    )(proj, proj, proj, proj, proj, cos, slo, shi)


def _out_kernel(ya_ref, yb_ref, yc_ref, ga_ref, gb_ref, gc_ref, w_ref, x_ref, gf_ref, x1_ref, xn_ref):
    y = jnp.concatenate([
        _rms(ya_ref[...].astype(F32), ga_ref[...]).astype(BF16),
        _rms(yb_ref[...].astype(F32), gb_ref[...]).astype(BF16),
        _rms(yc_ref[...].astype(F32), gc_ref[...]).astype(BF16),
    ], axis=1)
    x1 = x_ref[...] + _dot(y, w_ref[...])
    x1_ref[...] = x1
    xn_ref[...] = _rms(x1, gf_ref[...]).astype(xn_ref.dtype)


def _out_proj(ya, yb, yc, ga, gb, gc, w, x, gf, tm=TILE["out_proj"]):
    m, d = x.shape

    def rows(width):
        return pl.BlockSpec((tm, width), lambda i: (i, 0))

    def const(r, c):
        return pl.BlockSpec((r, c), lambda i: (0, 0))

    return pl.pallas_call(
        _out_kernel,
        grid=(m // tm,),
        in_specs=[rows(A_WIDTH), rows(B_WIDTH), rows(C_WIDTH), const(1, A_WIDTH), const(1, B_WIDTH),
                  const(1, C_WIDTH), const(d, d), rows(d), const(1, d)],
        out_specs=[rows(d), rows(d)],
        out_shape=[jax.ShapeDtypeStruct((m, d), F32), jax.ShapeDtypeStruct((m, d), BF16)],
        compiler_params=_params(("parallel",), "out_proj"),
        name="out_proj",
    )(ya, yb, yc, ga, gb, gc, w, x, gf)


FFN_SUBTILE = 256
FFN_ROWS = 1024
FFN_GATE_ROWS = 128
FFN_GUARD = 8

def _ffn_up_kernel(xn_ref, wg_ref, wu_ref, cg_ref, cu_ref, bg_ref, bu_ref, wd_ref, o_ref, wd_bf_ref,
                   hg, hu, wg_bf, wu_bf, *, seq):
    tn = o_ref.shape[1]
    slabs = FFN_SUBTILE // LANES
    wd_bf_ref[...] = wd_ref[...].astype(BF16)

    @pl.when(pl.program_id(1) == 0)
    def _():
        wg_bf[...] = wg_ref[...].astype(BF16)
        wu_bf[...] = wu_ref[...].astype(BF16)

    zeros = jnp.zeros((FFN_GUARD, LANES), F32)
    for h in (hg, hu):
        for s in range(tn // LANES):
            h[s, 0:FFN_GUARD, :] = zeros
            h[s, FFN_GUARD + seq:2 * FFN_GUARD + seq, :] = zeros

    def matmuls(c, r):
        cols = slice(c * FFN_SUBTILE, (c + 1) * FFN_SUBTILE)
        xr = xn_ref[r * FFN_ROWS:(r + 1) * FFN_ROWS, :]
        dst = slice(FFN_GUARD + r * FFN_ROWS, FFN_GUARD + (r + 1) * FFN_ROWS)
        for h, w in ((hg, wg_bf), (hu, wu_bf)):
            res = _dot(xr, w[:, cols])
            for s in range(slabs):
                h[c * slabs + s, dst, :] = res[:, s * LANES:(s + 1) * LANES]

    def gate_stage(c, r):
        for k in range(FFN_ROWS // FFN_GATE_ROWS):
            row0 = r * FFN_ROWS + k * FFN_GATE_ROWS
            base = FFN_GUARD + row0
            acts = []
            for s in range(slabs):
                slab = c * slabs + s
                lanes = slice(slab * LANES, (slab + 1) * LANES)

                def tap(h, t):
                    if t == 1:
                        return h[slab, base:base + FFN_GATE_ROWS, :]
                    return h[pl.ds(slab, 1, stride=2), pl.ds(base + t - 1, FFN_GATE_ROWS), :][0]

                def conv(h, c_ref, b_ref):
                    taps = [tap(h, t) * c_ref[t:t + 1, lanes] for t in range(3)]
                    return taps[0] + taps[1] + taps[2] + b_ref[:, lanes]

                gate = conv(hg, cg_ref, bg_ref)
                up = conv(hu, cu_ref, bu_ref)
                acts.append(gate * (1.0 / (1.0 + jnp.exp2(gate * -LOG2E))) * up)
            o_ref[row0:row0 + FFN_GATE_ROWS, c * FFN_SUBTILE:(c + 1) * FFN_SUBTILE] = (
                jnp.concatenate(acts, axis=1).astype(o_ref.dtype))

    units = [(c, r) for c in range(tn // FFN_SUBTILE) for r in range(seq // FFN_ROWS)]
    for i, unit in enumerate(units):
        matmuls(*unit)
        if i > 0:
            gate_stage(*units[i - 1])
    gate_stage(*units[-1])


def _ffn_up(xn, w_up, w_down, layer, conv_w, conv_b, batch, seq, tn=TILE["ffn_up"]):
    d = xn.shape[1]
    nj = D_FF // tn
    wd_rows = w_down.shape[1] // (nj * batch)
    return pl.pallas_call(
        functools.partial(_ffn_up_kernel, seq=seq),
        grid=(nj, batch),
        in_specs=[
            pl.BlockSpec((seq, d), lambda j, b: (b, 0)),
            pl.BlockSpec((None, d, tn), lambda j, b: (layer, 0, j)),
            pl.BlockSpec((None, d, tn), lambda j, b: (layer, 0, nj + j)),
            pl.BlockSpec((3, tn), lambda j, b: (0, j)),
            pl.BlockSpec((3, tn), lambda j, b: (0, nj + j)),
            pl.BlockSpec((1, tn), lambda j, b: (0, j)),
            pl.BlockSpec((1, tn), lambda j, b: (0, nj + j)),
            pl.BlockSpec((None, wd_rows, d), lambda j, b: (layer, j * batch + b, 0)),
        ],
        out_specs=[pl.BlockSpec((seq, tn), lambda j, b: (b, j)),
                   pl.BlockSpec((wd_rows, d), lambda j, b: (j * batch + b, 0))],
        out_shape=[jax.ShapeDtypeStruct((batch * seq, D_FF), BF16),
                   jax.ShapeDtypeStruct(w_down.shape[1:], BF16)],
        scratch_shapes=[pltpu.VMEM((tn // LANES, seq + 2 * FFN_GUARD, LANES), F32)] * 2
        + [pltpu.VMEM((d, tn), BF16)] * 2,
        compiler_params=_params(("parallel", "arbitrary"), "ffn_up"),
        name="ffn_up",
    )(xn, w_up, w_up, conv_w, conv_w, conv_b, conv_b, w_down)


def _ffn_down_kernel(a_ref, w_ref, x1_ref, g_ref, x2_ref, xn_ref):
    x2 = x1_ref[...] + _dot(a_ref[...], w_ref[...])
    x2_ref[...] = x2
    xn_ref[...] = _rms(x2, g_ref[...]).astype(xn_ref.dtype)


def _ffn_down_final_kernel(a_ref, w_ref, x1_ref, g_ref, xn_ref):
    x2 = x1_ref[...] + _dot(a_ref[...], w_ref[...])
    xn_ref[...] = _rms(x2, g_ref[...]).astype(xn_ref.dtype)


def _ffn_down(act, w, x1, g, final, tm=TILE["ffn_down"]):
    m, kdim = act.shape
    d = w.shape[1]
    tile = pl.BlockSpec((tm, d), lambda i: (i, 0))
    in_specs = [
        pl.BlockSpec((tm, kdim), lambda i: (i, 0)),
        pl.BlockSpec((kdim, d), lambda i: (0, 0), pipeline_mode=pl.Buffered(1)),
        tile,
        pl.BlockSpec((1, d), lambda i: (0, 0)),
    ]
    common = dict(grid=(m // tm,), in_specs=in_specs, compiler_params=_params(("parallel",), "ffn_down"))
    if final:
        xn = pl.pallas_call(_ffn_down_final_kernel, out_specs=tile, out_shape=jax.ShapeDtypeStruct((m, d), F32),
                            name="ffn_down_final", **common)(act, w, x1, g)
        return None, xn
    return pl.pallas_call(_ffn_down_kernel, out_specs=[tile, tile],
                          out_shape=[jax.ShapeDtypeStruct((m, d), F32), jax.ShapeDtypeStruct((m, d), BF16)],
                          name="ffn_down", **common)(act, w, x1, g)


def _rope_table(pos, dim, theta):
    f32 = np.float32
    inv = f32(theta) ** (-np.arange(0, dim, 2, dtype=f32) / f32(dim))
    ang = (pos.astype(f32)[:, None] * inv[None, :]).astype(np.float64)
    return np.cos(ang).astype(f32), np.sin(ang).astype(f32)


def _lane_tables(seq):
    t = np.arange(seq)
    z = lambda w: np.zeros((seq, w), np.float32)
    one = lambda w: np.ones((seq, w), np.float32)
    cat = lambda parts: jnp.asarray(np.concatenate(parts, 1))
    cr, sr = _rope_table(t // GRID_W, HEAD_DIM // 2, A_ROPE_THETA)
    cc, sc = _rope_table(t % GRID_W, HEAD_DIM // 2, A_ROPE_THETA)
    tab_a = (cat([cr, cr, cc, cc]), cat([-sr, z(32), -sc, z(32)]), cat([z(32), sr, z(32), sc]))
    cp, sp = _rope_table(t, PARTIAL_ROPE_DIM, PARTIAL_ROPE_THETA)
    tab_b = (cat([cp, cp, one(96)]), cat([-sp, z(112)]), cat([z(16), sp, z(96)]))
    cm, sm = _rope_table(t, C_ROPE_DIM, C_ROPE_THETA)
    tab_c = (cat([cm, cm, one(64)]), cat([-sm, z(96)]), cat([z(32), sm, z(64)]))
    return tab_a, tab_b, tab_c


def _prep_w_ckr(w_in_t, layer):
    ckr = w_in_t[layer, IN_N_SRC * IN_WBLK:, :]
    return jnp.pad(ckr, ((0, IN_WBLK - ckr.shape[0]), (0, 0)))


def _prep_w_uq(w):
    w = w.reshape(C_Q_RANK, C_HEADS, C_NOPE_DIM + C_ROPE_DIM)
    w = jnp.pad(w, ((0, 0), (0, 0), (0, C_QK_PAD - C_NOPE_DIM - C_ROPE_DIM)))
    return w.reshape(C_Q_RANK, C_HEADS * C_QK_PAD).astype(BF16)


def _prep_w_ukv(w):
    w = w.reshape(C_KV_RANK, C_HEADS, C_NOPE_DIM + C_V_DIM)
    kn = w[:, :, :C_NOPE_DIM].reshape(C_KV_RANK, C_HEADS * C_NOPE_DIM)
    v = w[:, :, C_NOPE_DIM:].reshape(C_KV_RANK, C_HEADS * C_V_DIM)
    return jnp.concatenate([kn, v], axis=1).astype(BF16)


def kernel(x, attn_norm, w_in, a_q_norm, a_k_norm, c_q_norm, c_kv_norm, w_uq, w_ukv, out_norm, w_out,
           ffn_norm, w_up, conv_w, conv_b, w_down, final_norm):
    batch, seq, d = x.shape
    depth = w_in.shape[0]
    tab_a, tab_b, tab_c = _lane_tables(seq)
    row = lambda v: v.reshape(1, -1)

    xr = x.reshape(batch * seq, d)
    xn = _rmsnorm_rows(xr, row(attn_norm[0]))
    w_in_t = jnp.swapaxes(w_in, 1, 2)
    for l in range(depth):
        proj = _in_proj(xn, w_in_t, l, _prep_w_ckr(w_in_t, l))
        qc, kvc = _latent_up(proj, row(c_q_norm[l]), row(c_kv_norm[l]), _prep_w_uq(w_uq[l]),
                             _prep_w_ukv(w_ukv[l]))
        ya, w_out_bf = _attn_a(proj, tab_a, row(a_q_norm[l]), row(a_k_norm[l]), w_out, l, batch, seq)
        yb = _attn_b(proj, tab_b, batch, seq)
        yc = _attn_c(qc, kvc, proj, tab_c, batch, seq)
        g = out_norm[l]
        x1, xn1 = _out_proj(ya, yb, yc, row(g[:A_WIDTH]), row(g[A_WIDTH:A_WIDTH + B_WIDTH]),
                            row(g[A_WIDTH + B_WIDTH:]), w_out_bf, xr, row(ffn_norm[l]))
        act, w_down_bf = _ffn_up(xn1, w_up, w_down, l, conv_w[l], row(conv_b[l]), batch, seq)
        last = l == depth - 1
        g_next = final_norm if last else attn_norm[l + 1]
        xr, xn = _ffn_down(act, w_down_bf, x1, row(g_next), last)
    return xn.reshape(batch, seq, d)
```

```python
import functools
import math

import jax
import jax.numpy as jnp
import numpy as np
from jax import lax
from jax.experimental import pallas as pl
from jax.experimental.pallas import tpu as pltpu

D_MODEL = 2048
HEAD_DIM = 128
A_HEADS = 6
A_KV_HEADS = 2
A_GROUP = A_HEADS // A_KV_HEADS
A_ROPE_THETA = 10000.0
B_HEADS = 4
B_PATTERNS = ((128, 1), (512, 4), (2048, 16))
B_N_GROUPS = 3
C_HEADS = 6
C_Q_RANK = 512
C_KV_RANK = 512
C_NOPE_DIM = 128
C_ROPE_DIM = 64
C_V_DIM = 128
C_ROPE_THETA = 10000.0
PARTIAL_ROPE_DIM = HEAD_DIM // 4
PARTIAL_ROPE_THETA = 500000.0
GRID_W = 64
D_FF = 5632
EPS = 1e-6

A_WIDTH = A_HEADS * HEAD_DIM
B_WIDTH = B_HEADS * HEAD_DIM
C_WIDTH = C_HEADS * C_V_DIM
C_QK_PAD = 256

LANES = 128
LOG2E = math.log2(math.e)
F32 = jnp.float32
BF16 = jnp.bfloat16

PROJ_WIDTH = 5120
CB_CQ, CB_CKV = 0, 1
CB_AQ, CB_AK, CB_AV = 8, 14, 16
CB_BQ, CB_BK, CB_BV = 18, 30, 34
CB_CKR = 38


TILE = dict(rmsnorm=512, in_proj=2048, latent_up=2048, attn_a=1024, attn_c=2048, out_proj=512, ffn_up=512,
            ffn_down=512)
VMEM_MIB = dict(rmsnorm=40, in_proj=60, latent_up=48, attn_a=48, attn_b=48, attn_c=48, out_proj=56, ffn_up=60,
                ffn_down=60)


def _params(semantics, call):
    return pltpu.CompilerParams(dimension_semantics=semantics, vmem_limit_bytes=VMEM_MIB[call] * 1024 * 1024)


def _rms(xf, g):
    return xf * lax.rsqrt(jnp.mean(xf * xf, axis=-1, keepdims=True) + EPS) * g


def _rope(xf, cos, sin_lo, sin_hi, shift):
    return xf * cos + pltpu.roll(xf, LANES - shift, 1) * sin_lo + pltpu.roll(xf, shift, 1) * sin_hi


def _dot(a, b):
    return jnp.dot(a, b, preferred_element_type=F32)


def _dot_nt(a, b):
    return lax.dot_general(a, b, (((1,), (1,)), ((), ())), preferred_element_type=F32)


def _norm_kernel(x_ref, g_ref, o_ref):
    o_ref[...] = _rms(x_ref[...], g_ref[...]).astype(o_ref.dtype)


def _rmsnorm_rows(x, g, tm=TILE["rmsnorm"]):
    m, d = x.shape
    return pl.pallas_call(
        _norm_kernel,
        grid=(m // tm,),
        in_specs=[pl.BlockSpec((tm, d), lambda i: (i, 0)), pl.BlockSpec((1, d), lambda i: (0, 0))],
        out_specs=pl.BlockSpec((tm, d), lambda i: (i, 0)),
        out_shape=jax.ShapeDtypeStruct((m, d), BF16),
        compiler_params=_params(("parallel",), "rmsnorm"),
        name="rmsnorm",
    )(x, g)


IN_WBLK = 256
IN_TN = 1024
IN_SRC_CQ = 15
IN_N_SRC = 19


def _in_proj_kernel(a_ref, *refs):
    w_refs, wck_ref, o_ref, w_bf = refs[:-3], refs[-3], refs[-2], refs[-1]
    j = pl.program_id(0)
    last = pl.num_programs(0) - 1

    @pl.when(pl.program_id(1) == 0)
    def _():
        for q, w_ref in enumerate(w_refs[:-1]):
            w_bf[q * IN_WBLK:(q + 1) * IN_WBLK, :] = w_ref[...].astype(BF16)
        tail = slice((len(w_refs) - 1) * IN_WBLK, len(w_refs) * IN_WBLK)

        @pl.when(j < last)
        def _():
            w_bf[tail, :] = w_refs[-1][...].astype(BF16)

        @pl.when(j == last)
        def _():
            w_bf[tail, :] = wck_ref[...].astype(BF16)

    o_ref[...] = _dot_nt(a_ref[...], w_bf[...]).astype(o_ref.dtype)


def _in_proj(a, w_in_t, layer, w_ckr_t, tm=TILE["in_proj"]):
    m, k = a.shape
    per_step = IN_TN // IN_WBLK

    def w_spec(q):
        def index(j, i):
            n = j * per_step + q
            return layer, jnp.where(n < 4, n + IN_SRC_CQ, n - 4), 0
        return pl.BlockSpec((None, IN_WBLK, k), index)

    return pl.pallas_call(
        _in_proj_kernel,
        grid=(PROJ_WIDTH // IN_TN, m // tm),
        in_specs=[pl.BlockSpec((tm, k), lambda j, i: (i, 0))] + [w_spec(q) for q in range(per_step)]
        + [pl.BlockSpec((IN_WBLK, k), lambda j, i: (0, 0))],
        out_specs=pl.BlockSpec((tm, IN_TN), lambda j, i: (i, j)),
        out_shape=jax.ShapeDtypeStruct((m, PROJ_WIDTH), BF16),
        scratch_shapes=[pltpu.VMEM((IN_TN, k), BF16)],
        compiler_params=_params(("parallel", "arbitrary"), "in_proj"),
        name="in_proj",
    )(a, *([w_in_t] * per_step), w_ckr_t)


def _latent_up_kernel(cq_ref, ckv_ref, gq_ref, gkv_ref, wq_ref, wkv_ref, q_ref, kv_ref, *, q_scale):
    cq = _rms(cq_ref[...].astype(F32), gq_ref[...]).astype(BF16)
    q_ref[...] = (_dot(cq, wq_ref[...]) * q_scale).astype(q_ref.dtype)
    ckv = _rms(ckv_ref[...].astype(F32), gkv_ref[...]).astype(BF16)
    kv_ref[...] = _dot(ckv, wkv_ref[...]).astype(kv_ref.dtype)


def _latent_up(proj, gq, gkv, wq, wkv, tm=TILE["latent_up"]):
    m = proj.shape[0]
    nq, nkv = wq.shape[1], wkv.shape[1]
    q_scale = (C_NOPE_DIM + C_ROPE_DIM) ** -0.5 * LOG2E
    return pl.pallas_call(
        functools.partial(_latent_up_kernel, q_scale=q_scale),
        grid=(m // tm,),
        in_specs=[
            pl.BlockSpec((tm, C_Q_RANK), lambda i: (i, CB_CQ)),
            pl.BlockSpec((tm, C_KV_RANK), lambda i: (i, CB_CKV)),
            pl.BlockSpec((1, C_Q_RANK), lambda i: (0, 0)),
            pl.BlockSpec((1, C_KV_RANK), lambda i: (0, 0)),
            pl.BlockSpec((C_Q_RANK, nq), lambda i: (0, 0)),
            pl.BlockSpec((C_KV_RANK, nkv), lambda i: (0, 0)),
        ],
        out_specs=[pl.BlockSpec((tm, nq), lambda i: (i, 0)), pl.BlockSpec((tm, nkv), lambda i: (i, 0))],
        out_shape=[jax.ShapeDtypeStruct((m, nq), BF16), jax.ShapeDtypeStruct((m, nkv), BF16)],
        compiler_params=_params(("parallel",), "latent_up"),
        name="latent_up",
    )(proj, proj, gq, gkv, wq, wkv)


A_ROW_CHUNK = 256
C_ROW_CHUNK = 256


def _attend(q_all, k_ref, v1_ref, dv, chunk):
    outs = []
    for c in range(q_all.shape[0] // chunk):
        q = q_all[c * chunk:(c + 1) * chunk]
        s = _dot_nt(q, k_ref[...])
        p = jnp.exp2(s - jnp.max(s, axis=-1, keepdims=True))
        ol = _dot(p.astype(BF16), v1_ref[...])
        outs.append(ol[:, :dv] * (1.0 / ol[:, dv:]))
    return outs


def _attn_a_kernel(q0_ref, q1_ref, q2_ref, k_ref, v_ref, cos_ref, slo_ref, shi_ref, gq_ref, gk_ref, wo_ref,
                   o_ref, wo_bf_ref, kbuf, v1buf, *, tq):
    qi = pl.program_id(2)
    wo_bf_ref[...] = wo_ref[...].astype(BF16)

    @pl.when(qi == 0)
    def _():
        k = _rms(k_ref[...].astype(F32), gk_ref[...])
        kbuf[...] = _rope(k, cos_ref[...], slo_ref[...], shi_ref[...], 32).astype(BF16)
        v1buf[:, :HEAD_DIM] = v_ref[...]

    @pl.when((pl.program_id(1) == 0) & (qi == 0))
    def _():
        v1buf[:, HEAD_DIM:] = jnp.ones((v1buf.shape[0], HEAD_DIM), BF16)

    rows = pl.ds(pl.multiple_of(qi * tq, tq), tq)
    cos, slo, shi = cos_ref[rows, :], slo_ref[rows, :], shi_ref[rows, :]
    q_scale = HEAD_DIM ** -0.5 * LOG2E
    qs = []
    for q_ref in (q0_ref, q1_ref, q2_ref):
        q = _rms(q_ref[...].astype(F32), gq_ref[...])
        qs.append((_rope(q, cos, slo, shi, 32) * q_scale).astype(BF16))
    q_all = jnp.concatenate(qs, axis=0)
    per_head = tq // A_ROW_CHUNK
    for c, o in enumerate(_attend(q_all, kbuf, v1buf, HEAD_DIM, A_ROW_CHUNK)):
        g, r = divmod(c, per_head)
        o_ref[r * A_ROW_CHUNK:(r + 1) * A_ROW_CHUNK, g * HEAD_DIM:(g + 1) * HEAD_DIM] = o.astype(o_ref.dtype)


def _attn_a(proj, tabs, gq, gk, w_out, layer, batch, seq, tq=TILE["attn_a"]):
    nq = seq // tq
    cos, slo, shi = tabs
    d = w_out.shape[2]
    wo_rows = w_out.shape[1] // (batch * A_KV_HEADS * nq)

    def wo_index(b, h, qi):
        return (b * A_KV_HEADS + h) * nq + qi

    def q_spec(g):
        return pl.BlockSpec((tq, HEAD_DIM), lambda b, h, qi: (b * nq + qi, CB_AQ + h * A_GROUP + g))

    tab_spec = pl.BlockSpec((seq, LANES), lambda b, h, qi: (0, 0))
    gain_spec = pl.BlockSpec((1, HEAD_DIM), lambda b, h, qi: (0, 0))
    return pl.pallas_call(
        functools.partial(_attn_a_kernel, tq=tq),
        grid=(batch, A_KV_HEADS, nq),
        in_specs=[
            q_spec(0), q_spec(1), q_spec(2),
            pl.BlockSpec((seq, HEAD_DIM), lambda b, h, qi: (b, CB_AK + h)),
            pl.BlockSpec((seq, HEAD_DIM), lambda b, h, qi: (b, CB_AV + h)),
            tab_spec, tab_spec, tab_spec, gain_spec, gain_spec,
            pl.BlockSpec((None, wo_rows, d), lambda b, h, qi: (layer, wo_index(b, h, qi), 0)),
        ],
        out_specs=[pl.BlockSpec((tq, A_GROUP * HEAD_DIM), lambda b, h, qi: (b * nq + qi, h)),
                   pl.BlockSpec((wo_rows, d), lambda b, h, qi: (wo_index(b, h, qi), 0))],
        out_shape=[jax.ShapeDtypeStruct((batch * seq, A_WIDTH), BF16),
                   jax.ShapeDtypeStruct(w_out.shape[1:], BF16)],
        scratch_shapes=[pltpu.VMEM((seq, HEAD_DIM), BF16), pltpu.VMEM((seq, 2 * HEAD_DIM), BF16)],
        compiler_params=_params(("parallel", "arbitrary", "arbitrary"), "attn_a"),
        name="attn_a",
    )(proj, proj, proj, proj, proj, cos, slo, shi, gq, gk, w_out)


def _attn_c_kernel(q_ref, kn_ref, v_ref, kr_ref, cos_ref, slo_ref, shi_ref, o_ref, kbuf, v1buf, *, tq):
    h, qi = pl.program_id(1), pl.program_id(2)

    @pl.when((h == 0) & (qi == 0))
    def _():
        kr = _rope(kr_ref[...].astype(F32), cos_ref[...], slo_ref[...], shi_ref[...], 32)
        kbuf[:, C_NOPE_DIM:] = kr.astype(BF16)
        v1buf[:, C_V_DIM:] = jnp.ones((v1buf.shape[0], C_V_DIM), BF16)

    @pl.when(qi == 0)
    def _():
        kbuf[:, :C_NOPE_DIM] = kn_ref[...]
        v1buf[:, :C_V_DIM] = v_ref[...]

    rows = pl.ds(pl.multiple_of(qi * tq, tq), tq)
    q_rope = _rope(q_ref[:, C_NOPE_DIM:].astype(F32), cos_ref[rows, :], slo_ref[rows, :], shi_ref[rows, :], 32)
    q_all = jnp.concatenate([q_ref[:, :C_NOPE_DIM], q_rope.astype(BF16)], axis=1)
    for c, o in enumerate(_attend(q_all, kbuf, v1buf, C_V_DIM, C_ROW_CHUNK)):
        o_ref[c * C_ROW_CHUNK:(c + 1) * C_ROW_CHUNK, :] = o.astype(o_ref.dtype)


def _attn_c(qc, kvc, proj, tabs, batch, seq, tq=TILE["attn_c"]):
    nq = seq // tq
    cos, slo, shi = tabs
    tab_spec = pl.BlockSpec((seq, LANES), lambda b, h, qi: (0, 0))
    return pl.pallas_call(
        functools.partial(_attn_c_kernel, tq=tq),
        grid=(batch, C_HEADS, nq),
        in_specs=[
            pl.BlockSpec((tq, C_QK_PAD), lambda b, h, qi: (b * nq + qi, h)),
            pl.BlockSpec((seq, C_NOPE_DIM), lambda b, h, qi: (b, h)),
            pl.BlockSpec((seq, C_V_DIM), lambda b, h, qi: (b, C_HEADS + h)),
            pl.BlockSpec((seq, LANES), lambda b, h, qi: (b, CB_CKR)),
            tab_spec, tab_spec, tab_spec,
        ],
        out_specs=pl.BlockSpec((tq, C_V_DIM), lambda b, h, qi: (b * nq + qi, h)),
        out_shape=jax.ShapeDtypeStruct((batch * seq, C_WIDTH), BF16),
        scratch_shapes=[pltpu.VMEM((seq, C_QK_PAD), BF16), pltpu.VMEM((seq, 2 * C_V_DIM), BF16)],
        compiler_params=_params(("parallel", "arbitrary", "arbitrary"), "attn_c"),
        name="attn_c",
    )(qc, kvc, kvc, proj, cos, slo, shi)


B_QBLK = 128
(B_HALF,) = {w // (2 * d) for w, d in B_PATTERNS}
assert 2 * B_HALF == B_QBLK
B_PREP_ROWS = 256
B_UNROLL = 16


def _attn_b_kernel(q0_ref, q1_ref, q2_ref, k_ref, v_ref, cos_ref, slo_ref, shi_ref, o_ref,
                   q0f, q1f, q2f, kf, vf, m_s, l_s, acc_s, bias_s, *, seq):
    q_refs = (q0_ref, q1_ref, q2_ref)
    q_bufs = (q0f, q1f, q2f)
    q_scale = HEAD_DIM ** -0.5 * LOG2E

    half = B_HALF

    @pl.when(pl.program_id(1) == 0)
    def _():
        shape = (B_QBLK, 2 * B_QBLK)
        rel = lax.broadcasted_iota(jnp.int32, shape, 1) - lax.broadcasted_iota(jnp.int32, shape, 0)
        for case, offset in enumerate((0, -half, -B_QBLK)):
            bias_s[case] = jnp.where(jnp.abs(rel + offset) <= half, 0.0, -jnp.inf)

    def prep(ci, carry):
        rows = pl.ds(pl.multiple_of(ci * B_PREP_ROWS, B_PREP_ROWS), B_PREP_ROWS)
        cos, slo, shi = cos_ref[rows, :], slo_ref[rows, :], shi_ref[rows, :]
        for q_ref, q_buf in zip(q_refs, q_bufs):
            q_buf[rows, :] = _rope(q_ref[rows, :].astype(F32), cos, slo, shi, 16) * q_scale
        kf[rows, :] = _rope(k_ref[rows, :].astype(F32), cos, slo, shi, 16)
        vf[rows, :] = v_ref[rows, :].astype(F32)
        return carry

    lax.fori_loop(0, seq // B_PREP_ROWS, prep, 0, unroll=4)

    for g, (window, dil) in reversed(list(enumerate(B_PATTERNS))):
        first = g == len(B_PATTERNS) - 1
        half = window // (2 * dil)
        length = seq // dil
        nblk = length // B_QBLK
        kwin = min(2 * B_QBLK, length)
        q_buf = q_bufs[g]

        def rows_of(start, size, dil=dil):
            return pl.ds(start, size) if dil == 1 else pl.ds(start, size, stride=dil)

        def block(n, carry, first=first, half=half, length=length, nblk=nblk, kwin=kwin, q_buf=q_buf, dil=dil,
                  rows_of=rows_of):
            r = n // nblk
            i = n % nblk
            k0 = jnp.clip(i * B_QBLK - half, 0, length - kwin)
            q_rows = rows_of(r + dil * B_QBLK * i, B_QBLK)
            k_rows = rows_of(r + dil * k0, kwin)
            q = q_buf[q_rows, :].astype(BF16)
            k = kf[k_rows, :].astype(BF16)
            v = vf[k_rows, :].astype(BF16)
            case = 0 if nblk == 1 else jnp.where(i == 0, 0, jnp.where(i == nblk - 1, 2, 1))
            s = _dot_nt(q, k) + bias_s[case, :, :kwin]
            m_b = jnp.max(s, axis=-1, keepdims=True)
            p = jnp.exp2(s - m_b)
            l_b = jnp.sum(p, axis=-1, keepdims=True)
            a_b = _dot(p.astype(BF16), v)
            full = (B_QBLK, HEAD_DIM)
            if first:
                m_s[q_rows, :] = jnp.broadcast_to(m_b, full)
                l_s[q_rows, :] = jnp.broadcast_to(l_b, full)
                acc_s[q_rows, :] = a_b
            else:
                m_o = m_s[q_rows, :]
                m_n = jnp.maximum(m_o, m_b)
                w_o = jnp.exp2(m_o - m_n)
                w_b = jnp.exp2(m_b - m_n)
                acc_s[q_rows, :] = acc_s[q_rows, :] * w_o + a_b * w_b
                l_s[q_rows, :] = l_s[q_rows, :] * w_o + l_b * w_b
                m_s[q_rows, :] = m_n
            return carry

        lax.fori_loop(0, seq // B_QBLK, block, 0, unroll=B_UNROLL)

    def finish(ci, carry):
        rows = pl.ds(pl.multiple_of(ci * B_PREP_ROWS, B_PREP_ROWS), B_PREP_ROWS)
        o_ref[rows, :] = (acc_s[rows, :] * (1.0 / l_s[rows, :])).astype(o_ref.dtype)
        return carry

    lax.fori_loop(0, seq // B_PREP_ROWS, finish, 0)


def _attn_b(proj, tabs, batch, seq):
    cos, slo, shi = tabs

    def q_spec(g):
        return pl.BlockSpec((seq, HEAD_DIM), lambda b, h: (b, CB_BQ + g * B_HEADS + h))

    tab_spec = pl.BlockSpec((seq, LANES), lambda b, h: (0, 0))
    slab = pltpu.VMEM((seq, HEAD_DIM), F32)
    return pl.pallas_call(
        functools.partial(_attn_b_kernel, seq=seq),
        grid=(batch, B_HEADS),
        in_specs=[
            q_spec(0), q_spec(1), q_spec(2),
            pl.BlockSpec((seq, HEAD_DIM), lambda b, h: (b, CB_BK + h)),
            pl.BlockSpec((seq, HEAD_DIM), lambda b, h: (b, CB_BV + h)),
            tab_spec, tab_spec, tab_spec,
        ],
        out_specs=pl.BlockSpec((seq, HEAD_DIM), lambda b, h: (b, h)),
        out_shape=jax.ShapeDtypeStruct((batch * seq, B_WIDTH), BF16),
        scratch_shapes=[slab] * 8 + [pltpu.VMEM((3, B_QBLK, 2 * B_QBLK), F32)],
        compiler_params=_params(("parallel", "arbitrary"), "attn_b"),
        name="attn_b",
    )(proj, proj, proj, proj, proj, cos, slo, shi)


def _out_kernel(ya_ref, yb_ref, yc_ref, ga_ref, gb_ref, gc_ref, w_ref, x_ref, gf_ref, x1_ref, xn_ref):
    y = jnp.concatenate([
        _rms(ya_ref[...].astype(F32), ga_ref[...]).astype(BF16),
        _rms(yb_ref[...].astype(F32), gb_ref[...]).astype(BF16),
        _rms(yc_ref[...].astype(F32), gc_ref[...]).astype(BF16),
    ], axis=1)
    x1 = x_ref[...] + _dot(y, w_ref[...])
    x1_ref[...] = x1
    xn_ref[...] = _rms(x1, gf_ref[...]).astype(xn_ref.dtype)


def _out_proj(ya, yb, yc, ga, gb, gc, w, x, gf, tm=TILE["out_proj"]):
    m, d = x.shape

    def rows(width):
        return pl.BlockSpec((tm, width), lambda i: (i, 0))

    def const(r, c):
        return pl.BlockSpec((r, c), lambda i: (0, 0))

    return pl.pallas_call(
        _out_kernel,
        grid=(m // tm,),
        in_specs=[rows(A_WIDTH), rows(B_WIDTH), rows(C_WIDTH), const(1, A_WIDTH), const(1, B_WIDTH),
                  const(1, C_WIDTH), const(d, d), rows(d), const(1, d)],
        out_specs=[rows(d), rows(d)],
        out_shape=[jax.ShapeDtypeStruct((m, d), F32), jax.ShapeDtypeStruct((m, d), BF16)],
        compiler_params=_params(("parallel",), "out_proj"),
        name="out_proj",
    )(ya, yb, yc, ga, gb, gc, w, x, gf)


FFN_SUBTILE = 256
FFN_ROWS = 1024
FFN_GATE_ROWS = 128
FFN_GUARD = 8

def _ffn_up_kernel(xn_ref, wg_ref, wu_ref, cg_ref, cu_ref, bg_ref, bu_ref, wd_ref, o_ref, wd_bf_ref,
                   hg, hu, wg_bf, wu_bf, *, seq):
    tn = o_ref.shape[1]
    slabs = FFN_SUBTILE // LANES
    wd_bf_ref[...] = wd_ref[...].astype(BF16)

    @pl.when(pl.program_id(1) == 0)
    def _():
        wg_bf[...] = wg_ref[...].astype(BF16)
        wu_bf[...] = wu_ref[...].astype(BF16)

    zeros = jnp.zeros((FFN_GUARD, LANES), F32)
    for h in (hg, hu):
        for s in range(tn // LANES):
            h[s, 0:FFN_GUARD, :] = zeros
            h[s, FFN_GUARD + seq:2 * FFN_GUARD + seq, :] = zeros

    def matmuls(c, r):
        cols = slice(c * FFN_SUBTILE, (c + 1) * FFN_SUBTILE)
        xr = xn_ref[r * FFN_ROWS:(r + 1) * FFN_ROWS, :]
        dst = slice(FFN_GUARD + r * FFN_ROWS, FFN_GUARD + (r + 1) * FFN_ROWS)
        for h, w in ((hg, wg_bf), (hu, wu_bf)):
            res = _dot(xr, w[:, cols])
            for s in range(slabs):
                h[c * slabs + s, dst, :] = res[:, s * LANES:(s + 1) * LANES]

    def gate_stage(c, r):
        for k in range(FFN_ROWS // FFN_GATE_ROWS):
            row0 = r * FFN_ROWS + k * FFN_GATE_ROWS
            base = FFN_GUARD + row0
            acts = []
            for s in range(slabs):
                slab = c * slabs + s
                lanes = slice(slab * LANES, (slab + 1) * LANES)

                def tap(h, t):
                    if t == 1:
                        return h[slab, base:base + FFN_GATE_ROWS, :]
                    return h[pl.ds(slab, 1, stride=2), pl.ds(base + t - 1, FFN_GATE_ROWS), :][0]

                def conv(h, c_ref, b_ref):
                    taps = [tap(h, t) * c_ref[t:t + 1, lanes] for t in range(3)]
                    return taps[0] + taps[1] + taps[2] + b_ref[:, lanes]

                gate = conv(hg, cg_ref, bg_ref)
                up = conv(hu, cu_ref, bu_ref)
                acts.append(gate * (1.0 / (1.0 + jnp.exp2(gate * -LOG2E))) * up)
            o_ref[row0:row0 + FFN_GATE_ROWS, c * FFN_SUBTILE:(c + 1) * FFN_SUBTILE] = (
                jnp.concatenate(acts, axis=1).astype(o_ref.dtype))

    units = [(c, r) for c in range(tn // FFN_SUBTILE) for r in range(seq // FFN_ROWS)]
    for i, unit in enumerate(units):
        matmuls(*unit)
        if i > 0:
            gate_stage(*units[i - 1])
    gate_stage(*units[-1])


def _ffn_up(xn, w_up, w_down, layer, conv_w, conv_b, batch, seq, tn=TILE["ffn_up"]):
    d = xn.shape[1]
    nj = D_FF // tn
    wd_rows = w_down.shape[1] // (nj * batch)
    return pl.pallas_call(
        functools.partial(_ffn_up_kernel, seq=seq),
        grid=(nj, batch),
        in_specs=[
            pl.BlockSpec((seq, d), lambda j, b: (b, 0)),
            pl.BlockSpec((None, d, tn), lambda j, b: (layer, 0, j)),
            pl.BlockSpec((None, d, tn), lambda j, b: (layer, 0, nj + j)),
            pl.BlockSpec((3, tn), lambda j, b: (0, j)),
            pl.BlockSpec((3, tn), lambda j, b: (0, nj + j)),
            pl.BlockSpec((1, tn), lambda j, b: (0, j)),
            pl.BlockSpec((1, tn), lambda j, b: (0, nj + j)),
            pl.BlockSpec((None, wd_rows, d), lambda j, b: (layer, j * batch + b, 0)),
        ],
        out_specs=[pl.BlockSpec((seq, tn), lambda j, b: (b, j)),
                   pl.BlockSpec((wd_rows, d), lambda j, b: (j * batch + b, 0))],
        out_shape=[jax.ShapeDtypeStruct((batch * seq, D_FF), BF16),
                   jax.ShapeDtypeStruct(w_down.shape[1:], BF16)],
        scratch_shapes=[pltpu.VMEM((tn // LANES, seq + 2 * FFN_GUARD, LANES), F32)] * 2
        + [pltpu.VMEM((d, tn), BF16)] * 2,
        compiler_params=_params(("parallel", "arbitrary"), "ffn_up"),
        name="ffn_up",
    )(xn, w_up, w_up, conv_w, conv_w, conv_b, conv_b, w_down)


def _ffn_down_kernel(a_ref, w_ref, x1_ref, g_ref, x2_ref, xn_ref):
    x2 = x1_ref[...] + _dot(a_ref[...], w_ref[...])
    x2_ref[...] = x2
    xn_ref[...] = _rms(x2, g_ref[...]).astype(xn_ref.dtype)


def _ffn_down_final_kernel(a_ref, w_ref, x1_ref, g_ref, xn_ref):
    x2 = x1_ref[...] + _dot(a_ref[...], w_ref[...])
    xn_ref[...] = _rms(x2, g_ref[...]).astype(xn_ref.dtype)


def _ffn_down(act, w, x1, g, final, tm=TILE["ffn_down"]):
    m, kdim = act.shape
    d = w.shape[1]
    tile = pl.BlockSpec((tm, d), lambda i: (i, 0))
    in_specs = [
        pl.BlockSpec((tm, kdim), lambda i: (i, 0)),
        pl.BlockSpec((kdim, d), lambda i: (0, 0), pipeline_mode=pl.Buffered(1)),
        tile,
        pl.BlockSpec((1, d), lambda i: (0, 0)),
    ]
    common = dict(grid=(m // tm,), in_specs=in_specs, compiler_params=_params(("parallel",), "ffn_down"))
    if final:
        xn = pl.pallas_call(_ffn_down_final_kernel, out_specs=tile, out_shape=jax.ShapeDtypeStruct((m, d), F32),
                            name="ffn_down_final", **common)(act, w, x1, g)
        return None, xn
    return pl.pallas_call(_ffn_down_kernel, out_specs=[tile, tile],
                          out_shape=[jax.ShapeDtypeStruct((m, d), F32), jax.ShapeDtypeStruct((m, d), BF16)],
                          name="ffn_down", **common)(act, w, x1, g)


def _rope_table(pos, dim, theta):
    f32 = np.float32
    inv = f32(theta) ** (-np.arange(0, dim, 2, dtype=f32) / f32(dim))
    ang = (pos.astype(f32)[:, None] * inv[None, :]).astype(np.float64)
    return np.cos(ang).astype(f32), np.sin(ang).astype(f32)


def _lane_tables(seq):
    t = np.arange(seq)
    z = lambda w: np.zeros((seq, w), np.float32)
    one = lambda w: np.ones((seq, w), np.float32)
    cat = lambda parts: jnp.asarray(np.concatenate(parts, 1))
    cr, sr = _rope_table(t // GRID_W, HEAD_DIM // 2, A_ROPE_THETA)
    cc, sc = _rope_table(t % GRID_W, HEAD_DIM // 2, A_ROPE_THETA)
    tab_a = (cat([cr, cr, cc, cc]), cat([-sr, z(32), -sc, z(32)]), cat([z(32), sr, z(32), sc]))
    cp, sp = _rope_table(t, PARTIAL_ROPE_DIM, PARTIAL_ROPE_THETA)
    tab_b = (cat([cp, cp, one(96)]), cat([-sp, z(112)]), cat([z(16), sp, z(96)]))
    cm, sm = _rope_table(t, C_ROPE_DIM, C_ROPE_THETA)
    tab_c = (cat([cm, cm, one(64)]), cat([-sm, z(96)]), cat([z(32), sm, z(64)]))
    return tab_a, tab_b, tab_c


def _prep_w_ckr(w_in_t, layer):
    ckr = w_in_t[layer, IN_N_SRC * IN_WBLK:, :]
    return jnp.pad(ckr, ((0, IN_WBLK - ckr.shape[0]), (0, 0)))


def _prep_w_uq(w):
    w = w.reshape(C_Q_RANK, C_HEADS, C_NOPE_DIM + C_ROPE_DIM)
    w = jnp.pad(w, ((0, 0), (0, 0), (0, C_QK_PAD - C_NOPE_DIM - C_ROPE_DIM)))
    return w.reshape(C_Q_RANK, C_HEADS * C_QK_PAD).astype(BF16)


def _prep_w_ukv(w):
    w = w.reshape(C_KV_RANK, C_HEADS, C_NOPE_DIM + C_V_DIM)
    kn = w[:, :, :C_NOPE_DIM].reshape(C_KV_RANK, C_HEADS * C_NOPE_DIM)
    v = w[:, :, C_NOPE_DIM:].reshape(C_KV_RANK, C_HEADS * C_V_DIM)
    return jnp.concatenate([kn, v], axis=1).astype(BF16)


def kernel(x, attn_norm, w_in, a_q_norm, a_k_norm, c_q_norm, c_kv_norm, w_uq, w_ukv, out_norm, w_out,
           ffn_norm, w_up, conv_w, conv_b, w_down, final_norm):
    batch, seq, d = x.shape
    depth = w_in.shape[0]
    tab_a, tab_b, tab_c = _lane_tables(seq)
    row = lambda v: v.reshape(1, -1)

    xr = x.reshape(batch * seq, d)
    xn = _rmsnorm_rows(xr, row(attn_norm[0]))
    w_in_t = jnp.swapaxes(w_in, 1, 2)
    for l in range(depth):
        proj = _in_proj(xn, w_in_t, l, _prep_w_ckr(w_in_t, l))
        qc, kvc = _latent_up(proj, row(c_q_norm[l]), row(c_kv_norm[l]), _prep_w_uq(w_uq[l]),
                             _prep_w_ukv(w_ukv[l]))
        ya, w_out_bf = _attn_a(proj, tab_a, row(a_q_norm[l]), row(a_k_norm[l]), w_out, l, batch, seq)
        yb = _attn_b(proj, tab_b, batch, seq)
        yc = _attn_c(qc, kvc, proj, tab_c, batch, seq)
        g = out_norm[l]
        x1, xn1 = _out_proj(ya, yb, yc, row(g[:A_WIDTH]), row(g[A_WIDTH:A_WIDTH + B_WIDTH]),
                            row(g[A_WIDTH + B_WIDTH:]), w_out_bf, xr, row(ffn_norm[l]))
        act, w_down_bf = _ffn_up(xn1, w_up, w_down, l, conv_w[l], row(conv_b[l]), batch, seq)
        last = l == depth - 1
        g_next = final_norm if last else attn_norm[l + 1]
        xr, xn = _ffn_down(act, w_down_bf, x1, row(g_next), last)
    return xn.reshape(batch, seq, d)
```
